```python
import jax, jax.numpy as jnp
from jax import lax
import numpy as np

D_MODEL = 1024
BATCH = 8
SEQ = 2048
DEPTH = 2

CTX_LEN = 256
GRID_W = 64
D_CONV = 512
HGRN_HEADS = 4
HGRN_HEAD_DIM = 128
D_HGRN = HGRN_HEADS * HGRN_HEAD_DIM
D_MIX = D_CONV + D_HGRN
CONV_KERNEL = 31
CHUNK = 64
N_EXPERTS = 16
N_GROUPS = 4
EXPERTS_PER_GROUP = N_EXPERTS // N_GROUPS
TOP_K = 2
D_EXPERT = 1024
EPS = 1e-6
D_IN = 2 * D_CONV + 5 * D_HGRN
SPLITS = [D_CONV, 2 * D_CONV, 2 * D_CONV + D_HGRN, 2 * D_CONV + 2 * D_HGRN,
          2 * D_CONV + 3 * D_HGRN, 2 * D_CONV + 4 * D_HGRN]

kernel_name = "hybrid_conv_hgrn2_moe_dit"


def rms_norm(x, g):
    xf = x.astype(jnp.float32)
    y = xf * lax.rsqrt(jnp.mean(xf * xf, axis=-1, keepdims=True) + EPS)
    return (y * g.astype(jnp.float32)).astype(x.dtype)


def layer_norm(x, g, b):
    xf = x.astype(jnp.float32)
    mu = jnp.mean(xf, axis=-1, keepdims=True)
    var = jnp.mean(jnp.square(xf - mu), axis=-1, keepdims=True)
    y = (xf - mu) * lax.rsqrt(var + EPS)
    return (y * g.astype(jnp.float32) + b.astype(jnp.float32)).astype(x.dtype)


def modulate(x, g, shift, scale):
    return rms_norm(x, g) * (1 + scale) + shift


def depthwise_conv(u, w, b):
    y = lax.conv_general_dilated(
        u, w[:, None, :].astype(u.dtype), window_strides=(1,),
        padding=[(CONV_KERNEL // 2, CONV_KERNEL // 2)],
        dimension_numbers=('NWC', 'WIO', 'NWC'), feature_group_count=u.shape[-1])
    return y + b.astype(u.dtype)


def conformer_conv(a, gate, w, b, ln_g, ln_b, orient):
    u = a * jax.nn.sigmoid(gate)
    bn, L, C = u.shape
    if orient is None:
        y = depthwise_conv(u, w, b)
    else:
        rows = L // GRID_W
        grid = u.reshape(bn, rows, GRID_W, C)
        if orient == 'col':
            grid = grid.transpose(0, 2, 1, 3)
        n1, n2 = grid.shape[1], grid.shape[2]
        y = depthwise_conv(grid.reshape(bn * n1, n2, C), w, b).reshape(bn, n1, n2, C)
        if orient == 'col':
            y = y.transpose(0, 2, 1, 3)
        y = y.reshape(bn, L, C)
    return jax.nn.silu(layer_norm(y, ln_g, ln_b))


def split_heads(t):
    bn, L, _ = t.shape
    return t.reshape(bn, L, HGRN_HEADS, HGRN_HEAD_DIM).transpose(0, 2, 1, 3)


def flip(t):
    return t[:, :, ::-1]


def lower_bounds(lb_param):
    p = jax.nn.softmax(lb_param.astype(jnp.float32), axis=0)
    return jnp.cumsum(p, axis=0) - p[0]


def forget_gate(f_raw, lb):
    f = lb + (1.0 - lb) * jax.nn.sigmoid(split_heads(f_raw).astype(jnp.float32))
    return jnp.log(f), 1.0 - f


def chunk_scan(q, k, v, logf, s0):
    q, k, v, logf = (t.astype(jnp.float32) for t in (q, k, v, logf))
    bn, h, L, dk = q.shape
    n = L // CHUNK

    def to_chunks(t):
        return jnp.moveaxis(t.reshape(bn, h, n, CHUNK, t.shape[-1]), 2, 0)

    tri = jnp.tril(jnp.ones((CHUNK, CHUNK), dtype=bool))[:, :, None]

    def step(S, inp):
        qc, kc, vc, gc = inp
        b = jnp.cumsum(gc, axis=2)
        o_inter = jnp.einsum('bhtd,bhdv->bhtv', qc * jnp.exp(b), S)
        diff = b[:, :, :, None, :] - b[:, :, None, :, :]
        decay = jnp.exp(jnp.where(tri, diff, -jnp.inf))
        att = jnp.einsum('bhtsd,bhsd->bhts', qc[:, :, :, None, :] * decay, kc)
        o = o_inter + jnp.einsum('bhts,bhsv->bhtv', att, vc)
        b_last = b[:, :, -1:, :]
        S = jnp.exp(b_last[:, :, 0, :, None]) * S + jnp.einsum(
            'bhsd,bhsv->bhdv', kc * jnp.exp(b_last - b), vc)
        return S, o

    S, o = lax.scan(step, s0, (to_chunks(q), to_chunks(k), to_chunks(v), to_chunks(logf)))
    o = jnp.moveaxis(o, 0, 2).reshape(bn, h, L, v.shape[-1])
    return o, S


def final_state(k, v, logf):
    b = jnp.cumsum(logf.astype(jnp.float32), axis=2)
    kd = k.astype(jnp.float32) * jnp.exp(b[:, :, -1:, :] - b)
    return jnp.einsum('bhsd,bhsv->bhdv', kd, v.astype(jnp.float32))


def hgrn_readout(o, og, norm_g, dtype):
    y = rms_norm(o, norm_g) * jax.nn.silu(split_heads(og).astype(jnp.float32))
    bn, h, L, dh = y.shape
    return y.transpose(0, 2, 1, 3).reshape(bn, L, h * dh).astype(dtype)


def moe(h, router_w, router_bias, w_gate, w_up, w_down):
    s = jax.nn.sigmoid(h.astype(jnp.float32) @ router_w.astype(jnp.float32))
    sb = s + router_bias.astype(jnp.float32)
    grp = sb.reshape(-1, N_GROUPS, EXPERTS_PER_GROUP)
    group_score = lax.top_k(grp, TOP_K)[0].sum(-1)
    best = jnp.argmax(group_score, axis=-1)
    in_group = (jnp.arange(N_EXPERTS) // EXPERTS_PER_GROUP)[None, :] == best[:, None]
    _, idx = lax.top_k(jnp.where(in_group, sb, -jnp.inf), TOP_K)
    w = jnp.take_along_axis(s, idx, axis=-1)
    w = w / jnp.sum(w, axis=-1, keepdims=True)
    combine = jnp.sum(jax.nn.one_hot(idx, N_EXPERTS, dtype=jnp.float32) * w[..., None], axis=1)
    combine = combine.astype(h.dtype)
    out = jnp.zeros_like(h)
    for e in range(N_EXPERTS):
        y = (jax.nn.silu(h @ w_gate[e]) * (h @ w_up[e])) @ w_down[e]
        out = out + combine[:, e:e + 1] * y
    return out


def setup_inputs(seed: int = 0) -> dict:
    key = jax.random.key(seed)
    ks = jax.random.split(key, 24)
    f32 = jnp.float32

    def nrm(k, shape, scale):
        return jax.random.normal(k, shape, f32) * scale

    return {
        "x": nrm(ks[0], (BATCH, SEQ, D_MODEL), 1.0),
        "c": nrm(ks[1], (BATCH, D_MODEL), 1.0),
        "ctx": nrm(ks[2], (BATCH, CTX_LEN, D_MODEL), 1.0),
        "c_ctx": nrm(ks[3], (D_MODEL,), 1.0),
        "w_ada": nrm(ks[4], (DEPTH, D_MODEL, 6 * D_MODEL), 0.5 * D_MODEL ** -0.5),
        "b_ada": nrm(ks[5], (DEPTH, 6 * D_MODEL), 0.02),
        "norm1_g": 1.0 + nrm(ks[6], (DEPTH, D_MODEL), 0.05),
        "norm2_g": 1.0 + nrm(ks[7], (DEPTH, D_MODEL), 0.05),
        "w_in": nrm(ks[8], (DEPTH, D_MODEL, D_IN), D_MODEL ** -0.5),
        "conv_w": nrm(ks[9], (DEPTH, CONV_KERNEL, D_CONV), CONV_KERNEL ** -0.5),
        "conv_b": nrm(ks[10], (DEPTH, D_CONV), 0.02),
        "conv_ln_g": 1.0 + nrm(ks[11], (DEPTH, D_CONV), 0.05),
        "conv_ln_b": nrm(ks[12], (DEPTH, D_CONV), 0.02),
        "lb_fwd": nrm(ks[13], (DEPTH, D_HGRN), 0.5),
        "lb_bwd": nrm(ks[14], (DEPTH, D_HGRN), 0.5),
        "hgrn_norm_g": 1.0 + nrm(ks[15], (DEPTH, HGRN_HEAD_DIM), 0.05),
        "w_out": nrm(ks[16], (DEPTH, D_MIX, D_MODEL), D_MIX ** -0.5),
        "router_w": nrm(ks[17], (D_MODEL, N_EXPERTS), D_MODEL ** -0.5),
        "router_bias": nrm(ks[18], (N_EXPERTS,), 0.01),
        "w_gate": nrm(ks[19], (DEPTH, N_EXPERTS, D_MODEL, D_EXPERT), D_MODEL ** -0.5),
        "w_up": nrm(ks[20], (DEPTH, N_EXPERTS, D_MODEL, D_EXPERT), D_MODEL ** -0.5),
        "w_down": nrm(ks[21], (DEPTH, N_EXPERTS, D_EXPERT, D_MODEL), D_EXPERT ** -0.5),
        "final_norm_g": 1.0 + nrm(ks[22], (D_MODEL,), 0.05),
    }


def reference(x, c, ctx, c_ctx, w_ada, b_ada, norm1_g, norm2_g, w_in, conv_w, conv_b,
              conv_ln_g, conv_ln_b, lb_fwd, lb_bwd, hgrn_norm_g, w_out, router_w,
              router_bias, w_gate, w_up, w_down, final_norm_g):
    bn, L, D = x.shape
    lbs_f = lower_bounds(lb_fwd)
    lbs_b = lower_bounds(lb_bwd)
    s_zero = jnp.zeros((bn, HGRN_HEADS, HGRN_HEAD_DIM, HGRN_HEAD_DIM), jnp.float32)

    for l in range(DEPTH):
        last = l == DEPTH - 1
        lb_f = lbs_f[l].reshape(HGRN_HEADS, 1, HGRN_HEAD_DIM)
        lb_b = lbs_b[l].reshape(HGRN_HEADS, 1, HGRN_HEAD_DIM)

        mod_x = jax.nn.silu(c) @ w_ada[l] + b_ada[l]
        mod_c = jax.nn.silu(c_ctx) @ w_ada[l] + b_ada[l]
        sh1, sc1, g1, sh2, sc2, g2 = jnp.split(mod_x[:, None, :], 6, axis=-1)
        sh1c, sc1c, g1c, sh2c, sc2c, g2c = jnp.split(mod_c, 6, axis=-1)

        hx = modulate(x, norm1_g[l], sh1, sc1)
        hc = modulate(ctx, norm1_g[l], sh1c, sc1c)
        ax, gx, qx, ix, ffx, fbx, ogx = jnp.split(hx @ w_in[l], SPLITS, axis=-1)

        if last:
            ic, ffc, fbc = jnp.split(hc @ w_in[l][:, SPLITS[2]:SPLITS[5]], 3, axis=-1)
            i_c = split_heads(ic)
            lf_cf, k_cf = forget_gate(ffc, lb_f)
            lf_cb, k_cb = forget_gate(fbc, lb_b)
            S_f = final_state(k_cf, i_c, lf_cf)
            S_b = final_state(flip(k_cb), flip(i_c), flip(lf_cb))
        else:
            ac, gc, qc, ic, ffc, fbc, ogc = jnp.split(hc @ w_in[l], SPLITS, axis=-1)
            q_c = split_heads(jax.nn.silu(qc))
            i_c = split_heads(ic)
            lf_cf, k_cf = forget_gate(ffc, lb_f)
            lf_cb, k_cb = forget_gate(fbc, lb_b)
            o_cf, S_f = chunk_scan(q_c, k_cf, i_c, lf_cf, s_zero)
            o_cb, S_b = chunk_scan(flip(q_c), flip(k_cb), flip(i_c), flip(lf_cb), s_zero)
            y_hc = hgrn_readout(o_cf + flip(o_cb), ogc, hgrn_norm_g[l], ctx.dtype)
            y_cc = conformer_conv(ac, gc, conv_w[l], conv_b[l], conv_ln_g[l], conv_ln_b[l], None)
            ctx = ctx + g1c * (jnp.concatenate([y_cc, y_hc], axis=-1) @ w_out[l])

        q_x = split_heads(jax.nn.silu(qx))
        i_x = split_heads(ix)
        lf_xf, k_xf = forget_gate(ffx, lb_f)
        lf_xb, k_xb = forget_gate(fbx, lb_b)
        o_xf, _ = chunk_scan(q_x, k_xf, i_x, lf_xf, S_f)
        o_xb, _ = chunk_scan(flip(q_x), flip(k_xb), flip(i_x), flip(lf_xb), S_b)
        y_hx = hgrn_readout(o_xf + flip(o_xb), ogx, hgrn_norm_g[l], x.dtype)

        orient = 'row' if l % 2 == 0 else 'col'
        y_cx = conformer_conv(ax, gx, conv_w[l], conv_b[l], conv_ln_g[l], conv_ln_b[l], orient)
        x = x + g1 * (jnp.concatenate([y_cx, y_hx], axis=-1) @ w_out[l])

        hx2 = modulate(x, norm2_g[l], sh2, sc2).reshape(bn * L, D)
        if last:
            out = moe(hx2, router_w, router_bias, w_gate[l], w_up[l], w_down[l])
            x = x + g2 * out.reshape(bn, L, D)
        else:
            hc2 = modulate(ctx, norm2_g[l], sh2c, sc2c).reshape(bn * CTX_LEN, D)
            out = moe(jnp.concatenate([hx2, hc2], axis=0), router_w, router_bias,
                      w_gate[l], w_up[l], w_down[l])
            x = x + g2 * out[:bn * L].reshape(bn, L, D)
            ctx = ctx + g2c * out[bn * L:].reshape(bn, CTX_LEN, D)

    return rms_norm(x, final_norm_g)
```

```python
import functools

import numpy as np
import jax
import jax.numpy as jnp
from jax import lax
from jax.experimental import pallas as pl
from jax.experimental.pallas import tpu as pltpu

F32 = jnp.float32
BF16 = jnp.bfloat16

EPS = 1e-6
GRID_W = 64
HEADS = 4
HEAD_DIM = 128
D_GROUP = HEADS * HEAD_DIM
N_IN_GROUPS = 7
CONV_K = 31
CONV_HALF = CONV_K // 2
CONV_GAP = 16
N_EXPERTS = 16
N_GROUPS = 4
EXPERTS_PER_GROUP = N_EXPERTS // N_GROUPS
SCAN_CHUNK = 64
LANES = 128
VMEM_LIMIT = 56 * 1024 * 1024


def _cparams(sem, vmem=None):
    return pltpu.CompilerParams(dimension_semantics=sem, vmem_limit_bytes=vmem)


def _sigmoid(x):
    return jax.nn.sigmoid(x)


def _silu(x):
    return x * _sigmoid(x)


def _ada_kernel(c_ref, w_ref, b_ref, o_ref):
    a = _silu(c_ref[...]).astype(BF16)
    w = w_ref[0].astype(BF16)
    o_ref[0] = jnp.dot(a, w, preferred_element_type=F32) + b_ref[0]


def _ada(cc, w_ada, b_ada, tn=1536):
    depth, d, d6 = w_ada.shape
    rows = cc.shape[0]
    return pl.pallas_call(
        _ada_kernel,
        out_shape=jax.ShapeDtypeStruct((depth, rows, d6), F32),
        grid=(depth, d6 // tn),
        in_specs=[
            pl.BlockSpec((rows, d), lambda l, j: (0, 0)),
            pl.BlockSpec((1, d, tn), lambda l, j: (l, 0, j)),
            pl.BlockSpec((1, 1, tn), lambda l, j: (l, 0, j)),
        ],
        out_specs=pl.BlockSpec((1, rows, tn), lambda l, j: (l, 0, j)),
        compiler_params=_cparams(("arbitrary", "arbitrary"), VMEM_LIMIT),
        name="ada",
    )(cc, w_ada, b_ada.reshape(depth, 1, d6))


def _modulate(x, g, shift, scale):
    r = lax.rsqrt(jnp.mean(x * x, axis=-1, keepdims=True) + EPS)
    return (x * r * g) * (1.0 + scale) + shift


def _inproj_kernel(x_ref, g_ref, sh_ref, sc_ref, w_ref, lbf_ref, lbb_ref,
                   u_ref, q_ref, i_ref, lff_ref, kf_ref, lfb_ref, kb_ref, og_ref):
    hb = _modulate(x_ref[...], g_ref[...], sh_ref[0], sc_ref[0]).astype(BF16)

    def proj(j):
        return jnp.dot(hb, w_ref[:, j * D_GROUP:(j + 1) * D_GROUP], preferred_element_type=F32)

    u_ref[...] = (proj(0) * _sigmoid(proj(1))).astype(BF16)
    q_ref[...] = _silu(proj(2)).astype(BF16)
    i_ref[...] = proj(3).astype(BF16)
    for j, lb_ref, lf_ref, k_ref in ((4, lbf_ref, lff_ref, kf_ref), (5, lbb_ref, lfb_ref, kb_ref)):
        lb = lb_ref[...]
        f = lb + (1.0 - lb) * _sigmoid(proj(j))
        lf_ref[...] = jnp.log(f)
        k_ref[...] = (1.0 - f).astype(BF16)
    og_ref[...] = _silu(proj(6)).astype(BF16)


def _inproj(xc, g, sh, sc, w_bf, lbf, lbb, tm, tiles_per_seq):
    n, d = xc.shape
    nb = sh.shape[0] - 1
    mod_spec = pl.BlockSpec((1, 1, d), lambda i: (jnp.minimum(i // tiles_per_seq, nb), 0, 0))
    row_spec = pl.BlockSpec((tm, D_GROUP), lambda i: (i, 0))
    vec_spec = pl.BlockSpec((1, D_GROUP), lambda i: (0, 0))
    outs = [jax.ShapeDtypeStruct((n, D_GROUP), dt)
            for dt in (BF16, BF16, BF16, F32, BF16, F32, BF16, BF16)]
    return pl.pallas_call(
        _inproj_kernel,
        out_shape=outs,
        grid=(n // tm,),
        in_specs=[
            pl.BlockSpec((tm, d), lambda i: (i, 0)),
            pl.BlockSpec((1, d), lambda i: (0, 0)),
            mod_spec, mod_spec,
            pl.BlockSpec((d, N_IN_GROUPS * D_GROUP), lambda i: (0, 0)),
            vec_spec, vec_spec,
        ],
        out_specs=[row_spec] * 8,
        compiler_params=_cparams(("arbitrary",), VMEM_LIMIT),
        name="inproj",
    )(xc, g, sh, sc, w_bf, lbf, lbb)


def _scan_tables(c, backward):
    levels = int(np.log2(c))
    r_all = np.zeros((levels * c + 2 * c + 8, c), np.float32)
    for lev in range(levels):
        h = 1 << lev
        for r in range(c):
            bd = (r // (2 * h)) * 2 * h + h
            if not backward:
                if r >= bd:
                    r_all[lev * c + r, bd:r + 1] = 1.0
                else:
                    r_all[lev * c + r, r + 1:bd] = 1.0
            else:
                if r < bd:
                    r_all[lev * c + r, r:bd] = 1.0
                else:
                    r_all[lev * c + r, bd:r] = 1.0
    base = levels * c
    for r in range(c):
        if not backward:
            r_all[base + r, :r + 1] = 1.0
            r_all[base + c + r, r + 1:] = 1.0
        else:
            r_all[base + r, r:] = 1.0
            r_all[base + c + r, :r] = 1.0
    r_all[base + 2 * c:, :] = 1.0
    return r_all


def _scan_kernel(*refs, n_chunks, compute_o):
    c = SCAN_CHUNK
    levels = int(np.log2(c))
    (rf_ref, rb_ref, qf_ref, vf_ref, lff_ref, kf_ref,
     qb_ref, vb_ref, lfb_ref, kb_ref, s0_ref) = refs[:11]
    pos = 11
    if compute_o:
        of_ref, ob_ref, sfin_ref, st_ref = refs[pos:pos + 4]
    else:
        sfin_ref, st_ref = refs[pos:pos + 2]
        of_ref = ob_ref = None
    j = pl.program_id(1)

    @pl.when(j == 0)
    def _():
        st_ref[...] = s0_ref[0]

    row = lax.broadcasted_iota(jnp.int32, (c, 1), 0)
    ri = lax.broadcasted_iota(jnp.int32, (c, c), 0)
    ci = lax.broadcasted_iota(jnp.int32, (c, c), 1)
    upper = [(row & (2 * (1 << lev) - 1)) >= (1 << lev) for lev in range(levels)]
    same_parent = [(ri >> (lev + 1)) == (ci >> (lev + 1)) for lev in range(levels)]
    diag = ri == ci
    dn_t = (((1,), (1,)), ((), ()))
    dn_tl = (((0,), (0,)), ((), ()))

    def one_chunk(dirn, r_ref, q_ref, v_ref, lf_ref, k_ref, o_ref, r0):
        rows = pl.ds(r0, c)
        g = lf_ref[rows, :]
        g_hi = g.astype(BF16)
        g_lo = (g - g_hi.astype(F32)).astype(BF16)
        rmat = r_ref[...]
        e_all = jnp.exp(jnp.dot(rmat, g_hi, preferred_element_type=F32)
                        + jnp.dot(rmat, g_lo, preferred_element_type=F32))
        base = levels * c
        for h in range(HEADS):
            hs = slice(h * HEAD_DIM, (h + 1) * HEAD_DIM)
            qh = q_ref[rows, hs].astype(F32)
            kh = k_ref[rows, hs].astype(F32)
            vh = v_ref[rows, hs]
            st = st_ref[dirn * HEADS + h]
            if compute_o:
                att = jnp.where(diag, lax.dot_general(qh.astype(BF16), kh.astype(BF16), dn_t,
                                                      preferred_element_type=F32), 0.0)
                for lev in range(levels):
                    e_l = e_all[lev * c:(lev + 1) * c, hs]
                    q_side = upper[lev] if dirn == 0 else jnp.logical_not(upper[lev])
                    qt = jnp.where(q_side, qh * e_l, 0.0).astype(BF16)
                    kt = jnp.where(q_side, 0.0, kh * e_l).astype(BF16)
                    a_l = lax.dot_general(qt, kt, dn_t, preferred_element_type=F32)
                    att = att + jnp.where(same_parent[lev], a_l, 0.0)
                qi = (qh * e_all[base:base + c, hs]).astype(BF16)
                o = lax.dot_general(qi, st.astype(BF16), dn_t, preferred_element_type=F32)
                o = o + jnp.dot(att.astype(BF16), vh, preferred_element_type=F32)
                o_ref[rows, hs] = o.astype(o_ref.dtype)
            kd = (kh * e_all[base + c:base + 2 * c, hs]).astype(BF16)
            e_tot = e_all[base + 2 * c:base + 2 * c + 1, hs]
            st_ref[dirn * HEADS + h] = st * e_tot + lax.dot_general(
                vh, kd, dn_tl, preferred_element_type=F32)

    def body(ci_, carry):
        r0f = pl.multiple_of(ci_ * c, c)
        r0b = pl.multiple_of((n_chunks - 1 - ci_) * c, c)
        one_chunk(0, rf_ref, qf_ref, vf_ref, lff_ref, kf_ref, of_ref, r0f)
        one_chunk(1, rb_ref, qb_ref, vb_ref, lfb_ref, kb_ref, ob_ref, r0b)
        return carry

    lax.fori_loop(0, n_chunks, body, 0)

    @pl.when(j == pl.num_programs(1) - 1)
    def _():
        sfin_ref[0] = st_ref[...]


def _scan(q, v, lff, kf, lfb, kb, s0, *, row_off, seq_len, compute_o):
    batch = s0.shape[0]
    t = min(256, seq_len)
    nt = seq_len // t
    off = row_off // t
    fwd = lambda b, j: (off + b * nt + j, 0)
    bwd = lambda b, j: (off + b * nt + (nt - 1 - j), 0)
    ofwd = lambda b, j: (b * nt + j, 0)
    obwd = lambda b, j: (b * nt + (nt - 1 - j), 0)
    blk = lambda im: pl.BlockSpec((t, D_GROUP), im)
    rf = jnp.asarray(_scan_tables(SCAN_CHUNK, False), BF16)
    rb = jnp.asarray(_scan_tables(SCAN_CHUNK, True), BF16)
    rspec = pl.BlockSpec(rf.shape, lambda b, j: (0, 0))
    sspec = pl.BlockSpec((1, 2 * HEADS, HEAD_DIM, HEAD_DIM), lambda b, j: (b, 0, 0, 0))
    in_specs = [rspec, rspec, blk(fwd), blk(fwd), blk(fwd), blk(fwd),
                blk(bwd), blk(bwd), blk(bwd), blk(bwd), sspec]
    args = [rf, rb, q, v, lff, kf, q, v, lfb, kb, s0]
    out_shape, out_specs = [], []
    if compute_o:
        out_shape += [jax.ShapeDtypeStruct((batch * seq_len, D_GROUP), BF16)] * 2
        out_specs += [blk(ofwd), blk(obwd)]
    out_shape.append(jax.ShapeDtypeStruct(s0.shape, F32))
    out_specs.append(sspec)
    kern = functools.partial(_scan_kernel, n_chunks=t // SCAN_CHUNK, compute_o=compute_o)
    return pl.pallas_call(
        kern,
        out_shape=out_shape,
        grid=(batch, nt),
        in_specs=in_specs,
        out_specs=out_specs,
        scratch_shapes=[pltpu.VMEM((2 * HEADS, HEAD_DIM, HEAD_DIM), F32)],
        compiler_params=_cparams(("arbitrary", "arbitrary"), VMEM_LIMIT),
        name="hgrn_scan",
    )(*args)


def _ln_silu(y, g, b):
    mu = jnp.mean(y, axis=-1, keepdims=True)
    yc = y - mu
    var = jnp.mean(yc * yc, axis=-1, keepdims=True)
    return _silu(yc * lax.rsqrt(var + EPS) * g + b)


def _conv_rows_kernel(u_ref, w_ref, b_ref, lg_ref, lb_ref, y_ref, pad_ref, acc_ref, *, seq, tb):
    nseq = tb // seq
    stride = seq + CONV_GAP
    pad_ref[...] = jnp.zeros(pad_ref.shape, F32)
    for s in range(nseq):
        pad_ref[CONV_GAP + s * stride:CONV_GAP + s * stride + seq, :] = (
            u_ref[s * seq:(s + 1) * seq, :].astype(F32))
    rc = 64
    for s in range(nseq):
        for r0 in range(0, seq, rc):
            for cl in range(D_GROUP // LANES):
                ls = slice(cl * LANES, (cl + 1) * LANES)
                acc = jnp.zeros((rc, LANES), F32)
                for k in range(CONV_K):
                    base = CONV_GAP + s * stride + r0 + k - CONV_HALF
                    acc = acc + w_ref[k:k + 1, ls] * pad_ref[base:base + rc, ls]
                acc_ref[s * seq + r0:s * seq + r0 + rc, ls] = acc
    y = acc_ref[...] + b_ref[...]
    y_ref[...] = _ln_silu(y, lg_ref[...], lb_ref[...]).astype(y_ref.dtype)


def _conv_rows(u, w, b, lg, lb, *, seq, row_off, n_rows):
    tb = max(seq, 512) if n_rows % max(seq, 512) == 0 else seq
    nseq = tb // seq
    off = row_off // tb
    vec = pl.BlockSpec((1, D_GROUP), lambda i: (0, 0))
    in_specs = [pl.BlockSpec((tb, D_GROUP), lambda i: (off + i, 0)),
                pl.BlockSpec((CONV_K, D_GROUP), lambda i: (0, 0)), vec, vec, vec]
    args = [u, w, b, lg, lb]
    kern = functools.partial(_conv_rows_kernel, seq=seq, tb=tb)
    return pl.pallas_call(
        kern,
        out_shape=jax.ShapeDtypeStruct((n_rows, D_GROUP), BF16),
        grid=(n_rows // tb,),
        in_specs=in_specs,
        out_specs=pl.BlockSpec((tb, D_GROUP), lambda i: (i, 0)),
        scratch_shapes=[pltpu.VMEM((CONV_GAP + nseq * (seq + CONV_GAP), D_GROUP), F32),
                        pltpu.VMEM((tb, D_GROUP), F32)],
        compiler_params=_cparams(("arbitrary",), VMEM_LIMIT),
        name="conv_rows",
    )(*args)


def _conv_cols_kernel(u_ref, w_ref, b_ref, lg_ref, lb_ref, y_ref, uf_ref):
    n_rows = u_ref.shape[1]
    uf_ref[...] = u_ref[0].astype(F32)
    bias = b_ref[...]
    lg = lg_ref[...]
    lb = lb_ref[...]
    for r in range(n_rows):
        acc = None
        for k in range(CONV_K):
            rr = r + k - CONV_HALF
            if 0 <= rr < n_rows:
                term = w_ref[k:k + 1, :] * uf_ref[rr]
                acc = term if acc is None else acc + term
        y_ref[0, r] = _ln_silu(acc + bias, lg, lb).astype(y_ref.dtype)


def _conv_cols(u, w, b, lg, lb, *, batch, seq_len):
    n_rows = seq_len // GRID_W
    wt = 16
    n_tok = batch * seq_len
    u4 = u[:n_tok].reshape(batch, n_rows, GRID_W, D_GROUP)
    vec = pl.BlockSpec((1, D_GROUP), lambda bi, wi: (0, 0))
    blk = pl.BlockSpec((1, n_rows, wt, D_GROUP), lambda bi, wi: (bi, 0, wi, 0))
    y4 = pl.pallas_call(
        _conv_cols_kernel,
        out_shape=jax.ShapeDtypeStruct((batch, n_rows, GRID_W, D_GROUP), BF16),
        grid=(batch, GRID_W // wt),
        in_specs=[blk, pl.BlockSpec((CONV_K, D_GROUP), lambda bi, wi: (0, 0)), vec, vec, vec],
        out_specs=blk,
        scratch_shapes=[pltpu.VMEM((n_rows, wt, D_GROUP), F32)],
        compiler_params=_cparams(("arbitrary", "arbitrary"), VMEM_LIMIT),
        name="conv_cols",
    )(u4, w, b, lg, lb)
    return y4.reshape(n_tok, D_GROUP)


def _outproj_kernel(*refs, n_lat_tiles, has_ctx):
    x_ref, ycx_ref, of_ref, ob_ref = refs[:4]
    og_ref, hg_ref, w_ref, g1_ref, o_ref = refs[-5:]
    ycx, of, ob = ycx_ref[...], of_ref[...], ob_ref[...]
    if has_ctx:
        is_ctx = pl.program_id(0) >= n_lat_tiles
        ycx, of, ob = (jnp.where(is_ctx, c_ref[...], a) for c_ref, a in zip(refs[4:7], (ycx, of, ob)))
    o = of.astype(F32) + ob.astype(F32)
    og = og_ref[...].astype(F32)
    hg = hg_ref[...]
    acc = jnp.dot(ycx, w_ref[0:D_GROUP, :], preferred_element_type=F32)
    for h in range(HEADS):
        hs = slice(h * HEAD_DIM, (h + 1) * HEAD_DIM)
        oh = o[:, hs]
        r = lax.rsqrt(jnp.mean(oh * oh, axis=-1, keepdims=True) + EPS)
        yh = ((oh * r * hg) * og[:, hs]).astype(BF16)
        acc = acc + jnp.dot(yh, w_ref[D_GROUP + h * HEAD_DIM:D_GROUP + (h + 1) * HEAD_DIM, :],
                            preferred_element_type=F32)
    o_ref[...] = x_ref[...] + g1_ref[0] * acc


def _outproj(xc, lat, ctx, og, hg, w_bf, g1, *, n_rows, tm, tiles_per_seq):
    d = xc.shape[1]
    nb = g1.shape[0] - 1
    n_lat_tiles = lat[0].shape[0] // tm
    row = lambda width: pl.BlockSpec((tm, width), lambda i: (i, 0))
    lat_spec = pl.BlockSpec((tm, D_GROUP), lambda i: (jnp.minimum(i, n_lat_tiles - 1), 0))
    ctx_spec = pl.BlockSpec((tm, D_GROUP), lambda i: (jnp.maximum(i - n_lat_tiles, 0), 0))
    mixers = list(lat) + (list(ctx) if ctx is not None else [])
    mixer_specs = [lat_spec] * 3 + ([ctx_spec] * 3 if ctx is not None else [])
    kern = functools.partial(_outproj_kernel, n_lat_tiles=n_lat_tiles, has_ctx=ctx is not None)
    return pl.pallas_call(
        kern,
        out_shape=jax.ShapeDtypeStruct((n_rows, d), F32),
        grid=(n_rows // tm,),
        in_specs=[row(d)] + mixer_specs + [row(D_GROUP),
                  pl.BlockSpec((1, HEAD_DIM), lambda i: (0, 0)),
                  pl.BlockSpec(w_bf.shape, lambda i: (0, 0)),
                  pl.BlockSpec((1, 1, d), lambda i: (jnp.minimum(i // tiles_per_seq, nb), 0, 0))],
        out_specs=row(d),
        compiler_params=_cparams(("arbitrary",), VMEM_LIMIT),
        name="outproj",
    )(xc, *mixers, og, hg, w_bf, g1)


def _router_kernel(x_ref, g_ref, sh_ref, sc_ref, rw_ref, rb_ref,
                   h_ref, e_ref, rank_ref, wcol_ref, cnt_ref, carry_ref):
    tm = x_ref.shape[0]
    i = pl.program_id(0)

    @pl.when(i == 0)
    def _():
        carry_ref[...] = jnp.zeros(carry_ref.shape, F32)

    h = _modulate(x_ref[...], g_ref[...], sh_ref[0], sc_ref[0])
    h_ref[...] = h
    logits = lax.dot_general(rw_ref[...], h, (((1,), (1,)), ((), ())),
                             precision=lax.Precision.HIGHEST, preferred_element_type=F32)
    s = _sigmoid(logits)
    sb = s + rb_ref[...]
    s_rows = [s[e:e + 1, :] for e in range(N_EXPERTS)]
    sb_rows = [sb[e:e + 1, :] for e in range(N_EXPERTS)]

    def group_score(g):
        v = sb_rows[g * EXPERTS_PER_GROUP:(g + 1) * EXPERTS_PER_GROUP]
        best = None
        for a in range(EXPERTS_PER_GROUP):
            for b in range(a + 1, EXPERTS_PER_GROUP):
                p = v[a] + v[b]
                best = p if best is None else jnp.maximum(best, p)
        return best

    cur = group_score(0)
    best_g = jnp.zeros(cur.shape, jnp.int32)
    for g in range(1, N_GROUPS):
        gs = group_score(g)
        upd = gs > cur
        best_g = jnp.where(upd, g, best_g)
        cur = jnp.where(upd, gs, cur)

    def pick(rows, jdx):
        out = rows[jdx]
        for g in range(1, N_GROUPS):
            out = jnp.where(best_g == g, rows[g * EXPERTS_PER_GROUP + jdx], out)
        return out

    vb = [pick(sb_rows, jdx) for jdx in range(EXPERTS_PER_GROUP)]
    vs = [pick(s_rows, jdx) for jdx in range(EXPERTS_PER_GROUP)]

    def first_argmax(vals):
        m = vals[0]
        for v in vals[1:]:
            m = jnp.maximum(m, v)
        idx = jnp.full(m.shape, EXPERTS_PER_GROUP - 1, jnp.int32)
        for jdx in range(EXPERTS_PER_GROUP - 2, -1, -1):
            idx = jnp.where(vals[jdx] == m, jdx, idx)
        return idx

    def take(vals, idx):
        out = vals[EXPERTS_PER_GROUP - 1]
        for jdx in range(EXPERTS_PER_GROUP - 2, -1, -1):
            out = jnp.where(idx == jdx, vals[jdx], out)
        return out

    i0 = first_argmax(vb)
    vb2 = [jnp.where(i0 == jdx, -jnp.inf, vb[jdx]) for jdx in range(EXPERTS_PER_GROUP)]
    i1 = first_argmax(vb2)
    s0 = take(vs, i0)
    s1 = take(vs, i1)
    tot = s0 + s1
    w0 = s0 / tot
    w1 = s1 / tot
    e0 = best_g * EXPERTS_PER_GROUP + i0
    e1 = best_g * EXPERTS_PER_GROUP + i1
    e_ref[0:1, :] = e0
    e_ref[1:2, :] = e1

    eid = lax.broadcasted_iota(jnp.int32, (N_EXPERTS, tm), 0)
    hit0 = eid == e0
    hit1 = eid == e1
    onehot = jnp.where(jnp.logical_or(hit0, hit1), 1.0, 0.0)
    ti = lax.broadcasted_iota(jnp.int32, (tm, tm), 0)
    tj = lax.broadcasted_iota(jnp.int32, (tm, tm), 1)
    before = jnp.where(ti < tj, 1.0, 0.0).astype(BF16)
    rank = jnp.dot(onehot.astype(BF16), before, preferred_element_type=F32) + carry_ref[:, 0:1]
    rank_ref[0:1, :] = jnp.sum(jnp.where(hit0, rank, 0.0), axis=0, keepdims=True).astype(jnp.int32)
    rank_ref[1:2, :] = jnp.sum(jnp.where(hit1, rank, 0.0), axis=0, keepdims=True).astype(jnp.int32)
    carry_ref[...] = carry_ref[...] + jnp.sum(onehot, axis=1, keepdims=True)
    cnt_ref[...] = carry_ref[...].astype(jnp.int32)

    wi = lax.broadcasted_iota(jnp.int32, (LANES, tm), 0)
    wrows = jnp.where(wi == 0, w0, jnp.where(wi == 1, w1, 0.0))
    wcol_ref[...] = wrows.T


def _router(xc, g, sh, sc, rw_t, rb, *, n_rows, tm, tiles_per_seq):
    d = xc.shape[1]
    nb = sh.shape[0] - 1
    mod_spec = pl.BlockSpec((1, 1, d), lambda i: (jnp.minimum(i // tiles_per_seq, nb), 0, 0))
    return pl.pallas_call(
        _router_kernel,
        out_shape=[jax.ShapeDtypeStruct((n_rows, d), F32),
                   jax.ShapeDtypeStruct((2, n_rows), jnp.int32),
                   jax.ShapeDtypeStruct((2, n_rows), jnp.int32),
                   jax.ShapeDtypeStruct((n_rows, LANES), F32),
                   jax.ShapeDtypeStruct((N_EXPERTS, LANES), jnp.int32)],
        grid=(n_rows // tm,),
        in_specs=[pl.BlockSpec((tm, d), lambda i: (i, 0)),
                  pl.BlockSpec((1, d), lambda i: (0, 0)),
                  mod_spec, mod_spec,
                  pl.BlockSpec((N_EXPERTS, d), lambda i: (0, 0)),
                  pl.BlockSpec((N_EXPERTS, 1), lambda i: (0, 0))],
        out_specs=[pl.BlockSpec((tm, d), lambda i: (i, 0)),
                   pl.BlockSpec((2, tm), lambda i: (0, i)),
                   pl.BlockSpec((2, tm), lambda i: (0, i)),
                   pl.BlockSpec((tm, LANES), lambda i: (i, 0)),
                   pl.BlockSpec((N_EXPERTS, LANES), lambda i: (0, 0))],
        scratch_shapes=[pltpu.VMEM((N_EXPERTS, LANES), F32)],
        compiler_params=_cparams(("arbitrary",), VMEM_LIMIT),
        name="router",
    )(xc, g, sh, sc, rw_t, rb)


def _row_copy(src_ref, src_row, dst_ref, dst_row, sem):
    return pltpu.make_async_copy(src_ref.at[pl.ds(src_row, 1), :],
                                 dst_ref.at[pl.ds(dst_row, 1), :], sem)


def _dispatch_kernel(dest_ref, h_ref, xs_in_ref, xs_ref, sem):
    del xs_in_ref
    tm = h_ref.shape[0]
    n = dest_ref.shape[0] // 2
    base = pl.program_id(0) * tm

    def issue(r, carry):
        for k in range(2):
            _row_copy(h_ref, r, xs_ref, dest_ref[k * n + base + r], sem).start()
        return carry

    lax.fori_loop(0, tm, issue, 0)

    def drain(r, carry):
        for k in range(2):
            _row_copy(h_ref, r, xs_ref, dest_ref[k * n + base + r], sem).wait()
        return carry

    lax.fori_loop(0, tm, drain, 0)


def _dispatch(dest, h, n_sorted, tm):
    n, d = h.shape
    xs0 = jnp.zeros((n_sorted, d), F32)
    grid_spec = pltpu.PrefetchScalarGridSpec(
        num_scalar_prefetch=1,
        grid=(n // tm,),
        in_specs=[pl.BlockSpec((tm, d), lambda i, dest_ref: (i, 0)),
                  pl.BlockSpec(memory_space=pl.ANY)],
        out_specs=pl.BlockSpec(memory_space=pl.ANY),
        scratch_shapes=[pltpu.SemaphoreType.DMA],
    )
    return pl.pallas_call(
        _dispatch_kernel,
        out_shape=jax.ShapeDtypeStruct((n_sorted, d), F32),
        grid_spec=grid_spec,
        input_output_aliases={2: 0},
        compiler_params=_cparams(("arbitrary",), VMEM_LIMIT),
        name="dispatch",
    )(dest, h, xs0)


def _moe_kernel(te_ref, nu_ref, xs_ref, wg_ref, wu_ref, wd_ref, ys_ref, wgb_ref, wub_ref, wdb_ref):
    i = pl.program_id(0)
    e = te_ref[i]
    prev = te_ref[jnp.maximum(i - 1, 0)]

    @pl.when(jnp.logical_or(i == 0, e != prev))
    def _():
        wgb_ref[...] = wg_ref[0].astype(BF16)
        wub_ref[...] = wu_ref[0].astype(BF16)
        wdb_ref[...] = wd_ref[0].astype(BF16)

    @pl.when(i < nu_ref[0])
    def _():
        x = xs_ref[...].astype(BF16)
        gate = jnp.dot(x, wgb_ref[...], preferred_element_type=F32)
        up = jnp.dot(x, wub_ref[...], preferred_element_type=F32)
        act = (_silu(gate) * up).astype(BF16)
        ys_ref[...] = jnp.dot(act, wdb_ref[...], preferred_element_type=F32)

    @pl.when(i >= nu_ref[0])
    def _():
        ys_ref[...] = jnp.zeros(ys_ref.shape, F32)


def _moe(tile_expert, n_used, xs, wg, wu, wd, tm):
    n_sorted, d = xs.shape
    de = wg.shape[-1]
    wmap = lambda i, te, nu: (te[i], 0, 0)
    grid_spec = pltpu.PrefetchScalarGridSpec(
        num_scalar_prefetch=2,
        grid=(n_sorted // tm,),
        in_specs=[pl.BlockSpec((tm, d), lambda i, te, nu: (i, 0)),
                  pl.BlockSpec((1, d, de), wmap),
                  pl.BlockSpec((1, d, de), wmap),
                  pl.BlockSpec((1, de, d), wmap)],
        out_specs=pl.BlockSpec((tm, d), lambda i, te, nu: (i, 0)),
        scratch_shapes=[pltpu.VMEM((d, de), BF16), pltpu.VMEM((d, de), BF16),
                        pltpu.VMEM((de, d), BF16)],
    )
    return pl.pallas_call(
        _moe_kernel,
        out_shape=jax.ShapeDtypeStruct((n_sorted, d), F32),
        grid_spec=grid_spec,
        compiler_params=_cparams(("arbitrary",), VMEM_LIMIT),
        name="moe_experts",
    )(tile_expert, n_used, xs, wg, wu, wd)


def _combine_kernel(dest_ref, x_ref, g2_ref, wcol_ref, fg_ref, ys_ref, o_ref, y0_ref, y1_ref, sem,
                    *, final_norm):
    tm = x_ref.shape[0]
    n = dest_ref.shape[0] // 2
    base = pl.program_id(0) * tm
    bufs = (y0_ref, y1_ref)

    def issue(r, carry):
        for k in range(2):
            _row_copy(ys_ref, dest_ref[k * n + base + r], bufs[k], r, sem).start()
        return carry

    lax.fori_loop(0, tm, issue, 0)

    def drain(r, carry):
        for k in range(2):
            _row_copy(ys_ref, dest_ref[k * n + base + r], bufs[k], r, sem).wait()
        return carry

    lax.fori_loop(0, tm, drain, 0)
    w = wcol_ref[...]
    out = w[:, 0:1] * y0_ref[...] + w[:, 1:2] * y1_ref[...]
    x = x_ref[...] + g2_ref[0] * out
    if final_norm:
        x = x * lax.rsqrt(jnp.mean(x * x, axis=-1, keepdims=True) + EPS) * fg_ref[...]
    o_ref[...] = x


def _combine(dest, xc, g2, wcol, fg, ys, *, n_rows, tm, tiles_per_seq, final_norm):
    d = xc.shape[1]
    nb = g2.shape[0] - 1
    grid_spec = pltpu.PrefetchScalarGridSpec(
        num_scalar_prefetch=1,
        grid=(n_rows // tm,),
        in_specs=[pl.BlockSpec((tm, d), lambda i, dr: (i, 0)),
                  pl.BlockSpec((1, 1, d), lambda i, dr: (jnp.minimum(i // tiles_per_seq, nb), 0, 0)),
                  pl.BlockSpec((tm, LANES), lambda i, dr: (i, 0)),
                  pl.BlockSpec((1, d), lambda i, dr: (0, 0)),
                  pl.BlockSpec(memory_space=pl.ANY)],
        out_specs=pl.BlockSpec((tm, d), lambda i, dr: (i, 0)),
        scratch_shapes=[pltpu.VMEM((tm, d), F32), pltpu.VMEM((tm, d), F32),
                        pltpu.SemaphoreType.DMA],
    )
    return pl.pallas_call(
        functools.partial(_combine_kernel, final_norm=final_norm),
        out_shape=jax.ShapeDtypeStruct((n_rows, d), F32),
        grid_spec=grid_spec,
        compiler_params=_cparams(("arbitrary",), VMEM_LIMIT),
        name="combine",
    )(dest, xc, g2, wcol, fg, ys)


def _lower_bounds(lb_param):
    p = jax.nn.softmax(lb_param.astype(F32), axis=0)
    return jnp.cumsum(p, axis=0) - p[0]


def _routing_tables(e01, rank01, counts, tm, n_tiles):
    cnt = counts[:, 0]
    padded = ((cnt + tm - 1) // tm) * tm
    ends = jnp.cumsum(padded)
    offs = ends - padded
    dest = (offs[e01] + rank01).reshape(-1).astype(jnp.int32)
    tile_start = jnp.arange(n_tiles, dtype=jnp.int32) * tm
    n_used = (ends[-1] // tm).astype(jnp.int32)
    te = jnp.sum((tile_start[:, None] >= ends[None, :]).astype(jnp.int32), axis=1)
    last = jnp.sum((jnp.maximum(ends[-1] - tm, 0) >= ends).astype(jnp.int32))
    te = jnp.where(tile_start < ends[-1], te, last).astype(jnp.int32)
    return dest, te, n_used.reshape(1)


def kernel(x, c, ctx, c_ctx, w_ada, b_ada, norm1_g, norm2_g, w_in, conv_w, conv_b, conv_ln_g,
           conv_ln_b, lb_fwd, lb_bwd, hgrn_norm_g, w_out, router_w, router_bias, w_gate, w_up,
           w_down, final_norm_g):
    bn, seq, d = x.shape
    ctx_len = ctx.shape[1]
    depth = w_ada.shape[0]
    n = bn * seq
    nc = bn * ctx_len
    tm = min(512, seq, nc)
    assert seq % tm == 0 and nc % tm == 0 and seq % GRID_W == 0
    tps = seq // tm

    pad_rows = (-(bn + 1)) % 8
    cc = jnp.concatenate([c, c_ctx[None, :], jnp.zeros((pad_rows, d), F32)], axis=0)
    mod = _ada(cc, w_ada, b_ada)[:, :bn + 1]
    mod = mod.reshape(depth, bn + 1, 6, 1, d)

    lbs_f = _lower_bounds(lb_fwd)
    lbs_b = _lower_bounds(lb_bwd)
    w_in_bf = w_in.astype(BF16)
    w_out_bf = w_out.astype(BF16)
    rw_t = router_w.T.astype(F32)
    rb = router_bias.reshape(N_EXPERTS, 1).astype(F32)
    hg = hgrn_norm_g.astype(F32)
    fg = final_norm_g.reshape(1, d).astype(F32)

    xc = jnp.concatenate([x.reshape(n, d), ctx.reshape(nc, d)], axis=0)
    s_zero = jnp.zeros((bn, 2 * HEADS, HEAD_DIM, HEAD_DIM), F32)

    for l in range(depth):
        last = l == depth - 1
        sh1, sc1, g1, sh2, sc2, g2 = (mod[l, :, k] for k in range(6))
        row = lambda a: a.reshape(1, -1).astype(F32)

        u, q, iv, lff, kf, lfb, kb, og = _inproj(
            xc, row(norm1_g[l]), sh1, sc1, w_in_bf[l], row(lbs_f[l]), row(lbs_b[l]), tm, tps)

        cw, cb = conv_w[l].astype(F32), row(conv_b[l])
        clg, clb = row(conv_ln_g[l]), row(conv_ln_b[l])
        scan = functools.partial(_scan, q, iv, lff, kf, lfb, kb)
        mix_ctx = None
        if last:
            (s_ctx,) = scan(s_zero, row_off=n, seq_len=ctx_len, compute_o=False)
            n_rows = n
        else:
            of_c, ob_c, s_ctx = scan(s_zero, row_off=n, seq_len=ctx_len, compute_o=True)
            ycx_c = _conv_rows(u, cw, cb, clg, clb, seq=ctx_len, row_off=n, n_rows=nc)
            mix_ctx = (ycx_c, of_c, ob_c)
            n_rows = n + nc
        of, ob, _ = scan(s_ctx, row_off=0, seq_len=seq, compute_o=True)
        if l % 2 == 0:
            ycx = _conv_rows(u, cw, cb, clg, clb, seq=GRID_W, row_off=0, n_rows=n)
        else:
            ycx = _conv_cols(u, cw, cb, clg, clb, batch=bn, seq_len=seq)

        xc = _outproj(xc, (ycx, of, ob), mix_ctx, og, hg[l].reshape(1, HEAD_DIM), w_out_bf[l], g1,
                      n_rows=n_rows, tm=tm, tiles_per_seq=tps)

        h2, e01, rank01, wcol, counts = _router(
            xc, row(norm2_g[l]), sh2, sc2, rw_t, rb, n_rows=n_rows, tm=tm, tiles_per_seq=tps)
        n_tiles = (2 * n_rows) // tm + N_EXPERTS
        dest, te, n_used = _routing_tables(e01, rank01, counts, tm, n_tiles)
        xs = _dispatch(dest, h2, n_tiles * tm, tm)
        ys = _moe(te, n_used, xs, w_gate[l], w_up[l], w_down[l], tm)
        xc = _combine(dest, xc, g2, wcol, fg, ys, n_rows=n_rows, tm=tm, tiles_per_seq=tps,
                      final_norm=last)

    return xc[:n].reshape(bn, seq, d)
```

```python
import functools

import numpy as np
import jax
import jax.numpy as jnp
from jax import lax
from jax.experimental import pallas as pl
from jax.experimental.pallas import tpu as pltpu

F32 = jnp.float32
BF16 = jnp.bfloat16
I32 = jnp.int32

EPS = 1e-6
GRID_W = 64
HEADS = 4
HEAD_DIM = 128
D_GROUP = HEADS * HEAD_DIM
N_IN_GROUPS = 7
CONV_K = 31
CONV_HALF = CONV_K // 2
CONV_GAP = 16
N_EXPERTS = 16
N_GROUPS = 4
EXPERTS_PER_GROUP = N_EXPERTS // N_GROUPS
SCAN_CHUNK = 64
LANES = 128
SUBLANES = 8
VMEM_LIMIT = 56 * 1024 * 1024


def _cparams(sem, vmem=None):
    return pltpu.CompilerParams(dimension_semantics=sem, vmem_limit_bytes=vmem)


def _sigmoid(x):
    return jax.nn.sigmoid(x)


def _silu(x):
    return x * _sigmoid(x)


def _ada_kernel(c_ref, w_ref, b_ref, o_ref):
    a = _silu(c_ref[...]).astype(BF16)
    w = w_ref[0].astype(BF16)
    o_ref[0] = jnp.dot(a, w, preferred_element_type=F32) + b_ref[0]


def _ada(cc, w_ada, b_ada, tn=1536):
    depth, d, d6 = w_ada.shape
    rows = cc.shape[0]
    return pl.pallas_call(
        _ada_kernel,
        out_shape=jax.ShapeDtypeStruct((depth, rows, d6), F32),
        grid=(depth, d6 // tn),
        in_specs=[
            pl.BlockSpec((rows, d), lambda l, j: (0, 0)),
            pl.BlockSpec((1, d, tn), lambda l, j: (l, 0, j)),
            pl.BlockSpec((1, 1, tn), lambda l, j: (l, 0, j)),
        ],
        out_specs=pl.BlockSpec((1, rows, tn), lambda l, j: (l, 0, j)),
        compiler_params=_cparams(("arbitrary", "arbitrary"), VMEM_LIMIT),
        name="ada",
    )(cc, w_ada, b_ada.reshape(depth, 1, d6))


def _modulate(x, g, shift, scale):
    r = lax.rsqrt(jnp.mean(x * x, axis=-1, keepdims=True) + EPS)
    return (x * r * g) * (1.0 + scale) + shift


def _inproj_kernel(x_ref, g_ref, sh_ref, sc_ref, w_ref, lbf_ref, lbb_ref,
                   u_ref, q_ref, i_ref, lff_ref, kf_ref, lfb_ref, kb_ref, og_ref):
    hb = _modulate(x_ref[...], g_ref[...], sh_ref[0], sc_ref[0]).astype(BF16)

    def proj(j):
        return jnp.dot(hb, w_ref[:, j * D_GROUP:(j + 1) * D_GROUP], preferred_element_type=F32)

    u_ref[...] = (proj(0) * _sigmoid(proj(1))).astype(BF16)
    q_ref[...] = _silu(proj(2)).astype(BF16)
    i_ref[...] = proj(3).astype(BF16)
    for j, lb_ref, lf_ref, k_ref in ((4, lbf_ref, lff_ref, kf_ref), (5, lbb_ref, lfb_ref, kb_ref)):
        lb = lb_ref[...]
        f = lb + (1.0 - lb) * _sigmoid(proj(j))
        lf_ref[...] = jnp.log(f)
        k_ref[...] = (1.0 - f).astype(BF16)
    og_ref[...] = _silu(proj(6)).astype(BF16)


def _inproj(xc, g, sh, sc, w_bf, lbf, lbb, tm, tiles_per_seq):
    n, d = xc.shape
    nb = sh.shape[0] - 1
    mod_spec = pl.BlockSpec((1, 1, d), lambda i: (jnp.minimum(i // tiles_per_seq, nb), 0, 0))
    row_spec = pl.BlockSpec((tm, D_GROUP), lambda i: (i, 0))
    vec_spec = pl.BlockSpec((1, D_GROUP), lambda i: (0, 0))
    outs = [jax.ShapeDtypeStruct((n, D_GROUP), dt)
            for dt in (BF16, BF16, BF16, F32, BF16, F32, BF16, BF16)]
    return pl.pallas_call(
        _inproj_kernel,
        out_shape=outs,
        grid=(n // tm,),
        in_specs=[
            pl.BlockSpec((tm, d), lambda i: (i, 0)),
            pl.BlockSpec((1, d), lambda i: (0, 0)),
            mod_spec, mod_spec,
            pl.BlockSpec((d, N_IN_GROUPS * D_GROUP), lambda i: (0, 0)),
            vec_spec, vec_spec,
        ],
        out_specs=[row_spec] * 8,
        compiler_params=_cparams(("arbitrary",), VMEM_LIMIT),
        name="inproj",
    )(xc, g, sh, sc, w_bf, lbf, lbb)


def _scan_tables(c, backward):
    levels = int(np.log2(c))
    r_all = np.zeros((levels * c + 2 * c + 8, c), np.float32)
    for lev in range(levels):
        h = 1 << lev
        for r in range(c):
            bd = (r // (2 * h)) * 2 * h + h
            if not backward:
                if r >= bd:
                    r_all[lev * c + r, bd:r + 1] = 1.0
                else:
                    r_all[lev * c + r, r + 1:bd] = 1.0
            else:
                if r < bd:
                    r_all[lev * c + r, r:bd] = 1.0
                else:
                    r_all[lev * c + r, bd:r] = 1.0
    base = levels * c
    for r in range(c):
        if not backward:
            r_all[base + r, :r + 1] = 1.0
            r_all[base + c + r, r + 1:] = 1.0
        else:
            r_all[base + r, r:] = 1.0
            r_all[base + c + r, :r] = 1.0
    r_all[base + 2 * c:, :] = 1.0
    return r_all


def _scan_kernel(*refs, n_chunks, compute_o):
    c = SCAN_CHUNK
    levels = int(np.log2(c))
    (rf_ref, rb_ref, qf_ref, vf_ref, lff_ref, kf_ref,
     qb_ref, vb_ref, lfb_ref, kb_ref, s0_ref) = refs[:11]
    pos = 11
    if compute_o:
        of_ref, ob_ref, sfin_ref, st_ref = refs[pos:pos + 4]
    else:
        sfin_ref, st_ref = refs[pos:pos + 2]
        of_ref = ob_ref = None
    j = pl.program_id(1)

    @pl.when(j == 0)
    def _():
        st_ref[...] = s0_ref[0]

    row = lax.broadcasted_iota(jnp.int32, (c, 1), 0)
    ri = lax.broadcasted_iota(jnp.int32, (c, c), 0)
    ci = lax.broadcasted_iota(jnp.int32, (c, c), 1)
    upper = [(row & (2 * (1 << lev) - 1)) >= (1 << lev) for lev in range(levels)]
    same_parent = [(ri >> (lev + 1)) == (ci >> (lev + 1)) for lev in range(levels)]
    diag = ri == ci
    dn_t = (((1,), (1,)), ((), ()))
    dn_tl = (((0,), (0,)), ((), ()))

    def one_chunk(dirn, r_ref, q_ref, v_ref, lf_ref, k_ref, o_ref, r0):
        rows = pl.ds(r0, c)
        g = lf_ref[rows, :]
        g_hi = g.astype(BF16)
        g_lo = (g - g_hi.astype(F32)).astype(BF16)
        rmat = r_ref[...]
        e_all = jnp.exp(jnp.dot(rmat, g_hi, preferred_element_type=F32)
                        + jnp.dot(rmat, g_lo, preferred_element_type=F32))
        base = levels * c
        for h in range(HEADS):
            hs = slice(h * HEAD_DIM, (h + 1) * HEAD_DIM)
            qh = q_ref[rows, hs].astype(F32)
            kh = k_ref[rows, hs].astype(F32)
            vh = v_ref[rows, hs]
            st = st_ref[dirn * HEADS + h]
            if compute_o:
                att = jnp.where(diag, lax.dot_general(qh.astype(BF16), kh.astype(BF16), dn_t,
                                                      preferred_element_type=F32), 0.0)
                for lev in range(levels):
                    e_l = e_all[lev * c:(lev + 1) * c, hs]
                    q_side = upper[lev] if dirn == 0 else jnp.logical_not(upper[lev])
                    qt = jnp.where(q_side, qh * e_l, 0.0).astype(BF16)
                    kt = jnp.where(q_side, 0.0, kh * e_l).astype(BF16)
                    a_l = lax.dot_general(qt, kt, dn_t, preferred_element_type=F32)
                    att = att + jnp.where(same_parent[lev], a_l, 0.0)
                qi = (qh * e_all[base:base + c, hs]).astype(BF16)
                o = lax.dot_general(qi, st.astype(BF16), dn_t, preferred_element_type=F32)
                o = o + jnp.dot(att.astype(BF16), vh, preferred_element_type=F32)
                o_ref[rows, hs] = o.astype(o_ref.dtype)
            kd = (kh * e_all[base + c:base + 2 * c, hs]).astype(BF16)
            e_tot = e_all[base + 2 * c:base + 2 * c + 1, hs]
            st_ref[dirn * HEADS + h] = st * e_tot + lax.dot_general(
                vh, kd, dn_tl, preferred_element_type=F32)

    def body(ci_, carry):
        r0f = pl.multiple_of(ci_ * c, c)
        r0b = pl.multiple_of((n_chunks - 1 - ci_) * c, c)
        one_chunk(0, rf_ref, qf_ref, vf_ref, lff_ref, kf_ref, of_ref, r0f)
        one_chunk(1, rb_ref, qb_ref, vb_ref, lfb_ref, kb_ref, ob_ref, r0b)
        return carry

    lax.fori_loop(0, n_chunks, body, 0)

    @pl.when(j == pl.num_programs(1) - 1)
    def _():
        sfin_ref[0] = st_ref[...]


def _scan(q, v, lff, kf, lfb, kb, s0, *, row_off, seq_len, compute_o):
    batch = s0.shape[0]
    t = min(256, seq_len)
    nt = seq_len // t
    off = row_off // t
    fwd = lambda b, j: (off + b * nt + j, 0)
    bwd = lambda b, j: (off + b * nt + (nt - 1 - j), 0)
    ofwd = lambda b, j: (b * nt + j, 0)
    obwd = lambda b, j: (b * nt + (nt - 1 - j), 0)
    blk = lambda im: pl.BlockSpec((t, D_GROUP), im)
    rf = jnp.asarray(_scan_tables(SCAN_CHUNK, False), BF16)
    rb = jnp.asarray(_scan_tables(SCAN_CHUNK, True), BF16)
    rspec = pl.BlockSpec(rf.shape, lambda b, j: (0, 0))
    sspec = pl.BlockSpec((1, 2 * HEADS, HEAD_DIM, HEAD_DIM), lambda b, j: (b, 0, 0, 0))
    in_specs = [rspec, rspec, blk(fwd), blk(fwd), blk(fwd), blk(fwd),
                blk(bwd), blk(bwd), blk(bwd), blk(bwd), sspec]
    args = [rf, rb, q, v, lff, kf, q, v, lfb, kb, s0]
    out_shape, out_specs = [], []
    if compute_o:
        out_shape += [jax.ShapeDtypeStruct((batch * seq_len, D_GROUP), BF16)] * 2
        out_specs += [blk(ofwd), blk(obwd)]
    out_shape.append(jax.ShapeDtypeStruct(s0.shape, F32))
    out_specs.append(sspec)
    kern = functools.partial(_scan_kernel, n_chunks=t // SCAN_CHUNK, compute_o=compute_o)
    return pl.pallas_call(
        kern,
        out_shape=out_shape,
        grid=(batch, nt),
        in_specs=in_specs,
        out_specs=out_specs,
        scratch_shapes=[pltpu.VMEM((2 * HEADS, HEAD_DIM, HEAD_DIM), F32)],
        compiler_params=_cparams(("arbitrary", "arbitrary"), VMEM_LIMIT),
        name="hgrn_scan",
    )(*args)


def _ln_silu(y, g, b):
    mu = jnp.mean(y, axis=-1, keepdims=True)
    yc = y - mu
    var = jnp.mean(yc * yc, axis=-1, keepdims=True)
    return _silu(yc * lax.rsqrt(var + EPS) * g + b)


def _conv_rows_kernel(u_ref, w_ref, b_ref, lg_ref, lb_ref, y_ref, pad_ref, acc_ref, *, seq, tb):
    nseq = tb // seq
    stride = seq + CONV_GAP
    pad_ref[...] = jnp.zeros(pad_ref.shape, F32)
    for s in range(nseq):
        pad_ref[CONV_GAP + s * stride:CONV_GAP + s * stride + seq, :] = (
            u_ref[s * seq:(s + 1) * seq, :].astype(F32))
    rc = 64
    for s in range(nseq):
        for r0 in range(0, seq, rc):
            for cl in range(D_GROUP // LANES):
                ls = slice(cl * LANES, (cl + 1) * LANES)
                acc = jnp.zeros((rc, LANES), F32)
                for k in range(CONV_K):
                    base = CONV_GAP + s * stride + r0 + k - CONV_HALF
                    acc = acc + w_ref[k:k + 1, ls] * pad_ref[base:base + rc, ls]
                acc_ref[s * seq + r0:s * seq + r0 + rc, ls] = acc
    y = acc_ref[...] + b_ref[...]
    y_ref[...] = _ln_silu(y, lg_ref[...], lb_ref[...]).astype(y_ref.dtype)


def _conv_rows(u, w, b, lg, lb, *, seq, row_off, n_rows):
    tb = max(seq, 512) if n_rows % max(seq, 512) == 0 else seq
    nseq = tb // seq
    off = row_off // tb
    vec = pl.BlockSpec((1, D_GROUP), lambda i: (0, 0))
    in_specs = [pl.BlockSpec((tb, D_GROUP), lambda i: (off + i, 0)),
                pl.BlockSpec((CONV_K, D_GROUP), lambda i: (0, 0)), vec, vec, vec]
    args = [u, w, b, lg, lb]
    kern = functools.partial(_conv_rows_kernel, seq=seq, tb=tb)
    return pl.pallas_call(
        kern,
        out_shape=jax.ShapeDtypeStruct((n_rows, D_GROUP), BF16),
        grid=(n_rows // tb,),
        in_specs=in_specs,
        out_specs=pl.BlockSpec((tb, D_GROUP), lambda i: (i, 0)),
        scratch_shapes=[pltpu.VMEM((CONV_GAP + nseq * (seq + CONV_GAP), D_GROUP), F32),
                        pltpu.VMEM((tb, D_GROUP), F32)],
        compiler_params=_cparams(("arbitrary",), VMEM_LIMIT),
        name="conv_rows",
    )(*args)


CONV_COL_TILE = 16


def _conv_cols_kernel(u_ref, w_ref, b_ref, lg_ref, lb_ref, y_ref, uf_ref, *, n_rows):
    uf_ref[...] = u_ref[...].astype(F32)
    bias = b_ref[...]
    lg = lg_ref[...]
    lb = lb_ref[...]

    def col_tile(wi, carry):
        w0 = wi * CONV_COL_TILE
        for r in range(n_rows):
            acc = None
            for k in range(CONV_K):
                rr = r + k - CONV_HALF
                if 0 <= rr < n_rows:
                    src = pl.ds(pl.multiple_of(rr * GRID_W + w0, CONV_COL_TILE), CONV_COL_TILE)
                    term = w_ref[k:k + 1, :] * uf_ref[src, :]
                    acc = term if acc is None else acc + term
            dst = pl.ds(pl.multiple_of(r * GRID_W + w0, CONV_COL_TILE), CONV_COL_TILE)
            y_ref[dst, :] = _ln_silu(acc + bias, lg, lb).astype(y_ref.dtype)
        return carry

    lax.fori_loop(0, GRID_W // CONV_COL_TILE, col_tile, 0)


def _conv_cols(u, w, b, lg, lb, *, batch, seq_len):
    vec = pl.BlockSpec((1, D_GROUP), lambda bi: (0, 0))
    blk = pl.BlockSpec((seq_len, D_GROUP), lambda bi: (bi, 0))
    return pl.pallas_call(
        functools.partial(_conv_cols_kernel, n_rows=seq_len // GRID_W),
        out_shape=jax.ShapeDtypeStruct((batch * seq_len, D_GROUP), BF16),
        grid=(batch,),
        in_specs=[blk, pl.BlockSpec((CONV_K, D_GROUP), lambda bi: (0, 0)), vec, vec, vec],
        out_specs=blk,
        scratch_shapes=[pltpu.VMEM((seq_len, D_GROUP), F32)],
        compiler_params=_cparams(("arbitrary",), VMEM_LIMIT),
        name="conv_cols",
    )(u, w, b, lg, lb)


def _outproj_kernel(*refs, n_lat_tiles, has_ctx):
    x_ref, ycx_ref, of_ref, ob_ref = refs[:4]
    og_ref, hg_ref, w_ref, g1_ref, o_ref = refs[-5:]
    ycx, of, ob = ycx_ref[...], of_ref[...], ob_ref[...]
    if has_ctx:
        is_ctx = pl.program_id(0) >= n_lat_tiles
        ycx, of, ob = (jnp.where(is_ctx, c_ref[...], a) for c_ref, a in zip(refs[4:7], (ycx, of, ob)))
    o = of.astype(F32) + ob.astype(F32)
    og = og_ref[...].astype(F32)
    hg = hg_ref[...]
    acc = jnp.dot(ycx, w_ref[0:D_GROUP, :], preferred_element_type=F32)
    for h in range(HEADS):
        hs = slice(h * HEAD_DIM, (h + 1) * HEAD_DIM)
        oh = o[:, hs]
        r = lax.rsqrt(jnp.mean(oh * oh, axis=-1, keepdims=True) + EPS)
        yh = ((oh * r * hg) * og[:, hs]).astype(BF16)
        acc = acc + jnp.dot(yh, w_ref[D_GROUP + h * HEAD_DIM:D_GROUP + (h + 1) * HEAD_DIM, :],
                            preferred_element_type=F32)
    o_ref[...] = x_ref[...] + g1_ref[0] * acc


def _outproj(xc, lat, ctx, og, hg, w_bf, g1, *, n_rows, tm, tiles_per_seq):
    d = xc.shape[1]
    nb = g1.shape[0] - 1
    n_lat_tiles = lat[0].shape[0] // tm
    row = lambda width: pl.BlockSpec((tm, width), lambda i: (i, 0))
    lat_spec = pl.BlockSpec((tm, D_GROUP), lambda i: (jnp.minimum(i, n_lat_tiles - 1), 0))
    ctx_spec = pl.BlockSpec((tm, D_GROUP), lambda i: (jnp.maximum(i - n_lat_tiles, 0), 0))
    mixers = list(lat) + (list(ctx) if ctx is not None else [])
    mixer_specs = [lat_spec] * 3 + ([ctx_spec] * 3 if ctx is not None else [])
    kern = functools.partial(_outproj_kernel, n_lat_tiles=n_lat_tiles, has_ctx=ctx is not None)
    return pl.pallas_call(
        kern,
        out_shape=jax.ShapeDtypeStruct((n_rows, d), F32),
        grid=(n_rows // tm,),
        in_specs=[row(d)] + mixer_specs + [row(D_GROUP),
                  pl.BlockSpec((1, HEAD_DIM), lambda i: (0, 0)),
                  pl.BlockSpec(w_bf.shape, lambda i: (0, 0)),
                  pl.BlockSpec((1, 1, d), lambda i: (jnp.minimum(i // tiles_per_seq, nb), 0, 0))],
        out_specs=row(d),
        compiler_params=_cparams(("arbitrary",), VMEM_LIMIT),
        name="outproj",
    )(xc, *mixers, og, hg, w_bf, g1)


def _router_kernel(x_ref, g_ref, sh_ref, sc_ref, rw_ref, rb_ref,
                   h_ref, e_ref, rank_ref, w_ref, cnt_ref):
    tm = x_ref.shape[0]
    h = _modulate(x_ref[...], g_ref[...], sh_ref[0], sc_ref[0])
    h_ref[...] = h.astype(h_ref.dtype)
    logits = lax.dot_general(rw_ref[...], h, (((1,), (1,)), ((), ())),
                             precision=lax.Precision.HIGHEST, preferred_element_type=F32)
    s = _sigmoid(logits)
    sb = s + rb_ref[...]
    s_rows = [s[e:e + 1, :] for e in range(N_EXPERTS)]
    sb_rows = [sb[e:e + 1, :] for e in range(N_EXPERTS)]

    def group_score(g):
        v = sb_rows[g * EXPERTS_PER_GROUP:(g + 1) * EXPERTS_PER_GROUP]
        best = None
        for a in range(EXPERTS_PER_GROUP):
            for b in range(a + 1, EXPERTS_PER_GROUP):
                p = v[a] + v[b]
                best = p if best is None else jnp.maximum(best, p)
        return best

    cur = group_score(0)
    best_g = jnp.zeros(cur.shape, I32)
    for g in range(1, N_GROUPS):
        gs = group_score(g)
        upd = gs > cur
        best_g = jnp.where(upd, g, best_g)
        cur = jnp.where(upd, gs, cur)

    def pick(rows, jdx):
        out = rows[jdx]
        for g in range(1, N_GROUPS):
            out = jnp.where(best_g == g, rows[g * EXPERTS_PER_GROUP + jdx], out)
        return out

    vb = [pick(sb_rows, jdx) for jdx in range(EXPERTS_PER_GROUP)]
    vs = [pick(s_rows, jdx) for jdx in range(EXPERTS_PER_GROUP)]

    def first_argmax(vals):
        m = vals[0]
        for v in vals[1:]:
            m = jnp.maximum(m, v)
        idx = jnp.full(m.shape, EXPERTS_PER_GROUP - 1, I32)
        for jdx in range(EXPERTS_PER_GROUP - 2, -1, -1):
            idx = jnp.where(vals[jdx] == m, jdx, idx)
        return idx

    def take(vals, idx):
        out = vals[EXPERTS_PER_GROUP - 1]
        for jdx in range(EXPERTS_PER_GROUP - 2, -1, -1):
            out = jnp.where(idx == jdx, vals[jdx], out)
        return out

    i0 = first_argmax(vb)
    vb2 = [jnp.where(i0 == jdx, -jnp.inf, vb[jdx]) for jdx in range(EXPERTS_PER_GROUP)]
    i1 = first_argmax(vb2)
    s0 = take(vs, i0)
    s1 = take(vs, i1)
    tot = s0 + s1
    w_ref[0:1, :] = s0 / tot
    w_ref[1:2, :] = s1 / tot
    e0 = best_g * EXPERTS_PER_GROUP + i0
    e1 = best_g * EXPERTS_PER_GROUP + i1
    e_ref[0:1, :] = e0
    e_ref[1:2, :] = e1

    eid = lax.broadcasted_iota(I32, (N_EXPERTS, tm), 0)
    hit0 = eid == e0
    hit1 = eid == e1
    onehot = jnp.where(jnp.logical_or(hit0, hit1), 1.0, 0.0)
    ti = lax.broadcasted_iota(I32, (tm, tm), 0)
    tj = lax.broadcasted_iota(I32, (tm, tm), 1)
    before = jnp.where(ti < tj, 1.0, 0.0).astype(BF16)
    rank = jnp.dot(onehot.astype(BF16), before, preferred_element_type=F32)
    rank_ref[0:1, :] = jnp.sum(jnp.where(hit0, rank, 0.0), axis=0, keepdims=True).astype(I32)
    rank_ref[1:2, :] = jnp.sum(jnp.where(hit1, rank, 0.0), axis=0, keepdims=True).astype(I32)
    cnt = jnp.sum(onehot, axis=1, keepdims=True).astype(I32)
    cnt_ref[0] = jnp.broadcast_to(cnt, (N_EXPERTS, LANES))


def _router(xc, g, sh, sc, rw_t, rb, *, n_rows, tm, tiles_per_seq):
    d = xc.shape[1]
    nb = sh.shape[0] - 1
    n_tt = n_rows // tm
    mod_spec = pl.BlockSpec((1, 1, d), lambda i: (jnp.minimum(i // tiles_per_seq, nb), 0, 0))
    pair = pl.BlockSpec((2, tm), lambda i: (0, i))
    return pl.pallas_call(
        _router_kernel,
        out_shape=[jax.ShapeDtypeStruct((n_rows, d), BF16),
                   jax.ShapeDtypeStruct((2, n_rows), I32),
                   jax.ShapeDtypeStruct((2, n_rows), I32),
                   jax.ShapeDtypeStruct((2, n_rows), F32),
                   jax.ShapeDtypeStruct((n_tt, N_EXPERTS, LANES), I32)],
        grid=(n_tt,),
        in_specs=[pl.BlockSpec((tm, d), lambda i: (i, 0)),
                  pl.BlockSpec((1, d), lambda i: (0, 0)),
                  mod_spec, mod_spec,
                  pl.BlockSpec((N_EXPERTS, d), lambda i: (0, 0)),
                  pl.BlockSpec((N_EXPERTS, 1), lambda i: (0, 0))],
        out_specs=[pl.BlockSpec((tm, d), lambda i: (i, 0)), pair, pair, pair,
                   pl.BlockSpec((1, N_EXPERTS, LANES), lambda i: (i, 0, 0))],
        compiler_params=_cparams(("arbitrary",), VMEM_LIMIT),
        name="router",
    )(xc, g, sh, sc, rw_t, rb)


def _tables_kernel(cnt_ref, seg_start_ref, seg_len_ref, te_ref, misc_ref, *, n_tt, n_et, tm):
    def expert(e, row0):
        def tile(i, pos):
            n = cnt_ref[i * N_EXPERTS + e]
            n8 = ((n + SUBLANES - 1) // SUBLANES) * SUBLANES
            seg_start_ref[i * N_EXPERTS + e] = pos
            seg_len_ref[i * N_EXPERTS + e] = n8
            return pos + n8

        end = lax.fori_loop(0, n_tt, tile, row0)
        padded_end = row0 + ((end - row0 + tm - 1) // tm) * tm
        misc_ref[1 + e] = end
        misc_ref[1 + N_EXPERTS + e] = padded_end - end

        def mark(j, carry):
            te_ref[j] = e
            return carry

        lax.fori_loop(row0 // tm, padded_end // tm, mark, 0)
        return padded_end

    total = row0 = 0
    for e in range(N_EXPERTS):
        row0 = expert(e, row0)
    total = row0
    n_used = total // tm
    misc_ref[0] = n_used
    last = te_ref[jnp.maximum(n_used - 1, 0)]

    def fill(j, carry):
        te_ref[j] = last
        return carry

    lax.fori_loop(n_used, n_et, fill, 0)


def _tables(cnt, *, n_tt, n_et, tm):
    smem = pl.BlockSpec(memory_space=pltpu.SMEM)
    return pl.pallas_call(
        functools.partial(_tables_kernel, n_tt=n_tt, n_et=n_et, tm=tm),
        out_shape=[jax.ShapeDtypeStruct((n_tt * N_EXPERTS,), I32),
                   jax.ShapeDtypeStruct((n_tt * N_EXPERTS,), I32),
                   jax.ShapeDtypeStruct((n_et,), I32),
                   jax.ShapeDtypeStruct((1 + 2 * N_EXPERTS,), I32)],
        in_specs=[smem],
        out_specs=[smem, smem, smem, smem],
        name="route_tables",
    )(cnt)


def _piece_sizes(max_rows):
    sizes, s = [], SUBLANES
    while s <= max_rows:
        sizes.append(s)
        s *= 2
    return sizes[::-1]


def _segment_copies(src_ref, src_row, dst_ref, dst_row, n_rows, sem, sizes, start):
    aligned = lambda r: r if isinstance(r, int) else pl.multiple_of(r, SUBLANES)
    for sz in sizes:
        src = pl.ds(aligned(src_row), sz)
        dst = pl.ds(aligned(dst_row), sz)

        @pl.when((n_rows & sz) != 0)
        def _(src=src, dst=dst):
            cp = pltpu.make_async_copy(src_ref.at[src, :], dst_ref.at[dst, :], sem)
            if start:
                cp.start()
            else:
                cp.wait()

        src_row = src_row + (n_rows & sz)
        dst_row = dst_row + (n_rows & sz)


def _local_slots(seg_len_ref, tile, e_ref, rank_ref):
    offs, lo = [], 0
    for e in range(N_EXPERTS):
        offs.append(lo)
        lo = lo + seg_len_ref[tile * N_EXPERTS + e]
    e01 = e_ref[...]
    slot = rank_ref[...]
    for e in range(N_EXPERTS):
        slot = slot + jnp.where(e01 == e, offs[e], 0)
    return slot, offs


def _compact_rows(tm):
    return ((2 * tm + N_EXPERTS * (SUBLANES - 1) + LANES - 1) // LANES) * LANES


def _dispatch_kernel(seg_start_ref, seg_len_ref, misc_ref, h_ref, e_ref, rank_ref, xs_ref,
                     xc_ref, zero_ref, sem, *, sizes):
    tm = h_ref.shape[0]
    rc = xc_ref.shape[0]
    i = pl.program_id(0)
    slot, offs = _local_slots(seg_len_ref, i, e_ref, rank_ref)
    rid = lax.broadcasted_iota(I32, (rc, tm), 0)
    sel = jnp.logical_or(rid == slot[0:1, :], rid == slot[1:2, :])
    perm = jnp.where(sel, 1.0, 0.0).astype(BF16)
    xc_ref[...] = jnp.dot(perm, h_ref[...], preferred_element_type=F32)

    def segments(start):
        for e in range(N_EXPERTS):
            _segment_copies(xc_ref, offs[e], xs_ref, seg_start_ref[i * N_EXPERTS + e],
                            seg_len_ref[i * N_EXPERTS + e], sem, sizes, start)

    segments(True)
    segments(False)

    @pl.when(i == pl.num_programs(0) - 1)
    def _():
        zero_ref[...] = jnp.zeros(zero_ref.shape, F32)
        for start in (True, False):
            for e in range(N_EXPERTS):
                _segment_copies(zero_ref, 0, xs_ref, misc_ref[1 + e], misc_ref[1 + N_EXPERTS + e],
                                sem, sizes, start)

        def clear_tile(j, carry):
            cp = pltpu.make_async_copy(zero_ref, xs_ref.at[pl.ds(pl.multiple_of(j * tm, tm), tm), :], sem)
            cp.start()
            cp.wait()
            return carry

        lax.fori_loop(misc_ref[0], xs_ref.shape[0] // tm, clear_tile, 0)


def _dispatch(seg_start, seg_len, misc, h, e01, rank01, *, n_sorted, tm):
    n, d = h.shape
    pair = pl.BlockSpec((2, tm), lambda i, *_: (0, i))
    grid_spec = pltpu.PrefetchScalarGridSpec(
        num_scalar_prefetch=3,
        grid=(n // tm,),
        in_specs=[pl.BlockSpec((tm, d), lambda i, *_: (i, 0)), pair, pair],
        out_specs=pl.BlockSpec(memory_space=pl.ANY),
        scratch_shapes=[pltpu.VMEM((_compact_rows(tm), d), F32), pltpu.VMEM((tm, d), F32),
                        pltpu.SemaphoreType.DMA],
    )
    return pl.pallas_call(
        functools.partial(_dispatch_kernel, sizes=_piece_sizes(tm)),
        out_shape=jax.ShapeDtypeStruct((n_sorted, d), F32),
        grid_spec=grid_spec,
        compiler_params=_cparams(("arbitrary",), VMEM_LIMIT),
        name="dispatch",
    )(seg_start, seg_len, misc, h, e01, rank01)


def _moe_kernel(te_ref, misc_ref, xs_ref, wg_ref, wu_ref, wd_ref, ys_ref, wgb_ref, wub_ref, wdb_ref):
    i = pl.program_id(0)
    e = te_ref[i]
    prev = te_ref[jnp.maximum(i - 1, 0)]

    @pl.when(jnp.logical_or(i == 0, e != prev))
    def _():
        wgb_ref[...] = wg_ref[0].astype(BF16)
        wub_ref[...] = wu_ref[0].astype(BF16)
        wdb_ref[...] = wd_ref[0].astype(BF16)

    @pl.when(i < misc_ref[0])
    def _():
        x = xs_ref[...].astype(BF16)
        gate = jnp.dot(x, wgb_ref[...], preferred_element_type=F32)
        up = jnp.dot(x, wub_ref[...], preferred_element_type=F32)
        act = (_silu(gate) * up).astype(BF16)
        ys_ref[...] = jnp.dot(act, wdb_ref[...], preferred_element_type=F32)

    @pl.when(i >= misc_ref[0])
    def _():
        ys_ref[...] = jnp.zeros(ys_ref.shape, F32)


def _moe(tile_expert, misc, xs, wg, wu, wd, tm):
    n_sorted, d = xs.shape
    de = wg.shape[-1]
    wmap = lambda i, te, misc_ref: (te[i], 0, 0)
    xmap = lambda i, te, misc_ref: (jnp.minimum(i, jnp.maximum(misc_ref[0] - 1, 0)), 0)
    grid_spec = pltpu.PrefetchScalarGridSpec(
        num_scalar_prefetch=2,
        grid=(n_sorted // tm,),
        in_specs=[pl.BlockSpec((tm, d), xmap),
                  pl.BlockSpec((1, d, de), wmap),
                  pl.BlockSpec((1, d, de), wmap),
                  pl.BlockSpec((1, de, d), wmap)],
        out_specs=pl.BlockSpec((tm, d), lambda i, te, misc_ref: (i, 0)),
        scratch_shapes=[pltpu.VMEM((d, de), BF16), pltpu.VMEM((d, de), BF16),
                        pltpu.VMEM((de, d), BF16)],
    )
    return pl.pallas_call(
        _moe_kernel,
        out_shape=jax.ShapeDtypeStruct((n_sorted, d), F32),
        grid_spec=grid_spec,
        compiler_params=_cparams(("arbitrary",), VMEM_LIMIT),
        name="moe_experts",
    )(tile_expert, misc, xs, wg, wu, wd)


def _combine_kernel(seg_start_ref, seg_len_ref, x_ref, g2_ref, e_ref, rank_ref, w_ref, fg_ref, ys_ref,
                    o_ref, yc_ref, sem, *, sizes, final_norm):
    tm = x_ref.shape[0]
    rc = yc_ref.shape[0]
    i = pl.program_id(0)

    @pl.when(i == 0)
    def _():
        yc_ref[...] = jnp.zeros(yc_ref.shape, F32)

    slot, offs = _local_slots(seg_len_ref, i, e_ref, rank_ref)
    for start in (True, False):
        for e in range(N_EXPERTS):
            _segment_copies(ys_ref, seg_start_ref[i * N_EXPERTS + e], yc_ref, offs[e],
                            seg_len_ref[i * N_EXPERTS + e], sem, sizes, start)

    rid = lax.broadcasted_iota(I32, (rc, tm), 0)
    w = w_ref[...]
    pw = (jnp.where(rid == slot[0:1, :], w[0:1, :], 0.0)
          + jnp.where(rid == slot[1:2, :], w[1:2, :], 0.0))
    yc = yc_ref[...]
    p_hi = pw.astype(BF16)
    p_lo = (pw - p_hi.astype(F32)).astype(BF16)
    y_hi = yc.astype(BF16)
    y_lo = (yc - y_hi.astype(F32)).astype(BF16)
    dn = (((0,), (0,)), ((), ()))
    out = (lax.dot_general(p_hi, y_hi, dn, preferred_element_type=F32)
           + lax.dot_general(p_hi, y_lo, dn, preferred_element_type=F32)
           + lax.dot_general(p_lo, y_hi, dn, preferred_element_type=F32))
    x = x_ref[...] + g2_ref[0] * out
    if final_norm:
        x = x * lax.rsqrt(jnp.mean(x * x, axis=-1, keepdims=True) + EPS) * fg_ref[...]
    o_ref[...] = x


def _combine(seg_start, seg_len, xc, g2, e01, rank01, w01, fg, ys, *, n_rows, tm, tiles_per_seq,
             final_norm):
    d = xc.shape[1]
    nb = g2.shape[0] - 1
    pair = pl.BlockSpec((2, tm), lambda i, *_: (0, i))
    grid_spec = pltpu.PrefetchScalarGridSpec(
        num_scalar_prefetch=2,
        grid=(n_rows // tm,),
        in_specs=[pl.BlockSpec((tm, d), lambda i, *_: (i, 0)),
                  pl.BlockSpec((1, 1, d), lambda i, *_: (jnp.minimum(i // tiles_per_seq, nb), 0, 0)),
                  pair, pair, pair,
                  pl.BlockSpec((1, d), lambda i, *_: (0, 0)),
                  pl.BlockSpec(memory_space=pl.ANY)],
        out_specs=pl.BlockSpec((tm, d), lambda i, *_: (i, 0)),
        scratch_shapes=[pltpu.VMEM((_compact_rows(tm), d), F32), pltpu.SemaphoreType.DMA],
    )
    return pl.pallas_call(
        functools.partial(_combine_kernel, sizes=_piece_sizes(tm), final_norm=final_norm),
        out_shape=jax.ShapeDtypeStruct((n_rows, d), F32),
        grid_spec=grid_spec,
        compiler_params=_cparams(("arbitrary",), VMEM_LIMIT),
        name="combine",
    )(seg_start, seg_len, xc, g2, e01, rank01, w01, fg, ys)


def _lower_bounds(lb_param):
    p = jax.nn.softmax(lb_param.astype(F32), axis=0)
    return jnp.cumsum(p, axis=0) - p[0]


def kernel(x, c, ctx, c_ctx, w_ada, b_ada, norm1_g, norm2_g, w_in, conv_w, conv_b, conv_ln_g,
           conv_ln_b, lb_fwd, lb_bwd, hgrn_norm_g, w_out, router_w, router_bias, w_gate, w_up,
           w_down, final_norm_g):
    bn, seq, d = x.shape
    ctx_len = ctx.shape[1]
    depth = w_ada.shape[0]
    n = bn * seq
    nc = bn * ctx_len
    tm = min(512, seq, nc)
    assert seq % tm == 0 and nc % tm == 0 and seq % GRID_W == 0
    tps = seq // tm

    pad_rows = (-(bn + 1)) % 8
    cc = jnp.concatenate([c, c_ctx[None, :], jnp.zeros((pad_rows, d), F32)], axis=0)
    mod = _ada(cc, w_ada, b_ada)[:, :bn + 1]
    mod = mod.reshape(depth, bn + 1, 6, 1, d)

    lbs_f = _lower_bounds(lb_fwd)
    lbs_b = _lower_bounds(lb_bwd)
    w_in_bf = w_in.astype(BF16)
    w_out_bf = w_out.astype(BF16)
    rw_t = router_w.T.astype(F32)
    rb = router_bias.reshape(N_EXPERTS, 1).astype(F32)
    hg = hgrn_norm_g.astype(F32)
    fg = final_norm_g.reshape(1, d).astype(F32)

    xc = jnp.concatenate([x.reshape(n, d), ctx.reshape(nc, d)], axis=0)
    s_zero = jnp.zeros((bn, 2 * HEADS, HEAD_DIM, HEAD_DIM), F32)

    for l in range(depth):
        last = l == depth - 1
        sh1, sc1, g1, sh2, sc2, g2 = (mod[l, :, k] for k in range(6))
        row = lambda a: a.reshape(1, -1).astype(F32)

        u, q, iv, lff, kf, lfb, kb, og = _inproj(
            xc, row(norm1_g[l]), sh1, sc1, w_in_bf[l], row(lbs_f[l]), row(lbs_b[l]), tm, tps)

        cw, cb = conv_w[l].astype(F32), row(conv_b[l])
        clg, clb = row(conv_ln_g[l]), row(conv_ln_b[l])
        scan = functools.partial(_scan, q, iv, lff, kf, lfb, kb)
        mix_ctx = None
        if last:
            (s_ctx,) = scan(s_zero, row_off=n, seq_len=ctx_len, compute_o=False)
            n_rows = n
        else:
            of_c, ob_c, s_ctx = scan(s_zero, row_off=n, seq_len=ctx_len, compute_o=True)
            ycx_c = _conv_rows(u, cw, cb, clg, clb, seq=ctx_len, row_off=n, n_rows=nc)
            mix_ctx = (ycx_c, of_c, ob_c)
            n_rows = n + nc
        of, ob, _ = scan(s_ctx, row_off=0, seq_len=seq, compute_o=True)
        if l % 2 == 0:
            ycx = _conv_rows(u, cw, cb, clg, clb, seq=GRID_W, row_off=0, n_rows=n)
        else:
            ycx = _conv_cols(u, cw, cb, clg, clb, batch=bn, seq_len=seq)

        xc = _outproj(xc, (ycx, of, ob), mix_ctx, og, hg[l].reshape(1, HEAD_DIM), w_out_bf[l], g1,
                      n_rows=n_rows, tm=tm, tiles_per_seq=tps)

        h2, e01, rank01, w01, cnt = _router(
            xc, row(norm2_g[l]), sh2, sc2, rw_t, rb, n_rows=n_rows, tm=tm, tiles_per_seq=tps)
        n_tt = n_rows // tm
        n_et = (2 * n_rows + n_tt * N_EXPERTS * (SUBLANES - 1) + tm - 1) // tm + N_EXPERTS
        seg_start, seg_len, te, misc = _tables(cnt[:, :, 0].reshape(-1), n_tt=n_tt, n_et=n_et, tm=tm)
        xs = _dispatch(seg_start, seg_len, misc, h2, e01, rank01, n_sorted=n_et * tm, tm=tm)
        ys = _moe(te, misc, xs, w_gate[l], w_up[l], w_down[l], tm)
        xc = _combine(seg_start, seg_len, xc, g2, e01, rank01, w01, fg, ys, n_rows=n_rows, tm=tm,
                      tiles_per_seq=tps, final_norm=last)

    return xc[:n].reshape(bn, seq, d)
```

```python
import functools

import numpy as np
import jax
import jax.numpy as jnp
from jax import lax
from jax.experimental import pallas as pl
from jax.experimental.pallas import tpu as pltpu

F32 = jnp.float32
BF16 = jnp.bfloat16
I32 = jnp.int32

EPS = 1e-6
GRID_W = 64
HEADS = 4
HEAD_DIM = 128
D_GROUP = HEADS * HEAD_DIM
N_IN_GROUPS = 7
CONV_K = 31
CONV_HALF = CONV_K // 2
CONV_GAP = 16
N_EXPERTS = 16
N_GROUPS = 4
EXPERTS_PER_GROUP = N_EXPERTS // N_GROUPS
SCAN_CHUNK = 64
SCAN_FAST_LIMIT = 60.0
LANES = 128
SUBLANES = 8
VMEM_LIMIT = 56 * 1024 * 1024


def _cparams(sem, vmem=None):
    return pltpu.CompilerParams(dimension_semantics=sem, vmem_limit_bytes=vmem)


def _sigmoid(x):
    return jax.nn.sigmoid(x)


def _silu(x):
    return x * _sigmoid(x)


def _ada_kernel(c_ref, w_ref, b_ref, o_ref):
    a = _silu(c_ref[...]).astype(BF16)
    w = w_ref[0].astype(BF16)
    o_ref[0] = jnp.dot(a, w, preferred_element_type=F32) + b_ref[0]


def _ada(cc, w_ada, b_ada, tn=1536):
    depth, d, d6 = w_ada.shape
    rows = cc.shape[0]
    return pl.pallas_call(
        _ada_kernel,
        out_shape=jax.ShapeDtypeStruct((depth, rows, d6), F32),
        grid=(depth, d6 // tn),
        in_specs=[
            pl.BlockSpec((rows, d), lambda l, j: (0, 0)),
            pl.BlockSpec((1, d, tn), lambda l, j: (l, 0, j)),
            pl.BlockSpec((1, 1, tn), lambda l, j: (l, 0, j)),
        ],
        out_specs=pl.BlockSpec((1, rows, tn), lambda l, j: (l, 0, j)),
        compiler_params=_cparams(("arbitrary", "arbitrary"), VMEM_LIMIT),
        name="ada",
    )(cc, w_ada, b_ada.reshape(depth, 1, d6))


def _modulate(x, g, shift, scale):
    r = lax.rsqrt(jnp.mean(x * x, axis=-1, keepdims=True) + EPS)
    return (x * r * g) * (1.0 + scale) + shift


def _rows_from(refs, n_lat_tiles):
    if len(refs) == 1:
        return refs[0][...]
    return jnp.where(pl.program_id(0) >= n_lat_tiles, refs[1][...], refs[0][...])


def _row_specs(arrays, tm):
    width = arrays[0].shape[1]
    if len(arrays) == 1:
        return [pl.BlockSpec((tm, width), lambda i, *_: (i, 0))], 0
    n_lat_tiles = arrays[0].shape[0] // tm
    return [pl.BlockSpec((tm, width), lambda i, *_: (jnp.minimum(i, n_lat_tiles - 1), 0)),
            pl.BlockSpec((tm, width), lambda i, *_: (jnp.maximum(i - n_lat_tiles, 0), 0))], n_lat_tiles


def _inproj_kernel(*refs, n_src, n_lat_tiles):
    x = _rows_from(refs[:n_src], n_lat_tiles)
    (g_ref, sh_ref, sc_ref, w_ref, lbf_ref, lbb_ref,
     u_ref, q_ref, i_ref, lff_ref, kf_ref, lfb_ref, kb_ref, og_ref) = refs[n_src:]
    hb = _modulate(x, g_ref[...], sh_ref[0], sc_ref[0]).astype(BF16)

    def proj(j):
        return jnp.dot(hb, w_ref[0, :, j * D_GROUP:(j + 1) * D_GROUP], preferred_element_type=F32)

    u_ref[...] = (proj(0) * _sigmoid(proj(1))).astype(BF16)
    q_ref[...] = _silu(proj(2)).astype(BF16)
    i_ref[...] = proj(3).astype(BF16)
    for j, lb_ref, lf_ref, k_ref in ((4, lbf_ref, lff_ref, kf_ref), (5, lbb_ref, lfb_ref, kb_ref)):
        lb = lb_ref[...]
        f = lb + (1.0 - lb) * _sigmoid(proj(j))
        lf_ref[...] = jnp.log(f)
        k_ref[...] = (1.0 - f).astype(BF16)
    og_ref[...] = _silu(proj(6)).astype(BF16)


def _inproj(xs, g, sh, sc, w_bf, layer, lbf, lbb, tm, tiles_per_seq):
    n = sum(a.shape[0] for a in xs)
    d = xs[0].shape[1]
    x_specs, n_lat_tiles = _row_specs(xs, tm)
    nb = sh.shape[0] - 1
    mod_spec = pl.BlockSpec((1, 1, d), lambda i: (jnp.minimum(i // tiles_per_seq, nb), 0, 0))
    row_spec = pl.BlockSpec((tm, D_GROUP), lambda i: (i, 0))
    vec_spec = pl.BlockSpec((1, D_GROUP), lambda i: (0, 0))
    outs = [jax.ShapeDtypeStruct((n, D_GROUP), dt)
            for dt in (BF16, BF16, BF16, F32, BF16, F32, BF16, BF16)]
    return pl.pallas_call(
        functools.partial(_inproj_kernel, n_src=len(xs), n_lat_tiles=n_lat_tiles),
        out_shape=outs,
        grid=(n // tm,),
        in_specs=x_specs + [
            pl.BlockSpec((1, d), lambda i: (0, 0)),
            mod_spec, mod_spec,
            pl.BlockSpec((1, d, N_IN_GROUPS * D_GROUP), lambda i: (layer, 0, 0)),
            vec_spec, vec_spec,
        ],
        out_specs=[row_spec] * 8,
        compiler_params=_cparams(("arbitrary",), VMEM_LIMIT),
        name="inproj",
    )(*xs, g, sh, sc, w_bf, lbf, lbb)


def _scan_tables(c, backward):
    levels = int(np.log2(c))
    r_all = np.zeros((levels * c + 2 * c + 8, c), np.float32)
    for lev in range(levels):
        h = 1 << lev
        for r in range(c):
            bd = (r // (2 * h)) * 2 * h + h
            if not backward:
                if r >= bd:
                    r_all[lev * c + r, bd:r + 1] = 1.0
                else:
                    r_all[lev * c + r, r + 1:bd] = 1.0
            else:
                if r < bd:
                    r_all[lev * c + r, r:bd] = 1.0
                else:
                    r_all[lev * c + r, bd:r] = 1.0
    base = levels * c
    for r in range(c):
        if not backward:
            r_all[base + r, :r + 1] = 1.0
            r_all[base + c + r, r + 1:] = 1.0
        else:
            r_all[base + r, r:] = 1.0
            r_all[base + c + r, :r] = 1.0
    r_all[base + 2 * c:, :] = 1.0
    return r_all


def _scan_kernel(*refs, n_chunks, compute_o):
    c = SCAN_CHUNK
    levels = int(np.log2(c))
    (rf_ref, rb_ref, qf_ref, vf_ref, lff_ref, kf_ref,
     qb_ref, vb_ref, lfb_ref, kb_ref, s0_ref) = refs[:11]
    pos = 11
    if compute_o:
        of_ref, ob_ref, sfin_ref, st_ref = refs[pos:pos + 4]
    else:
        sfin_ref, st_ref = refs[pos:pos + 2]
        of_ref = ob_ref = None
    j = pl.program_id(1)

    @pl.when(j == 0)
    def _():
        st_ref[...] = s0_ref[0]

    row = lax.broadcasted_iota(jnp.int32, (c, 1), 0)
    ri = lax.broadcasted_iota(jnp.int32, (c, c), 0)
    ci = lax.broadcasted_iota(jnp.int32, (c, c), 1)
    upper = [(row & (2 * (1 << lev) - 1)) >= (1 << lev) for lev in range(levels)]
    same_parent = [(ri >> (lev + 1)) == (ci >> (lev + 1)) for lev in range(levels)]
    diag = ri == ci
    dn_t = (((1,), (1,)), ((), ()))
    dn_tl = (((0,), (0,)), ((), ()))

    half = c // 2
    tri = [jnp.where(ci <= ri, 1.0, 0.0).astype(BF16), jnp.where(ci >= ri, 1.0, 0.0).astype(BF16)]
    first_half = [row < half, row >= half]
    re = lax.broadcasted_iota(jnp.int32, (c, 2 * c), 0)
    ce = lax.broadcasted_iota(jnp.int32, (c, 2 * c), 1)
    cs = ce & (c - 1)
    same_half = (re >= half) == (cs >= half)
    own = ce < c
    mask_ext = [
        (own & same_half & (cs <= re)) | (jnp.logical_not(own) & (re >= half) & (cs < half)),
        (own & same_half & (cs >= re)) | (jnp.logical_not(own) & (re < half) & (cs >= half)),
    ]

    mrow, trow = [half - 1, half], [c - 1, 0]

    def prefix(tri_mat, g):
        g_hi = g.astype(BF16)
        r1 = g - g_hi.astype(F32)
        g_mid = r1.astype(BF16)
        g_lo = (r1 - g_mid.astype(F32)).astype(BF16)
        return (jnp.dot(tri_mat, g_hi, preferred_element_type=F32)
                + jnp.dot(tri_mat, g_mid, preferred_element_type=F32)
                + jnp.dot(tri_mat, g_lo, preferred_element_type=F32))

    def next_state(st, hs, kh, vh, b, tot):
        kd = (kh * jnp.exp(tot[:, hs] - b[:, hs])).astype(BF16)
        return st * jnp.exp(tot[:, hs]) + lax.dot_general(vh, kd, dn_tl, preferred_element_type=F32)

    def state_only_chunk(dirn, lf_ref, k_ref, v_ref, r0):
        rows = pl.ds(r0, c)
        b = prefix(tri[dirn], lf_ref[rows, :])
        tot = b[trow[dirn]:trow[dirn] + 1, :]
        for h in range(HEADS):
            hs = slice(h * HEAD_DIM, (h + 1) * HEAD_DIM)
            st_ref[dirn * HEADS + h] = next_state(st_ref[dirn * HEADS + h], hs,
                                                  k_ref[rows, hs].astype(F32), v_ref[rows, hs], b, tot)

    def fast_chunk(dirn, q_ref, v_ref, k_ref, o_ref, rows, b, m, tot, states):
        fh = first_half[dirn]
        cdec = b - jnp.where(fh, 0.0, m)
        e_own = jnp.exp(cdec)
        e_own_inv = jnp.exp(-cdec)
        e_cross = jnp.where(fh, jnp.exp(jnp.minimum(m - b, 0.0)), 0.0)
        e_inc = jnp.exp(b)
        for h in range(HEADS):
            hs = slice(h * HEAD_DIM, (h + 1) * HEAD_DIM)
            qh = q_ref[rows, hs].astype(F32)
            kh = k_ref[rows, hs].astype(F32)
            vh = v_ref[rows, hs]
            st = states[dirn * HEADS + h]
            q1 = (qh * e_own[:, hs]).astype(BF16)
            kcat = jnp.concatenate([(kh * e_own_inv[:, hs]).astype(BF16),
                                    (kh * e_cross[:, hs]).astype(BF16)], axis=0)
            sc = lax.dot_general(q1, kcat, dn_t, preferred_element_type=F32)
            p = jnp.where(mask_ext[dirn], sc, 0.0).astype(BF16)
            qi = (qh * e_inc[:, hs]).astype(BF16)
            o = (jnp.dot(p, jnp.concatenate([vh, vh], axis=0), preferred_element_type=F32)
                 + lax.dot_general(qi, st.astype(BF16), dn_t, preferred_element_type=F32))
            o_ref[rows, hs] = o.astype(o_ref.dtype)
            states[dirn * HEADS + h] = next_state(st, hs, kh, vh, b, tot)

    def robust_chunk(dirn, r_ref, q_ref, v_ref, lf_ref, k_ref, o_ref, r0):
        rows = pl.ds(r0, c)
        g = lf_ref[rows, :]
        g_hi = g.astype(BF16)
        g_lo = (g - g_hi.astype(F32)).astype(BF16)
        rmat = r_ref[...]
        e_all = jnp.exp(jnp.dot(rmat, g_hi, preferred_element_type=F32)
                        + jnp.dot(rmat, g_lo, preferred_element_type=F32))
        base = levels * c
        for h in range(HEADS):
            hs = slice(h * HEAD_DIM, (h + 1) * HEAD_DIM)
            qh = q_ref[rows, hs].astype(F32)
            kh = k_ref[rows, hs].astype(F32)
            vh = v_ref[rows, hs]
            st = st_ref[dirn * HEADS + h]
            if compute_o:
                att = jnp.where(diag, lax.dot_general(qh.astype(BF16), kh.astype(BF16), dn_t,
                                                      preferred_element_type=F32), 0.0)
                for lev in range(levels):
                    e_l = e_all[lev * c:(lev + 1) * c, hs]
                    q_side = upper[lev] if dirn == 0 else jnp.logical_not(upper[lev])
                    qt = jnp.where(q_side, qh * e_l, 0.0).astype(BF16)
                    kt = jnp.where(q_side, 0.0, kh * e_l).astype(BF16)
                    a_l = lax.dot_general(qt, kt, dn_t, preferred_element_type=F32)
                    att = att + jnp.where(same_parent[lev], a_l, 0.0)
                qi = (qh * e_all[base:base + c, hs]).astype(BF16)
                o = lax.dot_general(qi, st.astype(BF16), dn_t, preferred_element_type=F32)
                o = o + jnp.dot(att.astype(BF16), vh, preferred_element_type=F32)
                o_ref[rows, hs] = o.astype(o_ref.dtype)
            kd = (kh * e_all[base + c:base + 2 * c, hs]).astype(BF16)
            e_tot = e_all[base + 2 * c:base + 2 * c + 1, hs]
            st_ref[dirn * HEADS + h] = st * e_tot + lax.dot_general(
                vh, kd, dn_tl, preferred_element_type=F32)

    dir_refs = ((rf_ref, qf_ref, vf_ref, lff_ref, kf_ref, of_ref),
                (rb_ref, qb_ref, vb_ref, lfb_ref, kb_ref, ob_ref))

    def chunk_rows(ci_):
        return pl.multiple_of(ci_ * c, c), pl.multiple_of((n_chunks - 1 - ci_) * c, c)

    if not compute_o:
        def state_body(ci_, carry):
            for dirn, r0 in enumerate(chunk_rows(ci_)):
                _, _, v_ref, lf_ref, k_ref, _ = dir_refs[dirn]
                state_only_chunk(dirn, lf_ref, k_ref, v_ref, r0)
            return carry

        lax.fori_loop(0, n_chunks, state_body, 0)
    else:
        t = n_chunks * c
        rt = lax.broadcasted_iota(jnp.int32, (t, t), 0)
        ct = lax.broadcasted_iota(jnp.int32, (t, t), 1)
        same_chunk = (rt >> levels) == (ct >> levels)
        b_blk, stats, worst = [], {}, None
        for dirn in range(2):
            tri_blk = jnp.where(same_chunk & ((ct <= rt) if dirn == 0 else (ct >= rt)), 1.0, 0.0)
            b_blk.append(prefix(tri_blk.astype(BF16), dir_refs[dirn][3][...]))
            for ch in range(n_chunks):
                m = b_blk[dirn][ch * c + mrow[dirn]:ch * c + mrow[dirn] + 1, :]
                tot = b_blk[dirn][ch * c + trow[dirn]:ch * c + trow[dirn] + 1, :]
                stats[dirn, ch] = (m, tot)
                w = jnp.minimum(m, tot - m)
                worst = w if worst is None else jnp.minimum(worst, w)
        fast_ok = jnp.min(worst) >= -SCAN_FAST_LIMIT

        @pl.when(fast_ok)
        def _():
            states = [st_ref[i] for i in range(2 * HEADS)]
            for step in range(n_chunks):
                for dirn in range(2):
                    ch = step if dirn == 0 else n_chunks - 1 - step
                    _, q_ref, v_ref, _, k_ref, o_ref = dir_refs[dirn]
                    rows = slice(ch * c, (ch + 1) * c)
                    fast_chunk(dirn, q_ref, v_ref, k_ref, o_ref, rows, b_blk[dirn][rows, :],
                               *stats[dirn, ch], states)
            for i in range(2 * HEADS):
                st_ref[i] = states[i]

        @pl.when(jnp.logical_not(fast_ok))
        def _():
            def robust_body(ci_, carry):
                for dirn, r0 in enumerate(chunk_rows(ci_)):
                    r_ref, q_ref, v_ref, lf_ref, k_ref, o_ref = dir_refs[dirn]
                    robust_chunk(dirn, r_ref, q_ref, v_ref, lf_ref, k_ref, o_ref, r0)
                return carry

            lax.fori_loop(0, n_chunks, robust_body, 0)

    @pl.when(j == pl.num_programs(1) - 1)
    def _():
        sfin_ref[0] = st_ref[...]


def _scan(q, v, lff, kf, lfb, kb, s0, *, row_off, seq_len, compute_o):
    batch = s0.shape[0]
    t = min(256, seq_len)
    nt = seq_len // t
    off = row_off // t
    fwd = lambda b, j: (off + b * nt + j, 0)
    bwd = lambda b, j: (off + b * nt + (nt - 1 - j), 0)
    ofwd = lambda b, j: (b * nt + j, 0)
    obwd = lambda b, j: (b * nt + (nt - 1 - j), 0)
    blk = lambda im: pl.BlockSpec((t, D_GROUP), im)
    rf = jnp.asarray(_scan_tables(SCAN_CHUNK, False), BF16)
    rb = jnp.asarray(_scan_tables(SCAN_CHUNK, True), BF16)
    rspec = pl.BlockSpec(rf.shape, lambda b, j: (0, 0))
    sspec = pl.BlockSpec((1, 2 * HEADS, HEAD_DIM, HEAD_DIM), lambda b, j: (b, 0, 0, 0))
    in_specs = [rspec, rspec, blk(fwd), blk(fwd), blk(fwd), blk(fwd),
                blk(bwd), blk(bwd), blk(bwd), blk(bwd), sspec]
    args = [rf, rb, q, v, lff, kf, q, v, lfb, kb, s0]
    out_shape, out_specs = [], []
    if compute_o:
        out_shape += [jax.ShapeDtypeStruct((batch * seq_len, D_GROUP), BF16)] * 2
        out_specs += [blk(ofwd), blk(obwd)]
    out_shape.append(jax.ShapeDtypeStruct(s0.shape, F32))
    out_specs.append(sspec)
    kern = functools.partial(_scan_kernel, n_chunks=t // SCAN_CHUNK, compute_o=compute_o)
    return pl.pallas_call(
        kern,
        out_shape=out_shape,
        grid=(batch, nt),
        in_specs=in_specs,
        out_specs=out_specs,
        scratch_shapes=[pltpu.VMEM((2 * HEADS, HEAD_DIM, HEAD_DIM), F32)],
        compiler_params=_cparams(("arbitrary", "arbitrary"), VMEM_LIMIT),
        name="hgrn_scan",
    )(*args)


def _ln_silu(y, g, b):
    mu = jnp.mean(y, axis=-1, keepdims=True)
    yc = y - mu
    var = jnp.mean(yc * yc, axis=-1, keepdims=True)
    return _silu(yc * lax.rsqrt(var + EPS) * g + b)


def _conv_rows_kernel(u_ref, w_ref, b_ref, lg_ref, lb_ref, y_ref, pad_ref, acc_ref, *, seq, tb):
    nseq = tb // seq
    stride = seq + CONV_GAP
    pad_ref[...] = jnp.zeros(pad_ref.shape, F32)
    for s in range(nseq):
        pad_ref[CONV_GAP + s * stride:CONV_GAP + s * stride + seq, :] = (
            u_ref[s * seq:(s + 1) * seq, :].astype(F32))
    rc = 64
    for s in range(nseq):
        for r0 in range(0, seq, rc):
            for cl in range(D_GROUP // LANES):
                ls = slice(cl * LANES, (cl + 1) * LANES)
                acc = jnp.zeros((rc, LANES), F32)
                for k in range(CONV_K):
                    base = CONV_GAP + s * stride + r0 + k - CONV_HALF
                    acc = acc + w_ref[k:k + 1, ls] * pad_ref[base:base + rc, ls]
                acc_ref[s * seq + r0:s * seq + r0 + rc, ls] = acc
    y = acc_ref[...] + b_ref[...]
    y_ref[...] = _ln_silu(y, lg_ref[...], lb_ref[...]).astype(y_ref.dtype)


def _conv_rows(u, w, b, lg, lb, *, seq, row_off, n_rows):
    tb = max(seq, 512) if n_rows % max(seq, 512) == 0 else seq
    nseq = tb // seq
    off = row_off // tb
    vec = pl.BlockSpec((1, D_GROUP), lambda i: (0, 0))
    in_specs = [pl.BlockSpec((tb, D_GROUP), lambda i: (off + i, 0)),
                pl.BlockSpec((CONV_K, D_GROUP), lambda i: (0, 0)), vec, vec, vec]
    args = [u, w, b, lg, lb]
    kern = functools.partial(_conv_rows_kernel, seq=seq, tb=tb)
    return pl.pallas_call(
        kern,
        out_shape=jax.ShapeDtypeStruct((n_rows, D_GROUP), BF16),
        grid=(n_rows // tb,),
        in_specs=in_specs,
        out_specs=pl.BlockSpec((tb, D_GROUP), lambda i: (i, 0)),
        scratch_shapes=[pltpu.VMEM((CONV_GAP + nseq * (seq + CONV_GAP), D_GROUP), F32),
                        pltpu.VMEM((tb, D_GROUP), F32)],
        compiler_params=_cparams(("arbitrary",), VMEM_LIMIT),
        name="conv_rows",
    )(*args)


CONV_COL_TILE = 16


def _conv_cols_kernel(u_ref, w_ref, b_ref, lg_ref, lb_ref, y_ref, uf_ref, *, n_rows):
    uf_ref[...] = u_ref[...].astype(F32)
    bias = b_ref[...]
    lg = lg_ref[...]
    lb = lb_ref[...]

    def col_tile(wi, carry):
        w0 = wi * CONV_COL_TILE
        for r in range(n_rows):
            acc = None
            for k in range(CONV_K):
                rr = r + k - CONV_HALF
                if 0 <= rr < n_rows:
                    src = pl.ds(pl.multiple_of(rr * GRID_W + w0, CONV_COL_TILE), CONV_COL_TILE)
                    term = w_ref[k:k + 1, :] * uf_ref[src, :]
                    acc = term if acc is None else acc + term
            dst = pl.ds(pl.multiple_of(r * GRID_W + w0, CONV_COL_TILE), CONV_COL_TILE)
            y_ref[dst, :] = _ln_silu(acc + bias, lg, lb).astype(y_ref.dtype)
        return carry

    lax.fori_loop(0, GRID_W // CONV_COL_TILE, col_tile, 0)


def _conv_cols(u, w, b, lg, lb, *, batch, seq_len):
    vec = pl.BlockSpec((1, D_GROUP), lambda bi: (0, 0))
    blk = pl.BlockSpec((seq_len, D_GROUP), lambda bi: (bi, 0))
    return pl.pallas_call(
        functools.partial(_conv_cols_kernel, n_rows=seq_len // GRID_W),
        out_shape=jax.ShapeDtypeStruct((batch * seq_len, D_GROUP), BF16),
        grid=(batch,),
        in_specs=[blk, pl.BlockSpec((CONV_K, D_GROUP), lambda bi: (0, 0)), vec, vec, vec],
        out_specs=blk,
        scratch_shapes=[pltpu.VMEM((seq_len, D_GROUP), F32)],
        compiler_params=_cparams(("arbitrary",), VMEM_LIMIT),
        name="conv_cols",
    )(u, w, b, lg, lb)


def _outproj_kernel(*refs, n_x, n_mix, n_lat_tiles):
    x = _rows_from(refs[:n_x], n_lat_tiles)
    pos = n_x
    ycx, of, ob = (_rows_from(refs[pos + k * n_mix:pos + (k + 1) * n_mix], n_lat_tiles) for k in range(3))
    og_ref, hg_ref, w_ref, g1_ref, o_ref = refs[-5:]
    o = of.astype(F32) + ob.astype(F32)
    og = og_ref[...].astype(F32)
    hg = hg_ref[...]
    acc = jnp.dot(ycx, w_ref[0, 0:D_GROUP, :], preferred_element_type=F32)
    for h in range(HEADS):
        hs = slice(h * HEAD_DIM, (h + 1) * HEAD_DIM)
        oh = o[:, hs]
        r = lax.rsqrt(jnp.mean(oh * oh, axis=-1, keepdims=True) + EPS)
        yh = ((oh * r * hg) * og[:, hs]).astype(BF16)
        acc = acc + jnp.dot(yh, w_ref[0, D_GROUP + h * HEAD_DIM:D_GROUP + (h + 1) * HEAD_DIM, :],
                            preferred_element_type=F32)
    o_ref[...] = x + g1_ref[0] * acc


def _outproj(xs, mix, og, hg, w_bf, layer, g1, *, n_rows, tm, tiles_per_seq):
    d = xs[0].shape[1]
    nb = g1.shape[0] - 1
    x_specs, n_lat_tiles = _row_specs(xs, tm)
    mix_specs, mix_args = [], []
    for arrays in mix:
        specs, nl = _row_specs(arrays, tm)
        assert len(arrays) == len(mix[0]) and (nl == 0 or n_lat_tiles in (0, nl))
        n_lat_tiles = max(n_lat_tiles, nl)
        mix_specs += specs
        mix_args += list(arrays)
    row = lambda width: pl.BlockSpec((tm, width), lambda i: (i, 0))
    kern = functools.partial(_outproj_kernel, n_x=len(xs), n_mix=len(mix[0]), n_lat_tiles=n_lat_tiles)
    return pl.pallas_call(
        kern,
        out_shape=jax.ShapeDtypeStruct((n_rows, d), F32),
        grid=(n_rows // tm,),
        in_specs=x_specs + mix_specs + [row(D_GROUP),
                  pl.BlockSpec((1, HEAD_DIM), lambda i: (0, 0)),
                  pl.BlockSpec((1,) + w_bf.shape[1:], lambda i: (layer, 0, 0)),
                  pl.BlockSpec((1, 1, d), lambda i: (jnp.minimum(i // tiles_per_seq, nb), 0, 0))],
        out_specs=row(d),
        compiler_params=_cparams(("arbitrary",), VMEM_LIMIT),
        name="outproj",
    )(*xs, *mix_args, og, hg, w_bf, g1)


def _router_kernel(x_ref, g_ref, sh_ref, sc_ref, rw_ref, rb_ref,
                   h_ref, e_ref, rank_ref, w_ref, cnt_ref):
    tm = x_ref.shape[0]
    h = _modulate(x_ref[...], g_ref[...], sh_ref[0], sc_ref[0])
    h_ref[...] = h.astype(h_ref.dtype)
    logits = lax.dot_general(rw_ref[...], h, (((1,), (1,)), ((), ())),
                             precision=lax.Precision.HIGHEST, preferred_element_type=F32)
    s = _sigmoid(logits)
    sb = s + rb_ref[...]
    s_rows = [s[e:e + 1, :] for e in range(N_EXPERTS)]
    sb_rows = [sb[e:e + 1, :] for e in range(N_EXPERTS)]

    def group_score(g):
        v = sb_rows[g * EXPERTS_PER_GROUP:(g + 1) * EXPERTS_PER_GROUP]
        best = None
        for a in range(EXPERTS_PER_GROUP):
            for b in range(a + 1, EXPERTS_PER_GROUP):
                p = v[a] + v[b]
                best = p if best is None else jnp.maximum(best, p)
        return best

    cur = group_score(0)
    best_g = jnp.zeros(cur.shape, I32)
    for g in range(1, N_GROUPS):
        gs = group_score(g)
        upd = gs > cur
        best_g = jnp.where(upd, g, best_g)
        cur = jnp.where(upd, gs, cur)

    def pick(rows, jdx):
        out = rows[jdx]
        for g in range(1, N_GROUPS):
            out = jnp.where(best_g == g, rows[g * EXPERTS_PER_GROUP + jdx], out)
        return out

    vb = [pick(sb_rows, jdx) for jdx in range(EXPERTS_PER_GROUP)]
    vs = [pick(s_rows, jdx) for jdx in range(EXPERTS_PER_GROUP)]

    def first_argmax(vals):
        m = vals[0]
        for v in vals[1:]:
            m = jnp.maximum(m, v)
        idx = jnp.full(m.shape, EXPERTS_PER_GROUP - 1, I32)
        for jdx in range(EXPERTS_PER_GROUP - 2, -1, -1):
            idx = jnp.where(vals[jdx] == m, jdx, idx)
        return idx

    def take(vals, idx):
        out = vals[EXPERTS_PER_GROUP - 1]
        for jdx in range(EXPERTS_PER_GROUP - 2, -1, -1):
            out = jnp.where(idx == jdx, vals[jdx], out)
        return out

    i0 = first_argmax(vb)
    vb2 = [jnp.where(i0 == jdx, -jnp.inf, vb[jdx]) for jdx in range(EXPERTS_PER_GROUP)]
    i1 = first_argmax(vb2)
    s0 = take(vs, i0)
    s1 = take(vs, i1)
    tot = s0 + s1
    w_ref[0:1, :] = s0 / tot
    w_ref[1:2, :] = s1 / tot
    e0 = best_g * EXPERTS_PER_GROUP + i0
    e1 = best_g * EXPERTS_PER_GROUP + i1
    e_ref[0:1, :] = e0
    e_ref[1:2, :] = e1

    eid = lax.broadcasted_iota(I32, (N_EXPERTS, tm), 0)
    hit0 = eid == e0
    hit1 = eid == e1
    onehot = jnp.where(jnp.logical_or(hit0, hit1), 1.0, 0.0)
    ti = lax.broadcasted_iota(I32, (tm, tm), 0)
    tj = lax.broadcasted_iota(I32, (tm, tm), 1)
    before = jnp.where(ti < tj, 1.0, 0.0).astype(BF16)
    rank = jnp.dot(onehot.astype(BF16), before, preferred_element_type=F32)
    rank_ref[0:1, :] = jnp.sum(jnp.where(hit0, rank, 0.0), axis=0, keepdims=True).astype(I32)
    rank_ref[1:2, :] = jnp.sum(jnp.where(hit1, rank, 0.0), axis=0, keepdims=True).astype(I32)
    cnt = jnp.sum(onehot, axis=1, keepdims=True).astype(I32)
    cnt_ref[0] = jnp.broadcast_to(cnt, (N_EXPERTS, LANES))


def _router(xc, g, sh, sc, rw_t, rb, *, n_rows, tm, tiles_per_seq):
    d = xc.shape[1]
    nb = sh.shape[0] - 1
    n_tt = n_rows // tm
    mod_spec = pl.BlockSpec((1, 1, d), lambda i: (jnp.minimum(i // tiles_per_seq, nb), 0, 0))
    pair = pl.BlockSpec((2, tm), lambda i: (0, i))
    return pl.pallas_call(
        _router_kernel,
        out_shape=[jax.ShapeDtypeStruct((n_rows, d), BF16),
                   jax.ShapeDtypeStruct((2, n_rows), I32),
                   jax.ShapeDtypeStruct((2, n_rows), I32),
                   jax.ShapeDtypeStruct((2, n_rows), F32),
                   jax.ShapeDtypeStruct((n_tt, N_EXPERTS, LANES), I32)],
        grid=(n_tt,),
        in_specs=[pl.BlockSpec((tm, d), lambda i: (i, 0)),
                  pl.BlockSpec((1, d), lambda i: (0, 0)),
                  mod_spec, mod_spec,
                  pl.BlockSpec((N_EXPERTS, d), lambda i: (0, 0)),
                  pl.BlockSpec((N_EXPERTS, 1), lambda i: (0, 0))],
        out_specs=[pl.BlockSpec((tm, d), lambda i: (i, 0)), pair, pair, pair,
                   pl.BlockSpec((1, N_EXPERTS, LANES), lambda i: (i, 0, 0))],
        compiler_params=_cparams(("arbitrary",), VMEM_LIMIT),
        name="router",
    )(xc, g, sh, sc, rw_t, rb)


def _tables_kernel(cnt_ref, seg_start_ref, seg_len_ref, te_ref, misc_ref, *, n_tt, n_et, tm):
    def expert(e, row0):
        def tile(i, pos):
            n = cnt_ref[i * N_EXPERTS + e]
            n8 = ((n + SUBLANES - 1) // SUBLANES) * SUBLANES
            seg_start_ref[i * N_EXPERTS + e] = pos
            seg_len_ref[i * N_EXPERTS + e] = n8
            return pos + n8

        end = lax.fori_loop(0, n_tt, tile, row0)
        padded_end = row0 + ((end - row0 + tm - 1) // tm) * tm
        misc_ref[1 + e] = end
        misc_ref[1 + N_EXPERTS + e] = padded_end - end

        def mark(j, carry):
            te_ref[j] = e
            return carry

        lax.fori_loop(row0 // tm, padded_end // tm, mark, 0)
        return padded_end

    total = row0 = 0
    for e in range(N_EXPERTS):
        row0 = expert(e, row0)
    total = row0
    n_used = total // tm
    misc_ref[0] = n_used
    last = te_ref[jnp.maximum(n_used - 1, 0)]

    def fill(j, carry):
        te_ref[j] = last
        return carry

    lax.fori_loop(n_used, n_et, fill, 0)


def _tables(cnt, *, n_tt, n_et, tm):
    smem = pl.BlockSpec(memory_space=pltpu.SMEM)
    return pl.pallas_call(
        functools.partial(_tables_kernel, n_tt=n_tt, n_et=n_et, tm=tm),
        out_shape=[jax.ShapeDtypeStruct((n_tt * N_EXPERTS,), I32),
                   jax.ShapeDtypeStruct((n_tt * N_EXPERTS,), I32),
                   jax.ShapeDtypeStruct((n_et,), I32),
                   jax.ShapeDtypeStruct((1 + 2 * N_EXPERTS,), I32)],
        in_specs=[smem],
        out_specs=[smem, smem, smem, smem],
        name="route_tables",
    )(cnt)


def _piece_sizes(max_rows):
    sizes, s = [], SUBLANES
    while s <= max_rows:
        sizes.append(s)
        s *= 2
    return sizes[::-1]


def _segment_copies(src_ref, src_row, dst_ref, dst_row, n_rows, sem, sizes, start):
    aligned = lambda r: r if isinstance(r, int) else pl.multiple_of(r, SUBLANES)
    for sz in sizes:
        src = pl.ds(aligned(src_row), sz)
        dst = pl.ds(aligned(dst_row), sz)

        @pl.when((n_rows & sz) != 0)
        def _(src=src, dst=dst):
            cp = pltpu.make_async_copy(src_ref.at[src, :], dst_ref.at[dst, :], sem)
            if start:
                cp.start()
            else:
                cp.wait()

        src_row = src_row + (n_rows & sz)
        dst_row = dst_row + (n_rows & sz)


def _local_slots(seg_len_ref, tile, e_ref, rank_ref):
    offs, lo = [], 0
    for e in range(N_EXPERTS):
        offs.append(lo)
        lo = lo + seg_len_ref[tile * N_EXPERTS + e]
    e01 = e_ref[...]
    slot = rank_ref[...]
    for e in range(N_EXPERTS):
        slot = slot + jnp.where(e01 == e, offs[e], 0)
    return slot, offs


def _compact_rows(tm):
    return ((2 * tm + N_EXPERTS * (SUBLANES - 1) + LANES - 1) // LANES) * LANES


def _dispatch_kernel(seg_start_ref, seg_len_ref, misc_ref, h_ref, e_ref, rank_ref, xs_ref,
                     xc_ref, zero_ref, sem, *, sizes):
    tm = h_ref.shape[0]
    rc = xc_ref.shape[0]
    i = pl.program_id(0)
    slot, offs = _local_slots(seg_len_ref, i, e_ref, rank_ref)
    rid = lax.broadcasted_iota(I32, (rc, tm), 0)
    sel = jnp.logical_or(rid == slot[0:1, :], rid == slot[1:2, :])
    perm = jnp.where(sel, 1.0, 0.0).astype(BF16)
    xc_ref[...] = jnp.dot(perm, h_ref[...], preferred_element_type=F32)

    def segments(start):
        for e in range(N_EXPERTS):
            _segment_copies(xc_ref, offs[e], xs_ref, seg_start_ref[i * N_EXPERTS + e],
                            seg_len_ref[i * N_EXPERTS + e], sem, sizes, start)

    segments(True)
    segments(False)

    @pl.when(i == pl.num_programs(0) - 1)
    def _():
        zero_ref[...] = jnp.zeros(zero_ref.shape, F32)
        for start in (True, False):
            for e in range(N_EXPERTS):
                _segment_copies(zero_ref, 0, xs_ref, misc_ref[1 + e], misc_ref[1 + N_EXPERTS + e],
                                sem, sizes, start)

        def clear_tile(j, carry):
            cp = pltpu.make_async_copy(zero_ref, xs_ref.at[pl.ds(pl.multiple_of(j * tm, tm), tm), :], sem)
            cp.start()
            cp.wait()
            return carry

        lax.fori_loop(misc_ref[0], xs_ref.shape[0] // tm, clear_tile, 0)


def _dispatch(seg_start, seg_len, misc, h, e01, rank01, *, n_sorted, tm):
    n, d = h.shape
    pair = pl.BlockSpec((2, tm), lambda i, *_: (0, i))
    grid_spec = pltpu.PrefetchScalarGridSpec(
        num_scalar_prefetch=3,
        grid=(n // tm,),
        in_specs=[pl.BlockSpec((tm, d), lambda i, *_: (i, 0)), pair, pair],
        out_specs=pl.BlockSpec(memory_space=pl.ANY),
        scratch_shapes=[pltpu.VMEM((_compact_rows(tm), d), F32), pltpu.VMEM((tm, d), F32),
                        pltpu.SemaphoreType.DMA],
    )
    return pl.pallas_call(
        functools.partial(_dispatch_kernel, sizes=_piece_sizes(tm)),
        out_shape=jax.ShapeDtypeStruct((n_sorted, d), F32),
        grid_spec=grid_spec,
        compiler_params=_cparams(("arbitrary",), VMEM_LIMIT),
        name="dispatch",
    )(seg_start, seg_len, misc, h, e01, rank01)


def _moe_kernel(te_ref, misc_ref, xs_ref, wg_ref, wu_ref, wd_ref, ys_ref, wgb_ref, wub_ref, wdb_ref):
    i = pl.program_id(0)
    e = te_ref[i]
    prev = te_ref[jnp.maximum(i - 1, 0)]

    @pl.when(jnp.logical_or(i == 0, e != prev))
    def _():
        wgb_ref[...] = wg_ref[0, 0].astype(BF16)
        wub_ref[...] = wu_ref[0, 0].astype(BF16)
        wdb_ref[...] = wd_ref[0, 0].astype(BF16)

    @pl.when(i < misc_ref[0])
    def _():
        x = xs_ref[...].astype(BF16)
        gate = jnp.dot(x, wgb_ref[...], preferred_element_type=F32)
        up = jnp.dot(x, wub_ref[...], preferred_element_type=F32)
        act = (_silu(gate) * up).astype(BF16)
        ys_ref[...] = jnp.dot(act, wdb_ref[...], preferred_element_type=F32)

    @pl.when(i >= misc_ref[0])
    def _():
        ys_ref[...] = jnp.zeros(ys_ref.shape, F32)


def _moe(tile_expert, misc, xs, wg, wu, wd, layer, tm):
    n_sorted, d = xs.shape
    de = wg.shape[-1]
    wmap = lambda i, te, misc_ref: (layer, te[i], 0, 0)
    xmap = lambda i, te, misc_ref: (jnp.minimum(i, jnp.maximum(misc_ref[0] - 1, 0)), 0)
    grid_spec = pltpu.PrefetchScalarGridSpec(
        num_scalar_prefetch=2,
        grid=(n_sorted // tm,),
        in_specs=[pl.BlockSpec((tm, d), xmap),
                  pl.BlockSpec((1, 1, d, de), wmap),
                  pl.BlockSpec((1, 1, d, de), wmap),
                  pl.BlockSpec((1, 1, de, d), wmap)],
        out_specs=pl.BlockSpec((tm, d), lambda i, te, misc_ref: (i, 0)),
        scratch_shapes=[pltpu.VMEM((d, de), BF16), pltpu.VMEM((d, de), BF16),
                        pltpu.VMEM((de, d), BF16)],
    )
    return pl.pallas_call(
        _moe_kernel,
        out_shape=jax.ShapeDtypeStruct((n_sorted, d), F32),
        grid_spec=grid_spec,
        compiler_params=_cparams(("arbitrary",), VMEM_LIMIT),
        name="moe_experts",
    )(tile_expert, misc, xs, wg, wu, wd)


def _combine_kernel(seg_start_ref, seg_len_ref, x_ref, g2_ref, e_ref, rank_ref, w_ref, fg_ref, ys_ref,
                    o_ref, yc_ref, sem, *, sizes, final_norm):
    tm = x_ref.shape[0]
    rc = yc_ref.shape[0]
    i = pl.program_id(0)

    @pl.when(i == 0)
    def _():
        yc_ref[...] = jnp.zeros(yc_ref.shape, F32)

    slot, offs = _local_slots(seg_len_ref, i, e_ref, rank_ref)
    for start in (True, False):
        for e in range(N_EXPERTS):
            _segment_copies(ys_ref, seg_start_ref[i * N_EXPERTS + e], yc_ref, offs[e],
                            seg_len_ref[i * N_EXPERTS + e], sem, sizes, start)

    rid = lax.broadcasted_iota(I32, (rc, tm), 0)
    w = w_ref[...]
    pw = (jnp.where(rid == slot[0:1, :], w[0:1, :], 0.0)
          + jnp.where(rid == slot[1:2, :], w[1:2, :], 0.0))
    yc = yc_ref[...]
    p_hi = pw.astype(BF16)
    p_lo = (pw - p_hi.astype(F32)).astype(BF16)
    y_hi = yc.astype(BF16)
    y_lo = (yc - y_hi.astype(F32)).astype(BF16)
    dn = (((0,), (0,)), ((), ()))
    out = (lax.dot_general(p_hi, y_hi, dn, preferred_element_type=F32)
           + lax.dot_general(p_hi, y_lo, dn, preferred_element_type=F32)
           + lax.dot_general(p_lo, y_hi, dn, preferred_element_type=F32))
    x = x_ref[...] + g2_ref[0] * out
    if final_norm:
        x = x * lax.rsqrt(jnp.mean(x * x, axis=-1, keepdims=True) + EPS) * fg_ref[...]
    o_ref[...] = x


def _combine(seg_start, seg_len, xc, g2, e01, rank01, w01, fg, ys, *, n_rows, tm, tiles_per_seq,
             final_norm):
    d = xc.shape[1]
    nb = g2.shape[0] - 1
    pair = pl.BlockSpec((2, tm), lambda i, *_: (0, i))
    grid_spec = pltpu.PrefetchScalarGridSpec(
        num_scalar_prefetch=2,
        grid=(n_rows // tm,),
        in_specs=[pl.BlockSpec((tm, d), lambda i, *_: (i, 0)),
                  pl.BlockSpec((1, 1, d), lambda i, *_: (jnp.minimum(i // tiles_per_seq, nb), 0, 0)),
                  pair, pair, pair,
                  pl.BlockSpec((1, d), lambda i, *_: (0, 0)),
                  pl.BlockSpec(memory_space=pl.ANY)],
        out_specs=pl.BlockSpec((tm, d), lambda i, *_: (i, 0)),
        scratch_shapes=[pltpu.VMEM((_compact_rows(tm), d), F32), pltpu.SemaphoreType.DMA],
    )
    return pl.pallas_call(
        functools.partial(_combine_kernel, sizes=_piece_sizes(tm), final_norm=final_norm),
        out_shape=jax.ShapeDtypeStruct((n_rows, d), F32),
        grid_spec=grid_spec,
        compiler_params=_cparams(("arbitrary",), VMEM_LIMIT),
        name="combine",
    )(seg_start, seg_len, xc, g2, e01, rank01, w01, fg, ys)


def _lower_bounds(lb_param):
    p = jax.nn.softmax(lb_param.astype(F32), axis=0)
    return jnp.cumsum(p, axis=0) - p[0]


def kernel(x, c, ctx, c_ctx, w_ada, b_ada, norm1_g, norm2_g, w_in, conv_w, conv_b, conv_ln_g,
           conv_ln_b, lb_fwd, lb_bwd, hgrn_norm_g, w_out, router_w, router_bias, w_gate, w_up,
           w_down, final_norm_g):
    bn, seq, d = x.shape
    ctx_len = ctx.shape[1]
    depth = w_ada.shape[0]
    n = bn * seq
    nc = bn * ctx_len
    tm = min(512, seq, nc)
    assert seq % tm == 0 and nc % tm == 0 and seq % GRID_W == 0
    tps = seq // tm

    pad_rows = (-(bn + 1)) % 8
    cc = jnp.concatenate([c, c_ctx[None, :], jnp.zeros((pad_rows, d), F32)], axis=0)
    mod = _ada(cc, w_ada, b_ada)[:, :bn + 1]
    mod = mod.reshape(depth, bn + 1, 6, 1, d)

    lbs_f = _lower_bounds(lb_fwd)
    lbs_b = _lower_bounds(lb_bwd)
    w_in_bf = w_in.astype(BF16)
    w_out_bf = w_out.astype(BF16)
    rw_t = router_w.T.astype(F32)
    rb = router_bias.reshape(N_EXPERTS, 1).astype(F32)
    hg = hgrn_norm_g.astype(F32)
    fg = final_norm_g.reshape(1, d).astype(F32)

    rows_in = [x.reshape(n, d), ctx.reshape(nc, d)]
    s_zero = jnp.zeros((bn, 2 * HEADS, HEAD_DIM, HEAD_DIM), F32)

    for l in range(depth):
        last = l == depth - 1
        sh1, sc1, g1, sh2, sc2, g2 = (mod[l, :, k] for k in range(6))
        row = lambda a: a.reshape(1, -1).astype(F32)

        u, q, iv, lff, kf, lfb, kb, og = _inproj(
            rows_in, row(norm1_g[l]), sh1, sc1, w_in_bf, l, row(lbs_f[l]), row(lbs_b[l]), tm, tps)

        cw, cb = conv_w[l].astype(F32), row(conv_b[l])
        clg, clb = row(conv_ln_g[l]), row(conv_ln_b[l])
        scan = functools.partial(_scan, q, iv, lff, kf, lfb, kb)
        if last:
            (s_ctx,) = scan(s_zero, row_off=n, seq_len=ctx_len, compute_o=False)
            mix_ctx = [[], [], []]
            n_rows = n
        else:
            of_c, ob_c, s_ctx = scan(s_zero, row_off=n, seq_len=ctx_len, compute_o=True)
            ycx_c = _conv_rows(u, cw, cb, clg, clb, seq=ctx_len, row_off=n, n_rows=nc)
            mix_ctx = [[ycx_c], [of_c], [ob_c]]
            n_rows = n + nc
        of, ob, _ = scan(s_ctx, row_off=0, seq_len=seq, compute_o=True)
        if l % 2 == 0:
            ycx = _conv_rows(u, cw, cb, clg, clb, seq=GRID_W, row_off=0, n_rows=n)
        else:
            ycx = _conv_cols(u, cw, cb, clg, clb, batch=bn, seq_len=seq)
        mix = [[a] + c_ for a, c_ in zip((ycx, of, ob), mix_ctx)]

        res_rows = rows_in if (len(rows_in) == 1 or not last) else [rows_in[0]]
        xc = _outproj(res_rows, mix, og, hg[l].reshape(1, HEAD_DIM), w_out_bf, l, g1,
                      n_rows=n_rows, tm=tm, tiles_per_seq=tps)

        h2, e01, rank01, w01, cnt = _router(
            xc, row(norm2_g[l]), sh2, sc2, rw_t, rb, n_rows=n_rows, tm=tm, tiles_per_seq=tps)
        n_tt = n_rows // tm
        n_et = (2 * n_rows + n_tt * N_EXPERTS * (SUBLANES - 1) + tm - 1) // tm + N_EXPERTS
        seg_start, seg_len, te, misc = _tables(cnt[:, :, 0].reshape(-1), n_tt=n_tt, n_et=n_et, tm=tm)
        xs = _dispatch(seg_start, seg_len, misc, h2, e01, rank01, n_sorted=n_et * tm, tm=tm)
        ys = _moe(te, misc, xs, w_gate, w_up, w_down, l, tm)
        xc = _combine(seg_start, seg_len, xc, g2, e01, rank01, w01, fg, ys, n_rows=n_rows, tm=tm,
                      tiles_per_seq=tps, final_norm=last)
        rows_in = [xc]

    return xc[:n].reshape(bn, seq, d)
```

```python
import functools

import numpy as np
import jax
import jax.numpy as jnp
from jax import lax
from jax.experimental import pallas as pl
from jax.experimental.pallas import tpu as pltpu

F32 = jnp.float32
BF16 = jnp.bfloat16
I32 = jnp.int32

EPS = 1e-6
GRID_W = 64
HEADS = 4
HEAD_DIM = 128
D_GROUP = HEADS * HEAD_DIM
N_IN_GROUPS = 7
CONV_K = 31
CONV_HALF = CONV_K // 2
CONV_GAP = 16
N_EXPERTS = 16
N_GROUPS = 4
EXPERTS_PER_GROUP = N_EXPERTS // N_GROUPS
SCAN_CHUNK = 64
SCAN_FAST_LIMIT = 60.0
LANES = 128
SUBLANES = 8
VMEM_LIMIT = 56 * 1024 * 1024


def _cparams(sem, vmem=None):
    return pltpu.CompilerParams(dimension_semantics=sem, vmem_limit_bytes=vmem)


def _sigmoid(x):
    return jax.nn.sigmoid(x)


def _silu(x):
    return x * _sigmoid(x)


def _ada_kernel(c_ref, w_ref, b_ref, o_ref):
    a = _silu(c_ref[...]).astype(BF16)
    w = w_ref[0].astype(BF16)
    o_ref[0] = jnp.dot(a, w, preferred_element_type=F32) + b_ref[0]


def _ada(cc, w_ada, b_ada, tn=1536):
    depth, d, d6 = w_ada.shape
    rows = cc.shape[0]
    return pl.pallas_call(
        _ada_kernel,
        out_shape=jax.ShapeDtypeStruct((depth, rows, d6), F32),
        grid=(depth, d6 // tn),
        in_specs=[
            pl.BlockSpec((rows, d), lambda l, j: (0, 0)),
            pl.BlockSpec((1, d, tn), lambda l, j: (l, 0, j)),
            pl.BlockSpec((1, 1, tn), lambda l, j: (l, 0, j)),
        ],
        out_specs=pl.BlockSpec((1, rows, tn), lambda l, j: (l, 0, j)),
        compiler_params=_cparams(("arbitrary", "arbitrary"), VMEM_LIMIT),
        name="ada",
    )(cc, w_ada, b_ada.reshape(depth, 1, d6))


def _modulate(x, g, shift, scale):
    r = lax.rsqrt(jnp.mean(x * x, axis=-1, keepdims=True) + EPS)
    return (x * r * g) * (1.0 + scale) + shift


def _rows_from(refs, n_lat_tiles):
    if len(refs) == 1:
        return refs[0][...]
    return jnp.where(pl.program_id(0) >= n_lat_tiles, refs[1][...], refs[0][...])


def _row_specs(arrays, tm):
    width = arrays[0].shape[1]
    if len(arrays) == 1:
        return [pl.BlockSpec((tm, width), lambda i, *_: (i, 0))], 0
    n_lat_tiles = arrays[0].shape[0] // tm
    return [pl.BlockSpec((tm, width), lambda i, *_: (jnp.minimum(i, n_lat_tiles - 1), 0)),
            pl.BlockSpec((tm, width), lambda i, *_: (jnp.maximum(i - n_lat_tiles, 0), 0))], n_lat_tiles


def _inproj_kernel(*refs, n_src, n_lat_tiles):
    x = _rows_from(refs[:n_src], n_lat_tiles)
    (g_ref, sh_ref, sc_ref, w_ref, lbf_ref, lbb_ref,
     u_ref, q_ref, i_ref, lff_ref, kf_ref, lfb_ref, kb_ref, og_ref) = refs[n_src:]
    hb = _modulate(x, g_ref[...], sh_ref[0], sc_ref[0]).astype(BF16)

    def proj(j):
        return jnp.dot(hb, w_ref[0, :, j * D_GROUP:(j + 1) * D_GROUP], preferred_element_type=F32)

    u_ref[...] = (proj(0) * _sigmoid(proj(1))).astype(BF16)
    q_ref[...] = _silu(proj(2)).astype(BF16)
    i_ref[...] = proj(3).astype(BF16)
    for j, lb_ref, lf_ref, k_ref in ((4, lbf_ref, lff_ref, kf_ref), (5, lbb_ref, lfb_ref, kb_ref)):
        lb = lb_ref[...]
        f = lb + (1.0 - lb) * _sigmoid(proj(j))
        lf_ref[...] = jnp.log(f)
        k_ref[...] = (1.0 - f).astype(BF16)
    og_ref[...] = _silu(proj(6)).astype(BF16)


def _inproj(xs, g, sh, sc, w_bf, layer, lbf, lbb, tm, tiles_per_seq):
    n = sum(a.shape[0] for a in xs)
    d = xs[0].shape[1]
    x_specs, n_lat_tiles = _row_specs(xs, tm)
    nb = sh.shape[0] - 1
    mod_spec = pl.BlockSpec((1, 1, d), lambda i: (jnp.minimum(i // tiles_per_seq, nb), 0, 0))
    row_spec = pl.BlockSpec((tm, D_GROUP), lambda i: (i, 0))
    vec_spec = pl.BlockSpec((1, D_GROUP), lambda i: (0, 0))
    outs = [jax.ShapeDtypeStruct((n, D_GROUP), dt)
            for dt in (BF16, BF16, BF16, F32, BF16, F32, BF16, BF16)]
    return pl.pallas_call(
        functools.partial(_inproj_kernel, n_src=len(xs), n_lat_tiles=n_lat_tiles),
        out_shape=outs,
        grid=(n // tm,),
        in_specs=x_specs + [
            pl.BlockSpec((1, d), lambda i: (0, 0)),
            mod_spec, mod_spec,
            pl.BlockSpec((1, d, N_IN_GROUPS * D_GROUP), lambda i: (layer, 0, 0)),
            vec_spec, vec_spec,
        ],
        out_specs=[row_spec] * 8,
        compiler_params=_cparams(("arbitrary",), VMEM_LIMIT),
        name="inproj",
    )(*xs, g, sh, sc, w_bf, lbf, lbb)


def _scan_tables(c, backward):
    levels = int(np.log2(c))
    r_all = np.zeros((levels * c + 2 * c + 8, c), np.float32)
    for lev in range(levels):
        h = 1 << lev
        for r in range(c):
            bd = (r // (2 * h)) * 2 * h + h
            if not backward:
                if r >= bd:
                    r_all[lev * c + r, bd:r + 1] = 1.0
                else:
                    r_all[lev * c + r, r + 1:bd] = 1.0
            else:
                if r < bd:
                    r_all[lev * c + r, r:bd] = 1.0
                else:
                    r_all[lev * c + r, bd:r] = 1.0
    base = levels * c
    for r in range(c):
        if not backward:
            r_all[base + r, :r + 1] = 1.0
            r_all[base + c + r, r + 1:] = 1.0
        else:
            r_all[base + r, r:] = 1.0
            r_all[base + c + r, :r] = 1.0
    r_all[base + 2 * c:, :] = 1.0
    return r_all


def _scan_kernel(*refs, n_chunks, compute_o):
    c = SCAN_CHUNK
    levels = int(np.log2(c))
    (rf_ref, rb_ref, qf_ref, vf_ref, lff_ref, kf_ref,
     qb_ref, vb_ref, lfb_ref, kb_ref, s0_ref) = refs[:11]
    pos = 11
    if compute_o:
        of_ref, ob_ref, sfin_ref, st_ref = refs[pos:pos + 4]
    else:
        sfin_ref, st_ref = refs[pos:pos + 2]
        of_ref = ob_ref = None
    j = pl.program_id(1)

    @pl.when(j == 0)
    def _():
        st_ref[...] = s0_ref[0]

    row = lax.broadcasted_iota(jnp.int32, (c, 1), 0)
    ri = lax.broadcasted_iota(jnp.int32, (c, c), 0)
    ci = lax.broadcasted_iota(jnp.int32, (c, c), 1)
    upper = [(row & (2 * (1 << lev) - 1)) >= (1 << lev) for lev in range(levels)]
    same_parent = [(ri >> (lev + 1)) == (ci >> (lev + 1)) for lev in range(levels)]
    diag = ri == ci
    dn_t = (((1,), (1,)), ((), ()))
    dn_tl = (((0,), (0,)), ((), ()))

    half = c // 2
    tri = [jnp.where(ci <= ri, 1.0, 0.0).astype(BF16), jnp.where(ci >= ri, 1.0, 0.0).astype(BF16)]
    first_half = [row < half, row >= half]
    re = lax.broadcasted_iota(jnp.int32, (c, 2 * c), 0)
    ce = lax.broadcasted_iota(jnp.int32, (c, 2 * c), 1)
    cs = ce & (c - 1)
    same_half = (re >= half) == (cs >= half)
    own = ce < c
    mask_ext = [
        (own & same_half & (cs <= re)) | (jnp.logical_not(own) & (re >= half) & (cs < half)),
        (own & same_half & (cs >= re)) | (jnp.logical_not(own) & (re < half) & (cs >= half)),
    ]

    mrow, trow = [half - 1, half], [c - 1, 0]

    def prefix(tri_mat, g):
        g_hi = g.astype(BF16)
        r1 = g - g_hi.astype(F32)
        g_mid = r1.astype(BF16)
        g_lo = (r1 - g_mid.astype(F32)).astype(BF16)
        return (jnp.dot(tri_mat, g_hi, preferred_element_type=F32)
                + jnp.dot(tri_mat, g_mid, preferred_element_type=F32)
                + jnp.dot(tri_mat, g_lo, preferred_element_type=F32))

    def next_state(st, hs, kh, vh, b, tot):
        kd = (kh * jnp.exp(tot[:, hs] - b[:, hs])).astype(BF16)
        return st * jnp.exp(tot[:, hs]) + lax.dot_general(vh, kd, dn_tl, preferred_element_type=F32)

    def state_only_chunk(dirn, lf_ref, k_ref, v_ref, r0):
        rows = pl.ds(r0, c)
        b = prefix(tri[dirn], lf_ref[rows, :])
        tot = b[trow[dirn]:trow[dirn] + 1, :]
        for h in range(HEADS):
            hs = slice(h * HEAD_DIM, (h + 1) * HEAD_DIM)
            st_ref[dirn * HEADS + h] = next_state(st_ref[dirn * HEADS + h], hs,
                                                  k_ref[rows, hs].astype(F32), v_ref[rows, hs], b, tot)

    def fast_chunk(dirn, q_ref, v_ref, k_ref, o_ref, rows, b, m, tot, states):
        fh = first_half[dirn]
        cdec = b - jnp.where(fh, 0.0, m)
        e_own = jnp.exp(cdec)
        e_own_inv = jnp.exp(-cdec)
        e_cross = jnp.where(fh, jnp.exp(jnp.minimum(m - b, 0.0)), 0.0)
        e_inc = jnp.exp(b)
        for h in range(HEADS):
            hs = slice(h * HEAD_DIM, (h + 1) * HEAD_DIM)
            qh = q_ref[rows, hs].astype(F32)
            kh = k_ref[rows, hs].astype(F32)
            vh = v_ref[rows, hs]
            st = states[dirn * HEADS + h]
            q1 = (qh * e_own[:, hs]).astype(BF16)
            kcat = jnp.concatenate([(kh * e_own_inv[:, hs]).astype(BF16),
                                    (kh * e_cross[:, hs]).astype(BF16)], axis=0)
            sc = lax.dot_general(q1, kcat, dn_t, preferred_element_type=F32)
            p = jnp.where(mask_ext[dirn], sc, 0.0).astype(BF16)
            qi = (qh * e_inc[:, hs]).astype(BF16)
            o = (jnp.dot(p, jnp.concatenate([vh, vh], axis=0), preferred_element_type=F32)
                 + lax.dot_general(qi, st.astype(BF16), dn_t, preferred_element_type=F32))
            o_ref[rows, hs] = o.astype(o_ref.dtype)
            states[dirn * HEADS + h] = next_state(st, hs, kh, vh, b, tot)

    def robust_chunk(dirn, r_ref, q_ref, v_ref, lf_ref, k_ref, o_ref, r0):
        rows = pl.ds(r0, c)
        g = lf_ref[rows, :]
        g_hi = g.astype(BF16)
        g_lo = (g - g_hi.astype(F32)).astype(BF16)
        rmat = r_ref[...]
        e_all = jnp.exp(jnp.dot(rmat, g_hi, preferred_element_type=F32)
                        + jnp.dot(rmat, g_lo, preferred_element_type=F32))
        base = levels * c
        for h in range(HEADS):
            hs = slice(h * HEAD_DIM, (h + 1) * HEAD_DIM)
            qh = q_ref[rows, hs].astype(F32)
            kh = k_ref[rows, hs].astype(F32)
            vh = v_ref[rows, hs]
            st = st_ref[dirn * HEADS + h]
            if compute_o:
                att = jnp.where(diag, lax.dot_general(qh.astype(BF16), kh.astype(BF16), dn_t,
                                                      preferred_element_type=F32), 0.0)
                for lev in range(levels):
                    e_l = e_all[lev * c:(lev + 1) * c, hs]
                    q_side = upper[lev] if dirn == 0 else jnp.logical_not(upper[lev])
                    qt = jnp.where(q_side, qh * e_l, 0.0).astype(BF16)
                    kt = jnp.where(q_side, 0.0, kh * e_l).astype(BF16)
                    a_l = lax.dot_general(qt, kt, dn_t, preferred_element_type=F32)
                    att = att + jnp.where(same_parent[lev], a_l, 0.0)
                qi = (qh * e_all[base:base + c, hs]).astype(BF16)
                o = lax.dot_general(qi, st.astype(BF16), dn_t, preferred_element_type=F32)
                o = o + jnp.dot(att.astype(BF16), vh, preferred_element_type=F32)
                o_ref[rows, hs] = o.astype(o_ref.dtype)
            kd = (kh * e_all[base + c:base + 2 * c, hs]).astype(BF16)
            e_tot = e_all[base + 2 * c:base + 2 * c + 1, hs]
            st_ref[dirn * HEADS + h] = st * e_tot + lax.dot_general(
                vh, kd, dn_tl, preferred_element_type=F32)

    dir_refs = ((rf_ref, qf_ref, vf_ref, lff_ref, kf_ref, of_ref),
                (rb_ref, qb_ref, vb_ref, lfb_ref, kb_ref, ob_ref))

    def chunk_rows(ci_):
        return pl.multiple_of(ci_ * c, c), pl.multiple_of((n_chunks - 1 - ci_) * c, c)

    if not compute_o:
        def state_body(ci_, carry):
            for dirn, r0 in enumerate(chunk_rows(ci_)):
                _, _, v_ref, lf_ref, k_ref, _ = dir_refs[dirn]
                state_only_chunk(dirn, lf_ref, k_ref, v_ref, r0)
            return carry

        lax.fori_loop(0, n_chunks, state_body, 0)
    else:
        t = n_chunks * c
        rt = lax.broadcasted_iota(jnp.int32, (t, t), 0)
        ct = lax.broadcasted_iota(jnp.int32, (t, t), 1)
        same_chunk = (rt >> levels) == (ct >> levels)
        b_blk, stats, worst = [], {}, None
        for dirn in range(2):
            tri_blk = jnp.where(same_chunk & ((ct <= rt) if dirn == 0 else (ct >= rt)), 1.0, 0.0)
            b_blk.append(prefix(tri_blk.astype(BF16), dir_refs[dirn][3][...]))
            for ch in range(n_chunks):
                m = b_blk[dirn][ch * c + mrow[dirn]:ch * c + mrow[dirn] + 1, :]
                tot = b_blk[dirn][ch * c + trow[dirn]:ch * c + trow[dirn] + 1, :]
                stats[dirn, ch] = (m, tot)
                w = jnp.minimum(m, tot - m)
                worst = w if worst is None else jnp.minimum(worst, w)
        fast_ok = jnp.min(worst) >= -SCAN_FAST_LIMIT

        @pl.when(fast_ok)
        def _():
            states = [st_ref[i] for i in range(2 * HEADS)]
            for step in range(n_chunks):
                for dirn in range(2):
                    ch = step if dirn == 0 else n_chunks - 1 - step
                    _, q_ref, v_ref, _, k_ref, o_ref = dir_refs[dirn]
                    rows = slice(ch * c, (ch + 1) * c)
                    fast_chunk(dirn, q_ref, v_ref, k_ref, o_ref, rows, b_blk[dirn][rows, :],
                               *stats[dirn, ch], states)
            for i in range(2 * HEADS):
                st_ref[i] = states[i]

        @pl.when(jnp.logical_not(fast_ok))
        def _():
            def robust_body(ci_, carry):
                for dirn, r0 in enumerate(chunk_rows(ci_)):
                    r_ref, q_ref, v_ref, lf_ref, k_ref, o_ref = dir_refs[dirn]
                    robust_chunk(dirn, r_ref, q_ref, v_ref, lf_ref, k_ref, o_ref, r0)
                return carry

            lax.fori_loop(0, n_chunks, robust_body, 0)

    @pl.when(j == pl.num_programs(1) - 1)
    def _():
        sfin_ref[0] = st_ref[...]


def _scan(q, v, lff, kf, lfb, kb, s0, *, row_off, seq_len, compute_o):
    batch = s0.shape[0]
    t = min(256, seq_len)
    nt = seq_len // t
    off = row_off // t
    fwd = lambda b, j: (off + b * nt + j, 0)
    bwd = lambda b, j: (off + b * nt + (nt - 1 - j), 0)
    ofwd = lambda b, j: (b * nt + j, 0)
    obwd = lambda b, j: (b * nt + (nt - 1 - j), 0)
    blk = lambda im: pl.BlockSpec((t, D_GROUP), im)
    rf = jnp.asarray(_scan_tables(SCAN_CHUNK, False), BF16)
    rb = jnp.asarray(_scan_tables(SCAN_CHUNK, True), BF16)
    rspec = pl.BlockSpec(rf.shape, lambda b, j: (0, 0))
    sspec = pl.BlockSpec((1, 2 * HEADS, HEAD_DIM, HEAD_DIM), lambda b, j: (b, 0, 0, 0))
    in_specs = [rspec, rspec, blk(fwd), blk(fwd), blk(fwd), blk(fwd),
                blk(bwd), blk(bwd), blk(bwd), blk(bwd), sspec]
    args = [rf, rb, q, v, lff, kf, q, v, lfb, kb, s0]
    out_shape, out_specs = [], []
    if compute_o:
        out_shape += [jax.ShapeDtypeStruct((batch * seq_len, D_GROUP), BF16)] * 2
        out_specs += [blk(ofwd), blk(obwd)]
    out_shape.append(jax.ShapeDtypeStruct(s0.shape, F32))
    out_specs.append(sspec)
    kern = functools.partial(_scan_kernel, n_chunks=t // SCAN_CHUNK, compute_o=compute_o)
    return pl.pallas_call(
        kern,
        out_shape=out_shape,
        grid=(batch, nt),
        in_specs=in_specs,
        out_specs=out_specs,
        scratch_shapes=[pltpu.VMEM((2 * HEADS, HEAD_DIM, HEAD_DIM), F32)],
        compiler_params=_cparams(("arbitrary", "arbitrary"), VMEM_LIMIT),
        name="hgrn_scan",
    )(*args)


def _ln_silu(y, g, b):
    mu = jnp.mean(y, axis=-1, keepdims=True)
    yc = y - mu
    var = jnp.mean(yc * yc, axis=-1, keepdims=True)
    return _silu(yc * lax.rsqrt(var + EPS) * g + b)


def _conv_rows_kernel(u_ref, w_ref, b_ref, lg_ref, lb_ref, y_ref, pad_ref, acc_ref, *, seq, tb):
    nseq = tb // seq
    stride = seq + CONV_GAP
    pad_ref[...] = jnp.zeros(pad_ref.shape, F32)
    for s in range(nseq):
        pad_ref[CONV_GAP + s * stride:CONV_GAP + s * stride + seq, :] = (
            u_ref[s * seq:(s + 1) * seq, :].astype(F32))
    rc = 64
    for s in range(nseq):
        for r0 in range(0, seq, rc):
            for cl in range(D_GROUP // LANES):
                ls = slice(cl * LANES, (cl + 1) * LANES)
                acc = jnp.zeros((rc, LANES), F32)
                for k in range(CONV_K):
                    base = CONV_GAP + s * stride + r0 + k - CONV_HALF
                    acc = acc + w_ref[k:k + 1, ls] * pad_ref[base:base + rc, ls]
                acc_ref[s * seq + r0:s * seq + r0 + rc, ls] = acc
    y = acc_ref[...] + b_ref[...]
    y_ref[...] = _ln_silu(y, lg_ref[...], lb_ref[...]).astype(y_ref.dtype)


def _conv_rows(u, w, b, lg, lb, *, seq, row_off, n_rows):
    tb = max(seq, 512) if n_rows % max(seq, 512) == 0 else seq
    nseq = tb // seq
    off = row_off // tb
    vec = pl.BlockSpec((1, D_GROUP), lambda i: (0, 0))
    in_specs = [pl.BlockSpec((tb, D_GROUP), lambda i: (off + i, 0)),
                pl.BlockSpec((CONV_K, D_GROUP), lambda i: (0, 0)), vec, vec, vec]
    args = [u, w, b, lg, lb]
    kern = functools.partial(_conv_rows_kernel, seq=seq, tb=tb)
    return pl.pallas_call(
        kern,
        out_shape=jax.ShapeDtypeStruct((n_rows, D_GROUP), BF16),
        grid=(n_rows // tb,),
        in_specs=in_specs,
        out_specs=pl.BlockSpec((tb, D_GROUP), lambda i: (i, 0)),
        scratch_shapes=[pltpu.VMEM((CONV_GAP + nseq * (seq + CONV_GAP), D_GROUP), F32),
                        pltpu.VMEM((tb, D_GROUP), F32)],
        compiler_params=_cparams(("arbitrary",), VMEM_LIMIT),
        name="conv_rows",
    )(*args)


CONV_COL_TILE = 16


def _conv_cols_kernel(u_ref, w_ref, b_ref, lg_ref, lb_ref, y_ref, uf_ref, *, n_rows):
    uf_ref[...] = u_ref[...].astype(F32)
    bias = b_ref[...]
    lg = lg_ref[...]
    lb = lb_ref[...]

    def col_tile(wi, carry):
        w0 = wi * CONV_COL_TILE
        for r in range(n_rows):
            acc = None
            for k in range(CONV_K):
                rr = r + k - CONV_HALF
                if 0 <= rr < n_rows:
                    src = pl.ds(pl.multiple_of(rr * GRID_W + w0, CONV_COL_TILE), CONV_COL_TILE)
                    term = w_ref[k:k + 1, :] * uf_ref[src, :]
                    acc = term if acc is None else acc + term
            dst = pl.ds(pl.multiple_of(r * GRID_W + w0, CONV_COL_TILE), CONV_COL_TILE)
            y_ref[dst, :] = _ln_silu(acc + bias, lg, lb).astype(y_ref.dtype)
        return carry

    lax.fori_loop(0, GRID_W // CONV_COL_TILE, col_tile, 0)


def _conv_cols(u, w, b, lg, lb, *, batch, seq_len):
    vec = pl.BlockSpec((1, D_GROUP), lambda bi: (0, 0))
    blk = pl.BlockSpec((seq_len, D_GROUP), lambda bi: (bi, 0))
    return pl.pallas_call(
        functools.partial(_conv_cols_kernel, n_rows=seq_len // GRID_W),
        out_shape=jax.ShapeDtypeStruct((batch * seq_len, D_GROUP), BF16),
        grid=(batch,),
        in_specs=[blk, pl.BlockSpec((CONV_K, D_GROUP), lambda bi: (0, 0)), vec, vec, vec],
        out_specs=blk,
        scratch_shapes=[pltpu.VMEM((seq_len, D_GROUP), F32)],
        compiler_params=_cparams(("arbitrary",), VMEM_LIMIT),
        name="conv_cols",
    )(u, w, b, lg, lb)


def _outproj_kernel(*refs, n_x, n_mix, n_lat_tiles):
    x = _rows_from(refs[:n_x], n_lat_tiles)
    pos = n_x
    ycx, of, ob = (_rows_from(refs[pos + k * n_mix:pos + (k + 1) * n_mix], n_lat_tiles) for k in range(3))
    og_ref, hg_ref, w_ref, g1_ref, o_ref = refs[-5:]
    o = of.astype(F32) + ob.astype(F32)
    og = og_ref[...].astype(F32)
    hg = hg_ref[...]
    acc = jnp.dot(ycx, w_ref[0, 0:D_GROUP, :], preferred_element_type=F32)
    for h in range(HEADS):
        hs = slice(h * HEAD_DIM, (h + 1) * HEAD_DIM)
        oh = o[:, hs]
        r = lax.rsqrt(jnp.mean(oh * oh, axis=-1, keepdims=True) + EPS)
        yh = ((oh * r * hg) * og[:, hs]).astype(BF16)
        acc = acc + jnp.dot(yh, w_ref[0, D_GROUP + h * HEAD_DIM:D_GROUP + (h + 1) * HEAD_DIM, :],
                            preferred_element_type=F32)
    o_ref[...] = x + g1_ref[0] * acc


def _outproj(xs, mix, og, hg, w_bf, layer, g1, *, n_rows, tm, tiles_per_seq):
    d = xs[0].shape[1]
    nb = g1.shape[0] - 1
    x_specs, n_lat_tiles = _row_specs(xs, tm)
    mix_specs, mix_args = [], []
    for arrays in mix:
        specs, nl = _row_specs(arrays, tm)
        assert len(arrays) == len(mix[0]) and (nl == 0 or n_lat_tiles in (0, nl))
        n_lat_tiles = max(n_lat_tiles, nl)
        mix_specs += specs
        mix_args += list(arrays)
    row = lambda width: pl.BlockSpec((tm, width), lambda i: (i, 0))
    kern = functools.partial(_outproj_kernel, n_x=len(xs), n_mix=len(mix[0]), n_lat_tiles=n_lat_tiles)
    return pl.pallas_call(
        kern,
        out_shape=jax.ShapeDtypeStruct((n_rows, d), F32),
        grid=(n_rows // tm,),
        in_specs=x_specs + mix_specs + [row(D_GROUP),
                  pl.BlockSpec((1, HEAD_DIM), lambda i: (0, 0)),
                  pl.BlockSpec((1,) + w_bf.shape[1:], lambda i: (layer, 0, 0)),
                  pl.BlockSpec((1, 1, d), lambda i: (jnp.minimum(i // tiles_per_seq, nb), 0, 0))],
        out_specs=row(d),
        compiler_params=_cparams(("arbitrary",), VMEM_LIMIT),
        name="outproj",
    )(*xs, *mix_args, og, hg, w_bf, g1)


def _router_kernel(x_ref, g_ref, sh_ref, sc_ref, rw_ref, rb_ref,
                   h_ref, e_ref, rank_ref, w_ref, cnt_ref):
    tm = x_ref.shape[0]
    h = _modulate(x_ref[...], g_ref[...], sh_ref[0], sc_ref[0])
    h_ref[...] = h.astype(h_ref.dtype)
    logits = lax.dot_general(rw_ref[...], h, (((1,), (1,)), ((), ())),
                             precision=lax.Precision.HIGHEST, preferred_element_type=F32)
    s = _sigmoid(logits)
    sb = s + rb_ref[...]
    s_rows = [s[e:e + 1, :] for e in range(N_EXPERTS)]
    sb_rows = [sb[e:e + 1, :] for e in range(N_EXPERTS)]

    def group_score(g):
        v = sb_rows[g * EXPERTS_PER_GROUP:(g + 1) * EXPERTS_PER_GROUP]
        best = None
        for a in range(EXPERTS_PER_GROUP):
            for b in range(a + 1, EXPERTS_PER_GROUP):
                p = v[a] + v[b]
                best = p if best is None else jnp.maximum(best, p)
        return best

    cur = group_score(0)
    best_g = jnp.zeros(cur.shape, I32)
    for g in range(1, N_GROUPS):
        gs = group_score(g)
        upd = gs > cur
        best_g = jnp.where(upd, g, best_g)
        cur = jnp.where(upd, gs, cur)

    def pick(rows, jdx):
        out = rows[jdx]
        for g in range(1, N_GROUPS):
            out = jnp.where(best_g == g, rows[g * EXPERTS_PER_GROUP + jdx], out)
        return out

    vb = [pick(sb_rows, jdx) for jdx in range(EXPERTS_PER_GROUP)]
    vs = [pick(s_rows, jdx) for jdx in range(EXPERTS_PER_GROUP)]

    def first_argmax(vals):
        m = vals[0]
        for v in vals[1:]:
            m = jnp.maximum(m, v)
        idx = jnp.full(m.shape, EXPERTS_PER_GROUP - 1, I32)
        for jdx in range(EXPERTS_PER_GROUP - 2, -1, -1):
            idx = jnp.where(vals[jdx] == m, jdx, idx)
        return idx

    def take(vals, idx):
        out = vals[EXPERTS_PER_GROUP - 1]
        for jdx in range(EXPERTS_PER_GROUP - 2, -1, -1):
            out = jnp.where(idx == jdx, vals[jdx], out)
        return out

    i0 = first_argmax(vb)
    vb2 = [jnp.where(i0 == jdx, -jnp.inf, vb[jdx]) for jdx in range(EXPERTS_PER_GROUP)]
    i1 = first_argmax(vb2)
    s0 = take(vs, i0)
    s1 = take(vs, i1)
    tot = s0 + s1
    w_ref[0:1, :] = s0 / tot
    w_ref[1:2, :] = s1 / tot
    e0 = best_g * EXPERTS_PER_GROUP + i0
    e1 = best_g * EXPERTS_PER_GROUP + i1
    e_ref[0:1, :] = e0
    e_ref[1:2, :] = e1

    eid = lax.broadcasted_iota(I32, (N_EXPERTS, tm), 0)
    hit0 = eid == e0
    hit1 = eid == e1
    onehot = jnp.where(jnp.logical_or(hit0, hit1), 1.0, 0.0)
    ti = lax.broadcasted_iota(I32, (tm, tm), 0)
    tj = lax.broadcasted_iota(I32, (tm, tm), 1)
    before = jnp.where(ti < tj, 1.0, 0.0).astype(BF16)
    rank = jnp.dot(onehot.astype(BF16), before, preferred_element_type=F32)
    rank_ref[0:1, :] = jnp.sum(jnp.where(hit0, rank, 0.0), axis=0, keepdims=True).astype(I32)
    rank_ref[1:2, :] = jnp.sum(jnp.where(hit1, rank, 0.0), axis=0, keepdims=True).astype(I32)
    cnt = jnp.sum(onehot, axis=1, keepdims=True).astype(I32)
    cnt_ref[0] = jnp.broadcast_to(cnt, (N_EXPERTS, LANES))


def _router(xc, g, sh, sc, rw_t, rb, *, n_rows, tm, tiles_per_seq):
    d = xc.shape[1]
    nb = sh.shape[0] - 1
    n_tt = n_rows // tm
    mod_spec = pl.BlockSpec((1, 1, d), lambda i: (jnp.minimum(i // tiles_per_seq, nb), 0, 0))
    pair = pl.BlockSpec((2, tm), lambda i: (0, i))
    return pl.pallas_call(
        _router_kernel,
        out_shape=[jax.ShapeDtypeStruct((n_rows, d), BF16),
                   jax.ShapeDtypeStruct((2, n_rows), I32),
                   jax.ShapeDtypeStruct((2, n_rows), I32),
                   jax.ShapeDtypeStruct((2, n_rows), F32),
                   jax.ShapeDtypeStruct((n_tt, N_EXPERTS, LANES), I32)],
        grid=(n_tt,),
        in_specs=[pl.BlockSpec((tm, d), lambda i: (i, 0)),
                  pl.BlockSpec((1, d), lambda i: (0, 0)),
                  mod_spec, mod_spec,
                  pl.BlockSpec((N_EXPERTS, d), lambda i: (0, 0)),
                  pl.BlockSpec((N_EXPERTS, 1), lambda i: (0, 0))],
        out_specs=[pl.BlockSpec((tm, d), lambda i: (i, 0)), pair, pair, pair,
                   pl.BlockSpec((1, N_EXPERTS, LANES), lambda i: (i, 0, 0))],
        compiler_params=_cparams(("arbitrary",), VMEM_LIMIT),
        name="router",
    )(xc, g, sh, sc, rw_t, rb)


def _tables_kernel(cnt_ref, seg_start_ref, seg_len_ref, te_ref, misc_ref, *, n_tt, n_et, tm):
    def expert(e, row0):
        def tile(i, pos):
            n = cnt_ref[i * N_EXPERTS + e]
            n8 = ((n + SUBLANES - 1) // SUBLANES) * SUBLANES
            seg_start_ref[i * N_EXPERTS + e] = pos
            seg_len_ref[i * N_EXPERTS + e] = n8
            return pos + n8

        end = lax.fori_loop(0, n_tt, tile, row0)
        padded_end = row0 + ((end - row0 + tm - 1) // tm) * tm
        misc_ref[1 + e] = end
        misc_ref[1 + N_EXPERTS + e] = padded_end - end

        def mark(j, carry):
            te_ref[j] = e
            return carry

        lax.fori_loop(row0 // tm, padded_end // tm, mark, 0)
        return padded_end

    total = row0 = 0
    for e in range(N_EXPERTS):
        row0 = expert(e, row0)
    total = row0
    n_used = total // tm
    misc_ref[0] = n_used
    last = te_ref[jnp.maximum(n_used - 1, 0)]

    def fill(j, carry):
        te_ref[j] = last
        return carry

    lax.fori_loop(n_used, n_et, fill, 0)


def _tables(cnt, *, n_tt, n_et, tm):
    smem = pl.BlockSpec(memory_space=pltpu.SMEM)
    return pl.pallas_call(
        functools.partial(_tables_kernel, n_tt=n_tt, n_et=n_et, tm=tm),
        out_shape=[jax.ShapeDtypeStruct((n_tt * N_EXPERTS,), I32),
                   jax.ShapeDtypeStruct((n_tt * N_EXPERTS,), I32),
                   jax.ShapeDtypeStruct((n_et,), I32),
                   jax.ShapeDtypeStruct((1 + 2 * N_EXPERTS,), I32)],
        in_specs=[smem],
        out_specs=[smem, smem, smem, smem],
        name="route_tables",
    )(cnt)


def _piece_sizes(max_rows):
    sizes, s = [], SUBLANES
    while s <= max_rows:
        sizes.append(s)
        s *= 2
    return sizes[::-1]


def _segment_copies(src_ref, src_row, dst_ref, dst_row, n_rows, sem, sizes, start):
    aligned = lambda r: r if isinstance(r, int) else pl.multiple_of(r, SUBLANES)
    for sz in sizes:
        src = pl.ds(aligned(src_row), sz)
        dst = pl.ds(aligned(dst_row), sz)

        @pl.when((n_rows & sz) != 0)
        def _(src=src, dst=dst):
            cp = pltpu.make_async_copy(src_ref.at[src, :], dst_ref.at[dst, :], sem)
            if start:
                cp.start()
            else:
                cp.wait()

        src_row = src_row + (n_rows & sz)
        dst_row = dst_row + (n_rows & sz)


def _local_slots(seg_len_ref, tile, e_ref, rank_ref):
    offs, lo = [], 0
    for e in range(N_EXPERTS):
        offs.append(lo)
        lo = lo + seg_len_ref[tile * N_EXPERTS + e]
    e01 = e_ref[...]
    slot = rank_ref[...]
    for e in range(N_EXPERTS):
        slot = slot + jnp.where(e01 == e, offs[e], 0)
    return slot, offs


def _compact_rows(tm):
    return ((2 * tm + N_EXPERTS * (SUBLANES - 1) + LANES - 1) // LANES) * LANES


def _dispatch_kernel(seg_start_ref, seg_len_ref, misc_ref, h_ref, e_ref, rank_ref, xs_ref,
                     xc_ref, zero_ref, sems, *, sizes):
    tm = h_ref.shape[0]
    rc = xc_ref.shape[1]
    i = pl.program_id(0)
    buf = i % 2
    slot, _ = _local_slots(seg_len_ref, i, e_ref, rank_ref)
    rid = lax.broadcasted_iota(I32, (rc, tm), 0)
    sel = jnp.logical_or(rid == slot[0:1, :], rid == slot[1:2, :])
    perm = jnp.where(sel, 1.0, 0.0).astype(BF16)
    xc_ref[buf] = jnp.dot(perm, h_ref[...], preferred_element_type=F32)

    def segments(tile, b, start):
        lo = 0
        for e in range(N_EXPERTS):
            n8 = seg_len_ref[tile * N_EXPERTS + e]
            _segment_copies(xc_ref.at[b], lo, xs_ref, seg_start_ref[tile * N_EXPERTS + e], n8,
                            sems.at[b], sizes, start)
            lo = lo + n8

    @pl.when(i > 0)
    def _():
        segments(i - 1, 1 - buf, False)

    segments(i, buf, True)

    @pl.when(i == pl.num_programs(0) - 1)
    def _():
        segments(i, buf, False)
        zero_ref[...] = jnp.zeros(zero_ref.shape, F32)
        for start in (True, False):
            for e in range(N_EXPERTS):
                _segment_copies(zero_ref, 0, xs_ref, misc_ref[1 + e], misc_ref[1 + N_EXPERTS + e],
                                sems.at[0], sizes, start)

        def clear_tile(j, carry):
            cp = pltpu.make_async_copy(zero_ref, xs_ref.at[pl.ds(pl.multiple_of(j * tm, tm), tm), :],
                                       sems.at[0])
            cp.start()
            cp.wait()
            return carry

        lax.fori_loop(misc_ref[0], xs_ref.shape[0] // tm, clear_tile, 0)


def _dispatch(seg_start, seg_len, misc, h, e01, rank01, *, n_sorted, tm):
    n, d = h.shape
    pair = pl.BlockSpec((2, tm), lambda i, *_: (0, i))
    grid_spec = pltpu.PrefetchScalarGridSpec(
        num_scalar_prefetch=3,
        grid=(n // tm,),
        in_specs=[pl.BlockSpec((tm, d), lambda i, *_: (i, 0)), pair, pair],
        out_specs=pl.BlockSpec(memory_space=pl.ANY),
        scratch_shapes=[pltpu.VMEM((2, _compact_rows(tm), d), F32), pltpu.VMEM((tm, d), F32),
                        pltpu.SemaphoreType.DMA((2,))],
    )
    return pl.pallas_call(
        functools.partial(_dispatch_kernel, sizes=_piece_sizes(tm)),
        out_shape=jax.ShapeDtypeStruct((n_sorted, d), F32),
        grid_spec=grid_spec,
        compiler_params=_cparams(("arbitrary",), VMEM_LIMIT),
        name="dispatch",
    )(seg_start, seg_len, misc, h, e01, rank01)


def _moe_kernel(te_ref, misc_ref, xs_ref, wg_hbm, wu_hbm, wd_hbm, ys_ref,
                wg_buf, wu_buf, wd_buf, wgb_ref, wub_ref, wdb_ref, run_ref, sems, *, layer, tm):
    i = pl.program_id(0)
    e = te_ref[i]
    prev = te_ref[jnp.maximum(i - 1, 0)]
    n_used = misc_ref[0]

    def fetch(expert, slot):
        return [pltpu.make_async_copy(w.at[layer, expert], buf.at[slot], sems.at[slot, k])
                for k, (w, buf) in enumerate(((wg_hbm, wg_buf), (wu_hbm, wu_buf), (wd_hbm, wd_buf)))]

    @pl.when(i == 0)
    def _():
        run_ref[0] = 0
        for cp in fetch(e, 0):
            cp.start()

    @pl.when(jnp.logical_or(i == 0, e != prev))
    def _():
        slot = run_ref[0] % 2
        for cp in fetch(e, slot):
            cp.wait()
        nxt = (misc_ref[1 + e] + misc_ref[1 + N_EXPERTS + e]) // tm

        @pl.when(nxt < n_used)
        def _():
            for cp in fetch(te_ref[nxt], 1 - slot):
                cp.start()

        wgb_ref[...] = wg_buf[slot].astype(BF16)
        wub_ref[...] = wu_buf[slot].astype(BF16)
        wdb_ref[...] = wd_buf[slot].astype(BF16)
        run_ref[0] = run_ref[0] + 1

    @pl.when(i < misc_ref[0])
    def _():
        x = xs_ref[...].astype(BF16)
        gate = jnp.dot(x, wgb_ref[...], preferred_element_type=F32)
        up = jnp.dot(x, wub_ref[...], preferred_element_type=F32)
        act = (_silu(gate) * up).astype(BF16)
        ys_ref[...] = jnp.dot(act, wdb_ref[...], preferred_element_type=F32)

    @pl.when(i >= misc_ref[0])
    def _():
        ys_ref[...] = jnp.zeros(ys_ref.shape, F32)


def _moe(tile_expert, misc, xs, wg, wu, wd, layer, tm):
    n_sorted, d = xs.shape
    de = wg.shape[-1]
    xmap = lambda i, te, misc_ref: (jnp.minimum(i, jnp.maximum(misc_ref[0] - 1, 0)), 0)
    hbm = pl.BlockSpec(memory_space=pl.ANY)
    grid_spec = pltpu.PrefetchScalarGridSpec(
        num_scalar_prefetch=2,
        grid=(n_sorted // tm,),
        in_specs=[pl.BlockSpec((tm, d), xmap), hbm, hbm, hbm],
        out_specs=pl.BlockSpec((tm, d), lambda i, te, misc_ref: (i, 0)),
        scratch_shapes=[pltpu.VMEM((2, d, de), F32), pltpu.VMEM((2, d, de), F32),
                        pltpu.VMEM((2, de, d), F32),
                        pltpu.VMEM((d, de), BF16), pltpu.VMEM((d, de), BF16),
                        pltpu.VMEM((de, d), BF16),
                        pltpu.SMEM((1,), I32), pltpu.SemaphoreType.DMA((2, 3))],
    )
    return pl.pallas_call(
        functools.partial(_moe_kernel, layer=layer, tm=tm),
        out_shape=jax.ShapeDtypeStruct((n_sorted, d), F32),
        grid_spec=grid_spec,
        compiler_params=_cparams(("arbitrary",), VMEM_LIMIT),
        name="moe_experts",
    )(tile_expert, misc, xs, wg, wu, wd)


def _combine_kernel(seg_start_ref, seg_len_ref, x_ref, g2_ref, e_ref, rank_ref, w_ref, fg_ref, ys_ref,
                    o_ref, yc_ref, sems, *, sizes, final_norm):
    tm = x_ref.shape[0]
    rc = yc_ref.shape[1]
    i = pl.program_id(0)
    buf = i % 2

    def segments(tile, b, start):
        lo = 0
        for e in range(N_EXPERTS):
            n8 = seg_len_ref[tile * N_EXPERTS + e]
            _segment_copies(ys_ref, seg_start_ref[tile * N_EXPERTS + e], yc_ref.at[b], lo, n8,
                            sems.at[b], sizes, start)
            lo = lo + n8

    @pl.when(i == 0)
    def _():
        yc_ref[...] = jnp.zeros(yc_ref.shape, F32)
        segments(0, 0, True)

    @pl.when(i + 1 < pl.num_programs(0))
    def _():
        segments(i + 1, 1 - buf, True)

    segments(i, buf, False)
    slot, _ = _local_slots(seg_len_ref, i, e_ref, rank_ref)
    rid = lax.broadcasted_iota(I32, (rc, tm), 0)
    w = w_ref[...]
    pw = (jnp.where(rid == slot[0:1, :], w[0:1, :], 0.0)
          + jnp.where(rid == slot[1:2, :], w[1:2, :], 0.0))
    yc = yc_ref[buf]
    p_hi = pw.astype(BF16)
    p_lo = (pw - p_hi.astype(F32)).astype(BF16)
    y_hi = yc.astype(BF16)
    y_lo = (yc - y_hi.astype(F32)).astype(BF16)
    dn = (((0,), (0,)), ((), ()))
    out = (lax.dot_general(p_hi, y_hi, dn, preferred_element_type=F32)
           + lax.dot_general(p_hi, y_lo, dn, preferred_element_type=F32)
           + lax.dot_general(p_lo, y_hi, dn, preferred_element_type=F32))
    x = x_ref[...] + g2_ref[0] * out
    if final_norm:
        x = x * lax.rsqrt(jnp.mean(x * x, axis=-1, keepdims=True) + EPS) * fg_ref[...]
    o_ref[...] = x


def _combine(seg_start, seg_len, xc, g2, e01, rank01, w01, fg, ys, *, n_rows, tm, tiles_per_seq,
             final_norm):
    d = xc.shape[1]
    nb = g2.shape[0] - 1
    pair = pl.BlockSpec((2, tm), lambda i, *_: (0, i))
    grid_spec = pltpu.PrefetchScalarGridSpec(
        num_scalar_prefetch=2,
        grid=(n_rows // tm,),
        in_specs=[pl.BlockSpec((tm, d), lambda i, *_: (i, 0)),
                  pl.BlockSpec((1, 1, d), lambda i, *_: (jnp.minimum(i // tiles_per_seq, nb), 0, 0)),
                  pair, pair, pair,
                  pl.BlockSpec((1, d), lambda i, *_: (0, 0)),
                  pl.BlockSpec(memory_space=pl.ANY)],
        out_specs=pl.BlockSpec((tm, d), lambda i, *_: (i, 0)),
        scratch_shapes=[pltpu.VMEM((2, _compact_rows(tm), d), F32), pltpu.SemaphoreType.DMA((2,))],
    )
    return pl.pallas_call(
        functools.partial(_combine_kernel, sizes=_piece_sizes(tm), final_norm=final_norm),
        out_shape=jax.ShapeDtypeStruct((n_rows, d), F32),
        grid_spec=grid_spec,
        compiler_params=_cparams(("arbitrary",), VMEM_LIMIT),
        name="combine",
    )(seg_start, seg_len, xc, g2, e01, rank01, w01, fg, ys)


def _lower_bounds(lb_param):
    p = jax.nn.softmax(lb_param.astype(F32), axis=0)
    return jnp.cumsum(p, axis=0) - p[0]


def kernel(x, c, ctx, c_ctx, w_ada, b_ada, norm1_g, norm2_g, w_in, conv_w, conv_b, conv_ln_g,
           conv_ln_b, lb_fwd, lb_bwd, hgrn_norm_g, w_out, router_w, router_bias, w_gate, w_up,
           w_down, final_norm_g):
    bn, seq, d = x.shape
    ctx_len = ctx.shape[1]
    depth = w_ada.shape[0]
    n = bn * seq
    nc = bn * ctx_len
    tm = min(512, seq, nc)
    assert seq % tm == 0 and nc % tm == 0 and seq % GRID_W == 0
    tps = seq // tm

    pad_rows = (-(bn + 1)) % 8
    cc = jnp.concatenate([c, c_ctx[None, :], jnp.zeros((pad_rows, d), F32)], axis=0)
    mod = _ada(cc, w_ada, b_ada)[:, :bn + 1]
    mod = mod.reshape(depth, bn + 1, 6, 1, d)

    lbs_f = _lower_bounds(lb_fwd)
    lbs_b = _lower_bounds(lb_bwd)
    w_in_bf = w_in.astype(BF16)
    w_out_bf = w_out.astype(BF16)
    rw_t = router_w.T.astype(F32)
    rb = router_bias.reshape(N_EXPERTS, 1).astype(F32)
    hg = hgrn_norm_g.astype(F32)
    fg = final_norm_g.reshape(1, d).astype(F32)

    rows_in = [x.reshape(n, d), ctx.reshape(nc, d)]
    s_zero = jnp.zeros((bn, 2 * HEADS, HEAD_DIM, HEAD_DIM), F32)

    for l in range(depth):
        last = l == depth - 1
        sh1, sc1, g1, sh2, sc2, g2 = (mod[l, :, k] for k in range(6))
        row = lambda a: a.reshape(1, -1).astype(F32)

        u, q, iv, lff, kf, lfb, kb, og = _inproj(
            rows_in, row(norm1_g[l]), sh1, sc1, w_in_bf, l, row(lbs_f[l]), row(lbs_b[l]), tm, tps)

        cw, cb = conv_w[l].astype(F32), row(conv_b[l])
        clg, clb = row(conv_ln_g[l]), row(conv_ln_b[l])
        scan = functools.partial(_scan, q, iv, lff, kf, lfb, kb)
        if last:
            (s_ctx,) = scan(s_zero, row_off=n, seq_len=ctx_len, compute_o=False)
            mix_ctx = [[], [], []]
            n_rows = n
        else:
            of_c, ob_c, s_ctx = scan(s_zero, row_off=n, seq_len=ctx_len, compute_o=True)
            ycx_c = _conv_rows(u, cw, cb, clg, clb, seq=ctx_len, row_off=n, n_rows=nc)
            mix_ctx = [[ycx_c], [of_c], [ob_c]]
            n_rows = n + nc
        of, ob, _ = scan(s_ctx, row_off=0, seq_len=seq, compute_o=True)
        if l % 2 == 0:
            ycx = _conv_rows(u, cw, cb, clg, clb, seq=GRID_W, row_off=0, n_rows=n)
        else:
            ycx = _conv_cols(u, cw, cb, clg, clb, batch=bn, seq_len=seq)
        mix = [[a] + c_ for a, c_ in zip((ycx, of, ob), mix_ctx)]

        res_rows = rows_in if (len(rows_in) == 1 or not last) else [rows_in[0]]
        xc = _outproj(res_rows, mix, og, hg[l].reshape(1, HEAD_DIM), w_out_bf, l, g1,
                      n_rows=n_rows, tm=tm, tiles_per_seq=tps)

        h2, e01, rank01, w01, cnt = _router(
            xc, row(norm2_g[l]), sh2, sc2, rw_t, rb, n_rows=n_rows, tm=tm, tiles_per_seq=tps)
        n_tt = n_rows // tm
        n_et = (2 * n_rows + n_tt * N_EXPERTS * (SUBLANES - 1) + tm - 1) // tm + N_EXPERTS
        seg_start, seg_len, te, misc = _tables(cnt[:, :, 0].reshape(-1), n_tt=n_tt, n_et=n_et, tm=tm)
        xs = _dispatch(seg_start, seg_len, misc, h2, e01, rank01, n_sorted=n_et * tm, tm=tm)
        ys = _moe(te, misc, xs, w_gate, w_up, w_down, l, tm)
        xc = _combine(seg_start, seg_len, xc, g2, e01, rank01, w01, fg, ys, n_rows=n_rows, tm=tm,
                      tiles_per_seq=tps, final_norm=last)
        rows_in = [xc]

    return xc[:n].reshape(bn, seq, d)
```

```python
import functools

import numpy as np
import jax
import jax.numpy as jnp
from jax import lax
from jax.experimental import pallas as pl
from jax.experimental.pallas import tpu as pltpu

F32 = jnp.float32
BF16 = jnp.bfloat16
I32 = jnp.int32

EPS = 1e-6
GRID_W = 64
HEADS = 4
HEAD_DIM = 128
D_GROUP = HEADS * HEAD_DIM
N_IN_GROUPS = 7
CONV_K = 31
CONV_HALF = CONV_K // 2
CONV_GAP = 16
CONV_ROW_GROUP = 1
CONV_PARTIALS = 2
N_EXPERTS = 16
N_GROUPS = 4
EXPERTS_PER_GROUP = N_EXPERTS // N_GROUPS
SCAN_CHUNK = 64
SCAN_FAST_LIMIT = 60.0
LANES = 128
SUBLANES = 8
VMEM_LIMIT = 56 * 1024 * 1024


def _cparams(sem, vmem=None):
    return pltpu.CompilerParams(dimension_semantics=sem, vmem_limit_bytes=vmem)


def _sigmoid(x):
    return 0.5 * jnp.tanh(0.5 * x) + 0.5


def _silu(x):
    return x * _sigmoid(x)


def _ada_kernel(c_ref, w_ref, b_ref, o_ref):
    a = _silu(c_ref[...]).astype(BF16)
    w = w_ref[0].astype(BF16)
    o_ref[0] = jnp.dot(a, w, preferred_element_type=F32) + b_ref[0]


def _ada(cc, w_ada, b_ada, tn=1536):
    depth, d, d6 = w_ada.shape
    rows = cc.shape[0]
    return pl.pallas_call(
        _ada_kernel,
        out_shape=jax.ShapeDtypeStruct((depth, rows, d6), F32),
        grid=(depth, d6 // tn),
        in_specs=[
            pl.BlockSpec((rows, d), lambda l, j: (0, 0)),
            pl.BlockSpec((1, d, tn), lambda l, j: (l, 0, j)),
            pl.BlockSpec((1, 1, tn), lambda l, j: (l, 0, j)),
        ],
        out_specs=pl.BlockSpec((1, rows, tn), lambda l, j: (l, 0, j)),
        compiler_params=_cparams(("arbitrary", "arbitrary"), VMEM_LIMIT),
        name="ada",
    )(cc, w_ada, b_ada.reshape(depth, 1, d6))


def _modulate(x, g, shift, scale):
    r = lax.rsqrt(jnp.mean(x * x, axis=-1, keepdims=True) + EPS)
    return (x * r * g) * (1.0 + scale) + shift


def _rows_from(refs, n_lat_tiles):
    if len(refs) == 1:
        return refs[0][...]
    return jnp.where(pl.program_id(0) >= n_lat_tiles, refs[1][...], refs[0][...])


def _row_specs(arrays, tm):
    width = arrays[0].shape[1]
    if len(arrays) == 1:
        return [pl.BlockSpec((tm, width), lambda i, *_: (i, 0))], 0
    n_lat_tiles = arrays[0].shape[0] // tm
    return [pl.BlockSpec((tm, width), lambda i, *_: (jnp.minimum(i, n_lat_tiles - 1), 0)),
            pl.BlockSpec((tm, width), lambda i, *_: (jnp.maximum(i - n_lat_tiles, 0), 0))], n_lat_tiles


def _inproj_kernel(*refs, n_src, n_lat_tiles):
    x = _rows_from(refs[:n_src], n_lat_tiles)
    (g_ref, sh_ref, sc_ref, w_ref, lbf_ref, lbb_ref,
     u_ref, q_ref, i_ref, lff_ref, kf_ref, lfb_ref, kb_ref, og_ref) = refs[n_src:]
    hb = _modulate(x, g_ref[...], sh_ref[0], sc_ref[0]).astype(BF16)

    def proj(j):
        return jnp.dot(hb, w_ref[0, :, j * D_GROUP:(j + 1) * D_GROUP], preferred_element_type=F32)

    u_ref[...] = (proj(0) * _sigmoid(proj(1))).astype(BF16)
    q_ref[...] = _silu(proj(2)).astype(BF16)
    i_ref[...] = proj(3).astype(BF16)
    for j, lb_ref, lf_ref, k_ref in ((4, lbf_ref, lff_ref, kf_ref), (5, lbb_ref, lfb_ref, kb_ref)):
        lb = lb_ref[...]
        f = lb + (1.0 - lb) * jax.nn.sigmoid(proj(j))
        lf_ref[...] = jnp.log(f)
        k_ref[...] = (1.0 - f).astype(BF16)
    og_ref[...] = _silu(proj(6)).astype(BF16)


def _inproj(xs, g, sh, sc, w_bf, layer, lbf, lbb, tm, tiles_per_seq):
    n = sum(a.shape[0] for a in xs)
    d = xs[0].shape[1]
    x_specs, n_lat_tiles = _row_specs(xs, tm)
    nb = sh.shape[0] - 1
    mod_spec = pl.BlockSpec((1, 1, d), lambda i: (jnp.minimum(i // tiles_per_seq, nb), 0, 0))
    row_spec = pl.BlockSpec((tm, D_GROUP), lambda i: (i, 0))
    vec_spec = pl.BlockSpec((1, D_GROUP), lambda i: (0, 0))
    outs = [jax.ShapeDtypeStruct((n, D_GROUP), dt)
            for dt in (BF16, BF16, BF16, F32, BF16, F32, BF16, BF16)]
    return pl.pallas_call(
        functools.partial(_inproj_kernel, n_src=len(xs), n_lat_tiles=n_lat_tiles),
        out_shape=outs,
        grid=(n // tm,),
        in_specs=x_specs + [
            pl.BlockSpec((1, d), lambda i: (0, 0)),
            mod_spec, mod_spec,
            pl.BlockSpec((1, d, N_IN_GROUPS * D_GROUP), lambda i: (layer, 0, 0)),
            vec_spec, vec_spec,
        ],
        out_specs=[row_spec] * 8,
        compiler_params=_cparams(("arbitrary",), VMEM_LIMIT),
        name="inproj",
    )(*xs, g, sh, sc, w_bf, lbf, lbb)


def _scan_tables(c, backward):
    levels = int(np.log2(c))
    r_all = np.zeros((levels * c + 2 * c + 8, c), np.float32)
    for lev in range(levels):
        h = 1 << lev
        for r in range(c):
            bd = (r // (2 * h)) * 2 * h + h
            if not backward:
                if r >= bd:
                    r_all[lev * c + r, bd:r + 1] = 1.0
                else:
                    r_all[lev * c + r, r + 1:bd] = 1.0
            else:
                if r < bd:
                    r_all[lev * c + r, r:bd] = 1.0
                else:
                    r_all[lev * c + r, bd:r] = 1.0
    base = levels * c
    for r in range(c):
        if not backward:
            r_all[base + r, :r + 1] = 1.0
            r_all[base + c + r, r + 1:] = 1.0
        else:
            r_all[base + r, r:] = 1.0
            r_all[base + c + r, :r] = 1.0
    r_all[base + 2 * c:, :] = 1.0
    return r_all


def _scan_kernel(*refs, n_chunks, compute_o):
    c = SCAN_CHUNK
    levels = int(np.log2(c))
    (rf_ref, rb_ref, qf_ref, vf_ref, lff_ref, kf_ref,
     qb_ref, vb_ref, lfb_ref, kb_ref, s0_ref, trif_ref, trib_ref) = refs[:13]
    pos = 13
    if compute_o:
        of_ref, ob_ref, sfin_ref, st_ref = refs[pos:pos + 4]
    else:
        sfin_ref, st_ref = refs[pos:pos + 2]
        of_ref = ob_ref = None
    j = pl.program_id(1)

    @pl.when(j == 0)
    def _():
        st_ref[...] = s0_ref[0]

    row = lax.broadcasted_iota(jnp.int32, (c, 1), 0)
    ri = lax.broadcasted_iota(jnp.int32, (c, c), 0)
    ci = lax.broadcasted_iota(jnp.int32, (c, c), 1)
    upper = [(row & (2 * (1 << lev) - 1)) >= (1 << lev) for lev in range(levels)]
    same_parent = [(ri >> (lev + 1)) == (ci >> (lev + 1)) for lev in range(levels)]
    diag = ri == ci
    dn_t = (((1,), (1,)), ((), ()))
    dn_tl = (((0,), (0,)), ((), ()))

    half = c // 2
    tri = [jnp.where(ci <= ri, 1.0, 0.0).astype(BF16), jnp.where(ci >= ri, 1.0, 0.0).astype(BF16)]
    first_half = [row < half, row >= half]
    re = lax.broadcasted_iota(jnp.int32, (c, 4 * c), 0)
    ce = lax.broadcasted_iota(jnp.int32, (c, 4 * c), 1)
    cs = ce & (c - 1)
    same_half = (re >= half) == (cs >= half)
    own = (ce & (2 * c - 1)) < c
    mask_ext = [
        (own & same_half & (cs <= re)) | (jnp.logical_not(own) & (re >= half) & (cs < half)),
        (own & same_half & (cs >= re)) | (jnp.logical_not(own) & (re < half) & (cs >= half)),
    ]

    mrow, trow = [half - 1, half], [c - 1, 0]

    def prefix(tri_mat, g):
        g_hi = g.astype(BF16)
        g_lo = (g - g_hi.astype(F32)).astype(BF16)
        return (jnp.dot(tri_mat, g_hi, preferred_element_type=F32)
                + jnp.dot(tri_mat, g_lo, preferred_element_type=F32))

    def next_state(st, hs, kh, vh, b, tot):
        kd = (kh * jnp.exp(tot[:, hs] - b[:, hs])).astype(BF16)
        return st * jnp.exp(tot[:, hs]) + lax.dot_general(vh, kd, dn_tl, preferred_element_type=F32)

    def state_only_chunk(dirn, lf_ref, k_ref, v_ref, r0):
        rows = pl.ds(r0, c)
        b = prefix(tri[dirn], lf_ref[rows, :])
        tot = b[trow[dirn]:trow[dirn] + 1, :]
        for h in range(HEADS):
            hs = slice(h * HEAD_DIM, (h + 1) * HEAD_DIM)
            st_ref[dirn * HEADS + h] = next_state(st_ref[dirn * HEADS + h], hs,
                                                  k_ref[rows, hs].astype(F32), v_ref[rows, hs], b, tot)

    def fast_chunk(dirn, q_ref, v_ref, k_ref, o_ref, rows, b, m, tot, states):
        fh = first_half[dirn]
        hd = HEAD_DIM

        def block_diag(a, bb):
            za = jnp.zeros((a.shape[0], bb.shape[1]), a.dtype)
            zb = jnp.zeros((bb.shape[0], a.shape[1]), a.dtype)
            return jnp.concatenate([jnp.concatenate([a, za], axis=1),
                                    jnp.concatenate([zb, bb], axis=1)], axis=0)

        for hp in range(HEADS // 2):
            ps = slice(2 * hp * hd, 2 * (hp + 1) * hd)
            bp, mp, tp = b[:, ps], m[:, ps], tot[:, ps]
            cdec = bp - jnp.where(fh, 0.0, mp)
            e_own = jnp.exp(cdec).astype(BF16)
            e_own_inv = jnp.exp(-cdec).astype(BF16)
            e_cross = jnp.where(fh, jnp.exp(jnp.minimum(mp - bp, 0.0)), 0.0).astype(BF16)
            e_inc = jnp.exp(bp).astype(BF16)
            e_dec = jnp.exp(tp - bp).astype(BF16)
            qp, kp, vp = q_ref[rows, ps], k_ref[rows, ps], v_ref[rows, ps]
            q1, qi = qp * e_own, qp * e_inc
            k1, k2, kd = kp * e_own_inv, kp * e_cross, kp * e_dec
            kcat = [jnp.concatenate([k1[:, s_], k2[:, s_]], axis=0) for s_ in (slice(0, hd), slice(hd, 2 * hd))]
            sc = lax.dot_general(q1, block_diag(*kcat), dn_t, preferred_element_type=F32)
            p = jnp.where(mask_ext[dirn], sc, 0.0).astype(BF16)
            v2 = [jnp.concatenate([vp[:, s_], vp[:, s_]], axis=0) for s_ in (slice(0, hd), slice(hd, 2 * hd))]
            st = [states[dirn * HEADS + 2 * hp + a] for a in range(2)]
            o = (jnp.dot(p, block_diag(*v2), preferred_element_type=F32)
                 + lax.dot_general(qi, block_diag(st[0].astype(BF16), st[1].astype(BF16)), dn_t,
                                   preferred_element_type=F32))
            o_ref[rows, ps] = o.astype(o_ref.dtype)
            upd = lax.dot_general(vp, kd, dn_tl, preferred_element_type=F32)
            e_tot = jnp.exp(tp)
            for a in range(2):
                sl = slice(a * hd, (a + 1) * hd)
                states[dirn * HEADS + 2 * hp + a] = st[a] * e_tot[:, sl] + upd[sl, sl]

    def robust_chunk(dirn, r_ref, q_ref, v_ref, lf_ref, k_ref, o_ref, r0):
        rows = pl.ds(r0, c)
        g = lf_ref[rows, :]
        g_hi = g.astype(BF16)
        g_lo = (g - g_hi.astype(F32)).astype(BF16)
        rmat = r_ref[...]
        e_all = jnp.exp(jnp.dot(rmat, g_hi, preferred_element_type=F32)
                        + jnp.dot(rmat, g_lo, preferred_element_type=F32))
        base = levels * c
        for h in range(HEADS):
            hs = slice(h * HEAD_DIM, (h + 1) * HEAD_DIM)
            qh = q_ref[rows, hs].astype(F32)
            kh = k_ref[rows, hs].astype(F32)
            vh = v_ref[rows, hs]
            st = st_ref[dirn * HEADS + h]
            if compute_o:
                att = jnp.where(diag, lax.dot_general(qh.astype(BF16), kh.astype(BF16), dn_t,
                                                      preferred_element_type=F32), 0.0)
                for lev in range(levels):
                    e_l = e_all[lev * c:(lev + 1) * c, hs]
                    q_side = upper[lev] if dirn == 0 else jnp.logical_not(upper[lev])
                    qt = jnp.where(q_side, qh * e_l, 0.0).astype(BF16)
                    kt = jnp.where(q_side, 0.0, kh * e_l).astype(BF16)
                    a_l = lax.dot_general(qt, kt, dn_t, preferred_element_type=F32)
                    att = att + jnp.where(same_parent[lev], a_l, 0.0)
                qi = (qh * e_all[base:base + c, hs]).astype(BF16)
                o = lax.dot_general(qi, st.astype(BF16), dn_t, preferred_element_type=F32)
                o = o + jnp.dot(att.astype(BF16), vh, preferred_element_type=F32)
                o_ref[rows, hs] = o.astype(o_ref.dtype)
            kd = (kh * e_all[base + c:base + 2 * c, hs]).astype(BF16)
            e_tot = e_all[base + 2 * c:base + 2 * c + 1, hs]
            st_ref[dirn * HEADS + h] = st * e_tot + lax.dot_general(
                vh, kd, dn_tl, preferred_element_type=F32)

    dir_refs = ((rf_ref, qf_ref, vf_ref, lff_ref, kf_ref, of_ref),
                (rb_ref, qb_ref, vb_ref, lfb_ref, kb_ref, ob_ref))

    def chunk_rows(ci_):
        return pl.multiple_of(ci_ * c, c), pl.multiple_of((n_chunks - 1 - ci_) * c, c)

    if not compute_o:
        def state_body(ci_, carry):
            for dirn, r0 in enumerate(chunk_rows(ci_)):
                _, _, v_ref, lf_ref, k_ref, _ = dir_refs[dirn]
                state_only_chunk(dirn, lf_ref, k_ref, v_ref, r0)
            return carry

        lax.fori_loop(0, n_chunks, state_body, 0)
    else:
        b_blk, stats, worst = [], {}, None
        for dirn in range(2):
            b_blk.append(prefix((trif_ref, trib_ref)[dirn][...], dir_refs[dirn][3][...]))
            for ch in range(n_chunks):
                m = b_blk[dirn][ch * c + mrow[dirn]:ch * c + mrow[dirn] + 1, :]
                tot = b_blk[dirn][ch * c + trow[dirn]:ch * c + trow[dirn] + 1, :]
                stats[dirn, ch] = (m, tot)
                w = jnp.minimum(m, tot - m)
                worst = w if worst is None else jnp.minimum(worst, w)
        fast_ok = jnp.min(worst) >= -SCAN_FAST_LIMIT

        @pl.when(fast_ok)
        def _():
            states = [st_ref[i] for i in range(2 * HEADS)]
            for step in range(n_chunks):
                for dirn in range(2):
                    ch = step if dirn == 0 else n_chunks - 1 - step
                    _, q_ref, v_ref, _, k_ref, o_ref = dir_refs[dirn]
                    rows = slice(ch * c, (ch + 1) * c)
                    fast_chunk(dirn, q_ref, v_ref, k_ref, o_ref, rows, b_blk[dirn][rows, :],
                               *stats[dirn, ch], states)
            for i in range(2 * HEADS):
                st_ref[i] = states[i]

        @pl.when(jnp.logical_not(fast_ok))
        def _():
            def robust_body(ci_, carry):
                for dirn, r0 in enumerate(chunk_rows(ci_)):
                    r_ref, q_ref, v_ref, lf_ref, k_ref, o_ref = dir_refs[dirn]
                    robust_chunk(dirn, r_ref, q_ref, v_ref, lf_ref, k_ref, o_ref, r0)
                return carry

            lax.fori_loop(0, n_chunks, robust_body, 0)

    @pl.when(j == pl.num_programs(1) - 1)
    def _():
        sfin_ref[0] = st_ref[...]


def _scan(q, v, lff, kf, lfb, kb, s0, *, row_off, seq_len, compute_o):
    batch = s0.shape[0]
    t = min(256, seq_len)
    nt = seq_len // t
    off = row_off // t
    fwd = lambda b, j: (off + b * nt + j, 0)
    bwd = lambda b, j: (off + b * nt + (nt - 1 - j), 0)
    ofwd = lambda b, j: (b * nt + j, 0)
    obwd = lambda b, j: (b * nt + (nt - 1 - j), 0)
    blk = lambda im: pl.BlockSpec((t, D_GROUP), im)
    rf = jnp.asarray(_scan_tables(SCAN_CHUNK, False), BF16)
    rb = jnp.asarray(_scan_tables(SCAN_CHUNK, True), BF16)
    rspec = pl.BlockSpec(rf.shape, lambda b, j: (0, 0))
    sspec = pl.BlockSpec((1, 2 * HEADS, HEAD_DIM, HEAD_DIM), lambda b, j: (b, 0, 0, 0))
    pos_t = np.arange(t)
    same_chunk = (pos_t[:, None] // SCAN_CHUNK) == (pos_t[None, :] // SCAN_CHUNK)
    tri_f = jnp.asarray(same_chunk & (pos_t[None, :] <= pos_t[:, None]), BF16)
    tri_b = jnp.asarray(same_chunk & (pos_t[None, :] >= pos_t[:, None]), BF16)
    tspec = pl.BlockSpec((t, t), lambda b, j: (0, 0))
    in_specs = [rspec, rspec, blk(fwd), blk(fwd), blk(fwd), blk(fwd),
                blk(bwd), blk(bwd), blk(bwd), blk(bwd), sspec, tspec, tspec]
    args = [rf, rb, q, v, lff, kf, q, v, lfb, kb, s0, tri_f, tri_b]
    out_shape, out_specs = [], []
    if compute_o:
        out_shape += [jax.ShapeDtypeStruct((batch * seq_len, D_GROUP), BF16)] * 2
        out_specs += [blk(ofwd), blk(obwd)]
    out_shape.append(jax.ShapeDtypeStruct(s0.shape, F32))
    out_specs.append(sspec)
    kern = functools.partial(_scan_kernel, n_chunks=t // SCAN_CHUNK, compute_o=compute_o)
    return pl.pallas_call(
        kern,
        out_shape=out_shape,
        grid=(batch, nt),
        in_specs=in_specs,
        out_specs=out_specs,
        scratch_shapes=[pltpu.VMEM((2 * HEADS, HEAD_DIM, HEAD_DIM), F32)],
        compiler_params=_cparams(("arbitrary", "arbitrary"), VMEM_LIMIT),
        name="hgrn_scan",
    )(*args)


def _ln_silu(y, g, b):
    mu = jnp.mean(y, axis=-1, keepdims=True)
    yc = y - mu
    var = jnp.mean(yc * yc, axis=-1, keepdims=True)
    return _silu(yc * lax.rsqrt(var + EPS) * g + b)


def _tree_sum(terms):
    while len(terms) > 1:
        terms = [terms[i] + terms[i + 1] if i + 1 < len(terms) else terms[i]
                 for i in range(0, len(terms), 2)]
    return terms[0]


def _conv_rows_kernel(u_ref, w_ref, b_ref, lg_ref, lb_ref, y_ref, pad_ref, acc_ref, *, seq, tb):
    nseq = tb // seq
    stride = seq + CONV_GAP
    n_pad = pad_ref.shape[2]
    n_lt = D_GROUP // LANES
    for cl in range(n_lt):
        ls = slice(cl * LANES, (cl + 1) * LANES)
        pad_ref[0, cl] = jnp.zeros(pad_ref.shape[2:], F32)
        for s in range(nseq):
            pad_ref[0, cl, CONV_GAP + s * stride:CONV_GAP + s * stride + seq, :] = (
                u_ref[s * seq:(s + 1) * seq, ls].astype(F32))
        for p in range(1, SUBLANES):
            pad_ref[p, cl, 0:n_pad - SUBLANES, :] = pad_ref[0, cl, p:p + n_pad - SUBLANES, :]
    rc = 64
    group = CONV_ROW_GROUP
    assert seq % rc == 0 and tb % (group * rc) == 0
    if seq == rc:
        member_off = [a * stride for a in range(group)]
        group_base = lambda g: g * (group * stride)
    else:
        gps = seq // (group * rc)
        assert seq % (group * rc) == 0
        member_off = [a * rc for a in range(group)]
        group_base = lambda g: (g // gps) * stride + (g % gps) * (group * rc)

    def one_group(g, carry):
        pad0 = pl.multiple_of(CONV_GAP + group_base(g), SUBLANES)
        out0 = pl.multiple_of(g * (group * rc), group * rc)
        for cl in range(n_lt):
            ls = slice(cl * LANES, (cl + 1) * LANES)
            parts = [[None] * CONV_PARTIALS for _ in range(group)]
            for k in range(CONV_K):
                off = k - CONV_HALF
                wk = w_ref[k:k + 1, ls]
                for a in range(group):
                    src = pl.ds(pad0 + member_off[a] + (off // SUBLANES) * SUBLANES, rc)
                    term = wk * pad_ref[off % SUBLANES, cl, src, :]
                    prev = parts[a][k % CONV_PARTIALS]
                    parts[a][k % CONV_PARTIALS] = term if prev is None else prev + term
            for a in range(group):
                acc_ref[pl.ds(out0 + a * rc, rc), ls] = _tree_sum(parts[a])
        return carry

    lax.fori_loop(0, tb // (group * rc), one_group, 0)
    y = acc_ref[...] + b_ref[...]
    y_ref[...] = _ln_silu(y, lg_ref[...], lb_ref[...]).astype(y_ref.dtype)


def _conv_rows(u, w, b, lg, lb, *, seq, row_off, n_rows):
    tb = max(seq, 512) if n_rows % max(seq, 512) == 0 else seq
    nseq = tb // seq
    off = row_off // tb
    vec = pl.BlockSpec((1, D_GROUP), lambda i: (0, 0))
    in_specs = [pl.BlockSpec((tb, D_GROUP), lambda i: (off + i, 0)),
                pl.BlockSpec((CONV_K, D_GROUP), lambda i: (0, 0)), vec, vec, vec]
    args = [u, w, b, lg, lb]
    kern = functools.partial(_conv_rows_kernel, seq=seq, tb=tb)
    return pl.pallas_call(
        kern,
        out_shape=jax.ShapeDtypeStruct((n_rows, D_GROUP), BF16),
        grid=(n_rows // tb,),
        in_specs=in_specs,
        out_specs=pl.BlockSpec((tb, D_GROUP), lambda i: (i, 0)),
        scratch_shapes=[pltpu.VMEM((SUBLANES, D_GROUP // LANES, CONV_GAP + nseq * (seq + CONV_GAP), LANES), F32),
                        pltpu.VMEM((tb, D_GROUP), F32)],
        compiler_params=_cparams(("arbitrary",), VMEM_LIMIT),
        name="conv_rows",
    )(*args)


CONV_COL_TILE = 16


def _conv_cols_kernel(u_ref, w_ref, b_ref, lg_ref, lb_ref, y_ref, uf_ref, *, n_rows):
    n_lt = D_GROUP // LANES
    row_stride = GRID_W + SUBLANES
    for cl in range(n_lt):
        ls = slice(cl * LANES, (cl + 1) * LANES)
        for r in range(n_rows):
            uf_ref[cl, r * row_stride:r * row_stride + GRID_W, :] = (
                u_ref[r * GRID_W:(r + 1) * GRID_W, ls].astype(F32))
    bias = b_ref[...]
    lg = lg_ref[...]
    lb = lb_ref[...]

    def col_tile(wi, carry):
        w0 = pl.multiple_of(wi * CONV_COL_TILE, CONV_COL_TILE)
        for r in range(n_rows):
            tiles = []
            for cl in range(n_lt):
                ls = slice(cl * LANES, (cl + 1) * LANES)
                parts = [None] * CONV_PARTIALS
                for k in range(CONV_K):
                    rr = r + k - CONV_HALF
                    if 0 <= rr < n_rows:
                        src = pl.ds(pl.multiple_of(rr * row_stride + w0, SUBLANES), CONV_COL_TILE)
                        term = w_ref[k:k + 1, ls] * uf_ref[cl, src, :]
                        prev = parts[k % CONV_PARTIALS]
                        parts[k % CONV_PARTIALS] = term if prev is None else prev + term
                tiles.append(_tree_sum([p for p in parts if p is not None]))
            dst = pl.ds(pl.multiple_of(r * GRID_W + w0, CONV_COL_TILE), CONV_COL_TILE)
            y_ref[dst, :] = _ln_silu(jnp.concatenate(tiles, axis=1) + bias, lg, lb).astype(y_ref.dtype)
        return carry

    lax.fori_loop(0, GRID_W // CONV_COL_TILE, col_tile, 0)


def _conv_cols(u, w, b, lg, lb, *, batch, seq_len):
    vec = pl.BlockSpec((1, D_GROUP), lambda bi: (0, 0))
    blk = pl.BlockSpec((seq_len, D_GROUP), lambda bi: (bi, 0))
    return pl.pallas_call(
        functools.partial(_conv_cols_kernel, n_rows=seq_len // GRID_W),
        out_shape=jax.ShapeDtypeStruct((batch * seq_len, D_GROUP), BF16),
        grid=(batch,),
        in_specs=[blk, pl.BlockSpec((CONV_K, D_GROUP), lambda bi: (0, 0)), vec, vec, vec],
        out_specs=blk,
        scratch_shapes=[pltpu.VMEM((D_GROUP // LANES, (seq_len // GRID_W) * (GRID_W + SUBLANES), LANES), F32)],
        compiler_params=_cparams(("arbitrary",), VMEM_LIMIT),
        name="conv_cols",
    )(u, w, b, lg, lb)


def _outproj_kernel(*refs, n_x, n_mix, n_lat_tiles):
    x = _rows_from(refs[:n_x], n_lat_tiles)
    pos = n_x
    ycx, of, ob = (_rows_from(refs[pos + k * n_mix:pos + (k + 1) * n_mix], n_lat_tiles) for k in range(3))
    og_ref, hg_ref, w_ref, g1_ref, o_ref = refs[-5:]
    o = of.astype(F32) + ob.astype(F32)
    og = og_ref[...].astype(F32)
    hg = hg_ref[...]
    acc = jnp.dot(ycx, w_ref[0, 0:D_GROUP, :], preferred_element_type=F32)
    for h in range(HEADS):
        hs = slice(h * HEAD_DIM, (h + 1) * HEAD_DIM)
        oh = o[:, hs]
        r = lax.rsqrt(jnp.mean(oh * oh, axis=-1, keepdims=True) + EPS)
        yh = ((oh * r * hg) * og[:, hs]).astype(BF16)
        acc = acc + jnp.dot(yh, w_ref[0, D_GROUP + h * HEAD_DIM:D_GROUP + (h + 1) * HEAD_DIM, :],
                            preferred_element_type=F32)
    o_ref[...] = x + g1_ref[0] * acc


def _outproj(xs, mix, og, hg, w_bf, layer, g1, *, n_rows, tm, tiles_per_seq):
    d = xs[0].shape[1]
    nb = g1.shape[0] - 1
    x_specs, n_lat_tiles = _row_specs(xs, tm)
    mix_specs, mix_args = [], []
    for arrays in mix:
        specs, nl = _row_specs(arrays, tm)
        assert len(arrays) == len(mix[0]) and (nl == 0 or n_lat_tiles in (0, nl))
        n_lat_tiles = max(n_lat_tiles, nl)
        mix_specs += specs
        mix_args += list(arrays)
    row = lambda width: pl.BlockSpec((tm, width), lambda i: (i, 0))
    kern = functools.partial(_outproj_kernel, n_x=len(xs), n_mix=len(mix[0]), n_lat_tiles=n_lat_tiles)
    return pl.pallas_call(
        kern,
        out_shape=jax.ShapeDtypeStruct((n_rows, d), F32),
        grid=(n_rows // tm,),
        in_specs=x_specs + mix_specs + [row(D_GROUP),
                  pl.BlockSpec((1, HEAD_DIM), lambda i: (0, 0)),
                  pl.BlockSpec((1,) + w_bf.shape[1:], lambda i: (layer, 0, 0)),
                  pl.BlockSpec((1, 1, d), lambda i: (jnp.minimum(i // tiles_per_seq, nb), 0, 0))],
        out_specs=row(d),
        compiler_params=_cparams(("arbitrary",), VMEM_LIMIT),
        name="outproj",
    )(*xs, *mix_args, og, hg, w_bf, g1)


def _router_kernel(x_ref, g_ref, sh_ref, sc_ref, rw_ref, rb_ref,
                   h_ref, e_ref, rank_ref, w_ref, cnt_ref):
    tm = x_ref.shape[0]
    h = _modulate(x_ref[...], g_ref[...], sh_ref[0], sc_ref[0])
    h_ref[...] = h.astype(h_ref.dtype)
    logits = lax.dot_general(rw_ref[...], h, (((1,), (1,)), ((), ())),
                             precision=lax.Precision.HIGHEST, preferred_element_type=F32)
    s = jax.nn.sigmoid(logits)
    sb = s + rb_ref[...]
    s_rows = [s[e:e + 1, :] for e in range(N_EXPERTS)]
    sb_rows = [sb[e:e + 1, :] for e in range(N_EXPERTS)]

    def group_score(g):
        v = sb_rows[g * EXPERTS_PER_GROUP:(g + 1) * EXPERTS_PER_GROUP]
        best = None
        for a in range(EXPERTS_PER_GROUP):
            for b in range(a + 1, EXPERTS_PER_GROUP):
                p = v[a] + v[b]
                best = p if best is None else jnp.maximum(best, p)
        return best

    cur = group_score(0)
    best_g = jnp.zeros(cur.shape, I32)
    for g in range(1, N_GROUPS):
        gs = group_score(g)
        upd = gs > cur
        best_g = jnp.where(upd, g, best_g)
        cur = jnp.where(upd, gs, cur)

    def pick(rows, jdx):
        out = rows[jdx]
        for g in range(1, N_GROUPS):
            out = jnp.where(best_g == g, rows[g * EXPERTS_PER_GROUP + jdx], out)
        return out

    vb = [pick(sb_rows, jdx) for jdx in range(EXPERTS_PER_GROUP)]
    vs = [pick(s_rows, jdx) for jdx in range(EXPERTS_PER_GROUP)]

    def first_argmax(vals):
        m = vals[0]
        for v in vals[1:]:
            m = jnp.maximum(m, v)
        idx = jnp.full(m.shape, EXPERTS_PER_GROUP - 1, I32)
        for jdx in range(EXPERTS_PER_GROUP - 2, -1, -1):
            idx = jnp.where(vals[jdx] == m, jdx, idx)
        return idx

    def take(vals, idx):
        out = vals[EXPERTS_PER_GROUP - 1]
        for jdx in range(EXPERTS_PER_GROUP - 2, -1, -1):
            out = jnp.where(idx == jdx, vals[jdx], out)
        return out

    i0 = first_argmax(vb)
    vb2 = [jnp.where(i0 == jdx, -jnp.inf, vb[jdx]) for jdx in range(EXPERTS_PER_GROUP)]
    i1 = first_argmax(vb2)
    s0 = take(vs, i0)
    s1 = take(vs, i1)
    tot = s0 + s1
    w_ref[0:1, :] = s0 / tot
    w_ref[1:2, :] = s1 / tot
    e0 = best_g * EXPERTS_PER_GROUP + i0
    e1 = best_g * EXPERTS_PER_GROUP + i1
    e_ref[0:1, :] = e0
    e_ref[1:2, :] = e1

    eid = lax.broadcasted_iota(I32, (N_EXPERTS, tm), 0)
    hit0 = eid == e0
    hit1 = eid == e1
    onehot = jnp.where(jnp.logical_or(hit0, hit1), 1.0, 0.0)
    ti = lax.broadcasted_iota(I32, (tm, tm), 0)
    tj = lax.broadcasted_iota(I32, (tm, tm), 1)
    before = jnp.where(ti < tj, 1.0, 0.0).astype(BF16)
    rank = jnp.dot(onehot.astype(BF16), before, preferred_element_type=F32)
    rank_ref[0:1, :] = jnp.sum(jnp.where(hit0, rank, 0.0), axis=0, keepdims=True).astype(I32)
    rank_ref[1:2, :] = jnp.sum(jnp.where(hit1, rank, 0.0), axis=0, keepdims=True).astype(I32)
    cnt = jnp.sum(onehot, axis=1, keepdims=True).astype(I32)
    cnt_ref[0] = jnp.broadcast_to(cnt, (N_EXPERTS, LANES))


def _router(xc, g, sh, sc, rw_t, rb, *, n_rows, tm, tiles_per_seq):
    d = xc.shape[1]
    nb = sh.shape[0] - 1
    n_tt = n_rows // tm
    mod_spec = pl.BlockSpec((1, 1, d), lambda i: (jnp.minimum(i // tiles_per_seq, nb), 0, 0))
    pair = pl.BlockSpec((2, tm), lambda i: (0, i))
    return pl.pallas_call(
        _router_kernel,
        out_shape=[jax.ShapeDtypeStruct((n_rows, d), BF16),
                   jax.ShapeDtypeStruct((2, n_rows), I32),
                   jax.ShapeDtypeStruct((2, n_rows), I32),
                   jax.ShapeDtypeStruct((2, n_rows), F32),
                   jax.ShapeDtypeStruct((n_tt, N_EXPERTS, LANES), I32)],
        grid=(n_tt,),
        in_specs=[pl.BlockSpec((tm, d), lambda i: (i, 0)),
                  pl.BlockSpec((1, d), lambda i: (0, 0)),
                  mod_spec, mod_spec,
                  pl.BlockSpec((N_EXPERTS, d), lambda i: (0, 0)),
                  pl.BlockSpec((N_EXPERTS, 1), lambda i: (0, 0))],
        out_specs=[pl.BlockSpec((tm, d), lambda i: (i, 0)), pair, pair, pair,
                   pl.BlockSpec((1, N_EXPERTS, LANES), lambda i: (i, 0, 0))],
        compiler_params=_cparams(("arbitrary",), VMEM_LIMIT),
        name="router",
    )(xc, g, sh, sc, rw_t, rb)


def _tables_kernel(cnt_ref, seg_start_ref, seg_len_ref, te_ref, misc_ref, *, n_tt, n_et, tm):
    def expert(e, row0):
        def tile(i, pos):
            n = cnt_ref[i * N_EXPERTS + e]
            n8 = ((n + SUBLANES - 1) // SUBLANES) * SUBLANES
            seg_start_ref[i * N_EXPERTS + e] = pos
            seg_len_ref[i * N_EXPERTS + e] = n8
            return pos + n8

        end = lax.fori_loop(0, n_tt, tile, row0)
        padded_end = row0 + ((end - row0 + tm - 1) // tm) * tm
        misc_ref[1 + e] = end
        misc_ref[1 + N_EXPERTS + e] = padded_end - end

        def mark(j, carry):
            te_ref[j] = e
            return carry

        lax.fori_loop(row0 // tm, padded_end // tm, mark, 0)
        return padded_end

    total = row0 = 0
    for e in range(N_EXPERTS):
        row0 = expert(e, row0)
    total = row0
    n_used = total // tm
    misc_ref[0] = n_used
    last = te_ref[jnp.maximum(n_used - 1, 0)]

    def fill(j, carry):
        te_ref[j] = last
        return carry

    lax.fori_loop(n_used, n_et, fill, 0)


def _tables(cnt, *, n_tt, n_et, tm):
    smem = pl.BlockSpec(memory_space=pltpu.SMEM)
    return pl.pallas_call(
        functools.partial(_tables_kernel, n_tt=n_tt, n_et=n_et, tm=tm),
        out_shape=[jax.ShapeDtypeStruct((n_tt * N_EXPERTS,), I32),
                   jax.ShapeDtypeStruct((n_tt * N_EXPERTS,), I32),
                   jax.ShapeDtypeStruct((n_et,), I32),
                   jax.ShapeDtypeStruct((1 + 2 * N_EXPERTS,), I32)],
        in_specs=[smem],
        out_specs=[smem, smem, smem, smem],
        name="route_tables",
    )(cnt)


def _piece_sizes(max_rows):
    sizes, s = [], SUBLANES
    while s <= max_rows:
        sizes.append(s)
        s *= 2
    return sizes[::-1]


def _segment_copies(src_ref, src_row, dst_ref, dst_row, n_rows, sem, sizes, start):
    aligned = lambda r: r if isinstance(r, int) else pl.multiple_of(r, SUBLANES)
    for sz in sizes:
        src = pl.ds(aligned(src_row), sz)
        dst = pl.ds(aligned(dst_row), sz)

        @pl.when((n_rows & sz) != 0)
        def _(src=src, dst=dst):
            cp = pltpu.make_async_copy(src_ref.at[src, :], dst_ref.at[dst, :], sem)
            if start:
                cp.start()
            else:
                cp.wait()

        src_row = src_row + (n_rows & sz)
        dst_row = dst_row + (n_rows & sz)


def _local_slots(seg_len_ref, tile, e_ref, rank_ref):
    offs, lo = [], 0
    for e in range(N_EXPERTS):
        offs.append(lo)
        lo = lo + seg_len_ref[tile * N_EXPERTS + e]
    e01 = e_ref[...]
    slot = rank_ref[...]
    for e in range(N_EXPERTS):
        slot = slot + jnp.where(e01 == e, offs[e], 0)
    return slot, offs


def _compact_rows(tm):
    return ((2 * tm + N_EXPERTS * (SUBLANES - 1) + LANES - 1) // LANES) * LANES


def _dispatch_kernel(seg_start_ref, seg_len_ref, misc_ref, h_ref, e_ref, rank_ref, xs_ref,
                     xc_ref, zero_ref, sems, *, sizes):
    tm = h_ref.shape[0]
    rc = xc_ref.shape[1]
    i = pl.program_id(0)
    buf = i % 2
    slot, _ = _local_slots(seg_len_ref, i, e_ref, rank_ref)
    rid = lax.broadcasted_iota(I32, (rc, tm), 0)
    sel = jnp.logical_or(rid == slot[0:1, :], rid == slot[1:2, :])
    perm = jnp.where(sel, 1.0, 0.0).astype(BF16)
    xc_ref[buf] = jnp.dot(perm, h_ref[...], preferred_element_type=F32)

    def segments(tile, b, start):
        lo = 0
        for e in range(N_EXPERTS):
            n8 = seg_len_ref[tile * N_EXPERTS + e]
            _segment_copies(xc_ref.at[b], lo, xs_ref, seg_start_ref[tile * N_EXPERTS + e], n8,
                            sems.at[b], sizes, start)
            lo = lo + n8

    @pl.when(i > 0)
    def _():
        segments(i - 1, 1 - buf, False)

    segments(i, buf, True)

    @pl.when(i == pl.num_programs(0) - 1)
    def _():
        segments(i, buf, False)
        zero_ref[...] = jnp.zeros(zero_ref.shape, F32)
        for start in (True, False):
            for e in range(N_EXPERTS):
                _segment_copies(zero_ref, 0, xs_ref, misc_ref[1 + e], misc_ref[1 + N_EXPERTS + e],
                                sems.at[0], sizes, start)

        def clear_tile(j, carry):
            cp = pltpu.make_async_copy(zero_ref, xs_ref.at[pl.ds(pl.multiple_of(j * tm, tm), tm), :],
                                       sems.at[0])
            cp.start()
            cp.wait()
            return carry

        lax.fori_loop(misc_ref[0], xs_ref.shape[0] // tm, clear_tile, 0)


def _dispatch(seg_start, seg_len, misc, h, e01, rank01, *, n_sorted, tm):
    n, d = h.shape
    pair = pl.BlockSpec((2, tm), lambda i, *_: (0, i))
    grid_spec = pltpu.PrefetchScalarGridSpec(
        num_scalar_prefetch=3,
        grid=(n // tm,),
        in_specs=[pl.BlockSpec((tm, d), lambda i, *_: (i, 0)), pair, pair],
        out_specs=pl.BlockSpec(memory_space=pl.ANY),
        scratch_shapes=[pltpu.VMEM((2, _compact_rows(tm), d), F32), pltpu.VMEM((tm, d), F32),
                        pltpu.SemaphoreType.DMA((2,))],
    )
    return pl.pallas_call(
        functools.partial(_dispatch_kernel, sizes=_piece_sizes(tm)),
        out_shape=jax.ShapeDtypeStruct((n_sorted, d), F32),
        grid_spec=grid_spec,
        compiler_params=_cparams(("arbitrary",), VMEM_LIMIT),
        name="dispatch",
    )(seg_start, seg_len, misc, h, e01, rank01)


def _moe_kernel(te_ref, misc_ref, xs_ref, wg_hbm, wu_hbm, wd_hbm, ys_ref,
                wg_buf, wu_buf, wd_buf, wgb_ref, wub_ref, wdb_ref, run_ref, sems, *, layer, tm):
    i = pl.program_id(0)
    e = te_ref[i]
    prev = te_ref[jnp.maximum(i - 1, 0)]
    n_used = misc_ref[0]

    def fetch(expert, slot):
        return [pltpu.make_async_copy(w.at[layer, expert], buf.at[slot], sems.at[slot, k])
                for k, (w, buf) in enumerate(((wg_hbm, wg_buf), (wu_hbm, wu_buf), (wd_hbm, wd_buf)))]

    @pl.when(i == 0)
    def _():
        run_ref[0] = 0
        for cp in fetch(e, 0):
            cp.start()

    @pl.when(jnp.logical_or(i == 0, e != prev))
    def _():
        slot = run_ref[0] % 2
        for cp in fetch(e, slot):
            cp.wait()
        nxt = (misc_ref[1 + e] + misc_ref[1 + N_EXPERTS + e]) // tm

        @pl.when(nxt < n_used)
        def _():
            for cp in fetch(te_ref[nxt], 1 - slot):
                cp.start()

        wgb_ref[...] = wg_buf[slot].astype(BF16)
        wub_ref[...] = wu_buf[slot].astype(BF16)
        wdb_ref[...] = wd_buf[slot].astype(BF16)
        run_ref[0] = run_ref[0] + 1

    @pl.when(i < misc_ref[0])
    def _():
        x = xs_ref[...].astype(BF16)
        gate = jnp.dot(x, wgb_ref[...], preferred_element_type=F32)
        up = jnp.dot(x, wub_ref[...], preferred_element_type=F32)
        act = (_silu(gate) * up).astype(BF16)
        ys_ref[...] = jnp.dot(act, wdb_ref[...], preferred_element_type=F32)

    @pl.when(i >= misc_ref[0])
    def _():
        ys_ref[...] = jnp.zeros(ys_ref.shape, F32)


def _moe(tile_expert, misc, xs, wg, wu, wd, layer, tm):
    n_sorted, d = xs.shape
    de = wg.shape[-1]
    xmap = lambda i, te, misc_ref: (jnp.minimum(i, jnp.maximum(misc_ref[0] - 1, 0)), 0)
    hbm = pl.BlockSpec(memory_space=pl.ANY)
    grid_spec = pltpu.PrefetchScalarGridSpec(
        num_scalar_prefetch=2,
        grid=(n_sorted // tm,),
        in_specs=[pl.BlockSpec((tm, d), xmap), hbm, hbm, hbm],
        out_specs=pl.BlockSpec((tm, d), lambda i, te, misc_ref: (i, 0)),
        scratch_shapes=[pltpu.VMEM((2, d, de), F32), pltpu.VMEM((2, d, de), F32),
                        pltpu.VMEM((2, de, d), F32),
                        pltpu.VMEM((d, de), BF16), pltpu.VMEM((d, de), BF16),
                        pltpu.VMEM((de, d), BF16),
                        pltpu.SMEM((1,), I32), pltpu.SemaphoreType.DMA((2, 3))],
    )
    return pl.pallas_call(
        functools.partial(_moe_kernel, layer=layer, tm=tm),
        out_shape=jax.ShapeDtypeStruct((n_sorted, d), F32),
        grid_spec=grid_spec,
        compiler_params=_cparams(("arbitrary",), VMEM_LIMIT),
        name="moe_experts",
    )(tile_expert, misc, xs, wg, wu, wd)


def _combine_kernel(seg_start_ref, seg_len_ref, x_ref, g2_ref, e_ref, rank_ref, w_ref, fg_ref, ys_ref,
                    o_ref, yc_ref, sems, *, sizes, final_norm):
    tm = x_ref.shape[0]
    rc = yc_ref.shape[1]
    i = pl.program_id(0)
    buf = i % 2

    def segments(tile, b, start):
        lo = 0
        for e in range(N_EXPERTS):
            n8 = seg_len_ref[tile * N_EXPERTS + e]
            _segment_copies(ys_ref, seg_start_ref[tile * N_EXPERTS + e], yc_ref.at[b], lo, n8,
                            sems.at[b], sizes, start)
            lo = lo + n8

    @pl.when(i == 0)
    def _():
        yc_ref[...] = jnp.zeros(yc_ref.shape, F32)
        segments(0, 0, True)

    @pl.when(i + 1 < pl.num_programs(0))
    def _():
        segments(i + 1, 1 - buf, True)

    segments(i, buf, False)
    slot, _ = _local_slots(seg_len_ref, i, e_ref, rank_ref)
    rid = lax.broadcasted_iota(I32, (rc, tm), 0)
    w = w_ref[...]
    pw = (jnp.where(rid == slot[0:1, :], w[0:1, :], 0.0)
          + jnp.where(rid == slot[1:2, :], w[1:2, :], 0.0))
    yc = yc_ref[buf]
    p_hi = pw.astype(BF16)
    p_lo = (pw - p_hi.astype(F32)).astype(BF16)
    y_hi = yc.astype(BF16)
    y_lo = (yc - y_hi.astype(F32)).astype(BF16)
    dn = (((0,), (0,)), ((), ()))
    out = (lax.dot_general(p_hi, y_hi, dn, preferred_element_type=F32)
           + lax.dot_general(p_hi, y_lo, dn, preferred_element_type=F32)
           + lax.dot_general(p_lo, y_hi, dn, preferred_element_type=F32))
    x = x_ref[...] + g2_ref[0] * out
    if final_norm:
        x = x * lax.rsqrt(jnp.mean(x * x, axis=-1, keepdims=True) + EPS) * fg_ref[...]
    o_ref[...] = x


def _combine(seg_start, seg_len, xc, g2, e01, rank01, w01, fg, ys, *, n_rows, tm, tiles_per_seq,
             final_norm):
    d = xc.shape[1]
    nb = g2.shape[0] - 1
    pair = pl.BlockSpec((2, tm), lambda i, *_: (0, i))
    grid_spec = pltpu.PrefetchScalarGridSpec(
        num_scalar_prefetch=2,
        grid=(n_rows // tm,),
        in_specs=[pl.BlockSpec((tm, d), lambda i, *_: (i, 0)),
                  pl.BlockSpec((1, 1, d), lambda i, *_: (jnp.minimum(i // tiles_per_seq, nb), 0, 0)),
                  pair, pair, pair,
                  pl.BlockSpec((1, d), lambda i, *_: (0, 0)),
                  pl.BlockSpec(memory_space=pl.ANY)],
        out_specs=pl.BlockSpec((tm, d), lambda i, *_: (i, 0)),
        scratch_shapes=[pltpu.VMEM((2, _compact_rows(tm), d), F32), pltpu.SemaphoreType.DMA((2,))],
    )
    return pl.pallas_call(
        functools.partial(_combine_kernel, sizes=_piece_sizes(tm), final_norm=final_norm),
        out_shape=jax.ShapeDtypeStruct((n_rows, d), F32),
        grid_spec=grid_spec,
        compiler_params=_cparams(("arbitrary",), VMEM_LIMIT),
        name="combine",
    )(seg_start, seg_len, xc, g2, e01, rank01, w01, fg, ys)


def _lower_bounds(lb_param):
    p = jax.nn.softmax(lb_param.astype(F32), axis=0)
    return jnp.cumsum(p, axis=0) - p[0]


def kernel(x, c, ctx, c_ctx, w_ada, b_ada, norm1_g, norm2_g, w_in, conv_w, conv_b, conv_ln_g,
           conv_ln_b, lb_fwd, lb_bwd, hgrn_norm_g, w_out, router_w, router_bias, w_gate, w_up,
           w_down, final_norm_g):
    bn, seq, d = x.shape
    ctx_len = ctx.shape[1]
    depth = w_ada.shape[0]
    n = bn * seq
    nc = bn * ctx_len
    tm = min(512, seq, nc)
    assert seq % tm == 0 and nc % tm == 0 and seq % GRID_W == 0
    tps = seq // tm

    pad_rows = (-(bn + 1)) % 8
    cc = jnp.concatenate([c, c_ctx[None, :], jnp.zeros((pad_rows, d), F32)], axis=0)
    mod = _ada(cc, w_ada, b_ada)[:, :bn + 1]
    mod = mod.reshape(depth, bn + 1, 6, 1, d)

    lbs_f = _lower_bounds(lb_fwd)
    lbs_b = _lower_bounds(lb_bwd)
    w_in_bf = w_in.astype(BF16)
    w_out_bf = w_out.astype(BF16)
    rw_t = router_w.T.astype(F32)
    rb = router_bias.reshape(N_EXPERTS, 1).astype(F32)
    hg = hgrn_norm_g.astype(F32)
    fg = final_norm_g.reshape(1, d).astype(F32)

    rows_in = [x.reshape(n, d), ctx.reshape(nc, d)]
    s_zero = jnp.zeros((bn, 2 * HEADS, HEAD_DIM, HEAD_DIM), F32)

    for l in range(depth):
        last = l == depth - 1
        sh1, sc1, g1, sh2, sc2, g2 = (mod[l, :, k] for k in range(6))
        row = lambda a: a.reshape(1, -1).astype(F32)

        u, q, iv, lff, kf, lfb, kb, og = _inproj(
            rows_in, row(norm1_g[l]), sh1, sc1, w_in_bf, l, row(lbs_f[l]), row(lbs_b[l]), tm, tps)

        cw, cb = conv_w[l].astype(F32), row(conv_b[l])
        clg, clb = row(conv_ln_g[l]), row(conv_ln_b[l])
        scan = functools.partial(_scan, q, iv, lff, kf, lfb, kb)
        if last:
            (s_ctx,) = scan(s_zero, row_off=n, seq_len=ctx_len, compute_o=False)
            mix_ctx = [[], [], []]
            n_rows = n
        else:
            of_c, ob_c, s_ctx = scan(s_zero, row_off=n, seq_len=ctx_len, compute_o=True)
            ycx_c = _conv_rows(u, cw, cb, clg, clb, seq=ctx_len, row_off=n, n_rows=nc)
            mix_ctx = [[ycx_c], [of_c], [ob_c]]
            n_rows = n + nc
        of, ob, _ = scan(s_ctx, row_off=0, seq_len=seq, compute_o=True)
        if l % 2 == 0:
            ycx = _conv_rows(u, cw, cb, clg, clb, seq=GRID_W, row_off=0, n_rows=n)
        else:
            ycx = _conv_cols(u, cw, cb, clg, clb, batch=bn, seq_len=seq)
        mix = [[a] + c_ for a, c_ in zip((ycx, of, ob), mix_ctx)]

        res_rows = rows_in if (len(rows_in) == 1 or not last) else [rows_in[0]]
        xc = _outproj(res_rows, mix, og, hg[l].reshape(1, HEAD_DIM), w_out_bf, l, g1,
                      n_rows=n_rows, tm=tm, tiles_per_seq=tps)

        h2, e01, rank01, w01, cnt = _router(
            xc, row(norm2_g[l]), sh2, sc2, rw_t, rb, n_rows=n_rows, tm=tm, tiles_per_seq=tps)
        n_tt = n_rows // tm
        n_et = (2 * n_rows + n_tt * N_EXPERTS * (SUBLANES - 1) + tm - 1) // tm + N_EXPERTS
        seg_start, seg_len, te, misc = _tables(cnt[:, :, 0].reshape(-1), n_tt=n_tt, n_et=n_et, tm=tm)
        xs = _dispatch(seg_start, seg_len, misc, h2, e01, rank01, n_sorted=n_et * tm, tm=tm)
        ys = _moe(te, misc, xs, w_gate, w_up, w_down, l, tm)
        xc = _combine(seg_start, seg_len, xc, g2, e01, rank01, w01, fg, ys, n_rows=n_rows, tm=tm,
                      tiles_per_seq=tps, final_norm=last)
        rows_in = [xc]

    return xc[:n].reshape(bn, seq, d)
```

```python
import functools

import numpy as np
import jax
import jax.numpy as jnp
from jax import lax
from jax.experimental import pallas as pl
from jax.experimental.pallas import tpu as pltpu

F32 = jnp.float32
BF16 = jnp.bfloat16
I32 = jnp.int32

EPS = 1e-6
GRID_W = 64
HEADS = 4
HEAD_DIM = 128
D_GROUP = HEADS * HEAD_DIM
N_IN_GROUPS = 7
CONV_K = 31
CONV_HALF = CONV_K // 2
CONV_GAP = 16
CONV_ROW_GROUP = 1
CONV_PARTIALS = 2
N_EXPERTS = 16
N_GROUPS = 4
EXPERTS_PER_GROUP = N_EXPERTS // N_GROUPS
SCAN_CHUNK = 64
SCAN_FAST_LIMIT = 60.0
LANES = 128
SUBLANES = 8
VMEM_LIMIT = 56 * 1024 * 1024


def _cparams(sem, vmem=None):
    return pltpu.CompilerParams(dimension_semantics=sem, vmem_limit_bytes=vmem)


def _sigmoid(x):
    return 0.5 * jnp.tanh(0.5 * x) + 0.5


def _silu(x):
    return x * _sigmoid(x)


def _ada_kernel(c_ref, w_ref, b_ref, o_ref):
    a = _silu(c_ref[...]).astype(BF16)
    w = w_ref[0].astype(BF16)
    o_ref[0] = jnp.dot(a, w, preferred_element_type=F32) + b_ref[0]


def _ada(cc, w_ada, b_ada, tn=1536):
    depth, d, d6 = w_ada.shape
    rows = cc.shape[0]
    return pl.pallas_call(
        _ada_kernel,
        out_shape=jax.ShapeDtypeStruct((depth, rows, d6), F32),
        grid=(depth, d6 // tn),
        in_specs=[
            pl.BlockSpec((rows, d), lambda l, j: (0, 0)),
            pl.BlockSpec((1, d, tn), lambda l, j: (l, 0, j)),
            pl.BlockSpec((1, 1, tn), lambda l, j: (l, 0, j)),
        ],
        out_specs=pl.BlockSpec((1, rows, tn), lambda l, j: (l, 0, j)),
        compiler_params=_cparams(("arbitrary", "arbitrary"), VMEM_LIMIT),
        name="ada",
    )(cc, w_ada, b_ada.reshape(depth, 1, d6))


def _modulate(x, g, shift, scale):
    r = lax.rsqrt(jnp.mean(x * x, axis=-1, keepdims=True) + EPS)
    return (x * r * g) * (1.0 + scale) + shift


def _rows_from(refs, n_lat_tiles):
    if len(refs) == 1:
        return refs[0][...]
    return jnp.where(pl.program_id(0) >= n_lat_tiles, refs[1][...], refs[0][...])


def _row_specs(arrays, tm):
    width = arrays[0].shape[1]
    if len(arrays) == 1:
        return [pl.BlockSpec((tm, width), lambda i, *_: (i, 0))], 0
    n_lat_tiles = arrays[0].shape[0] // tm
    return [pl.BlockSpec((tm, width), lambda i, *_: (jnp.minimum(i, n_lat_tiles - 1), 0)),
            pl.BlockSpec((tm, width), lambda i, *_: (jnp.maximum(i - n_lat_tiles, 0), 0))], n_lat_tiles


def _inproj_kernel(*refs, n_src, n_lat_tiles):
    x = _rows_from(refs[:n_src], n_lat_tiles)
    (g_ref, sh_ref, sc_ref, w_ref, lbf_ref, lbb_ref,
     u_ref, q_ref, i_ref, lff_ref, kf_ref, lfb_ref, kb_ref, og_ref) = refs[n_src:]
    hb = _modulate(x, g_ref[...], sh_ref[0], sc_ref[0]).astype(BF16)

    def proj(j):
        return jnp.dot(hb, w_ref[0, :, j * D_GROUP:(j + 1) * D_GROUP], preferred_element_type=F32)

    u_ref[...] = (proj(0) * _sigmoid(proj(1))).astype(BF16)
    q_ref[...] = _silu(proj(2)).astype(BF16)
    i_ref[...] = proj(3).astype(BF16)
    for j, lb_ref, lf_ref, k_ref in ((4, lbf_ref, lff_ref, kf_ref), (5, lbb_ref, lfb_ref, kb_ref)):
        lb = lb_ref[...]
        f = lb + (1.0 - lb) * jax.nn.sigmoid(proj(j))
        lf_ref[...] = jnp.log(f)
        k_ref[...] = (1.0 - f).astype(BF16)
    og_ref[...] = _silu(proj(6)).astype(BF16)


def _inproj(xs, g, sh, sc, w_bf, layer, lbf, lbb, tm, tiles_per_seq):
    n = sum(a.shape[0] for a in xs)
    d = xs[0].shape[1]
    x_specs, n_lat_tiles = _row_specs(xs, tm)
    nb = sh.shape[0] - 1
    mod_spec = pl.BlockSpec((1, 1, d), lambda i: (jnp.minimum(i // tiles_per_seq, nb), 0, 0))
    row_spec = pl.BlockSpec((tm, D_GROUP), lambda i: (i, 0))
    vec_spec = pl.BlockSpec((1, D_GROUP), lambda i: (0, 0))
    outs = [jax.ShapeDtypeStruct((n, D_GROUP), dt)
            for dt in (BF16, BF16, BF16, F32, BF16, F32, BF16, BF16)]
    return pl.pallas_call(
        functools.partial(_inproj_kernel, n_src=len(xs), n_lat_tiles=n_lat_tiles),
        out_shape=outs,
        grid=(n // tm,),
        in_specs=x_specs + [
            pl.BlockSpec((1, d), lambda i: (0, 0)),
            mod_spec, mod_spec,
            pl.BlockSpec((1, d, N_IN_GROUPS * D_GROUP), lambda i: (layer, 0, 0)),
            vec_spec, vec_spec,
        ],
        out_specs=[row_spec] * 8,
        compiler_params=_cparams(("arbitrary",), VMEM_LIMIT),
        name="inproj",
    )(*xs, g, sh, sc, w_bf, lbf, lbb)


def _scan_tables(c, backward):
    levels = int(np.log2(c))
    r_all = np.zeros((levels * c + 2 * c + 8, c), np.float32)
    for lev in range(levels):
        h = 1 << lev
        for r in range(c):
            bd = (r // (2 * h)) * 2 * h + h
            if not backward:
                if r >= bd:
                    r_all[lev * c + r, bd:r + 1] = 1.0
                else:
                    r_all[lev * c + r, r + 1:bd] = 1.0
            else:
                if r < bd:
                    r_all[lev * c + r, r:bd] = 1.0
                else:
                    r_all[lev * c + r, bd:r] = 1.0
    base = levels * c
    for r in range(c):
        if not backward:
            r_all[base + r, :r + 1] = 1.0
            r_all[base + c + r, r + 1:] = 1.0
        else:
            r_all[base + r, r:] = 1.0
            r_all[base + c + r, :r] = 1.0
    r_all[base + 2 * c:, :] = 1.0
    return r_all


def _scan_kernel(*refs, n_chunks, compute_o):
    c = SCAN_CHUNK
    levels = int(np.log2(c))
    (rf_ref, rb_ref, qf_ref, vf_ref, lff_ref, kf_ref,
     qb_ref, vb_ref, lfb_ref, kb_ref, s0_ref, trif_ref, trib_ref) = refs[:13]
    pos = 13
    if compute_o:
        of_ref, ob_ref, sfin_ref, st_ref = refs[pos:pos + 4]
    else:
        sfin_ref, st_ref = refs[pos:pos + 2]
        of_ref = ob_ref = None
    j = pl.program_id(1)

    @pl.when(j == 0)
    def _():
        st_ref[...] = s0_ref[0]

    row = lax.broadcasted_iota(jnp.int32, (c, 1), 0)
    ri = lax.broadcasted_iota(jnp.int32, (c, c), 0)
    ci = lax.broadcasted_iota(jnp.int32, (c, c), 1)
    upper = [(row & (2 * (1 << lev) - 1)) >= (1 << lev) for lev in range(levels)]
    same_parent = [(ri >> (lev + 1)) == (ci >> (lev + 1)) for lev in range(levels)]
    diag = ri == ci
    dn_t = (((1,), (1,)), ((), ()))
    dn_tl = (((0,), (0,)), ((), ()))

    half = c // 2
    tri = [jnp.where(ci <= ri, 1.0, 0.0).astype(BF16), jnp.where(ci >= ri, 1.0, 0.0).astype(BF16)]
    first_half = [row < half, row >= half]
    re = lax.broadcasted_iota(jnp.int32, (c, 4 * c), 0)
    ce = lax.broadcasted_iota(jnp.int32, (c, 4 * c), 1)
    cs = ce & (c - 1)
    same_half = (re >= half) == (cs >= half)
    own = (ce & (2 * c - 1)) < c
    mask_ext = [
        (own & same_half & (cs <= re)) | (jnp.logical_not(own) & (re >= half) & (cs < half)),
        (own & same_half & (cs >= re)) | (jnp.logical_not(own) & (re < half) & (cs >= half)),
    ]

    mrow, trow = [half - 1, half], [c - 1, 0]

    def prefix(tri_mat, g):
        g_hi = g.astype(BF16)
        g_lo = (g - g_hi.astype(F32)).astype(BF16)
        return (jnp.dot(tri_mat, g_hi, preferred_element_type=F32)
                + jnp.dot(tri_mat, g_lo, preferred_element_type=F32))

    def next_state(st, hs, kh, vh, b, tot):
        kd = (kh * jnp.exp(tot[:, hs] - b[:, hs])).astype(BF16)
        return st * jnp.exp(tot[:, hs]) + lax.dot_general(vh, kd, dn_tl, preferred_element_type=F32)

    def state_only_chunk(dirn, lf_ref, k_ref, v_ref, r0):
        rows = pl.ds(r0, c)
        b = prefix(tri[dirn], lf_ref[rows, :])
        tot = b[trow[dirn]:trow[dirn] + 1, :]
        for h in range(HEADS):
            hs = slice(h * HEAD_DIM, (h + 1) * HEAD_DIM)
            st_ref[dirn * HEADS + h] = next_state(st_ref[dirn * HEADS + h], hs,
                                                  k_ref[rows, hs].astype(F32), v_ref[rows, hs], b, tot)

    def fast_chunk(dirn, q_ref, v_ref, k_ref, o_ref, rows, b, m, tot, states):
        fh = first_half[dirn]
        hd = HEAD_DIM

        def block_diag(a, bb):
            za = jnp.zeros((a.shape[0], bb.shape[1]), a.dtype)
            zb = jnp.zeros((bb.shape[0], a.shape[1]), a.dtype)
            return jnp.concatenate([jnp.concatenate([a, za], axis=1),
                                    jnp.concatenate([zb, bb], axis=1)], axis=0)

        for hp in range(HEADS // 2):
            ps = slice(2 * hp * hd, 2 * (hp + 1) * hd)
            bp, mp, tp = b[:, ps], m[:, ps], tot[:, ps]
            cdec = bp - jnp.where(fh, 0.0, mp)
            e_own = jnp.exp(cdec).astype(BF16)
            e_own_inv = jnp.exp(-cdec).astype(BF16)
            e_cross = jnp.where(fh, jnp.exp(jnp.minimum(mp - bp, 0.0)), 0.0).astype(BF16)
            e_inc = jnp.exp(bp).astype(BF16)
            e_dec = jnp.exp(tp - bp).astype(BF16)
            qp, kp, vp = q_ref[rows, ps], k_ref[rows, ps], v_ref[rows, ps]
            q1, qi = qp * e_own, qp * e_inc
            k1, k2, kd = kp * e_own_inv, kp * e_cross, kp * e_dec
            kcat = [jnp.concatenate([k1[:, s_], k2[:, s_]], axis=0) for s_ in (slice(0, hd), slice(hd, 2 * hd))]
            sc = lax.dot_general(q1, block_diag(*kcat), dn_t, preferred_element_type=F32)
            p = jnp.where(mask_ext[dirn], sc, 0.0).astype(BF16)
            v2 = [jnp.concatenate([vp[:, s_], vp[:, s_]], axis=0) for s_ in (slice(0, hd), slice(hd, 2 * hd))]
            st = [states[dirn * HEADS + 2 * hp + a] for a in range(2)]
            o = (jnp.dot(p, block_diag(*v2), preferred_element_type=F32)
                 + lax.dot_general(qi, block_diag(st[0].astype(BF16), st[1].astype(BF16)), dn_t,
                                   preferred_element_type=F32))
            o_ref[rows, ps] = o.astype(o_ref.dtype)
            upd = lax.dot_general(vp, kd, dn_tl, preferred_element_type=F32)
            e_tot = jnp.exp(tp)
            for a in range(2):
                sl = slice(a * hd, (a + 1) * hd)
                states[dirn * HEADS + 2 * hp + a] = st[a] * e_tot[:, sl] + upd[sl, sl]

    def robust_chunk(dirn, r_ref, q_ref, v_ref, lf_ref, k_ref, o_ref, r0):
        rows = pl.ds(r0, c)
        g = lf_ref[rows, :]
        g_hi = g.astype(BF16)
        g_lo = (g - g_hi.astype(F32)).astype(BF16)
        rmat = r_ref[...]
        e_all = jnp.exp(jnp.dot(rmat, g_hi, preferred_element_type=F32)
                        + jnp.dot(rmat, g_lo, preferred_element_type=F32))
        base = levels * c
        for h in range(HEADS):
            hs = slice(h * HEAD_DIM, (h + 1) * HEAD_DIM)
            qh = q_ref[rows, hs].astype(F32)
            kh = k_ref[rows, hs].astype(F32)
            vh = v_ref[rows, hs]
            st = st_ref[dirn * HEADS + h]
            if compute_o:
                att = jnp.where(diag, lax.dot_general(qh.astype(BF16), kh.astype(BF16), dn_t,
                                                      preferred_element_type=F32), 0.0)
                for lev in range(levels):
                    e_l = e_all[lev * c:(lev + 1) * c, hs]
                    q_side = upper[lev] if dirn == 0 else jnp.logical_not(upper[lev])
                    qt = jnp.where(q_side, qh * e_l, 0.0).astype(BF16)
                    kt = jnp.where(q_side, 0.0, kh * e_l).astype(BF16)
                    a_l = lax.dot_general(qt, kt, dn_t, preferred_element_type=F32)
                    att = att + jnp.where(same_parent[lev], a_l, 0.0)
                qi = (qh * e_all[base:base + c, hs]).astype(BF16)
                o = lax.dot_general(qi, st.astype(BF16), dn_t, preferred_element_type=F32)
                o = o + jnp.dot(att.astype(BF16), vh, preferred_element_type=F32)
                o_ref[rows, hs] = o.astype(o_ref.dtype)
            kd = (kh * e_all[base + c:base + 2 * c, hs]).astype(BF16)
            e_tot = e_all[base + 2 * c:base + 2 * c + 1, hs]
            st_ref[dirn * HEADS + h] = st * e_tot + lax.dot_general(
                vh, kd, dn_tl, preferred_element_type=F32)

    dir_refs = ((rf_ref, qf_ref, vf_ref, lff_ref, kf_ref, of_ref),
                (rb_ref, qb_ref, vb_ref, lfb_ref, kb_ref, ob_ref))

    def chunk_rows(ci_):
        return pl.multiple_of(ci_ * c, c), pl.multiple_of((n_chunks - 1 - ci_) * c, c)

    if not compute_o:
        def state_body(ci_, carry):
            for dirn, r0 in enumerate(chunk_rows(ci_)):
                _, _, v_ref, lf_ref, k_ref, _ = dir_refs[dirn]
                state_only_chunk(dirn, lf_ref, k_ref, v_ref, r0)
            return carry

        lax.fori_loop(0, n_chunks, state_body, 0)
    else:
        b_blk, stats, worst = [], {}, None
        for dirn in range(2):
            b_blk.append(prefix((trif_ref, trib_ref)[dirn][...], dir_refs[dirn][3][...]))
            for ch in range(n_chunks):
                m = b_blk[dirn][ch * c + mrow[dirn]:ch * c + mrow[dirn] + 1, :]
                tot = b_blk[dirn][ch * c + trow[dirn]:ch * c + trow[dirn] + 1, :]
                stats[dirn, ch] = (m, tot)
                w = jnp.minimum(m, tot - m)
                worst = w if worst is None else jnp.minimum(worst, w)
        fast_ok = jnp.min(worst) >= -SCAN_FAST_LIMIT

        @pl.when(fast_ok)
        def _():
            states = [st_ref[i] for i in range(2 * HEADS)]
            for step in range(n_chunks):
                for dirn in range(2):
                    ch = step if dirn == 0 else n_chunks - 1 - step
                    _, q_ref, v_ref, _, k_ref, o_ref = dir_refs[dirn]
                    rows = slice(ch * c, (ch + 1) * c)
                    fast_chunk(dirn, q_ref, v_ref, k_ref, o_ref, rows, b_blk[dirn][rows, :],
                               *stats[dirn, ch], states)
            for i in range(2 * HEADS):
                st_ref[i] = states[i]

        @pl.when(jnp.logical_not(fast_ok))
        def _():
            def robust_body(ci_, carry):
                for dirn, r0 in enumerate(chunk_rows(ci_)):
                    r_ref, q_ref, v_ref, lf_ref, k_ref, o_ref = dir_refs[dirn]
                    robust_chunk(dirn, r_ref, q_ref, v_ref, lf_ref, k_ref, o_ref, r0)
                return carry

            lax.fori_loop(0, n_chunks, robust_body, 0)

    @pl.when(j == pl.num_programs(1) - 1)
    def _():
        sfin_ref[0] = st_ref[...]


def _scan(q, v, lff, kf, lfb, kb, s0, *, row_off, seq_len, compute_o):
    batch = s0.shape[0]
    t = min(256, seq_len)
    nt = seq_len // t
    off = row_off // t
    fwd = lambda b, j: (off + b * nt + j, 0)
    bwd = lambda b, j: (off + b * nt + (nt - 1 - j), 0)
    ofwd = lambda b, j: (b * nt + j, 0)
    obwd = lambda b, j: (b * nt + (nt - 1 - j), 0)
    blk = lambda im: pl.BlockSpec((t, D_GROUP), im)
    rf = jnp.asarray(_scan_tables(SCAN_CHUNK, False), BF16)
    rb = jnp.asarray(_scan_tables(SCAN_CHUNK, True), BF16)
    rspec = pl.BlockSpec(rf.shape, lambda b, j: (0, 0))
    sspec = pl.BlockSpec((1, 2 * HEADS, HEAD_DIM, HEAD_DIM), lambda b, j: (b, 0, 0, 0))
    pos_t = np.arange(t)
    same_chunk = (pos_t[:, None] // SCAN_CHUNK) == (pos_t[None, :] // SCAN_CHUNK)
    tri_f = jnp.asarray(same_chunk & (pos_t[None, :] <= pos_t[:, None]), BF16)
    tri_b = jnp.asarray(same_chunk & (pos_t[None, :] >= pos_t[:, None]), BF16)
    tspec = pl.BlockSpec((t, t), lambda b, j: (0, 0))
    in_specs = [rspec, rspec, blk(fwd), blk(fwd), blk(fwd), blk(fwd),
                blk(bwd), blk(bwd), blk(bwd), blk(bwd), sspec, tspec, tspec]
    args = [rf, rb, q, v, lff, kf, q, v, lfb, kb, s0, tri_f, tri_b]
    out_shape, out_specs = [], []
    if compute_o:
        out_shape += [jax.ShapeDtypeStruct((batch * seq_len, D_GROUP), BF16)] * 2
        out_specs += [blk(ofwd), blk(obwd)]
    out_shape.append(jax.ShapeDtypeStruct(s0.shape, F32))
    out_specs.append(sspec)
    kern = functools.partial(_scan_kernel, n_chunks=t // SCAN_CHUNK, compute_o=compute_o)
    return pl.pallas_call(
        kern,
        out_shape=out_shape,
        grid=(batch, nt),
        in_specs=in_specs,
        out_specs=out_specs,
        scratch_shapes=[pltpu.VMEM((2 * HEADS, HEAD_DIM, HEAD_DIM), F32)],
        compiler_params=_cparams(("arbitrary", "arbitrary"), VMEM_LIMIT),
        name="hgrn_scan",
    )(*args)


def _ln_silu(y, g, b):
    mu = jnp.mean(y, axis=-1, keepdims=True)
    yc = y - mu
    var = jnp.mean(yc * yc, axis=-1, keepdims=True)
    return _silu(yc * lax.rsqrt(var + EPS) * g + b)


def _tree_sum(terms):
    while len(terms) > 1:
        terms = [terms[i] + terms[i + 1] if i + 1 < len(terms) else terms[i]
                 for i in range(0, len(terms), 2)]
    return terms[0]


def _conv_rows_kernel(u_ref, w_ref, b_ref, lg_ref, lb_ref, y_ref, pad_ref, acc_ref, *, seq, tb):
    nseq = tb // seq
    stride = seq + CONV_GAP
    n_pad = pad_ref.shape[2]
    n_lt = D_GROUP // LANES
    for cl in range(n_lt):
        ls = slice(cl * LANES, (cl + 1) * LANES)
        pad_ref[0, cl] = jnp.zeros(pad_ref.shape[2:], F32)
        for s in range(nseq):
            pad_ref[0, cl, CONV_GAP + s * stride:CONV_GAP + s * stride + seq, :] = (
                u_ref[s * seq:(s + 1) * seq, ls].astype(F32))
        for p in range(1, SUBLANES):
            pad_ref[p, cl, 0:n_pad - SUBLANES, :] = pad_ref[0, cl, p:p + n_pad - SUBLANES, :]
    rc = 64
    group = CONV_ROW_GROUP
    assert seq % rc == 0 and tb % (group * rc) == 0
    if seq == rc:
        member_off = [a * stride for a in range(group)]
        group_base = lambda g: g * (group * stride)
    else:
        gps = seq // (group * rc)
        assert seq % (group * rc) == 0
        member_off = [a * rc for a in range(group)]
        group_base = lambda g: (g // gps) * stride + (g % gps) * (group * rc)

    def one_group(g, carry):
        pad0 = pl.multiple_of(CONV_GAP + group_base(g), SUBLANES)
        out0 = pl.multiple_of(g * (group * rc), group * rc)
        for cl in range(n_lt):
            ls = slice(cl * LANES, (cl + 1) * LANES)
            parts = [[None] * CONV_PARTIALS for _ in range(group)]
            for k in range(CONV_K):
                off = k - CONV_HALF
                wk = w_ref[k:k + 1, ls]
                for a in range(group):
                    src = pl.ds(pad0 + member_off[a] + (off // SUBLANES) * SUBLANES, rc)
                    term = wk * pad_ref[off % SUBLANES, cl, src, :]
                    prev = parts[a][k % CONV_PARTIALS]
                    parts[a][k % CONV_PARTIALS] = term if prev is None else prev + term
            for a in range(group):
                acc_ref[pl.ds(out0 + a * rc, rc), ls] = _tree_sum(parts[a])
        return carry

    lax.fori_loop(0, tb // (group * rc), one_group, 0)
    y = acc_ref[...] + b_ref[...]
    y_ref[...] = _ln_silu(y, lg_ref[...], lb_ref[...]).astype(y_ref.dtype)


def _conv_rows(u, w, b, lg, lb, *, seq, row_off, n_rows):
    tb = max(seq, 512) if n_rows % max(seq, 512) == 0 else seq
    nseq = tb // seq
    off = row_off // tb
    vec = pl.BlockSpec((1, D_GROUP), lambda i: (0, 0))
    in_specs = [pl.BlockSpec((tb, D_GROUP), lambda i: (off + i, 0)),
                pl.BlockSpec((CONV_K, D_GROUP), lambda i: (0, 0)), vec, vec, vec]
    args = [u, w, b, lg, lb]
    kern = functools.partial(_conv_rows_kernel, seq=seq, tb=tb)
    return pl.pallas_call(
        kern,
        out_shape=jax.ShapeDtypeStruct((n_rows, D_GROUP), BF16),
        grid=(n_rows // tb,),
        in_specs=in_specs,
        out_specs=pl.BlockSpec((tb, D_GROUP), lambda i: (i, 0)),
        scratch_shapes=[pltpu.VMEM((SUBLANES, D_GROUP // LANES, CONV_GAP + nseq * (seq + CONV_GAP), LANES), F32),
                        pltpu.VMEM((tb, D_GROUP), F32)],
        compiler_params=_cparams(("arbitrary",), VMEM_LIMIT),
        name="conv_rows",
    )(*args)


CONV_COL_TILE = 16


def _conv_cols_kernel(u_ref, w_ref, b_ref, lg_ref, lb_ref, y_ref, uf_ref, *, n_rows):
    n_lt = D_GROUP // LANES
    row_stride = GRID_W + SUBLANES
    for cl in range(n_lt):
        ls = slice(cl * LANES, (cl + 1) * LANES)
        for r in range(n_rows):
            uf_ref[cl, r * row_stride:r * row_stride + GRID_W, :] = (
                u_ref[r * GRID_W:(r + 1) * GRID_W, ls].astype(F32))
    bias = b_ref[...]
    lg = lg_ref[...]
    lb = lb_ref[...]

    def col_tile(wi, carry):
        w0 = pl.multiple_of(wi * CONV_COL_TILE, CONV_COL_TILE)
        for r in range(n_rows):
            tiles = []
            for cl in range(n_lt):
                ls = slice(cl * LANES, (cl + 1) * LANES)
                parts = [None] * CONV_PARTIALS
                for k in range(CONV_K):
                    rr = r + k - CONV_HALF
                    if 0 <= rr < n_rows:
                        src = pl.ds(pl.multiple_of(rr * row_stride + w0, SUBLANES), CONV_COL_TILE)
                        term = w_ref[k:k + 1, ls] * uf_ref[cl, src, :]
                        prev = parts[k % CONV_PARTIALS]
                        parts[k % CONV_PARTIALS] = term if prev is None else prev + term
                tiles.append(_tree_sum([p for p in parts if p is not None]))
            dst = pl.ds(pl.multiple_of(r * GRID_W + w0, CONV_COL_TILE), CONV_COL_TILE)
            y_ref[dst, :] = _ln_silu(jnp.concatenate(tiles, axis=1) + bias, lg, lb).astype(y_ref.dtype)
        return carry

    lax.fori_loop(0, GRID_W // CONV_COL_TILE, col_tile, 0)


def _conv_cols(u, w, b, lg, lb, *, batch, seq_len):
    vec = pl.BlockSpec((1, D_GROUP), lambda bi: (0, 0))
    blk = pl.BlockSpec((seq_len, D_GROUP), lambda bi: (bi, 0))
    return pl.pallas_call(
        functools.partial(_conv_cols_kernel, n_rows=seq_len // GRID_W),
        out_shape=jax.ShapeDtypeStruct((batch * seq_len, D_GROUP), BF16),
        grid=(batch,),
        in_specs=[blk, pl.BlockSpec((CONV_K, D_GROUP), lambda bi: (0, 0)), vec, vec, vec],
        out_specs=blk,
        scratch_shapes=[pltpu.VMEM((D_GROUP // LANES, (seq_len // GRID_W) * (GRID_W + SUBLANES), LANES), F32)],
        compiler_params=_cparams(("arbitrary",), VMEM_LIMIT),
        name="conv_cols",
    )(u, w, b, lg, lb)


def _outproj_kernel(*refs, n_x, n_mix, n_lat_tiles):
    x = _rows_from(refs[:n_x], n_lat_tiles)
    pos = n_x
    ycx, of, ob = (_rows_from(refs[pos + k * n_mix:pos + (k + 1) * n_mix], n_lat_tiles) for k in range(3))
    og_ref, hg_ref, w_ref, g1_ref = refs[pos + 3 * n_mix:pos + 3 * n_mix + 4]
    route_in = refs[pos + 3 * n_mix + 4:pos + 3 * n_mix + 9]
    o_ref = refs[pos + 3 * n_mix + 9]
    route_out = refs[pos + 3 * n_mix + 10:]
    o = of.astype(F32) + ob.astype(F32)
    og = og_ref[...].astype(F32)
    hg = hg_ref[...]
    acc = jnp.dot(ycx, w_ref[0, 0:D_GROUP, :], preferred_element_type=F32)
    for h in range(HEADS):
        hs = slice(h * HEAD_DIM, (h + 1) * HEAD_DIM)
        oh = o[:, hs]
        r = lax.rsqrt(jnp.mean(oh * oh, axis=-1, keepdims=True) + EPS)
        yh = ((oh * r * hg) * og[:, hs]).astype(BF16)
        acc = acc + jnp.dot(yh, w_ref[0, D_GROUP + h * HEAD_DIM:D_GROUP + (h + 1) * HEAD_DIM, :],
                            preferred_element_type=F32)
    x = x + g1_ref[0] * acc
    o_ref[...] = x
    _route(x, *route_in, *route_out)


def _outproj_route(xs, mix, og, hg, w_bf, layer, g1, g2n, sh2, sc2, rw_t, rb, *, n_rows, tm, tiles_per_seq):
    d = xs[0].shape[1]
    nb = g1.shape[0] - 1
    n_tt = n_rows // tm
    x_specs, n_lat_tiles = _row_specs(xs, tm)
    mix_specs, mix_args = [], []
    for arrays in mix:
        specs, nl = _row_specs(arrays, tm)
        assert len(arrays) == len(mix[0]) and (nl == 0 or n_lat_tiles in (0, nl))
        n_lat_tiles = max(n_lat_tiles, nl)
        mix_specs += specs
        mix_args += list(arrays)
    row = lambda width: pl.BlockSpec((tm, width), lambda i: (i, 0))
    mod_spec = pl.BlockSpec((1, 1, d), lambda i: (jnp.minimum(i // tiles_per_seq, nb), 0, 0))
    pair = pl.BlockSpec((2, tm), lambda i: (0, i))
    kern = functools.partial(_outproj_kernel, n_x=len(xs), n_mix=len(mix[0]), n_lat_tiles=n_lat_tiles)
    return pl.pallas_call(
        kern,
        out_shape=[jax.ShapeDtypeStruct((n_rows, d), F32),
                   jax.ShapeDtypeStruct((n_rows, d), BF16),
                   jax.ShapeDtypeStruct((2, n_rows), I32),
                   jax.ShapeDtypeStruct((2, n_rows), I32),
                   jax.ShapeDtypeStruct((2, n_rows), F32),
                   jax.ShapeDtypeStruct((n_tt, N_EXPERTS, LANES), I32)],
        grid=(n_tt,),
        in_specs=x_specs + mix_specs + [row(D_GROUP),
                  pl.BlockSpec((1, HEAD_DIM), lambda i: (0, 0)),
                  pl.BlockSpec((1,) + w_bf.shape[1:], lambda i: (layer, 0, 0)),
                  mod_spec,
                  pl.BlockSpec((1, d), lambda i: (0, 0)), mod_spec, mod_spec,
                  pl.BlockSpec((N_EXPERTS, d), lambda i: (0, 0)),
                  pl.BlockSpec((N_EXPERTS, 1), lambda i: (0, 0))],
        out_specs=[row(d), row(d), pair, pair, pair,
                   pl.BlockSpec((1, N_EXPERTS, LANES), lambda i: (i, 0, 0))],
        compiler_params=_cparams(("arbitrary",), VMEM_LIMIT),
        name="outproj_route",
    )(*xs, *mix_args, og, hg, w_bf, g1, g2n, sh2, sc2, rw_t, rb)


def _route(x, g_ref, sh_ref, sc_ref, rw_ref, rb_ref, h_ref, e_ref, rank_ref, w_ref, cnt_ref):
    tm = x.shape[0]
    h = _modulate(x, g_ref[...], sh_ref[0], sc_ref[0])
    h_ref[...] = h.astype(h_ref.dtype)
    logits = lax.dot_general(rw_ref[...], h, (((1,), (1,)), ((), ())),
                             precision=lax.Precision.HIGHEST, preferred_element_type=F32)
    s = jax.nn.sigmoid(logits)
    sb = s + rb_ref[...]
    s_rows = [s[e:e + 1, :] for e in range(N_EXPERTS)]
    sb_rows = [sb[e:e + 1, :] for e in range(N_EXPERTS)]

    def group_score(g):
        v = sb_rows[g * EXPERTS_PER_GROUP:(g + 1) * EXPERTS_PER_GROUP]
        best = None
        for a in range(EXPERTS_PER_GROUP):
            for b in range(a + 1, EXPERTS_PER_GROUP):
                p = v[a] + v[b]
                best = p if best is None else jnp.maximum(best, p)
        return best

    cur = group_score(0)
    best_g = jnp.zeros(cur.shape, I32)
    for g in range(1, N_GROUPS):
        gs = group_score(g)
        upd = gs > cur
        best_g = jnp.where(upd, g, best_g)
        cur = jnp.where(upd, gs, cur)

    def pick(rows, jdx):
        out = rows[jdx]
        for g in range(1, N_GROUPS):
            out = jnp.where(best_g == g, rows[g * EXPERTS_PER_GROUP + jdx], out)
        return out

    vb = [pick(sb_rows, jdx) for jdx in range(EXPERTS_PER_GROUP)]
    vs = [pick(s_rows, jdx) for jdx in range(EXPERTS_PER_GROUP)]

    def first_argmax(vals):
        m = vals[0]
        for v in vals[1:]:
            m = jnp.maximum(m, v)
        idx = jnp.full(m.shape, EXPERTS_PER_GROUP - 1, I32)
        for jdx in range(EXPERTS_PER_GROUP - 2, -1, -1):
            idx = jnp.where(vals[jdx] == m, jdx, idx)
        return idx

    def take(vals, idx):
        out = vals[EXPERTS_PER_GROUP - 1]
        for jdx in range(EXPERTS_PER_GROUP - 2, -1, -1):
            out = jnp.where(idx == jdx, vals[jdx], out)
        return out

    i0 = first_argmax(vb)
    vb2 = [jnp.where(i0 == jdx, -jnp.inf, vb[jdx]) for jdx in range(EXPERTS_PER_GROUP)]
    i1 = first_argmax(vb2)
    s0 = take(vs, i0)
    s1 = take(vs, i1)
    tot = s0 + s1
    w_ref[0:1, :] = s0 / tot
    w_ref[1:2, :] = s1 / tot
    e0 = best_g * EXPERTS_PER_GROUP + i0
    e1 = best_g * EXPERTS_PER_GROUP + i1
    e_ref[0:1, :] = e0
    e_ref[1:2, :] = e1

    eid = lax.broadcasted_iota(I32, (N_EXPERTS, tm), 0)
    hit0 = eid == e0
    hit1 = eid == e1
    onehot = jnp.where(jnp.logical_or(hit0, hit1), 1.0, 0.0)
    ti = lax.broadcasted_iota(I32, (tm, tm), 0)
    tj = lax.broadcasted_iota(I32, (tm, tm), 1)
    before = jnp.where(ti < tj, 1.0, 0.0).astype(BF16)
    rank = jnp.dot(onehot.astype(BF16), before, preferred_element_type=F32)
    rank_ref[0:1, :] = jnp.sum(jnp.where(hit0, rank, 0.0), axis=0, keepdims=True).astype(I32)
    rank_ref[1:2, :] = jnp.sum(jnp.where(hit1, rank, 0.0), axis=0, keepdims=True).astype(I32)
    cnt = jnp.sum(onehot, axis=1, keepdims=True).astype(I32)
    cnt_ref[0] = jnp.broadcast_to(cnt, (N_EXPERTS, LANES))


def _tables_kernel(cnt_ref, seg_start_ref, seg_len_ref, te_ref, misc_ref, *, n_tt, n_et, tm):
    def expert(e, row0):
        def tile(i, pos):
            n = cnt_ref[i * N_EXPERTS + e]
            n8 = ((n + SUBLANES - 1) // SUBLANES) * SUBLANES
            seg_start_ref[i * N_EXPERTS + e] = pos
            seg_len_ref[i * N_EXPERTS + e] = n8
            return pos + n8

        end = lax.fori_loop(0, n_tt, tile, row0)
        padded_end = row0 + ((end - row0 + tm - 1) // tm) * tm
        misc_ref[1 + e] = end
        misc_ref[1 + N_EXPERTS + e] = padded_end - end

        def mark(j, carry):
            te_ref[j] = e
            return carry

        lax.fori_loop(row0 // tm, padded_end // tm, mark, 0)
        return padded_end

    total = row0 = 0
    for e in range(N_EXPERTS):
        row0 = expert(e, row0)
    total = row0
    n_used = total // tm
    misc_ref[0] = n_used
    last = te_ref[jnp.maximum(n_used - 1, 0)]

    def fill(j, carry):
        te_ref[j] = last
        return carry

    lax.fori_loop(n_used, n_et, fill, 0)


def _tables(cnt, *, n_tt, n_et, tm):
    smem = pl.BlockSpec(memory_space=pltpu.SMEM)
    return pl.pallas_call(
        functools.partial(_tables_kernel, n_tt=n_tt, n_et=n_et, tm=tm),
        out_shape=[jax.ShapeDtypeStruct((n_tt * N_EXPERTS,), I32),
                   jax.ShapeDtypeStruct((n_tt * N_EXPERTS,), I32),
                   jax.ShapeDtypeStruct((n_et,), I32),
                   jax.ShapeDtypeStruct((1 + 2 * N_EXPERTS,), I32)],
        in_specs=[smem],
        out_specs=[smem, smem, smem, smem],
        name="route_tables",
    )(cnt)


def _piece_sizes(max_rows):
    sizes, s = [], SUBLANES
    while s <= max_rows:
        sizes.append(s)
        s *= 2
    return sizes[::-1]


def _segment_copies(src_ref, src_row, dst_ref, dst_row, n_rows, sem, sizes, start):
    aligned = lambda r: r if isinstance(r, int) else pl.multiple_of(r, SUBLANES)
    for sz in sizes:
        src = pl.ds(aligned(src_row), sz)
        dst = pl.ds(aligned(dst_row), sz)

        @pl.when((n_rows & sz) != 0)
        def _(src=src, dst=dst):
            cp = pltpu.make_async_copy(src_ref.at[src, :], dst_ref.at[dst, :], sem)
            if start:
                cp.start()
            else:
                cp.wait()

        src_row = src_row + (n_rows & sz)
        dst_row = dst_row + (n_rows & sz)


def _local_slots(seg_len_ref, tile, e_ref, rank_ref):
    offs, lo = [], 0
    for e in range(N_EXPERTS):
        offs.append(lo)
        lo = lo + seg_len_ref[tile * N_EXPERTS + e]
    e01 = e_ref[...]
    slot = rank_ref[...]
    for e in range(N_EXPERTS):
        slot = slot + jnp.where(e01 == e, offs[e], 0)
    return slot, offs


def _compact_rows(tm):
    return ((2 * tm + N_EXPERTS * (SUBLANES - 1) + LANES - 1) // LANES) * LANES


def _dispatch_kernel(seg_start_ref, seg_len_ref, misc_ref, h_ref, e_ref, rank_ref, xs_ref,
                     xc_ref, zero_ref, sems, *, sizes):
    tm = h_ref.shape[0]
    rc = xc_ref.shape[1]
    i = pl.program_id(0)
    buf = i % 2
    slot, _ = _local_slots(seg_len_ref, i, e_ref, rank_ref)
    rid = lax.broadcasted_iota(I32, (rc, tm), 0)
    sel = jnp.logical_or(rid == slot[0:1, :], rid == slot[1:2, :])
    perm = jnp.where(sel, 1.0, 0.0).astype(BF16)
    xc_ref[buf] = jnp.dot(perm, h_ref[...], preferred_element_type=F32)

    def segments(tile, b, start):
        lo = 0
        for e in range(N_EXPERTS):
            n8 = seg_len_ref[tile * N_EXPERTS + e]
            _segment_copies(xc_ref.at[b], lo, xs_ref, seg_start_ref[tile * N_EXPERTS + e], n8,
                            sems.at[b], sizes, start)
            lo = lo + n8

    @pl.when(i > 0)
    def _():
        segments(i - 1, 1 - buf, False)

    segments(i, buf, True)

    @pl.when(i == pl.num_programs(0) - 1)
    def _():
        segments(i, buf, False)
        zero_ref[...] = jnp.zeros(zero_ref.shape, F32)
        for start in (True, False):
            for e in range(N_EXPERTS):
                _segment_copies(zero_ref, 0, xs_ref, misc_ref[1 + e], misc_ref[1 + N_EXPERTS + e],
                                sems.at[0], sizes, start)

        def clear_tile(j, carry):
            cp = pltpu.make_async_copy(zero_ref, xs_ref.at[pl.ds(pl.multiple_of(j * tm, tm), tm), :],
                                       sems.at[0])
            cp.start()
            cp.wait()
            return carry

        lax.fori_loop(misc_ref[0], xs_ref.shape[0] // tm, clear_tile, 0)


def _dispatch(seg_start, seg_len, misc, h, e01, rank01, *, n_sorted, tm):
    n, d = h.shape
    pair = pl.BlockSpec((2, tm), lambda i, *_: (0, i))
    grid_spec = pltpu.PrefetchScalarGridSpec(
        num_scalar_prefetch=3,
        grid=(n // tm,),
        in_specs=[pl.BlockSpec((tm, d), lambda i, *_: (i, 0)), pair, pair],
        out_specs=pl.BlockSpec(memory_space=pl.ANY),
        scratch_shapes=[pltpu.VMEM((2, _compact_rows(tm), d), F32), pltpu.VMEM((tm, d), F32),
                        pltpu.SemaphoreType.DMA((2,))],
    )
    return pl.pallas_call(
        functools.partial(_dispatch_kernel, sizes=_piece_sizes(tm)),
        out_shape=jax.ShapeDtypeStruct((n_sorted, d), F32),
        grid_spec=grid_spec,
        compiler_params=_cparams(("arbitrary",), VMEM_LIMIT),
        name="dispatch",
    )(seg_start, seg_len, misc, h, e01, rank01)


def _moe_kernel(te_ref, misc_ref, xs_ref, wg_hbm, wu_hbm, wd_hbm, ys_ref,
                wg_buf, wu_buf, wd_buf, wgb_ref, wub_ref, wdb_ref, run_ref, sems, *, layer, tm):
    i = pl.program_id(0)
    e = te_ref[i]
    prev = te_ref[jnp.maximum(i - 1, 0)]
    n_used = misc_ref[0]

    def fetch(expert, slot):
        return [pltpu.make_async_copy(w.at[layer, expert], buf.at[slot], sems.at[slot, k])
                for k, (w, buf) in enumerate(((wg_hbm, wg_buf), (wu_hbm, wu_buf), (wd_hbm, wd_buf)))]

    @pl.when(i == 0)
    def _():
        run_ref[0] = 0
        for cp in fetch(e, 0):
            cp.start()

    @pl.when(jnp.logical_or(i == 0, e != prev))
    def _():
        slot = run_ref[0] % 2
        for cp in fetch(e, slot):
            cp.wait()
        nxt = (misc_ref[1 + e] + misc_ref[1 + N_EXPERTS + e]) // tm

        @pl.when(nxt < n_used)
        def _():
            for cp in fetch(te_ref[nxt], 1 - slot):
                cp.start()

        wgb_ref[...] = wg_buf[slot].astype(BF16)
        wub_ref[...] = wu_buf[slot].astype(BF16)
        wdb_ref[...] = wd_buf[slot].astype(BF16)
        run_ref[0] = run_ref[0] + 1

    @pl.when(i < misc_ref[0])
    def _():
        x = xs_ref[...].astype(BF16)
        gate = jnp.dot(x, wgb_ref[...], preferred_element_type=F32)
        up = jnp.dot(x, wub_ref[...], preferred_element_type=F32)
        act = (_silu(gate) * up).astype(BF16)
        ys_ref[...] = jnp.dot(act, wdb_ref[...], preferred_element_type=F32)

    @pl.when(i >= misc_ref[0])
    def _():
        ys_ref[...] = jnp.zeros(ys_ref.shape, F32)


def _moe(tile_expert, misc, xs, wg, wu, wd, layer, tm):
    n_sorted, d = xs.shape
    de = wg.shape[-1]
    xmap = lambda i, te, misc_ref: (jnp.minimum(i, jnp.maximum(misc_ref[0] - 1, 0)), 0)
    hbm = pl.BlockSpec(memory_space=pl.ANY)
    grid_spec = pltpu.PrefetchScalarGridSpec(
        num_scalar_prefetch=2,
        grid=(n_sorted // tm,),
        in_specs=[pl.BlockSpec((tm, d), xmap), hbm, hbm, hbm],
        out_specs=pl.BlockSpec((tm, d), lambda i, te, misc_ref: (i, 0)),
        scratch_shapes=[pltpu.VMEM((2, d, de), F32), pltpu.VMEM((2, d, de), F32),
                        pltpu.VMEM((2, de, d), F32),
                        pltpu.VMEM((d, de), BF16), pltpu.VMEM((d, de), BF16),
                        pltpu.VMEM((de, d), BF16),
                        pltpu.SMEM((1,), I32), pltpu.SemaphoreType.DMA((2, 3))],
    )
    return pl.pallas_call(
        functools.partial(_moe_kernel, layer=layer, tm=tm),
        out_shape=jax.ShapeDtypeStruct((n_sorted, d), F32),
        grid_spec=grid_spec,
        compiler_params=_cparams(("arbitrary",), VMEM_LIMIT),
        name="moe_experts",
    )(tile_expert, misc, xs, wg, wu, wd)


def _combine_kernel(seg_start_ref, seg_len_ref, x_ref, g2_ref, e_ref, rank_ref, w_ref, fg_ref, ys_ref,
                    o_ref, yc_ref, sems, *, sizes, final_norm):
    tm = x_ref.shape[0]
    rc = yc_ref.shape[1]
    i = pl.program_id(0)
    buf = i % 2

    def segments(tile, b, start):
        lo = 0
        for e in range(N_EXPERTS):
            n8 = seg_len_ref[tile * N_EXPERTS + e]
            _segment_copies(ys_ref, seg_start_ref[tile * N_EXPERTS + e], yc_ref.at[b], lo, n8,
                            sems.at[b], sizes, start)
            lo = lo + n8

    @pl.when(i == 0)
    def _():
        yc_ref[...] = jnp.zeros(yc_ref.shape, F32)
        segments(0, 0, True)

    @pl.when(i + 1 < pl.num_programs(0))
    def _():
        segments(i + 1, 1 - buf, True)

    segments(i, buf, False)
    slot, _ = _local_slots(seg_len_ref, i, e_ref, rank_ref)
    rid = lax.broadcasted_iota(I32, (rc, tm), 0)
    w = w_ref[...]
    pw = (jnp.where(rid == slot[0:1, :], w[0:1, :], 0.0)
          + jnp.where(rid == slot[1:2, :], w[1:2, :], 0.0))
    dn = (((0,), (0,)), ((), ()))
    out = lax.dot_general(pw.astype(BF16), yc_ref[buf].astype(BF16), dn, preferred_element_type=F32)
    x = x_ref[...] + g2_ref[0] * out
    if final_norm:
        x = x * lax.rsqrt(jnp.mean(x * x, axis=-1, keepdims=True) + EPS) * fg_ref[...]
    o_ref[...] = x


def _combine(seg_start, seg_len, xc, g2, e01, rank01, w01, fg, ys, *, n_rows, tm, tiles_per_seq,
             final_norm):
    d = xc.shape[1]
    nb = g2.shape[0] - 1
    pair = pl.BlockSpec((2, tm), lambda i, *_: (0, i))
    grid_spec = pltpu.PrefetchScalarGridSpec(
        num_scalar_prefetch=2,
        grid=(n_rows // tm,),
        in_specs=[pl.BlockSpec((tm, d), lambda i, *_: (i, 0)),
                  pl.BlockSpec((1, 1, d), lambda i, *_: (jnp.minimum(i // tiles_per_seq, nb), 0, 0)),
                  pair, pair, pair,
                  pl.BlockSpec((1, d), lambda i, *_: (0, 0)),
                  pl.BlockSpec(memory_space=pl.ANY)],
        out_specs=pl.BlockSpec((tm, d), lambda i, *_: (i, 0)),
        scratch_shapes=[pltpu.VMEM((2, _compact_rows(tm), d), F32), pltpu.SemaphoreType.DMA((2,))],
    )
    return pl.pallas_call(
        functools.partial(_combine_kernel, sizes=_piece_sizes(tm), final_norm=final_norm),
        out_shape=jax.ShapeDtypeStruct((n_rows, d), F32),
        grid_spec=grid_spec,
        compiler_params=_cparams(("arbitrary",), VMEM_LIMIT),
        name="combine",
    )(seg_start, seg_len, xc, g2, e01, rank01, w01, fg, ys)


def _lower_bounds(lb_param):
    p = jax.nn.softmax(lb_param.astype(F32), axis=0)
    return jnp.cumsum(p, axis=0) - p[0]


def kernel(x, c, ctx, c_ctx, w_ada, b_ada, norm1_g, norm2_g, w_in, conv_w, conv_b, conv_ln_g,
           conv_ln_b, lb_fwd, lb_bwd, hgrn_norm_g, w_out, router_w, router_bias, w_gate, w_up,
           w_down, final_norm_g):
    bn, seq, d = x.shape
    ctx_len = ctx.shape[1]
    depth = w_ada.shape[0]
    n = bn * seq
    nc = bn * ctx_len
    tm = min(512, seq, nc)
    assert seq % tm == 0 and nc % tm == 0 and seq % GRID_W == 0
    tps = seq // tm

    pad_rows = (-(bn + 1)) % 8
    cc = jnp.concatenate([c, c_ctx[None, :], jnp.zeros((pad_rows, d), F32)], axis=0)
    mod = _ada(cc, w_ada, b_ada)[:, :bn + 1]
    mod = mod.reshape(depth, bn + 1, 6, 1, d)

    lbs_f = _lower_bounds(lb_fwd)
    lbs_b = _lower_bounds(lb_bwd)
    w_in_bf = w_in.astype(BF16)
    w_out_bf = w_out.astype(BF16)
    rw_t = router_w.T.astype(F32)
    rb = router_bias.reshape(N_EXPERTS, 1).astype(F32)
    hg = hgrn_norm_g.astype(F32)
    fg = final_norm_g.reshape(1, d).astype(F32)

    rows_in = [x.reshape(n, d), ctx.reshape(nc, d)]
    s_zero = jnp.zeros((bn, 2 * HEADS, HEAD_DIM, HEAD_DIM), F32)

    for l in range(depth):
        last = l == depth - 1
        sh1, sc1, g1, sh2, sc2, g2 = (mod[l, :, k] for k in range(6))
        row = lambda a: a.reshape(1, -1).astype(F32)

        u, q, iv, lff, kf, lfb, kb, og = _inproj(
            rows_in, row(norm1_g[l]), sh1, sc1, w_in_bf, l, row(lbs_f[l]), row(lbs_b[l]), tm, tps)

        cw, cb = conv_w[l].astype(F32), row(conv_b[l])
        clg, clb = row(conv_ln_g[l]), row(conv_ln_b[l])
        scan = functools.partial(_scan, q, iv, lff, kf, lfb, kb)
        if last:
            (s_ctx,) = scan(s_zero, row_off=n, seq_len=ctx_len, compute_o=False)
            mix_ctx = [[], [], []]
            n_rows = n
        else:
            of_c, ob_c, s_ctx = scan(s_zero, row_off=n, seq_len=ctx_len, compute_o=True)
            ycx_c = _conv_rows(u, cw, cb, clg, clb, seq=ctx_len, row_off=n, n_rows=nc)
            mix_ctx = [[ycx_c], [of_c], [ob_c]]
            n_rows = n + nc
        of, ob, _ = scan(s_ctx, row_off=0, seq_len=seq, compute_o=True)
        if l % 2 == 0:
            ycx = _conv_rows(u, cw, cb, clg, clb, seq=GRID_W, row_off=0, n_rows=n)
        else:
            ycx = _conv_cols(u, cw, cb, clg, clb, batch=bn, seq_len=seq)
        mix = [[a] + c_ for a, c_ in zip((ycx, of, ob), mix_ctx)]

        res_rows = rows_in if (len(rows_in) == 1 or not last) else [rows_in[0]]
        xc, h2, e01, rank01, w01, cnt = _outproj_route(
            res_rows, mix, og, hg[l].reshape(1, HEAD_DIM), w_out_bf, l, g1,
            row(norm2_g[l]), sh2, sc2, rw_t, rb, n_rows=n_rows, tm=tm, tiles_per_seq=tps)
        n_tt = n_rows // tm
        n_et = (2 * n_rows + n_tt * N_EXPERTS * (SUBLANES - 1) + tm - 1) // tm + N_EXPERTS
        seg_start, seg_len, te, misc = _tables(cnt[:, :, 0].reshape(-1), n_tt=n_tt, n_et=n_et, tm=tm)
        xs = _dispatch(seg_start, seg_len, misc, h2, e01, rank01, n_sorted=n_et * tm, tm=tm)
        ys = _moe(te, misc, xs, w_gate, w_up, w_down, l, tm)
        xc = _combine(seg_start, seg_len, xc, g2, e01, rank01, w01, fg, ys, n_rows=n_rows, tm=tm,
                      tiles_per_seq=tps, final_norm=last)
        rows_in = [xc]

    return xc[:n].reshape(bn, seq, d)
```

```python
import functools

import numpy as np
import jax
import jax.numpy as jnp
from jax import lax
from jax.experimental import pallas as pl
from jax.experimental.pallas import tpu as pltpu

F32 = jnp.float32
BF16 = jnp.bfloat16
I32 = jnp.int32

EPS = 1e-6
GRID_W = 64
HEADS = 4
HEAD_DIM = 128
D_GROUP = HEADS * HEAD_DIM
N_IN_GROUPS = 7
CONV_K = 31
CONV_HALF = CONV_K // 2
CONV_GAP = 16
CONV_ROW_GROUP = 1
CONV_PARTIALS = 2
N_EXPERTS = 16
N_GROUPS = 4
EXPERTS_PER_GROUP = N_EXPERTS // N_GROUPS
SCAN_CHUNK = 64
SCAN_FAST_LIMIT = 60.0
LANES = 128
SUBLANES = 8
VMEM_LIMIT = 56 * 1024 * 1024


def _cparams(sem, vmem=None):
    return pltpu.CompilerParams(dimension_semantics=sem, vmem_limit_bytes=vmem)


def _sigmoid(x):
    return 0.5 * jnp.tanh(0.5 * x) + 0.5


def _silu(x):
    return x * _sigmoid(x)


def _ada_kernel(c_ref, w_ref, b_ref, o_ref):
    a = _silu(c_ref[...]).astype(BF16)
    w = w_ref[0].astype(BF16)
    o_ref[0] = jnp.dot(a, w, preferred_element_type=F32) + b_ref[0]


def _ada(cc, w_ada, b_ada, tn=1536):
    depth, d, d6 = w_ada.shape
    rows = cc.shape[0]
    return pl.pallas_call(
        _ada_kernel,
        out_shape=jax.ShapeDtypeStruct((depth, rows, d6), F32),
        grid=(depth, d6 // tn),
        in_specs=[
            pl.BlockSpec((rows, d), lambda l, j: (0, 0)),
            pl.BlockSpec((1, d, tn), lambda l, j: (l, 0, j)),
            pl.BlockSpec((1, 1, tn), lambda l, j: (l, 0, j)),
        ],
        out_specs=pl.BlockSpec((1, rows, tn), lambda l, j: (l, 0, j)),
        compiler_params=_cparams(("arbitrary", "arbitrary"), VMEM_LIMIT),
        name="ada",
    )(cc, w_ada, b_ada.reshape(depth, 1, d6))


def _modulate(x, g, shift, scale):
    r = lax.rsqrt(jnp.mean(x * x, axis=-1, keepdims=True) + EPS)
    return (x * r * g) * (1.0 + scale) + shift


def _rows_from(refs, n_lat_tiles):
    if len(refs) == 1:
        return refs[0][...]
    return jnp.where(pl.program_id(0) >= n_lat_tiles, refs[1][...], refs[0][...])


def _row_specs(arrays, tm):
    width = arrays[0].shape[1]
    if len(arrays) == 1:
        return [pl.BlockSpec((tm, width), lambda i, *_: (i, 0))], 0
    n_lat_tiles = arrays[0].shape[0] // tm
    return [pl.BlockSpec((tm, width), lambda i, *_: (jnp.minimum(i, n_lat_tiles - 1), 0)),
            pl.BlockSpec((tm, width), lambda i, *_: (jnp.maximum(i - n_lat_tiles, 0), 0))], n_lat_tiles


def _inproj_kernel(*refs, n_src, n_lat_tiles):
    x = _rows_from(refs[:n_src], n_lat_tiles)
    (g_ref, sh_ref, sc_ref, w_ref, lbf_ref, lbb_ref,
     u_ref, q_ref, i_ref, lff_ref, kf_ref, lfb_ref, kb_ref, og_ref) = refs[n_src:]
    hb = _modulate(x, g_ref[...], sh_ref[0], sc_ref[0]).astype(BF16)

    def proj(j):
        return jnp.dot(hb, w_ref[0, :, j * D_GROUP:(j + 1) * D_GROUP], preferred_element_type=F32)

    u_ref[...] = (proj(0) * _sigmoid(proj(1))).astype(BF16)
    q_ref[...] = _silu(proj(2)).astype(BF16)
    i_ref[...] = proj(3).astype(BF16)
    for j, lb_ref, lf_ref, k_ref in ((4, lbf_ref, lff_ref, kf_ref), (5, lbb_ref, lfb_ref, kb_ref)):
        lb = lb_ref[...]
        f = lb + (1.0 - lb) * jax.nn.sigmoid(proj(j))
        lf_ref[...] = jnp.log(f)
        k_ref[...] = (1.0 - f).astype(BF16)
    og_ref[...] = _silu(proj(6)).astype(BF16)


def _inproj(xs, g, sh, sc, w_bf, layer, lbf, lbb, tm, tiles_per_seq):
    n = sum(a.shape[0] for a in xs)
    d = xs[0].shape[1]
    x_specs, n_lat_tiles = _row_specs(xs, tm)
    nb = sh.shape[0] - 1
    mod_spec = pl.BlockSpec((1, 1, d), lambda i: (jnp.minimum(i // tiles_per_seq, nb), 0, 0))
    row_spec = pl.BlockSpec((tm, D_GROUP), lambda i: (i, 0))
    vec_spec = pl.BlockSpec((1, D_GROUP), lambda i: (0, 0))
    outs = [jax.ShapeDtypeStruct((n, D_GROUP), dt)
            for dt in (BF16, BF16, BF16, F32, BF16, F32, BF16, BF16)]
    return pl.pallas_call(
        functools.partial(_inproj_kernel, n_src=len(xs), n_lat_tiles=n_lat_tiles),
        out_shape=outs,
        grid=(n // tm,),
        in_specs=x_specs + [
            pl.BlockSpec((1, d), lambda i: (0, 0)),
            mod_spec, mod_spec,
            pl.BlockSpec((1, d, N_IN_GROUPS * D_GROUP), lambda i: (layer, 0, 0)),
            vec_spec, vec_spec,
        ],
        out_specs=[row_spec] * 8,
        compiler_params=_cparams(("arbitrary",), VMEM_LIMIT),
        name="inproj",
    )(*xs, g, sh, sc, w_bf, lbf, lbb)


def _scan_tables(c, backward):
    levels = int(np.log2(c))
    r_all = np.zeros((levels * c + 2 * c + 8, c), np.float32)
    for lev in range(levels):
        h = 1 << lev
        for r in range(c):
            bd = (r // (2 * h)) * 2 * h + h
            if not backward:
                if r >= bd:
                    r_all[lev * c + r, bd:r + 1] = 1.0
                else:
                    r_all[lev * c + r, r + 1:bd] = 1.0
            else:
                if r < bd:
                    r_all[lev * c + r, r:bd] = 1.0
                else:
                    r_all[lev * c + r, bd:r] = 1.0
    base = levels * c
    for r in range(c):
        if not backward:
            r_all[base + r, :r + 1] = 1.0
            r_all[base + c + r, r + 1:] = 1.0
        else:
            r_all[base + r, r:] = 1.0
            r_all[base + c + r, :r] = 1.0
    r_all[base + 2 * c:, :] = 1.0
    return r_all


def _scan_kernel(*refs, n_chunks, compute_o):
    c = SCAN_CHUNK
    levels = int(np.log2(c))
    (rf_ref, rb_ref, qf_ref, vf_ref, lff_ref, kf_ref,
     qb_ref, vb_ref, lfb_ref, kb_ref, s0_ref, trif_ref, trib_ref) = refs[:13]
    pos = 13
    if compute_o:
        lff_next_ref, lfb_next_ref = refs[pos:pos + 2]
        of_ref, ob_ref, sfin_ref, st_ref, bsum_ref, flag_ref = refs[pos + 2:pos + 8]
    else:
        sfin_ref, st_ref = refs[pos:pos + 2]
        of_ref = ob_ref = None
    j = pl.program_id(1)

    @pl.when(j == 0)
    def _():
        st_ref[...] = s0_ref[0]

    row = lax.broadcasted_iota(jnp.int32, (c, 1), 0)
    ri = lax.broadcasted_iota(jnp.int32, (c, c), 0)
    ci = lax.broadcasted_iota(jnp.int32, (c, c), 1)
    upper = [(row & (2 * (1 << lev) - 1)) >= (1 << lev) for lev in range(levels)]
    same_parent = [(ri >> (lev + 1)) == (ci >> (lev + 1)) for lev in range(levels)]
    diag = ri == ci
    dn_t = (((1,), (1,)), ((), ()))
    dn_tl = (((0,), (0,)), ((), ()))

    half = c // 2
    tri = [jnp.where(ci <= ri, 1.0, 0.0).astype(BF16), jnp.where(ci >= ri, 1.0, 0.0).astype(BF16)]
    first_half = [row < half, row >= half]
    re = lax.broadcasted_iota(jnp.int32, (c, 4 * c), 0)
    ce = lax.broadcasted_iota(jnp.int32, (c, 4 * c), 1)
    cs = ce & (c - 1)
    same_half = (re >= half) == (cs >= half)
    own = (ce & (2 * c - 1)) < c
    mask_ext = [
        (own & same_half & (cs <= re)) | (jnp.logical_not(own) & (re >= half) & (cs < half)),
        (own & same_half & (cs >= re)) | (jnp.logical_not(own) & (re < half) & (cs >= half)),
    ]

    mrow, trow = [half - 1, half], [c - 1, 0]

    def prefix(tri_mat, g):
        g_hi = g.astype(BF16)
        g_lo = (g - g_hi.astype(F32)).astype(BF16)
        return (jnp.dot(tri_mat, g_hi, preferred_element_type=F32)
                + jnp.dot(tri_mat, g_lo, preferred_element_type=F32))

    def next_state(st, hs, kh, vh, b, tot):
        kd = (kh * jnp.exp(tot[:, hs] - b[:, hs])).astype(BF16)
        return st * jnp.exp(tot[:, hs]) + lax.dot_general(vh, kd, dn_tl, preferred_element_type=F32)

    def state_only_chunk(dirn, lf_ref, k_ref, v_ref, r0):
        rows = pl.ds(r0, c)
        b = prefix(tri[dirn], lf_ref[rows, :])
        tot = b[trow[dirn]:trow[dirn] + 1, :]
        for h in range(HEADS):
            hs = slice(h * HEAD_DIM, (h + 1) * HEAD_DIM)
            st_ref[dirn * HEADS + h] = next_state(st_ref[dirn * HEADS + h], hs,
                                                  k_ref[rows, hs].astype(F32), v_ref[rows, hs], b, tot)

    def fast_chunk(dirn, q_ref, v_ref, k_ref, o_ref, rows, b, m, tot, states):
        fh = first_half[dirn]
        hd = HEAD_DIM

        def block_diag(a, bb):
            za = jnp.zeros((a.shape[0], bb.shape[1]), a.dtype)
            zb = jnp.zeros((bb.shape[0], a.shape[1]), a.dtype)
            return jnp.concatenate([jnp.concatenate([a, za], axis=1),
                                    jnp.concatenate([zb, bb], axis=1)], axis=0)

        for hp in range(HEADS // 2):
            ps = slice(2 * hp * hd, 2 * (hp + 1) * hd)
            bp, mp, tp = b[:, ps], m[:, ps], tot[:, ps]
            cdec = bp - jnp.where(fh, 0.0, mp)
            e_own = jnp.exp(cdec).astype(BF16)
            e_own_inv = jnp.exp(-cdec).astype(BF16)
            e_cross = jnp.where(fh, jnp.exp(jnp.minimum(mp - bp, 0.0)), 0.0).astype(BF16)
            e_inc = jnp.exp(bp).astype(BF16)
            e_dec = jnp.exp(tp - bp).astype(BF16)
            qp, kp, vp = q_ref[rows, ps], k_ref[rows, ps], v_ref[rows, ps]
            q1, qi = qp * e_own, qp * e_inc
            k1, k2, kd = kp * e_own_inv, kp * e_cross, kp * e_dec
            kcat = [jnp.concatenate([k1[:, s_], k2[:, s_]], axis=0) for s_ in (slice(0, hd), slice(hd, 2 * hd))]
            sc = lax.dot_general(q1, block_diag(*kcat), dn_t, preferred_element_type=F32)
            p = jnp.where(mask_ext[dirn], sc, 0.0).astype(BF16)
            v2 = [jnp.concatenate([vp[:, s_], vp[:, s_]], axis=0) for s_ in (slice(0, hd), slice(hd, 2 * hd))]
            st = [states[dirn * HEADS + 2 * hp + a] for a in range(2)]
            o = (jnp.dot(p, block_diag(*v2), preferred_element_type=F32)
                 + lax.dot_general(qi, block_diag(st[0].astype(BF16), st[1].astype(BF16)), dn_t,
                                   preferred_element_type=F32))
            o_ref[rows, ps] = o.astype(o_ref.dtype)
            upd = lax.dot_general(vp, kd, dn_tl, preferred_element_type=F32)
            e_tot = jnp.exp(tp)
            for a in range(2):
                sl = slice(a * hd, (a + 1) * hd)
                states[dirn * HEADS + 2 * hp + a] = st[a] * e_tot[:, sl] + upd[sl, sl]

    def robust_chunk(dirn, r_ref, q_ref, v_ref, lf_ref, k_ref, o_ref, r0):
        rows = pl.ds(r0, c)
        g = lf_ref[rows, :]
        g_hi = g.astype(BF16)
        g_lo = (g - g_hi.astype(F32)).astype(BF16)
        rmat = r_ref[...]
        e_all = jnp.exp(jnp.dot(rmat, g_hi, preferred_element_type=F32)
                        + jnp.dot(rmat, g_lo, preferred_element_type=F32))
        base = levels * c
        for h in range(HEADS):
            hs = slice(h * HEAD_DIM, (h + 1) * HEAD_DIM)
            qh = q_ref[rows, hs].astype(F32)
            kh = k_ref[rows, hs].astype(F32)
            vh = v_ref[rows, hs]
            st = st_ref[dirn * HEADS + h]
            if compute_o:
                att = jnp.where(diag, lax.dot_general(qh.astype(BF16), kh.astype(BF16), dn_t,
                                                      preferred_element_type=F32), 0.0)
                for lev in range(levels):
                    e_l = e_all[lev * c:(lev + 1) * c, hs]
                    q_side = upper[lev] if dirn == 0 else jnp.logical_not(upper[lev])
                    qt = jnp.where(q_side, qh * e_l, 0.0).astype(BF16)
                    kt = jnp.where(q_side, 0.0, kh * e_l).astype(BF16)
                    a_l = lax.dot_general(qt, kt, dn_t, preferred_element_type=F32)
                    att = att + jnp.where(same_parent[lev], a_l, 0.0)
                qi = (qh * e_all[base:base + c, hs]).astype(BF16)
                o = lax.dot_general(qi, st.astype(BF16), dn_t, preferred_element_type=F32)
                o = o + jnp.dot(att.astype(BF16), vh, preferred_element_type=F32)
                o_ref[rows, hs] = o.astype(o_ref.dtype)
            kd = (kh * e_all[base + c:base + 2 * c, hs]).astype(BF16)
            e_tot = e_all[base + 2 * c:base + 2 * c + 1, hs]
            st_ref[dirn * HEADS + h] = st * e_tot + lax.dot_general(
                vh, kd, dn_tl, preferred_element_type=F32)

    dir_refs = ((rf_ref, qf_ref, vf_ref, lff_ref, kf_ref, of_ref),
                (rb_ref, qb_ref, vb_ref, lfb_ref, kb_ref, ob_ref))

    def chunk_rows(ci_):
        return pl.multiple_of(ci_ * c, c), pl.multiple_of((n_chunks - 1 - ci_) * c, c)

    if not compute_o:
        def state_body(ci_, carry):
            for dirn, r0 in enumerate(chunk_rows(ci_)):
                _, _, v_ref, lf_ref, k_ref, _ = dir_refs[dirn]
                state_only_chunk(dirn, lf_ref, k_ref, v_ref, r0)
            return carry

        lax.fori_loop(0, n_chunks, state_body, 0)
    else:
        step_id = pl.program_id(0) * pl.num_programs(1) + j
        cur = step_id % 2

        def prepare(lf_refs, slot):
            worst = None
            for dirn in range(2):
                bsum = prefix((trif_ref, trib_ref)[dirn][...], lf_refs[dirn][...])
                bsum_ref[slot, dirn] = bsum
                for ch in range(n_chunks):
                    m = bsum[ch * c + mrow[dirn]:ch * c + mrow[dirn] + 1, :]
                    tot = bsum[ch * c + trow[dirn]:ch * c + trow[dirn] + 1, :]
                    w = jnp.minimum(m, tot - m)
                    worst = w if worst is None else jnp.minimum(worst, w)
            flag_ref[slot] = (jnp.min(worst) >= -SCAN_FAST_LIMIT).astype(jnp.int32)

        @pl.when(step_id == 0)
        def _():
            prepare((lff_ref, lfb_ref), 0)

        fast_ok = flag_ref[cur] == 1

        @pl.when(fast_ok)
        def _():
            states = [st_ref[i] for i in range(2 * HEADS)]
            for step in range(n_chunks):
                for dirn in range(2):
                    ch = step if dirn == 0 else n_chunks - 1 - step
                    _, q_ref, v_ref, _, k_ref, o_ref = dir_refs[dirn]
                    rows = slice(ch * c, (ch + 1) * c)
                    m = bsum_ref[cur, dirn, ch * c + mrow[dirn]:ch * c + mrow[dirn] + 1, :]
                    tot = bsum_ref[cur, dirn, ch * c + trow[dirn]:ch * c + trow[dirn] + 1, :]
                    fast_chunk(dirn, q_ref, v_ref, k_ref, o_ref, rows, bsum_ref[cur, dirn, rows, :],
                               m, tot, states)
            for i in range(2 * HEADS):
                st_ref[i] = states[i]
            prepare((lff_next_ref, lfb_next_ref), 1 - cur)

        @pl.when(jnp.logical_not(fast_ok))
        def _():
            def robust_body(ci_, carry):
                for dirn, r0 in enumerate(chunk_rows(ci_)):
                    r_ref, q_ref, v_ref, lf_ref, k_ref, o_ref = dir_refs[dirn]
                    robust_chunk(dirn, r_ref, q_ref, v_ref, lf_ref, k_ref, o_ref, r0)
                return carry

            prepare((lff_next_ref, lfb_next_ref), 1 - cur)
            lax.fori_loop(0, n_chunks, robust_body, 0)

    @pl.when(j == pl.num_programs(1) - 1)
    def _():
        sfin_ref[0] = st_ref[...]


def _scan(q, v, lff, kf, lfb, kb, s0, *, row_off, seq_len, compute_o):
    batch = s0.shape[0]
    t = min(256, seq_len)
    nt = seq_len // t
    off = row_off // t
    fwd = lambda b, j: (off + b * nt + j, 0)
    bwd = lambda b, j: (off + b * nt + (nt - 1 - j), 0)
    ofwd = lambda b, j: (b * nt + j, 0)
    obwd = lambda b, j: (b * nt + (nt - 1 - j), 0)
    blk = lambda im: pl.BlockSpec((t, D_GROUP), im)
    rf = jnp.asarray(_scan_tables(SCAN_CHUNK, False), BF16)
    rb = jnp.asarray(_scan_tables(SCAN_CHUNK, True), BF16)
    rspec = pl.BlockSpec(rf.shape, lambda b, j: (0, 0))
    sspec = pl.BlockSpec((1, 2 * HEADS, HEAD_DIM, HEAD_DIM), lambda b, j: (b, 0, 0, 0))
    pos_t = np.arange(t)
    same_chunk = (pos_t[:, None] // SCAN_CHUNK) == (pos_t[None, :] // SCAN_CHUNK)
    tri_f = jnp.asarray(same_chunk & (pos_t[None, :] <= pos_t[:, None]), BF16)
    tri_b = jnp.asarray(same_chunk & (pos_t[None, :] >= pos_t[:, None]), BF16)
    tspec = pl.BlockSpec((t, t), lambda b, j: (0, 0))
    in_specs = [rspec, rspec, blk(fwd), blk(fwd), blk(fwd), blk(fwd),
                blk(bwd), blk(bwd), blk(bwd), blk(bwd), sspec, tspec, tspec]
    args = [rf, rb, q, v, lff, kf, q, v, lfb, kb, s0, tri_f, tri_b]
    out_shape, out_specs = [], []
    scratch = [pltpu.VMEM((2 * HEADS, HEAD_DIM, HEAD_DIM), F32)]
    if compute_o:
        def next_step(b, j):
            nxt = jnp.minimum(b * nt + j + 1, batch * nt - 1)
            return nxt // nt, nxt % nt

        in_specs += [blk(lambda b, j: fwd(*next_step(b, j))), blk(lambda b, j: bwd(*next_step(b, j)))]
        args += [lff, lfb]
        out_shape += [jax.ShapeDtypeStruct((batch * seq_len, D_GROUP), BF16)] * 2
        out_specs += [blk(ofwd), blk(obwd)]
        scratch += [pltpu.VMEM((2, 2, t, D_GROUP), F32), pltpu.SMEM((2,), I32)]
    out_shape.append(jax.ShapeDtypeStruct(s0.shape, F32))
    out_specs.append(sspec)
    kern = functools.partial(_scan_kernel, n_chunks=t // SCAN_CHUNK, compute_o=compute_o)
    return pl.pallas_call(
        kern,
        out_shape=out_shape,
        grid=(batch, nt),
        in_specs=in_specs,
        out_specs=out_specs,
        scratch_shapes=scratch,
        compiler_params=_cparams(("arbitrary", "arbitrary"), VMEM_LIMIT),
        name="hgrn_scan",
    )(*args)


def _ln_silu(y, g, b):
    mu = jnp.mean(y, axis=-1, keepdims=True)
    yc = y - mu
    var = jnp.mean(yc * yc, axis=-1, keepdims=True)
    return _silu(yc * lax.rsqrt(var + EPS) * g + b)


def _tree_sum(terms):
    while len(terms) > 1:
        terms = [terms[i] + terms[i + 1] if i + 1 < len(terms) else terms[i]
                 for i in range(0, len(terms), 2)]
    return terms[0]


def _conv_rows_kernel(u_ref, w_ref, b_ref, lg_ref, lb_ref, y_ref, pad_ref, acc_ref, *, seq, tb):
    nseq = tb // seq
    stride = seq + CONV_GAP
    n_pad = pad_ref.shape[2]
    n_lt = D_GROUP // LANES
    for cl in range(n_lt):
        ls = slice(cl * LANES, (cl + 1) * LANES)
        pad_ref[0, cl] = jnp.zeros(pad_ref.shape[2:], F32)
        for s in range(nseq):
            pad_ref[0, cl, CONV_GAP + s * stride:CONV_GAP + s * stride + seq, :] = (
                u_ref[s * seq:(s + 1) * seq, ls].astype(F32))
        for p in range(1, SUBLANES):
            pad_ref[p, cl, 0:n_pad - SUBLANES, :] = pad_ref[0, cl, p:p + n_pad - SUBLANES, :]
    rc = 64
    group = CONV_ROW_GROUP
    assert seq % rc == 0 and tb % (group * rc) == 0
    if seq == rc:
        member_off = [a * stride for a in range(group)]
        group_base = lambda g: g * (group * stride)
    else:
        gps = seq // (group * rc)
        assert seq % (group * rc) == 0
        member_off = [a * rc for a in range(group)]
        group_base = lambda g: (g // gps) * stride + (g % gps) * (group * rc)

    def one_group(g, carry):
        pad0 = pl.multiple_of(CONV_GAP + group_base(g), SUBLANES)
        out0 = pl.multiple_of(g * (group * rc), group * rc)
        for cl in range(n_lt):
            ls = slice(cl * LANES, (cl + 1) * LANES)
            parts = [[None] * CONV_PARTIALS for _ in range(group)]
            for k in range(CONV_K):
                off = k - CONV_HALF
                wk = w_ref[k:k + 1, ls]
                for a in range(group):
                    src = pl.ds(pad0 + member_off[a] + (off // SUBLANES) * SUBLANES, rc)
                    term = wk * pad_ref[off % SUBLANES, cl, src, :]
                    prev = parts[a][k % CONV_PARTIALS]
                    parts[a][k % CONV_PARTIALS] = term if prev is None else prev + term
            for a in range(group):
                acc_ref[pl.ds(out0 + a * rc, rc), ls] = _tree_sum(parts[a])
        return carry

    lax.fori_loop(0, tb // (group * rc), one_group, 0)
    y = acc_ref[...] + b_ref[...]
    y_ref[...] = _ln_silu(y, lg_ref[...], lb_ref[...]).astype(y_ref.dtype)


def _conv_rows(u, w, b, lg, lb, *, seq, row_off, n_rows):
    tb = max(seq, 512) if n_rows % max(seq, 512) == 0 else seq
    nseq = tb // seq
    off = row_off // tb
    vec = pl.BlockSpec((1, D_GROUP), lambda i: (0, 0))
    in_specs = [pl.BlockSpec((tb, D_GROUP), lambda i: (off + i, 0)),
                pl.BlockSpec((CONV_K, D_GROUP), lambda i: (0, 0)), vec, vec, vec]
    args = [u, w, b, lg, lb]
    kern = functools.partial(_conv_rows_kernel, seq=seq, tb=tb)
    return pl.pallas_call(
        kern,
        out_shape=jax.ShapeDtypeStruct((n_rows, D_GROUP), BF16),
        grid=(n_rows // tb,),
        in_specs=in_specs,
        out_specs=pl.BlockSpec((tb, D_GROUP), lambda i: (i, 0)),
        scratch_shapes=[pltpu.VMEM((SUBLANES, D_GROUP // LANES, CONV_GAP + nseq * (seq + CONV_GAP), LANES), F32),
                        pltpu.VMEM((tb, D_GROUP), F32)],
        compiler_params=_cparams(("arbitrary",), VMEM_LIMIT),
        name="conv_rows",
    )(*args)


CONV_COL_TILE = 16


def _conv_cols_kernel(u_ref, w_ref, b_ref, lg_ref, lb_ref, y_ref, uf_ref, *, n_rows):
    n_lt = D_GROUP // LANES
    row_stride = GRID_W + SUBLANES
    for cl in range(n_lt):
        ls = slice(cl * LANES, (cl + 1) * LANES)
        for r in range(n_rows):
            uf_ref[cl, r * row_stride:r * row_stride + GRID_W, :] = (
                u_ref[r * GRID_W:(r + 1) * GRID_W, ls].astype(F32))
    bias = b_ref[...]
    lg = lg_ref[...]
    lb = lb_ref[...]

    def col_tile(wi, carry):
        w0 = pl.multiple_of(wi * CONV_COL_TILE, CONV_COL_TILE)
        for r in range(n_rows):
            tiles = []
            for cl in range(n_lt):
                ls = slice(cl * LANES, (cl + 1) * LANES)
                parts = [None] * CONV_PARTIALS
                for k in range(CONV_K):
                    rr = r + k - CONV_HALF
                    if 0 <= rr < n_rows:
                        src = pl.ds(pl.multiple_of(rr * row_stride + w0, SUBLANES), CONV_COL_TILE)
                        term = w_ref[k:k + 1, ls] * uf_ref[cl, src, :]
                        prev = parts[k % CONV_PARTIALS]
                        parts[k % CONV_PARTIALS] = term if prev is None else prev + term
                tiles.append(_tree_sum([p for p in parts if p is not None]))
            dst = pl.ds(pl.multiple_of(r * GRID_W + w0, CONV_COL_TILE), CONV_COL_TILE)
            y_ref[dst, :] = _ln_silu(jnp.concatenate(tiles, axis=1) + bias, lg, lb).astype(y_ref.dtype)
        return carry

    lax.fori_loop(0, GRID_W // CONV_COL_TILE, col_tile, 0)


def _conv_cols(u, w, b, lg, lb, *, batch, seq_len):
    vec = pl.BlockSpec((1, D_GROUP), lambda bi: (0, 0))
    blk = pl.BlockSpec((seq_len, D_GROUP), lambda bi: (bi, 0))
    return pl.pallas_call(
        functools.partial(_conv_cols_kernel, n_rows=seq_len // GRID_W),
        out_shape=jax.ShapeDtypeStruct((batch * seq_len, D_GROUP), BF16),
        grid=(batch,),
        in_specs=[blk, pl.BlockSpec((CONV_K, D_GROUP), lambda bi: (0, 0)), vec, vec, vec],
        out_specs=blk,
        scratch_shapes=[pltpu.VMEM((D_GROUP // LANES, (seq_len // GRID_W) * (GRID_W + SUBLANES), LANES), F32)],
        compiler_params=_cparams(("arbitrary",), VMEM_LIMIT),
        name="conv_cols",
    )(u, w, b, lg, lb)


def _outproj_kernel(*refs, n_x, n_mix, n_lat_tiles):
    x = _rows_from(refs[:n_x], n_lat_tiles)
    pos = n_x
    ycx, of, ob = (_rows_from(refs[pos + k * n_mix:pos + (k + 1) * n_mix], n_lat_tiles) for k in range(3))
    og_ref, hg_ref, w_ref, g1_ref = refs[pos + 3 * n_mix:pos + 3 * n_mix + 4]
    route_in = refs[pos + 3 * n_mix + 4:pos + 3 * n_mix + 10]
    o_ref = refs[pos + 3 * n_mix + 10]
    route_out = refs[pos + 3 * n_mix + 11:]
    o = of.astype(F32) + ob.astype(F32)
    og = og_ref[...].astype(F32)
    hg = hg_ref[...]
    acc = jnp.dot(ycx, w_ref[0, 0:D_GROUP, :], preferred_element_type=F32)
    for h in range(HEADS):
        hs = slice(h * HEAD_DIM, (h + 1) * HEAD_DIM)
        oh = o[:, hs]
        r = lax.rsqrt(jnp.mean(oh * oh, axis=-1, keepdims=True) + EPS)
        yh = ((oh * r * hg) * og[:, hs]).astype(BF16)
        acc = acc + jnp.dot(yh, w_ref[0, D_GROUP + h * HEAD_DIM:D_GROUP + (h + 1) * HEAD_DIM, :],
                            preferred_element_type=F32)
    x = x + g1_ref[0] * acc
    o_ref[...] = x
    _route(x, *route_in, *route_out)


def _outproj_route(xs, mix, og, hg, w_bf, layer, g1, g2n, sh2, sc2, rw_t, rb, *, n_rows, tm, tiles_per_seq):
    d = xs[0].shape[1]
    nb = g1.shape[0] - 1
    n_tt = n_rows // tm
    x_specs, n_lat_tiles = _row_specs(xs, tm)
    mix_specs, mix_args = [], []
    for arrays in mix:
        specs, nl = _row_specs(arrays, tm)
        assert len(arrays) == len(mix[0]) and (nl == 0 or n_lat_tiles in (0, nl))
        n_lat_tiles = max(n_lat_tiles, nl)
        mix_specs += specs
        mix_args += list(arrays)
    row = lambda width: pl.BlockSpec((tm, width), lambda i: (i, 0))
    mod_spec = pl.BlockSpec((1, 1, d), lambda i: (jnp.minimum(i // tiles_per_seq, nb), 0, 0))
    pair = pl.BlockSpec((2, tm), lambda i: (0, i))
    kern = functools.partial(_outproj_kernel, n_x=len(xs), n_mix=len(mix[0]), n_lat_tiles=n_lat_tiles)
    before = jnp.asarray(np.arange(tm)[:, None] < np.arange(tm)[None, :], BF16)
    return pl.pallas_call(
        kern,
        out_shape=[jax.ShapeDtypeStruct((n_rows, d), F32),
                   jax.ShapeDtypeStruct((n_rows, d), BF16),
                   jax.ShapeDtypeStruct((2, n_rows), I32),
                   jax.ShapeDtypeStruct((2, n_rows), I32),
                   jax.ShapeDtypeStruct((2, n_rows), F32),
                   jax.ShapeDtypeStruct((n_tt, N_EXPERTS, LANES), I32)],
        grid=(n_tt,),
        in_specs=x_specs + mix_specs + [row(D_GROUP),
                  pl.BlockSpec((1, HEAD_DIM), lambda i: (0, 0)),
                  pl.BlockSpec((1,) + w_bf.shape[1:], lambda i: (layer, 0, 0)),
                  mod_spec,
                  pl.BlockSpec((1, d), lambda i: (0, 0)), mod_spec, mod_spec,
                  pl.BlockSpec((N_EXPERTS, d), lambda i: (0, 0)),
                  pl.BlockSpec((N_EXPERTS, 1), lambda i: (0, 0)),
                  pl.BlockSpec((tm, tm), lambda i: (0, 0))],
        out_specs=[row(d), row(d), pair, pair, pair,
                   pl.BlockSpec((1, N_EXPERTS, LANES), lambda i: (i, 0, 0))],
        compiler_params=_cparams(("arbitrary",), VMEM_LIMIT),
        name="outproj_route",
    )(*xs, *mix_args, og, hg, w_bf, g1, g2n, sh2, sc2, rw_t, rb, before)


def _route(x, g_ref, sh_ref, sc_ref, rw_ref, rb_ref, before_ref, h_ref, e_ref, rank_ref, w_ref, cnt_ref):
    tm = x.shape[0]
    h = _modulate(x, g_ref[...], sh_ref[0], sc_ref[0])
    h_ref[...] = h.astype(h_ref.dtype)
    logits = lax.dot_general(rw_ref[...], h, (((1,), (1,)), ((), ())),
                             precision=lax.Precision.HIGHEST, preferred_element_type=F32)
    s = jax.nn.sigmoid(logits)
    sb = s + rb_ref[...]
    s_rows = [s[e:e + 1, :] for e in range(N_EXPERTS)]
    sb_rows = [sb[e:e + 1, :] for e in range(N_EXPERTS)]

    def group_score(g):
        v = sb_rows[g * EXPERTS_PER_GROUP:(g + 1) * EXPERTS_PER_GROUP]
        best = None
        for a in range(EXPERTS_PER_GROUP):
            for b in range(a + 1, EXPERTS_PER_GROUP):
                p = v[a] + v[b]
                best = p if best is None else jnp.maximum(best, p)
        return best

    cur = group_score(0)
    best_g = jnp.zeros(cur.shape, I32)
    for g in range(1, N_GROUPS):
        gs = group_score(g)
        upd = gs > cur
        best_g = jnp.where(upd, g, best_g)
        cur = jnp.where(upd, gs, cur)

    def pick(rows, jdx):
        out = rows[jdx]
        for g in range(1, N_GROUPS):
            out = jnp.where(best_g == g, rows[g * EXPERTS_PER_GROUP + jdx], out)
        return out

    vb = [pick(sb_rows, jdx) for jdx in range(EXPERTS_PER_GROUP)]
    vs = [pick(s_rows, jdx) for jdx in range(EXPERTS_PER_GROUP)]

    def first_argmax(vals):
        m = vals[0]
        for v in vals[1:]:
            m = jnp.maximum(m, v)
        idx = jnp.full(m.shape, EXPERTS_PER_GROUP - 1, I32)
        for jdx in range(EXPERTS_PER_GROUP - 2, -1, -1):
            idx = jnp.where(vals[jdx] == m, jdx, idx)
        return idx

    def take(vals, idx):
        out = vals[EXPERTS_PER_GROUP - 1]
        for jdx in range(EXPERTS_PER_GROUP - 2, -1, -1):
            out = jnp.where(idx == jdx, vals[jdx], out)
        return out

    i0 = first_argmax(vb)
    vb2 = [jnp.where(i0 == jdx, -jnp.inf, vb[jdx]) for jdx in range(EXPERTS_PER_GROUP)]
    i1 = first_argmax(vb2)
    s0 = take(vs, i0)
    s1 = take(vs, i1)
    tot = s0 + s1
    w_ref[0:1, :] = s0 / tot
    w_ref[1:2, :] = s1 / tot
    e0 = best_g * EXPERTS_PER_GROUP + i0
    e1 = best_g * EXPERTS_PER_GROUP + i1
    e_ref[0:1, :] = e0
    e_ref[1:2, :] = e1

    eid = lax.broadcasted_iota(I32, (N_EXPERTS, tm), 0)
    hit0 = eid == e0
    hit1 = eid == e1
    onehot = jnp.where(jnp.logical_or(hit0, hit1), 1.0, 0.0)
    rank = jnp.dot(onehot.astype(BF16), before_ref[...], preferred_element_type=F32)
    rank_ref[0:1, :] = jnp.sum(jnp.where(hit0, rank, 0.0), axis=0, keepdims=True).astype(I32)
    rank_ref[1:2, :] = jnp.sum(jnp.where(hit1, rank, 0.0), axis=0, keepdims=True).astype(I32)
    cnt = jnp.sum(onehot, axis=1, keepdims=True).astype(I32)
    cnt_ref[0] = jnp.broadcast_to(cnt, (N_EXPERTS, LANES))


def _tables_kernel(cnt_ref, seg_start_ref, seg_len_ref, te_ref, misc_ref, *, n_tt, n_et, tm):
    def expert(e, row0):
        def tile(i, pos):
            n = cnt_ref[i * N_EXPERTS + e]
            n8 = ((n + SUBLANES - 1) // SUBLANES) * SUBLANES
            seg_start_ref[i * N_EXPERTS + e] = pos
            seg_len_ref[i * N_EXPERTS + e] = n8
            return pos + n8

        end = lax.fori_loop(0, n_tt, tile, row0)
        padded_end = row0 + ((end - row0 + tm - 1) // tm) * tm
        misc_ref[1 + e] = end
        misc_ref[1 + N_EXPERTS + e] = padded_end - end

        def mark(j, carry):
            te_ref[j] = e
            return carry

        lax.fori_loop(row0 // tm, padded_end // tm, mark, 0)
        return padded_end

    total = row0 = 0
    for e in range(N_EXPERTS):
        row0 = expert(e, row0)
    total = row0
    n_used = total // tm
    misc_ref[0] = n_used
    last = te_ref[jnp.maximum(n_used - 1, 0)]

    def fill(j, carry):
        te_ref[j] = last
        return carry

    lax.fori_loop(n_used, n_et, fill, 0)


def _tables(cnt, *, n_tt, n_et, tm):
    smem = pl.BlockSpec(memory_space=pltpu.SMEM)
    return pl.pallas_call(
        functools.partial(_tables_kernel, n_tt=n_tt, n_et=n_et, tm=tm),
        out_shape=[jax.ShapeDtypeStruct((n_tt * N_EXPERTS,), I32),
                   jax.ShapeDtypeStruct((n_tt * N_EXPERTS,), I32),
                   jax.ShapeDtypeStruct((n_et,), I32),
                   jax.ShapeDtypeStruct((1 + 2 * N_EXPERTS,), I32)],
        in_specs=[smem],
        out_specs=[smem, smem, smem, smem],
        name="route_tables",
    )(cnt)


def _piece_sizes(max_rows):
    sizes, s = [], SUBLANES
    while s <= max_rows:
        sizes.append(s)
        s *= 2
    return sizes[::-1]


def _segment_copies(src_ref, src_row, dst_ref, dst_row, n_rows, sem, sizes, start):
    aligned = lambda r: r if isinstance(r, int) else pl.multiple_of(r, SUBLANES)
    for sz in sizes:
        src = pl.ds(aligned(src_row), sz)
        dst = pl.ds(aligned(dst_row), sz)

        @pl.when((n_rows & sz) != 0)
        def _(src=src, dst=dst):
            cp = pltpu.make_async_copy(src_ref.at[src, :], dst_ref.at[dst, :], sem)
            if start:
                cp.start()
            else:
                cp.wait()

        src_row = src_row + (n_rows & sz)
        dst_row = dst_row + (n_rows & sz)


def _local_slots(seg_len_ref, tile, e_ref, rank_ref):
    offs, lo = [], 0
    for e in range(N_EXPERTS):
        offs.append(lo)
        lo = lo + seg_len_ref[tile * N_EXPERTS + e]
    e01 = e_ref[...]
    slot = rank_ref[...]
    for e in range(N_EXPERTS):
        slot = slot + jnp.where(e01 == e, offs[e], 0)
    return slot, offs


def _compact_rows(tm):
    return ((2 * tm + N_EXPERTS * (SUBLANES - 1) + LANES - 1) // LANES) * LANES


def _dispatch_kernel(seg_start_ref, seg_len_ref, misc_ref, h_ref, e_ref, rank_ref, xs_ref,
                     xc_ref, zero_ref, sems, *, sizes):
    tm = h_ref.shape[0]
    rc = xc_ref.shape[1]
    i = pl.program_id(0)
    buf = i % 2
    slot, _ = _local_slots(seg_len_ref, i, e_ref, rank_ref)
    rid = lax.broadcasted_iota(I32, (rc, tm), 0)
    sel = jnp.logical_or(rid == slot[0:1, :], rid == slot[1:2, :])
    perm = jnp.where(sel, 1.0, 0.0).astype(BF16)
    xc_ref[buf] = jnp.dot(perm, h_ref[...], preferred_element_type=F32)

    def segments(tile, b, start):
        lo = 0
        for e in range(N_EXPERTS):
            n8 = seg_len_ref[tile * N_EXPERTS + e]
            _segment_copies(xc_ref.at[b], lo, xs_ref, seg_start_ref[tile * N_EXPERTS + e], n8,
                            sems.at[b], sizes, start)
            lo = lo + n8

    @pl.when(i > 0)
    def _():
        segments(i - 1, 1 - buf, False)

    segments(i, buf, True)

    @pl.when(i == pl.num_programs(0) - 1)
    def _():
        segments(i, buf, False)
        zero_ref[...] = jnp.zeros(zero_ref.shape, F32)
        for start in (True, False):
            for e in range(N_EXPERTS):
                _segment_copies(zero_ref, 0, xs_ref, misc_ref[1 + e], misc_ref[1 + N_EXPERTS + e],
                                sems.at[0], sizes, start)

        def clear_tile(j, carry):
            cp = pltpu.make_async_copy(zero_ref, xs_ref.at[pl.ds(pl.multiple_of(j * tm, tm), tm), :],
                                       sems.at[0])
            cp.start()
            cp.wait()
            return carry

        lax.fori_loop(misc_ref[0], xs_ref.shape[0] // tm, clear_tile, 0)


def _dispatch(seg_start, seg_len, misc, h, e01, rank01, *, n_sorted, tm):
    n, d = h.shape
    pair = pl.BlockSpec((2, tm), lambda i, *_: (0, i))
    grid_spec = pltpu.PrefetchScalarGridSpec(
        num_scalar_prefetch=3,
        grid=(n // tm,),
        in_specs=[pl.BlockSpec((tm, d), lambda i, *_: (i, 0)), pair, pair],
        out_specs=pl.BlockSpec(memory_space=pl.ANY),
        scratch_shapes=[pltpu.VMEM((2, _compact_rows(tm), d), F32), pltpu.VMEM((tm, d), F32),
                        pltpu.SemaphoreType.DMA((2,))],
    )
    return pl.pallas_call(
        functools.partial(_dispatch_kernel, sizes=_piece_sizes(tm)),
        out_shape=jax.ShapeDtypeStruct((n_sorted, d), F32),
        grid_spec=grid_spec,
        compiler_params=_cparams(("arbitrary",), VMEM_LIMIT),
        name="dispatch",
    )(seg_start, seg_len, misc, h, e01, rank01)


def _moe_kernel(te_ref, misc_ref, xs_ref, wg_hbm, wu_hbm, wd_hbm, ys_ref,
                wg_buf, wu_buf, wd_buf, wgb_ref, wub_ref, wdb_ref, run_ref, sems, *, layer, tm):
    i = pl.program_id(0)
    e = te_ref[i]
    prev = te_ref[jnp.maximum(i - 1, 0)]
    n_used = misc_ref[0]

    def fetch(expert, slot):
        return [pltpu.make_async_copy(w.at[layer, expert], buf.at[slot], sems.at[slot, k])
                for k, (w, buf) in enumerate(((wg_hbm, wg_buf), (wu_hbm, wu_buf), (wd_hbm, wd_buf)))]

    @pl.when(i == 0)
    def _():
        run_ref[0] = 0
        for cp in fetch(e, 0):
            cp.start()

    @pl.when(jnp.logical_or(i == 0, e != prev))
    def _():
        slot = run_ref[0] % 2
        for cp in fetch(e, slot):
            cp.wait()
        nxt = (misc_ref[1 + e] + misc_ref[1 + N_EXPERTS + e]) // tm

        @pl.when(nxt < n_used)
        def _():
            for cp in fetch(te_ref[nxt], 1 - slot):
                cp.start()

        wgb_ref[...] = wg_buf[slot].astype(BF16)
        wub_ref[...] = wu_buf[slot].astype(BF16)
        wdb_ref[...] = wd_buf[slot].astype(BF16)
        run_ref[0] = run_ref[0] + 1

    @pl.when(i < misc_ref[0])
    def _():
        x = xs_ref[...].astype(BF16)
        gate = jnp.dot(x, wgb_ref[...], preferred_element_type=F32)
        up = jnp.dot(x, wub_ref[...], preferred_element_type=F32)
        act = (_silu(gate) * up).astype(BF16)
        ys_ref[...] = jnp.dot(act, wdb_ref[...], preferred_element_type=F32)

    @pl.when(i >= misc_ref[0])
    def _():
        ys_ref[...] = jnp.zeros(ys_ref.shape, F32)


def _moe(tile_expert, misc, xs, wg, wu, wd, layer, tm):
    n_sorted, d = xs.shape
    de = wg.shape[-1]
    xmap = lambda i, te, misc_ref: (jnp.minimum(i, jnp.maximum(misc_ref[0] - 1, 0)), 0)
    hbm = pl.BlockSpec(memory_space=pl.ANY)
    grid_spec = pltpu.PrefetchScalarGridSpec(
        num_scalar_prefetch=2,
        grid=(n_sorted // tm,),
        in_specs=[pl.BlockSpec((tm, d), xmap), hbm, hbm, hbm],
        out_specs=pl.BlockSpec((tm, d), lambda i, te, misc_ref: (i, 0)),
        scratch_shapes=[pltpu.VMEM((2, d, de), F32), pltpu.VMEM((2, d, de), F32),
                        pltpu.VMEM((2, de, d), F32),
                        pltpu.VMEM((d, de), BF16), pltpu.VMEM((d, de), BF16),
                        pltpu.VMEM((de, d), BF16),
                        pltpu.SMEM((1,), I32), pltpu.SemaphoreType.DMA((2, 3))],
    )
    return pl.pallas_call(
        functools.partial(_moe_kernel, layer=layer, tm=tm),
        out_shape=jax.ShapeDtypeStruct((n_sorted, d), F32),
        grid_spec=grid_spec,
        compiler_params=_cparams(("arbitrary",), VMEM_LIMIT),
        name="moe_experts",
    )(tile_expert, misc, xs, wg, wu, wd)


def _combine_kernel(seg_start_ref, seg_len_ref, x_ref, g2_ref, e_ref, rank_ref, w_ref, fg_ref, ys_ref,
                    o_ref, yc_ref, sems, *, sizes, final_norm):
    tm = x_ref.shape[0]
    rc = yc_ref.shape[1]
    i = pl.program_id(0)
    buf = i % 2

    def segments(tile, b, start):
        lo = 0
        for e in range(N_EXPERTS):
            n8 = seg_len_ref[tile * N_EXPERTS + e]
            _segment_copies(ys_ref, seg_start_ref[tile * N_EXPERTS + e], yc_ref.at[b], lo, n8,
                            sems.at[b], sizes, start)
            lo = lo + n8

    @pl.when(i == 0)
    def _():
        yc_ref[...] = jnp.zeros(yc_ref.shape, F32)
        segments(0, 0, True)

    @pl.when(i + 1 < pl.num_programs(0))
    def _():
        segments(i + 1, 1 - buf, True)

    segments(i, buf, False)
    slot, _ = _local_slots(seg_len_ref, i, e_ref, rank_ref)
    rid = lax.broadcasted_iota(I32, (rc, tm), 0)
    w = w_ref[...]
    pw = (jnp.where(rid == slot[0:1, :], w[0:1, :], 0.0)
          + jnp.where(rid == slot[1:2, :], w[1:2, :], 0.0))
    dn = (((0,), (0,)), ((), ()))
    out = lax.dot_general(pw.astype(BF16), yc_ref[buf].astype(BF16), dn, preferred_element_type=F32)
    x = x_ref[...] + g2_ref[0] * out
    if final_norm:
        x = x * lax.rsqrt(jnp.mean(x * x, axis=-1, keepdims=True) + EPS) * fg_ref[...]
    o_ref[...] = x


def _combine(seg_start, seg_len, xc, g2, e01, rank01, w01, fg, ys, *, n_rows, tm, tiles_per_seq,
             final_norm):
    d = xc.shape[1]
    nb = g2.shape[0] - 1
    pair = pl.BlockSpec((2, tm), lambda i, *_: (0, i))
    grid_spec = pltpu.PrefetchScalarGridSpec(
        num_scalar_prefetch=2,
        grid=(n_rows // tm,),
        in_specs=[pl.BlockSpec((tm, d), lambda i, *_: (i, 0)),
                  pl.BlockSpec((1, 1, d), lambda i, *_: (jnp.minimum(i // tiles_per_seq, nb), 0, 0)),
                  pair, pair, pair,
                  pl.BlockSpec((1, d), lambda i, *_: (0, 0)),
                  pl.BlockSpec(memory_space=pl.ANY)],
        out_specs=pl.BlockSpec((tm, d), lambda i, *_: (i, 0)),
        scratch_shapes=[pltpu.VMEM((2, _compact_rows(tm), d), F32), pltpu.SemaphoreType.DMA((2,))],
    )
    return pl.pallas_call(
        functools.partial(_combine_kernel, sizes=_piece_sizes(tm), final_norm=final_norm),
        out_shape=jax.ShapeDtypeStruct((n_rows, d), F32),
        grid_spec=grid_spec,
        compiler_params=_cparams(("arbitrary",), VMEM_LIMIT),
        name="combine",
    )(seg_start, seg_len, xc, g2, e01, rank01, w01, fg, ys)


def _lower_bounds(lb_param):
    p = jax.nn.softmax(lb_param.astype(F32), axis=0)
    return jnp.cumsum(p, axis=0) - p[0]


def kernel(x, c, ctx, c_ctx, w_ada, b_ada, norm1_g, norm2_g, w_in, conv_w, conv_b, conv_ln_g,
           conv_ln_b, lb_fwd, lb_bwd, hgrn_norm_g, w_out, router_w, router_bias, w_gate, w_up,
           w_down, final_norm_g):
    bn, seq, d = x.shape
    ctx_len = ctx.shape[1]
    depth = w_ada.shape[0]
    n = bn * seq
    nc = bn * ctx_len
    tm = min(512, seq, nc)
    assert seq % tm == 0 and nc % tm == 0 and seq % GRID_W == 0
    tps = seq // tm

    pad_rows = (-(bn + 1)) % 8
    cc = jnp.concatenate([c, c_ctx[None, :], jnp.zeros((pad_rows, d), F32)], axis=0)
    mod = _ada(cc, w_ada, b_ada)[:, :bn + 1]
    mod = mod.reshape(depth, bn + 1, 6, 1, d)

    lbs_f = _lower_bounds(lb_fwd)
    lbs_b = _lower_bounds(lb_bwd)
    w_in_bf = w_in.astype(BF16)
    w_out_bf = w_out.astype(BF16)
    rw_t = router_w.T.astype(F32)
    rb = router_bias.reshape(N_EXPERTS, 1).astype(F32)
    hg = hgrn_norm_g.astype(F32)
    fg = final_norm_g.reshape(1, d).astype(F32)

    rows_in = [x.reshape(n, d), ctx.reshape(nc, d)]
    s_zero = jnp.zeros((bn, 2 * HEADS, HEAD_DIM, HEAD_DIM), F32)

    for l in range(depth):
        last = l == depth - 1
        sh1, sc1, g1, sh2, sc2, g2 = (mod[l, :, k] for k in range(6))
        row = lambda a: a.reshape(1, -1).astype(F32)

        u, q, iv, lff, kf, lfb, kb, og = _inproj(
            rows_in, row(norm1_g[l]), sh1, sc1, w_in_bf, l, row(lbs_f[l]), row(lbs_b[l]), tm, tps)

        cw, cb = conv_w[l].astype(F32), row(conv_b[l])
        clg, clb = row(conv_ln_g[l]), row(conv_ln_b[l])
        scan = functools.partial(_scan, q, iv, lff, kf, lfb, kb)
        if last:
            (s_ctx,) = scan(s_zero, row_off=n, seq_len=ctx_len, compute_o=False)
            mix_ctx = [[], [], []]
            n_rows = n
        else:
            of_c, ob_c, s_ctx = scan(s_zero, row_off=n, seq_len=ctx_len, compute_o=True)
            ycx_c = _conv_rows(u, cw, cb, clg, clb, seq=ctx_len, row_off=n, n_rows=nc)
            mix_ctx = [[ycx_c], [of_c], [ob_c]]
            n_rows = n + nc
        of, ob, _ = scan(s_ctx, row_off=0, seq_len=seq, compute_o=True)
        if l % 2 == 0:
            ycx = _conv_rows(u, cw, cb, clg, clb, seq=GRID_W, row_off=0, n_rows=n)
        else:
            ycx = _conv_cols(u, cw, cb, clg, clb, batch=bn, seq_len=seq)
        mix = [[a] + c_ for a, c_ in zip((ycx, of, ob), mix_ctx)]

        res_rows = rows_in if (len(rows_in) == 1 or not last) else [rows_in[0]]
        xc, h2, e01, rank01, w01, cnt = _outproj_route(
            res_rows, mix, og, hg[l].reshape(1, HEAD_DIM), w_out_bf, l, g1,
            row(norm2_g[l]), sh2, sc2, rw_t, rb, n_rows=n_rows, tm=tm, tiles_per_seq=tps)
        n_tt = n_rows // tm
        n_et = (2 * n_rows + n_tt * N_EXPERTS * (SUBLANES - 1) + tm - 1) // tm + N_EXPERTS
        seg_start, seg_len, te, misc = _tables(cnt[:, :, 0].reshape(-1), n_tt=n_tt, n_et=n_et, tm=tm)
        xs = _dispatch(seg_start, seg_len, misc, h2, e01, rank01, n_sorted=n_et * tm, tm=tm)
        ys = _moe(te, misc, xs, w_gate, w_up, w_down, l, tm)
        xc = _combine(seg_start, seg_len, xc, g2, e01, rank01, w01, fg, ys, n_rows=n_rows, tm=tm,
                      tiles_per_seq=tps, final_norm=last)
        rows_in = [xc]

    return xc[:n].reshape(bn, seq, d)
```

```python
import functools

import numpy as np
import jax
import jax.numpy as jnp
from jax import lax
from jax.experimental import pallas as pl
from jax.experimental.pallas import tpu as pltpu

F32 = jnp.float32
BF16 = jnp.bfloat16
I32 = jnp.int32

EPS = 1e-6
GRID_W = 64
HEADS = 4
HEAD_DIM = 128
D_GROUP = HEADS * HEAD_DIM
N_IN_GROUPS = 7
CONV_K = 31
CONV_HALF = CONV_K // 2
CONV_GAP = 16
CONV_ROW_GROUP = 1
CONV_PARTIALS = 2
N_EXPERTS = 16
N_GROUPS = 4
EXPERTS_PER_GROUP = N_EXPERTS // N_GROUPS
SCAN_CHUNK = 64
SCAN_FAST_LIMIT = 60.0
LANES = 128
SUBLANES = 8
SEGMENT_CHUNK = 64
VMEM_LIMIT = 56 * 1024 * 1024


def _cparams(sem, vmem=None):
    return pltpu.CompilerParams(dimension_semantics=sem, vmem_limit_bytes=vmem)


def _sigmoid(x):
    return 0.5 * jnp.tanh(0.5 * x) + 0.5


def _silu(x):
    return x * _sigmoid(x)


def _ada_kernel(c_ref, w_ref, b_ref, o_ref):
    a = _silu(c_ref[...]).astype(BF16)
    w = w_ref[0].astype(BF16)
    o_ref[0] = jnp.dot(a, w, preferred_element_type=F32) + b_ref[0]


def _ada(cc, w_ada, b_ada, tn=1536):
    depth, d, d6 = w_ada.shape
    rows = cc.shape[0]
    return pl.pallas_call(
        _ada_kernel,
        out_shape=jax.ShapeDtypeStruct((depth, rows, d6), F32),
        grid=(depth, d6 // tn),
        in_specs=[
            pl.BlockSpec((rows, d), lambda l, j: (0, 0)),
            pl.BlockSpec((1, d, tn), lambda l, j: (l, 0, j)),
            pl.BlockSpec((1, 1, tn), lambda l, j: (l, 0, j)),
        ],
        out_specs=pl.BlockSpec((1, rows, tn), lambda l, j: (l, 0, j)),
        compiler_params=_cparams(("arbitrary", "arbitrary"), VMEM_LIMIT),
        name="ada",
    )(cc, w_ada, b_ada.reshape(depth, 1, d6))


def _modulate(x, g, shift, scale):
    r = lax.rsqrt(jnp.mean(x * x, axis=-1, keepdims=True) + EPS)
    return (x * r * g) * (1.0 + scale) + shift


def _rows_from(refs, n_lat_tiles):
    if len(refs) == 1:
        return refs[0][...]
    return jnp.where(pl.program_id(0) >= n_lat_tiles, refs[1][...], refs[0][...])


def _row_specs(arrays, tm):
    width = arrays[0].shape[1]
    if len(arrays) == 1:
        return [pl.BlockSpec((tm, width), lambda i, *_: (i, 0))], 0
    n_lat_tiles = arrays[0].shape[0] // tm
    return [pl.BlockSpec((tm, width), lambda i, *_: (jnp.minimum(i, n_lat_tiles - 1), 0)),
            pl.BlockSpec((tm, width), lambda i, *_: (jnp.maximum(i - n_lat_tiles, 0), 0))], n_lat_tiles


def _inproj_kernel(*refs, n_src, n_lat_tiles):
    x = _rows_from(refs[:n_src], n_lat_tiles)
    (g_ref, sh_ref, sc_ref, w_ref, lbf_ref, lbb_ref,
     u_ref, q_ref, i_ref, lff_ref, kf_ref, lfb_ref, kb_ref, og_ref) = refs[n_src:]
    hb = _modulate(x, g_ref[...], sh_ref[0], sc_ref[0]).astype(BF16)

    def proj(j):
        return jnp.dot(hb, w_ref[0, :, j * D_GROUP:(j + 1) * D_GROUP], preferred_element_type=F32)

    u_ref[...] = (proj(0) * _sigmoid(proj(1))).astype(BF16)
    q_ref[...] = _silu(proj(2)).astype(BF16)
    i_ref[...] = proj(3).astype(BF16)
    for j, lb_ref, lf_ref, k_ref in ((4, lbf_ref, lff_ref, kf_ref), (5, lbb_ref, lfb_ref, kb_ref)):
        lb = lb_ref[...]
        f = lb + (1.0 - lb) * jax.nn.sigmoid(proj(j))
        lf_ref[...] = jnp.log(f)
        k_ref[...] = (1.0 - f).astype(BF16)
    og_ref[...] = _silu(proj(6)).astype(BF16)


def _inproj(xs, g, sh, sc, w_bf, layer, lbf, lbb, tm, tiles_per_seq):
    n = sum(a.shape[0] for a in xs)
    d = xs[0].shape[1]
    x_specs, n_lat_tiles = _row_specs(xs, tm)
    nb = sh.shape[0] - 1
    mod_spec = pl.BlockSpec((1, 1, d), lambda i: (jnp.minimum(i // tiles_per_seq, nb), 0, 0))
    row_spec = pl.BlockSpec((tm, D_GROUP), lambda i: (i, 0))
    vec_spec = pl.BlockSpec((1, D_GROUP), lambda i: (0, 0))
    outs = [jax.ShapeDtypeStruct((n, D_GROUP), dt)
            for dt in (BF16, BF16, BF16, F32, BF16, F32, BF16, BF16)]
    return pl.pallas_call(
        functools.partial(_inproj_kernel, n_src=len(xs), n_lat_tiles=n_lat_tiles),
        out_shape=outs,
        grid=(n // tm,),
        in_specs=x_specs + [
            pl.BlockSpec((1, d), lambda i: (0, 0)),
            mod_spec, mod_spec,
            pl.BlockSpec((1, d, N_IN_GROUPS * D_GROUP), lambda i: (layer, 0, 0)),
            vec_spec, vec_spec,
        ],
        out_specs=[row_spec] * 8,
        compiler_params=_cparams(("arbitrary",), VMEM_LIMIT),
        name="inproj",
    )(*xs, g, sh, sc, w_bf, lbf, lbb)


def _scan_tables(c, backward):
    levels = int(np.log2(c))
    r_all = np.zeros((levels * c + 2 * c + 8, c), np.float32)
    for lev in range(levels):
        h = 1 << lev
        for r in range(c):
            bd = (r // (2 * h)) * 2 * h + h
            if not backward:
                if r >= bd:
                    r_all[lev * c + r, bd:r + 1] = 1.0
                else:
                    r_all[lev * c + r, r + 1:bd] = 1.0
            else:
                if r < bd:
                    r_all[lev * c + r, r:bd] = 1.0
                else:
                    r_all[lev * c + r, bd:r] = 1.0
    base = levels * c
    for r in range(c):
        if not backward:
            r_all[base + r, :r + 1] = 1.0
            r_all[base + c + r, r + 1:] = 1.0
        else:
            r_all[base + r, r:] = 1.0
            r_all[base + c + r, :r] = 1.0
    r_all[base + 2 * c:, :] = 1.0
    return r_all


def _scan_kernel(*refs, n_chunks, compute_o):
    c = SCAN_CHUNK
    levels = int(np.log2(c))
    (rf_ref, rb_ref, qf_ref, vf_ref, lff_ref, kf_ref,
     qb_ref, vb_ref, lfb_ref, kb_ref, s0_ref, trif_ref, trib_ref) = refs[:13]
    pos = 13
    if compute_o:
        lff_next_ref, lfb_next_ref = refs[pos:pos + 2]
        of_ref, ob_ref, sfin_ref, st_ref, bsum_ref, flag_ref = refs[pos + 2:pos + 8]
    else:
        sfin_ref, st_ref = refs[pos:pos + 2]
        of_ref = ob_ref = None
    j = pl.program_id(1)

    @pl.when(j == 0)
    def _():
        st_ref[...] = s0_ref[0]

    row = lax.broadcasted_iota(jnp.int32, (c, 1), 0)
    ri = lax.broadcasted_iota(jnp.int32, (c, c), 0)
    ci = lax.broadcasted_iota(jnp.int32, (c, c), 1)
    upper = [(row & (2 * (1 << lev) - 1)) >= (1 << lev) for lev in range(levels)]
    same_parent = [(ri >> (lev + 1)) == (ci >> (lev + 1)) for lev in range(levels)]
    diag = ri == ci
    dn_t = (((1,), (1,)), ((), ()))
    dn_tl = (((0,), (0,)), ((), ()))

    half = c // 2
    tri = [jnp.where(ci <= ri, 1.0, 0.0).astype(BF16), jnp.where(ci >= ri, 1.0, 0.0).astype(BF16)]
    first_half = [row < half, row >= half]
    re = lax.broadcasted_iota(jnp.int32, (c, 4 * c), 0)
    ce = lax.broadcasted_iota(jnp.int32, (c, 4 * c), 1)
    cs = ce & (c - 1)
    same_half = (re >= half) == (cs >= half)
    own = (ce & (2 * c - 1)) < c
    mask_ext = [
        (own & same_half & (cs <= re)) | (jnp.logical_not(own) & (re >= half) & (cs < half)),
        (own & same_half & (cs >= re)) | (jnp.logical_not(own) & (re < half) & (cs >= half)),
    ]

    mrow, trow = [half - 1, half], [c - 1, 0]

    def prefix(tri_mat, g):
        g_hi = g.astype(BF16)
        g_lo = (g - g_hi.astype(F32)).astype(BF16)
        return (jnp.dot(tri_mat, g_hi, preferred_element_type=F32)
                + jnp.dot(tri_mat, g_lo, preferred_element_type=F32))

    def next_state(st, hs, kh, vh, b, tot):
        kd = (kh * jnp.exp(tot[:, hs] - b[:, hs])).astype(BF16)
        return st * jnp.exp(tot[:, hs]) + lax.dot_general(vh, kd, dn_tl, preferred_element_type=F32)

    def state_only_chunk(dirn, lf_ref, k_ref, v_ref, r0):
        rows = pl.ds(r0, c)
        b = prefix(tri[dirn], lf_ref[rows, :])
        tot = b[trow[dirn]:trow[dirn] + 1, :]
        for h in range(HEADS):
            hs = slice(h * HEAD_DIM, (h + 1) * HEAD_DIM)
            st_ref[dirn * HEADS + h] = next_state(st_ref[dirn * HEADS + h], hs,
                                                  k_ref[rows, hs].astype(F32), v_ref[rows, hs], b, tot)

    def fast_chunk(dirn, q_ref, v_ref, k_ref, o_ref, rows, b, m, tot, states):
        fh = first_half[dirn]
        hd = HEAD_DIM

        def block_diag(a, bb):
            za = jnp.zeros((a.shape[0], bb.shape[1]), a.dtype)
            zb = jnp.zeros((bb.shape[0], a.shape[1]), a.dtype)
            return jnp.concatenate([jnp.concatenate([a, za], axis=1),
                                    jnp.concatenate([zb, bb], axis=1)], axis=0)

        for hp in range(HEADS // 2):
            ps = slice(2 * hp * hd, 2 * (hp + 1) * hd)
            bp, mp, tp = b[:, ps], m[:, ps], tot[:, ps]
            cdec = bp - jnp.where(fh, 0.0, mp)
            e_own = jnp.exp(cdec).astype(BF16)
            e_own_inv = jnp.exp(-cdec).astype(BF16)
            e_cross = jnp.where(fh, jnp.exp(jnp.minimum(mp - bp, 0.0)), 0.0).astype(BF16)
            e_inc = jnp.exp(bp).astype(BF16)
            e_dec = jnp.exp(tp - bp).astype(BF16)
            qp, kp, vp = q_ref[rows, ps], k_ref[rows, ps], v_ref[rows, ps]
            q1, qi = qp * e_own, qp * e_inc
            k1, k2, kd = kp * e_own_inv, kp * e_cross, kp * e_dec
            kcat = [jnp.concatenate([k1[:, s_], k2[:, s_]], axis=0) for s_ in (slice(0, hd), slice(hd, 2 * hd))]
            sc = lax.dot_general(q1, block_diag(*kcat), dn_t, preferred_element_type=F32)
            p = jnp.where(mask_ext[dirn], sc, 0.0).astype(BF16)
            v2 = [jnp.concatenate([vp[:, s_], vp[:, s_]], axis=0) for s_ in (slice(0, hd), slice(hd, 2 * hd))]
            st = [states[dirn * HEADS + 2 * hp + a] for a in range(2)]
            o = (jnp.dot(p, block_diag(*v2), preferred_element_type=F32)
                 + lax.dot_general(qi, block_diag(st[0].astype(BF16), st[1].astype(BF16)), dn_t,
                                   preferred_element_type=F32))
            o_ref[rows, ps] = o.astype(o_ref.dtype)
            upd = lax.dot_general(vp, kd, dn_tl, preferred_element_type=F32)
            e_tot = jnp.exp(tp)
            for a in range(2):
                sl = slice(a * hd, (a + 1) * hd)
                states[dirn * HEADS + 2 * hp + a] = st[a] * e_tot[:, sl] + upd[sl, sl]

    def robust_chunk(dirn, r_ref, q_ref, v_ref, lf_ref, k_ref, o_ref, r0):
        rows = pl.ds(r0, c)
        g = lf_ref[rows, :]
        g_hi = g.astype(BF16)
        g_lo = (g - g_hi.astype(F32)).astype(BF16)
        rmat = r_ref[...]
        e_all = jnp.exp(jnp.dot(rmat, g_hi, preferred_element_type=F32)
                        + jnp.dot(rmat, g_lo, preferred_element_type=F32))
        base = levels * c
        for h in range(HEADS):
            hs = slice(h * HEAD_DIM, (h + 1) * HEAD_DIM)
            qh = q_ref[rows, hs].astype(F32)
            kh = k_ref[rows, hs].astype(F32)
            vh = v_ref[rows, hs]
            st = st_ref[dirn * HEADS + h]
            if compute_o:
                att = jnp.where(diag, lax.dot_general(qh.astype(BF16), kh.astype(BF16), dn_t,
                                                      preferred_element_type=F32), 0.0)
                for lev in range(levels):
                    e_l = e_all[lev * c:(lev + 1) * c, hs]
                    q_side = upper[lev] if dirn == 0 else jnp.logical_not(upper[lev])
                    qt = jnp.where(q_side, qh * e_l, 0.0).astype(BF16)
                    kt = jnp.where(q_side, 0.0, kh * e_l).astype(BF16)
                    a_l = lax.dot_general(qt, kt, dn_t, preferred_element_type=F32)
                    att = att + jnp.where(same_parent[lev], a_l, 0.0)
                qi = (qh * e_all[base:base + c, hs]).astype(BF16)
                o = lax.dot_general(qi, st.astype(BF16), dn_t, preferred_element_type=F32)
                o = o + jnp.dot(att.astype(BF16), vh, preferred_element_type=F32)
                o_ref[rows, hs] = o.astype(o_ref.dtype)
            kd = (kh * e_all[base + c:base + 2 * c, hs]).astype(BF16)
            e_tot = e_all[base + 2 * c:base + 2 * c + 1, hs]
            st_ref[dirn * HEADS + h] = st * e_tot + lax.dot_general(
                vh, kd, dn_tl, preferred_element_type=F32)

    dir_refs = ((rf_ref, qf_ref, vf_ref, lff_ref, kf_ref, of_ref),
                (rb_ref, qb_ref, vb_ref, lfb_ref, kb_ref, ob_ref))

    def chunk_rows(ci_):
        return pl.multiple_of(ci_ * c, c), pl.multiple_of((n_chunks - 1 - ci_) * c, c)

    if not compute_o:
        def state_body(ci_, carry):
            for dirn, r0 in enumerate(chunk_rows(ci_)):
                _, _, v_ref, lf_ref, k_ref, _ = dir_refs[dirn]
                state_only_chunk(dirn, lf_ref, k_ref, v_ref, r0)
            return carry

        lax.fori_loop(0, n_chunks, state_body, 0)
    else:
        step_id = pl.program_id(0) * pl.num_programs(1) + j
        cur = step_id % 2

        def prepare(lf_refs, slot):
            worst = None
            for dirn in range(2):
                bsum = prefix((trif_ref, trib_ref)[dirn][...], lf_refs[dirn][...])
                bsum_ref[slot, dirn] = bsum
                for ch in range(n_chunks):
                    m = bsum[ch * c + mrow[dirn]:ch * c + mrow[dirn] + 1, :]
                    tot = bsum[ch * c + trow[dirn]:ch * c + trow[dirn] + 1, :]
                    w = jnp.minimum(m, tot - m)
                    worst = w if worst is None else jnp.minimum(worst, w)
            flag_ref[slot] = (jnp.min(worst) >= -SCAN_FAST_LIMIT).astype(jnp.int32)

        @pl.when(step_id == 0)
        def _():
            prepare((lff_ref, lfb_ref), 0)

        fast_ok = flag_ref[cur] == 1

        @pl.when(fast_ok)
        def _():
            states = [st_ref[i] for i in range(2 * HEADS)]
            for step in range(n_chunks):
                for dirn in range(2):
                    ch = step if dirn == 0 else n_chunks - 1 - step
                    _, q_ref, v_ref, _, k_ref, o_ref = dir_refs[dirn]
                    rows = slice(ch * c, (ch + 1) * c)
                    m = bsum_ref[cur, dirn, ch * c + mrow[dirn]:ch * c + mrow[dirn] + 1, :]
                    tot = bsum_ref[cur, dirn, ch * c + trow[dirn]:ch * c + trow[dirn] + 1, :]
                    fast_chunk(dirn, q_ref, v_ref, k_ref, o_ref, rows, bsum_ref[cur, dirn, rows, :],
                               m, tot, states)
            for i in range(2 * HEADS):
                st_ref[i] = states[i]
            prepare((lff_next_ref, lfb_next_ref), 1 - cur)

        @pl.when(jnp.logical_not(fast_ok))
        def _():
            def robust_body(ci_, carry):
                for dirn, r0 in enumerate(chunk_rows(ci_)):
                    r_ref, q_ref, v_ref, lf_ref, k_ref, o_ref = dir_refs[dirn]
                    robust_chunk(dirn, r_ref, q_ref, v_ref, lf_ref, k_ref, o_ref, r0)
                return carry

            prepare((lff_next_ref, lfb_next_ref), 1 - cur)
            lax.fori_loop(0, n_chunks, robust_body, 0)

    @pl.when(j == pl.num_programs(1) - 1)
    def _():
        sfin_ref[0] = st_ref[...]


def _scan(q, v, lff, kf, lfb, kb, s0, *, row_off, seq_len, compute_o):
    batch = s0.shape[0]
    t = min(256, seq_len)
    nt = seq_len // t
    off = row_off // t
    fwd = lambda b, j: (off + b * nt + j, 0)
    bwd = lambda b, j: (off + b * nt + (nt - 1 - j), 0)
    ofwd = lambda b, j: (b * nt + j, 0)
    obwd = lambda b, j: (b * nt + (nt - 1 - j), 0)
    blk = lambda im: pl.BlockSpec((t, D_GROUP), im)
    rf = jnp.asarray(_scan_tables(SCAN_CHUNK, False), BF16)
    rb = jnp.asarray(_scan_tables(SCAN_CHUNK, True), BF16)
    rspec = pl.BlockSpec(rf.shape, lambda b, j: (0, 0))
    sspec = pl.BlockSpec((1, 2 * HEADS, HEAD_DIM, HEAD_DIM), lambda b, j: (b, 0, 0, 0))
    pos_t = np.arange(t)
    same_chunk = (pos_t[:, None] // SCAN_CHUNK) == (pos_t[None, :] // SCAN_CHUNK)
    tri_f = jnp.asarray(same_chunk & (pos_t[None, :] <= pos_t[:, None]), BF16)
    tri_b = jnp.asarray(same_chunk & (pos_t[None, :] >= pos_t[:, None]), BF16)
    tspec = pl.BlockSpec((t, t), lambda b, j: (0, 0))
    in_specs = [rspec, rspec, blk(fwd), blk(fwd), blk(fwd), blk(fwd),
                blk(bwd), blk(bwd), blk(bwd), blk(bwd), sspec, tspec, tspec]
    args = [rf, rb, q, v, lff, kf, q, v, lfb, kb, s0, tri_f, tri_b]
    out_shape, out_specs = [], []
    scratch = [pltpu.VMEM((2 * HEADS, HEAD_DIM, HEAD_DIM), F32)]
    if compute_o:
        def next_step(b, j):
            nxt = jnp.minimum(b * nt + j + 1, batch * nt - 1)
            return nxt // nt, nxt % nt

        in_specs += [blk(lambda b, j: fwd(*next_step(b, j))), blk(lambda b, j: bwd(*next_step(b, j)))]
        args += [lff, lfb]
        out_shape += [jax.ShapeDtypeStruct((batch * seq_len, D_GROUP), BF16)] * 2
        out_specs += [blk(ofwd), blk(obwd)]
        scratch += [pltpu.VMEM((2, 2, t, D_GROUP), F32), pltpu.SMEM((2,), I32)]
    out_shape.append(jax.ShapeDtypeStruct(s0.shape, F32))
    out_specs.append(sspec)
    kern = functools.partial(_scan_kernel, n_chunks=t // SCAN_CHUNK, compute_o=compute_o)
    return pl.pallas_call(
        kern,
        out_shape=out_shape,
        grid=(batch, nt),
        in_specs=in_specs,
        out_specs=out_specs,
        scratch_shapes=scratch,
        compiler_params=_cparams(("arbitrary", "arbitrary"), VMEM_LIMIT),
        name="hgrn_scan",
    )(*args)


def _ln_silu(y, g, b):
    mu = jnp.mean(y, axis=-1, keepdims=True)
    yc = y - mu
    var = jnp.mean(yc * yc, axis=-1, keepdims=True)
    return _silu(yc * lax.rsqrt(var + EPS) * g + b)


def _tree_sum(terms):
    while len(terms) > 1:
        terms = [terms[i] + terms[i + 1] if i + 1 < len(terms) else terms[i]
                 for i in range(0, len(terms), 2)]
    return terms[0]


def _conv_rows_kernel(u_ref, w_ref, b_ref, lg_ref, lb_ref, y_ref, pad_ref, acc_ref, *, seq, tb):
    nseq = tb // seq
    stride = seq + CONV_GAP
    n_pad = pad_ref.shape[2]
    n_lt = D_GROUP // LANES
    for cl in range(n_lt):
        ls = slice(cl * LANES, (cl + 1) * LANES)
        pad_ref[0, cl] = jnp.zeros(pad_ref.shape[2:], F32)
        for s in range(nseq):
            pad_ref[0, cl, CONV_GAP + s * stride:CONV_GAP + s * stride + seq, :] = (
                u_ref[s * seq:(s + 1) * seq, ls].astype(F32))
        for p in range(1, SUBLANES):
            pad_ref[p, cl, 0:n_pad - SUBLANES, :] = pad_ref[0, cl, p:p + n_pad - SUBLANES, :]
    rc = 64
    group = CONV_ROW_GROUP
    assert seq % rc == 0 and tb % (group * rc) == 0
    if seq == rc:
        member_off = [a * stride for a in range(group)]
        group_base = lambda g: g * (group * stride)
    else:
        gps = seq // (group * rc)
        assert seq % (group * rc) == 0
        member_off = [a * rc for a in range(group)]
        group_base = lambda g: (g // gps) * stride + (g % gps) * (group * rc)

    def one_group(g, carry):
        pad0 = pl.multiple_of(CONV_GAP + group_base(g), SUBLANES)
        out0 = pl.multiple_of(g * (group * rc), group * rc)
        for cl in range(n_lt):
            ls = slice(cl * LANES, (cl + 1) * LANES)
            parts = [[None] * CONV_PARTIALS for _ in range(group)]
            for k in range(CONV_K):
                off = k - CONV_HALF
                wk = w_ref[k:k + 1, ls]
                for a in range(group):
                    src = pl.ds(pad0 + member_off[a] + (off // SUBLANES) * SUBLANES, rc)
                    term = wk * pad_ref[off % SUBLANES, cl, src, :]
                    prev = parts[a][k % CONV_PARTIALS]
                    parts[a][k % CONV_PARTIALS] = term if prev is None else prev + term
            for a in range(group):
                acc_ref[pl.ds(out0 + a * rc, rc), ls] = _tree_sum(parts[a])
        return carry

    lax.fori_loop(0, tb // (group * rc), one_group, 0)
    y = acc_ref[...] + b_ref[...]
    y_ref[...] = _ln_silu(y, lg_ref[...], lb_ref[...]).astype(y_ref.dtype)


def _conv_rows(u, w, b, lg, lb, *, seq, row_off, n_rows):
    tb = max(seq, 512) if n_rows % max(seq, 512) == 0 else seq
    nseq = tb // seq
    off = row_off // tb
    vec = pl.BlockSpec((1, D_GROUP), lambda i: (0, 0))
    in_specs = [pl.BlockSpec((tb, D_GROUP), lambda i: (off + i, 0)),
                pl.BlockSpec((CONV_K, D_GROUP), lambda i: (0, 0)), vec, vec, vec]
    args = [u, w, b, lg, lb]
    kern = functools.partial(_conv_rows_kernel, seq=seq, tb=tb)
    return pl.pallas_call(
        kern,
        out_shape=jax.ShapeDtypeStruct((n_rows, D_GROUP), BF16),
        grid=(n_rows // tb,),
        in_specs=in_specs,
        out_specs=pl.BlockSpec((tb, D_GROUP), lambda i: (i, 0)),
        scratch_shapes=[pltpu.VMEM((SUBLANES, D_GROUP // LANES, CONV_GAP + nseq * (seq + CONV_GAP), LANES), F32),
                        pltpu.VMEM((tb, D_GROUP), F32)],
        compiler_params=_cparams(("arbitrary",), VMEM_LIMIT),
        name="conv_rows",
    )(*args)


CONV_COL_TILE = 16


def _conv_cols_kernel(u_ref, w_ref, b_ref, lg_ref, lb_ref, y_ref, uf_ref, *, n_rows):
    n_lt = D_GROUP // LANES
    row_stride = GRID_W + SUBLANES
    for cl in range(n_lt):
        ls = slice(cl * LANES, (cl + 1) * LANES)
        for r in range(n_rows):
            uf_ref[cl, r * row_stride:r * row_stride + GRID_W, :] = (
                u_ref[r * GRID_W:(r + 1) * GRID_W, ls].astype(F32))
    bias = b_ref[...]
    lg = lg_ref[...]
    lb = lb_ref[...]

    def col_tile(wi, carry):
        w0 = pl.multiple_of(wi * CONV_COL_TILE, CONV_COL_TILE)
        for r in range(n_rows):
            tiles = []
            for cl in range(n_lt):
                ls = slice(cl * LANES, (cl + 1) * LANES)
                parts = [None] * CONV_PARTIALS
                for k in range(CONV_K):
                    rr = r + k - CONV_HALF
                    if 0 <= rr < n_rows:
                        src = pl.ds(pl.multiple_of(rr * row_stride + w0, SUBLANES), CONV_COL_TILE)
                        term = w_ref[k:k + 1, ls] * uf_ref[cl, src, :]
                        prev = parts[k % CONV_PARTIALS]
                        parts[k % CONV_PARTIALS] = term if prev is None else prev + term
                tiles.append(_tree_sum([p for p in parts if p is not None]))
            dst = pl.ds(pl.multiple_of(r * GRID_W + w0, CONV_COL_TILE), CONV_COL_TILE)
            y_ref[dst, :] = _ln_silu(jnp.concatenate(tiles, axis=1) + bias, lg, lb).astype(y_ref.dtype)
        return carry

    lax.fori_loop(0, GRID_W // CONV_COL_TILE, col_tile, 0)


def _conv_cols(u, w, b, lg, lb, *, batch, seq_len):
    vec = pl.BlockSpec((1, D_GROUP), lambda bi: (0, 0))
    blk = pl.BlockSpec((seq_len, D_GROUP), lambda bi: (bi, 0))
    return pl.pallas_call(
        functools.partial(_conv_cols_kernel, n_rows=seq_len // GRID_W),
        out_shape=jax.ShapeDtypeStruct((batch * seq_len, D_GROUP), BF16),
        grid=(batch,),
        in_specs=[blk, pl.BlockSpec((CONV_K, D_GROUP), lambda bi: (0, 0)), vec, vec, vec],
        out_specs=blk,
        scratch_shapes=[pltpu.VMEM((D_GROUP // LANES, (seq_len // GRID_W) * (GRID_W + SUBLANES), LANES), F32)],
        compiler_params=_cparams(("arbitrary",), VMEM_LIMIT),
        name="conv_cols",
    )(u, w, b, lg, lb)


def _outproj_kernel(*refs, n_x, n_mix, n_lat_tiles):
    x = _rows_from(refs[:n_x], n_lat_tiles)
    pos = n_x
    ycx, of, ob = (_rows_from(refs[pos + k * n_mix:pos + (k + 1) * n_mix], n_lat_tiles) for k in range(3))
    og_ref, hg_ref, w_ref, g1_ref = refs[pos + 3 * n_mix:pos + 3 * n_mix + 4]
    route_in = refs[pos + 3 * n_mix + 4:pos + 3 * n_mix + 10]
    o_ref = refs[pos + 3 * n_mix + 10]
    route_out = refs[pos + 3 * n_mix + 11:]
    o = of.astype(F32) + ob.astype(F32)
    og = og_ref[...].astype(F32)
    hg = hg_ref[...]
    acc = jnp.dot(ycx, w_ref[0, 0:D_GROUP, :], preferred_element_type=F32)
    for h in range(HEADS):
        hs = slice(h * HEAD_DIM, (h + 1) * HEAD_DIM)
        oh = o[:, hs]
        r = lax.rsqrt(jnp.mean(oh * oh, axis=-1, keepdims=True) + EPS)
        yh = ((oh * r * hg) * og[:, hs]).astype(BF16)
        acc = acc + jnp.dot(yh, w_ref[0, D_GROUP + h * HEAD_DIM:D_GROUP + (h + 1) * HEAD_DIM, :],
                            preferred_element_type=F32)
    x = x + g1_ref[0] * acc
    o_ref[...] = x
    _route(x, *route_in, *route_out)


def _outproj_route(xs, mix, og, hg, w_bf, layer, g1, g2n, sh2, sc2, rw_t, rb, *, n_rows, tm, tiles_per_seq):
    d = xs[0].shape[1]
    nb = g1.shape[0] - 1
    n_tt = n_rows // tm
    x_specs, n_lat_tiles = _row_specs(xs, tm)
    mix_specs, mix_args = [], []
    for arrays in mix:
        specs, nl = _row_specs(arrays, tm)
        assert len(arrays) == len(mix[0]) and (nl == 0 or n_lat_tiles in (0, nl))
        n_lat_tiles = max(n_lat_tiles, nl)
        mix_specs += specs
        mix_args += list(arrays)
    row = lambda width: pl.BlockSpec((tm, width), lambda i: (i, 0))
    mod_spec = pl.BlockSpec((1, 1, d), lambda i: (jnp.minimum(i // tiles_per_seq, nb), 0, 0))
    pair = pl.BlockSpec((2, tm), lambda i: (0, i))
    kern = functools.partial(_outproj_kernel, n_x=len(xs), n_mix=len(mix[0]), n_lat_tiles=n_lat_tiles)
    before = jnp.asarray(np.arange(tm)[:, None] < np.arange(tm)[None, :], BF16)
    return pl.pallas_call(
        kern,
        out_shape=[jax.ShapeDtypeStruct((n_rows, d), F32),
                   jax.ShapeDtypeStruct((n_rows, d), BF16),
                   jax.ShapeDtypeStruct((2, n_rows), I32),
                   jax.ShapeDtypeStruct((2, n_rows), I32),
                   jax.ShapeDtypeStruct((2, n_rows), F32),
                   jax.ShapeDtypeStruct((n_tt, N_EXPERTS, LANES), I32)],
        grid=(n_tt,),
        in_specs=x_specs + mix_specs + [row(D_GROUP),
                  pl.BlockSpec((1, HEAD_DIM), lambda i: (0, 0)),
                  pl.BlockSpec((1,) + w_bf.shape[1:], lambda i: (layer, 0, 0)),
                  mod_spec,
                  pl.BlockSpec((1, d), lambda i: (0, 0)), mod_spec, mod_spec,
                  pl.BlockSpec((N_EXPERTS, d), lambda i: (0, 0)),
                  pl.BlockSpec((N_EXPERTS, 1), lambda i: (0, 0)),
                  pl.BlockSpec((tm, tm), lambda i: (0, 0))],
        out_specs=[row(d), row(d), pair, pair, pair,
                   pl.BlockSpec((1, N_EXPERTS, LANES), lambda i: (i, 0, 0))],
        compiler_params=_cparams(("arbitrary",), VMEM_LIMIT),
        name="outproj_route",
    )(*xs, *mix_args, og, hg, w_bf, g1, g2n, sh2, sc2, rw_t, rb, before)


def _route(x, g_ref, sh_ref, sc_ref, rw_ref, rb_ref, before_ref, h_ref, e_ref, rank_ref, w_ref, cnt_ref):
    tm = x.shape[0]
    h = _modulate(x, g_ref[...], sh_ref[0], sc_ref[0])
    h_ref[...] = h.astype(h_ref.dtype)
    logits = lax.dot_general(rw_ref[...], h, (((1,), (1,)), ((), ())),
                             precision=lax.Precision.HIGHEST, preferred_element_type=F32)
    s = jax.nn.sigmoid(logits)
    sb = s + rb_ref[...]
    s_rows = [s[e:e + 1, :] for e in range(N_EXPERTS)]
    sb_rows = [sb[e:e + 1, :] for e in range(N_EXPERTS)]

    def group_score(g):
        v = sb_rows[g * EXPERTS_PER_GROUP:(g + 1) * EXPERTS_PER_GROUP]
        best = None
        for a in range(EXPERTS_PER_GROUP):
            for b in range(a + 1, EXPERTS_PER_GROUP):
                p = v[a] + v[b]
                best = p if best is None else jnp.maximum(best, p)
        return best

    cur = group_score(0)
    best_g = jnp.zeros(cur.shape, I32)
    for g in range(1, N_GROUPS):
        gs = group_score(g)
        upd = gs > cur
        best_g = jnp.where(upd, g, best_g)
        cur = jnp.where(upd, gs, cur)

    def pick(rows, jdx):
        out = rows[jdx]
        for g in range(1, N_GROUPS):
            out = jnp.where(best_g == g, rows[g * EXPERTS_PER_GROUP + jdx], out)
        return out

    vb = [pick(sb_rows, jdx) for jdx in range(EXPERTS_PER_GROUP)]
    vs = [pick(s_rows, jdx) for jdx in range(EXPERTS_PER_GROUP)]

    def first_argmax(vals):
        m = vals[0]
        for v in vals[1:]:
            m = jnp.maximum(m, v)
        idx = jnp.full(m.shape, EXPERTS_PER_GROUP - 1, I32)
        for jdx in range(EXPERTS_PER_GROUP - 2, -1, -1):
            idx = jnp.where(vals[jdx] == m, jdx, idx)
        return idx

    def take(vals, idx):
        out = vals[EXPERTS_PER_GROUP - 1]
        for jdx in range(EXPERTS_PER_GROUP - 2, -1, -1):
            out = jnp.where(idx == jdx, vals[jdx], out)
        return out

    i0 = first_argmax(vb)
    vb2 = [jnp.where(i0 == jdx, -jnp.inf, vb[jdx]) for jdx in range(EXPERTS_PER_GROUP)]
    i1 = first_argmax(vb2)
    s0 = take(vs, i0)
    s1 = take(vs, i1)
    tot = s0 + s1
    w_ref[0:1, :] = s0 / tot
    w_ref[1:2, :] = s1 / tot
    e0 = best_g * EXPERTS_PER_GROUP + i0
    e1 = best_g * EXPERTS_PER_GROUP + i1
    e_ref[0:1, :] = e0
    e_ref[1:2, :] = e1

    eid = lax.broadcasted_iota(I32, (N_EXPERTS, tm), 0)
    hit0 = eid == e0
    hit1 = eid == e1
    onehot = jnp.where(jnp.logical_or(hit0, hit1), 1.0, 0.0)
    rank = jnp.dot(onehot.astype(BF16), before_ref[...], preferred_element_type=F32)
    rank_ref[0:1, :] = jnp.sum(jnp.where(hit0, rank, 0.0), axis=0, keepdims=True).astype(I32)
    rank_ref[1:2, :] = jnp.sum(jnp.where(hit1, rank, 0.0), axis=0, keepdims=True).astype(I32)
    cnt = jnp.sum(onehot, axis=1, keepdims=True).astype(I32)
    cnt_ref[0] = jnp.broadcast_to(cnt, (N_EXPERTS, LANES))


def _tables_kernel(cnt_ref, seg_start_ref, seg_len_ref, te_ref, misc_ref, *, n_tt, n_et, tm):
    def expert(e, row0):
        def tile(i, pos):
            n = cnt_ref[i * N_EXPERTS + e]
            n8 = ((n + SUBLANES - 1) // SUBLANES) * SUBLANES
            seg_start_ref[i * N_EXPERTS + e] = pos
            seg_len_ref[i * N_EXPERTS + e] = n8
            return pos + n8

        end = lax.fori_loop(0, n_tt, tile, row0)
        padded_end = row0 + ((end - row0 + tm - 1) // tm) * tm
        misc_ref[1 + e] = end
        misc_ref[1 + N_EXPERTS + e] = padded_end - end

        def mark(j, carry):
            te_ref[j] = e
            return carry

        lax.fori_loop(row0 // tm, padded_end // tm, mark, 0)
        return padded_end

    total = row0 = 0
    for e in range(N_EXPERTS):
        row0 = expert(e, row0)
    total = row0
    n_used = total // tm
    misc_ref[0] = n_used
    last = te_ref[jnp.maximum(n_used - 1, 0)]

    def fill(j, carry):
        te_ref[j] = last
        return carry

    lax.fori_loop(n_used, n_et, fill, 0)


def _tables(cnt, *, n_tt, n_et, tm):
    smem = pl.BlockSpec(memory_space=pltpu.SMEM)
    return pl.pallas_call(
        functools.partial(_tables_kernel, n_tt=n_tt, n_et=n_et, tm=tm),
        out_shape=[jax.ShapeDtypeStruct((n_tt * N_EXPERTS,), I32),
                   jax.ShapeDtypeStruct((n_tt * N_EXPERTS,), I32),
                   jax.ShapeDtypeStruct((n_et,), I32),
                   jax.ShapeDtypeStruct((1 + 2 * N_EXPERTS,), I32)],
        in_specs=[smem],
        out_specs=[smem, smem, smem, smem],
        name="route_tables",
    )(cnt)


def _piece_sizes(max_rows):
    sizes, s = [], min(SEGMENT_CHUNK, max_rows)
    while s >= SUBLANES:
        sizes.append(s)
        s //= 2
    return sizes


def _segment_copies(src_ref, src_row, dst_ref, dst_row, n_rows, sem, sizes, start):
    aligned = lambda r: r if isinstance(r, int) else pl.multiple_of(r, SUBLANES)

    def one(src0, dst0, sz):
        cp = pltpu.make_async_copy(src_ref.at[pl.ds(aligned(src0), sz), :],
                                   dst_ref.at[pl.ds(aligned(dst0), sz), :], sem)
        if start:
            cp.start()
        else:
            cp.wait()

    chunk = sizes[0]
    n_bulk = lax.shift_right_logical(n_rows, jnp.int32(chunk.bit_length() - 1))

    def bulk(t, carry):
        one(src_row + t * chunk, dst_row + t * chunk, chunk)
        return carry

    lax.fori_loop(0, n_bulk, bulk, 0)
    src_row = src_row + n_bulk * chunk
    dst_row = dst_row + n_bulk * chunk
    for sz in sizes[1:]:
        @pl.when((n_rows & sz) != 0)
        def _(src_row=src_row, dst_row=dst_row, sz=sz):
            one(src_row, dst_row, sz)

        src_row = src_row + (n_rows & sz)
        dst_row = dst_row + (n_rows & sz)


def _local_slots(seg_len_ref, tile, e_ref, rank_ref):
    offs, lo = [], 0
    for e in range(N_EXPERTS):
        offs.append(lo)
        lo = lo + seg_len_ref[tile * N_EXPERTS + e]
    e01 = e_ref[...]
    slot = rank_ref[...]
    for e in range(N_EXPERTS):
        slot = slot + jnp.where(e01 == e, offs[e], 0)
    return slot, offs


def _compact_rows(tm):
    return ((2 * tm + N_EXPERTS * (SUBLANES - 1) + LANES - 1) // LANES) * LANES


def _dispatch_kernel(seg_start_ref, seg_len_ref, misc_ref, h_ref, e_ref, rank_ref, xs_ref,
                     xc_ref, zero_ref, sems, *, sizes):
    tm = h_ref.shape[0]
    rc = xc_ref.shape[1]
    i = pl.program_id(0)
    buf = i % 2
    slot, _ = _local_slots(seg_len_ref, i, e_ref, rank_ref)
    rid = lax.broadcasted_iota(I32, (rc, tm), 0)
    sel = jnp.logical_or(rid == slot[0:1, :], rid == slot[1:2, :])
    perm = jnp.where(sel, 1.0, 0.0).astype(BF16)
    xc_ref[buf] = jnp.dot(perm, h_ref[...], preferred_element_type=F32)

    def segments(tile, b, start):
        lo = 0
        for e in range(N_EXPERTS):
            n8 = seg_len_ref[tile * N_EXPERTS + e]
            _segment_copies(xc_ref.at[b], lo, xs_ref, seg_start_ref[tile * N_EXPERTS + e], n8,
                            sems.at[b], sizes, start)
            lo = lo + n8

    @pl.when(i > 0)
    def _():
        segments(i - 1, 1 - buf, False)

    segments(i, buf, True)

    @pl.when(i == pl.num_programs(0) - 1)
    def _():
        segments(i, buf, False)
        zero_ref[...] = jnp.zeros(zero_ref.shape, F32)
        for start in (True, False):
            for e in range(N_EXPERTS):
                _segment_copies(zero_ref, 0, xs_ref, misc_ref[1 + e], misc_ref[1 + N_EXPERTS + e],
                                sems.at[0], sizes, start)

        def clear_tile(j, carry):
            cp = pltpu.make_async_copy(zero_ref, xs_ref.at[pl.ds(pl.multiple_of(j * tm, tm), tm), :],
                                       sems.at[0])
            cp.start()
            cp.wait()
            return carry

        lax.fori_loop(misc_ref[0], xs_ref.shape[0] // tm, clear_tile, 0)


def _dispatch(seg_start, seg_len, misc, h, e01, rank01, *, n_sorted, tm):
    n, d = h.shape
    pair = pl.BlockSpec((2, tm), lambda i, *_: (0, i))
    grid_spec = pltpu.PrefetchScalarGridSpec(
        num_scalar_prefetch=3,
        grid=(n // tm,),
        in_specs=[pl.BlockSpec((tm, d), lambda i, *_: (i, 0)), pair, pair],
        out_specs=pl.BlockSpec(memory_space=pl.ANY),
        scratch_shapes=[pltpu.VMEM((2, _compact_rows(tm), d), F32), pltpu.VMEM((tm, d), F32),
                        pltpu.SemaphoreType.DMA((2,))],
    )
    return pl.pallas_call(
        functools.partial(_dispatch_kernel, sizes=_piece_sizes(tm)),
        out_shape=jax.ShapeDtypeStruct((n_sorted, d), F32),
        grid_spec=grid_spec,
        compiler_params=_cparams(("arbitrary",), VMEM_LIMIT),
        name="dispatch",
    )(seg_start, seg_len, misc, h, e01, rank01)


def _moe_kernel(te_ref, misc_ref, xs_ref, wg_hbm, wu_hbm, wd_hbm, ys_ref,
                wg_buf, wu_buf, wd_buf, wgb_ref, wub_ref, wdb_ref, run_ref, sems, *, layer, tm):
    i = pl.program_id(0)
    e = te_ref[i]
    prev = te_ref[jnp.maximum(i - 1, 0)]
    n_used = misc_ref[0]

    def fetch(expert, slot):
        return [pltpu.make_async_copy(w.at[layer, expert], buf.at[slot], sems.at[slot, k])
                for k, (w, buf) in enumerate(((wg_hbm, wg_buf), (wu_hbm, wu_buf), (wd_hbm, wd_buf)))]

    @pl.when(i == 0)
    def _():
        run_ref[0] = 0
        for cp in fetch(e, 0):
            cp.start()

    @pl.when(jnp.logical_or(i == 0, e != prev))
    def _():
        slot = run_ref[0] % 2
        for cp in fetch(e, slot):
            cp.wait()
        nxt = (misc_ref[1 + e] + misc_ref[1 + N_EXPERTS + e]) // tm

        @pl.when(nxt < n_used)
        def _():
            for cp in fetch(te_ref[nxt], 1 - slot):
                cp.start()

        wgb_ref[...] = wg_buf[slot].astype(BF16)
        wub_ref[...] = wu_buf[slot].astype(BF16)
        wdb_ref[...] = wd_buf[slot].astype(BF16)
        run_ref[0] = run_ref[0] + 1

    @pl.when(i < misc_ref[0])
    def _():
        x = xs_ref[...].astype(BF16)
        gate = jnp.dot(x, wgb_ref[...], preferred_element_type=F32)
        up = jnp.dot(x, wub_ref[...], preferred_element_type=F32)
        act = (_silu(gate) * up).astype(BF16)
        ys_ref[...] = jnp.dot(act, wdb_ref[...], preferred_element_type=F32)

    @pl.when(i >= misc_ref[0])
    def _():
        ys_ref[...] = jnp.zeros(ys_ref.shape, F32)


def _moe(tile_expert, misc, xs, wg, wu, wd, layer, tm):
    n_sorted, d = xs.shape
    de = wg.shape[-1]
    xmap = lambda i, te, misc_ref: (jnp.minimum(i, jnp.maximum(misc_ref[0] - 1, 0)), 0)
    hbm = pl.BlockSpec(memory_space=pl.ANY)
    grid_spec = pltpu.PrefetchScalarGridSpec(
        num_scalar_prefetch=2,
        grid=(n_sorted // tm,),
        in_specs=[pl.BlockSpec((tm, d), xmap), hbm, hbm, hbm],
        out_specs=pl.BlockSpec((tm, d), lambda i, te, misc_ref: (i, 0)),
        scratch_shapes=[pltpu.VMEM((2, d, de), F32), pltpu.VMEM((2, d, de), F32),
                        pltpu.VMEM((2, de, d), F32),
                        pltpu.VMEM((d, de), BF16), pltpu.VMEM((d, de), BF16),
                        pltpu.VMEM((de, d), BF16),
                        pltpu.SMEM((1,), I32), pltpu.SemaphoreType.DMA((2, 3))],
    )
    return pl.pallas_call(
        functools.partial(_moe_kernel, layer=layer, tm=tm),
        out_shape=jax.ShapeDtypeStruct((n_sorted, d), F32),
        grid_spec=grid_spec,
        compiler_params=_cparams(("arbitrary",), VMEM_LIMIT),
        name="moe_experts",
    )(tile_expert, misc, xs, wg, wu, wd)


def _combine_kernel(seg_start_ref, seg_len_ref, x_ref, g2_ref, e_ref, rank_ref, w_ref, fg_ref, ys_ref,
                    o_ref, yc_ref, sems, *, sizes, final_norm):
    tm = x_ref.shape[0]
    rc = yc_ref.shape[1]
    i = pl.program_id(0)
    buf = i % 2

    def segments(tile, b, start):
        lo = 0
        for e in range(N_EXPERTS):
            n8 = seg_len_ref[tile * N_EXPERTS + e]
            _segment_copies(ys_ref, seg_start_ref[tile * N_EXPERTS + e], yc_ref.at[b], lo, n8,
                            sems.at[b], sizes, start)
            lo = lo + n8

    @pl.when(i == 0)
    def _():
        yc_ref[...] = jnp.zeros(yc_ref.shape, F32)
        segments(0, 0, True)

    @pl.when(i + 1 < pl.num_programs(0))
    def _():
        segments(i + 1, 1 - buf, True)

    segments(i, buf, False)
    slot, _ = _local_slots(seg_len_ref, i, e_ref, rank_ref)
    rid = lax.broadcasted_iota(I32, (rc, tm), 0)
    w = w_ref[...]
    pw = (jnp.where(rid == slot[0:1, :], w[0:1, :], 0.0)
          + jnp.where(rid == slot[1:2, :], w[1:2, :], 0.0))
    dn = (((0,), (0,)), ((), ()))
    out = lax.dot_general(pw.astype(BF16), yc_ref[buf].astype(BF16), dn, preferred_element_type=F32)
    x = x_ref[...] + g2_ref[0] * out
    if final_norm:
        x = x * lax.rsqrt(jnp.mean(x * x, axis=-1, keepdims=True) + EPS) * fg_ref[...]
    o_ref[...] = x


def _combine(seg_start, seg_len, xc, g2, e01, rank01, w01, fg, ys, *, n_rows, tm, tiles_per_seq,
             final_norm):
    d = xc.shape[1]
    nb = g2.shape[0] - 1
    pair = pl.BlockSpec((2, tm), lambda i, *_: (0, i))
    grid_spec = pltpu.PrefetchScalarGridSpec(
        num_scalar_prefetch=2,
        grid=(n_rows // tm,),
        in_specs=[pl.BlockSpec((tm, d), lambda i, *_: (i, 0)),
                  pl.BlockSpec((1, 1, d), lambda i, *_: (jnp.minimum(i // tiles_per_seq, nb), 0, 0)),
                  pair, pair, pair,
                  pl.BlockSpec((1, d), lambda i, *_: (0, 0)),
                  pl.BlockSpec(memory_space=pl.ANY)],
        out_specs=pl.BlockSpec((tm, d), lambda i, *_: (i, 0)),
        scratch_shapes=[pltpu.VMEM((2, _compact_rows(tm), d), F32), pltpu.SemaphoreType.DMA((2,))],
    )
    return pl.pallas_call(
        functools.partial(_combine_kernel, sizes=_piece_sizes(tm), final_norm=final_norm),
        out_shape=jax.ShapeDtypeStruct((n_rows, d), F32),
        grid_spec=grid_spec,
        compiler_params=_cparams(("arbitrary",), VMEM_LIMIT),
        name="combine",
    )(seg_start, seg_len, xc, g2, e01, rank01, w01, fg, ys)


def _lower_bounds(lb_param):
    p = jax.nn.softmax(lb_param.astype(F32), axis=0)
    return jnp.cumsum(p, axis=0) - p[0]


def kernel(x, c, ctx, c_ctx, w_ada, b_ada, norm1_g, norm2_g, w_in, conv_w, conv_b, conv_ln_g,
           conv_ln_b, lb_fwd, lb_bwd, hgrn_norm_g, w_out, router_w, router_bias, w_gate, w_up,
           w_down, final_norm_g):
    bn, seq, d = x.shape
    ctx_len = ctx.shape[1]
    depth = w_ada.shape[0]
    n = bn * seq
    nc = bn * ctx_len
    tm = min(512, seq, nc)
    assert seq % tm == 0 and nc % tm == 0 and seq % GRID_W == 0
    tps = seq // tm

    pad_rows = (-(bn + 1)) % 8
    cc = jnp.concatenate([c, c_ctx[None, :], jnp.zeros((pad_rows, d), F32)], axis=0)
    mod = _ada(cc, w_ada, b_ada)[:, :bn + 1]
    mod = mod.reshape(depth, bn + 1, 6, 1, d)

    lbs_f = _lower_bounds(lb_fwd)
    lbs_b = _lower_bounds(lb_bwd)
    w_in_bf = w_in.astype(BF16)
    w_out_bf = w_out.astype(BF16)
    rw_t = router_w.T.astype(F32)
    rb = router_bias.reshape(N_EXPERTS, 1).astype(F32)
    hg = hgrn_norm_g.astype(F32)
    fg = final_norm_g.reshape(1, d).astype(F32)

    rows_in = [x.reshape(n, d), ctx.reshape(nc, d)]
    s_zero = jnp.zeros((bn, 2 * HEADS, HEAD_DIM, HEAD_DIM), F32)

    for l in range(depth):
        last = l == depth - 1
        sh1, sc1, g1, sh2, sc2, g2 = (mod[l, :, k] for k in range(6))
        row = lambda a: a.reshape(1, -1).astype(F32)

        u, q, iv, lff, kf, lfb, kb, og = _inproj(
            rows_in, row(norm1_g[l]), sh1, sc1, w_in_bf, l, row(lbs_f[l]), row(lbs_b[l]), tm, tps)

        cw, cb = conv_w[l].astype(F32), row(conv_b[l])
        clg, clb = row(conv_ln_g[l]), row(conv_ln_b[l])
        scan = functools.partial(_scan, q, iv, lff, kf, lfb, kb)
        if last:
            (s_ctx,) = scan(s_zero, row_off=n, seq_len=ctx_len, compute_o=False)
            mix_ctx = [[], [], []]
            n_rows = n
        else:
            of_c, ob_c, s_ctx = scan(s_zero, row_off=n, seq_len=ctx_len, compute_o=True)
            ycx_c = _conv_rows(u, cw, cb, clg, clb, seq=ctx_len, row_off=n, n_rows=nc)
            mix_ctx = [[ycx_c], [of_c], [ob_c]]
            n_rows = n + nc
        of, ob, _ = scan(s_ctx, row_off=0, seq_len=seq, compute_o=True)
        if l % 2 == 0:
            ycx = _conv_rows(u, cw, cb, clg, clb, seq=GRID_W, row_off=0, n_rows=n)
        else:
            ycx = _conv_cols(u, cw, cb, clg, clb, batch=bn, seq_len=seq)
        mix = [[a] + c_ for a, c_ in zip((ycx, of, ob), mix_ctx)]

        res_rows = rows_in if (len(rows_in) == 1 or not last) else [rows_in[0]]
        xc, h2, e01, rank01, w01, cnt = _outproj_route(
            res_rows, mix, og, hg[l].reshape(1, HEAD_DIM), w_out_bf, l, g1,
            row(norm2_g[l]), sh2, sc2, rw_t, rb, n_rows=n_rows, tm=tm, tiles_per_seq=tps)
        n_tt = n_rows // tm
        n_et = (2 * n_rows + n_tt * N_EXPERTS * (SUBLANES - 1) + tm - 1) // tm + N_EXPERTS
        seg_start, seg_len, te, misc = _tables(cnt[:, :, 0].reshape(-1), n_tt=n_tt, n_et=n_et, tm=tm)
        xs = _dispatch(seg_start, seg_len, misc, h2, e01, rank01, n_sorted=n_et * tm, tm=tm)
        ys = _moe(te, misc, xs, w_gate, w_up, w_down, l, tm)
        xc = _combine(seg_start, seg_len, xc, g2, e01, rank01, w01, fg, ys, n_rows=n_rows, tm=tm,
                      tiles_per_seq=tps, final_norm=last)
        rows_in = [xc]

    return xc[:n].reshape(bn, seq, d)
```

```python
import functools

import numpy as np
import jax
import jax.numpy as jnp
from jax import lax
from jax.experimental import pallas as pl
from jax.experimental.pallas import tpu as pltpu

F32 = jnp.float32
BF16 = jnp.bfloat16
I32 = jnp.int32

EPS = 1e-6
GRID_W = 64
HEADS = 4
HEAD_DIM = 128
D_GROUP = HEADS * HEAD_DIM
N_IN_GROUPS = 7
CONV_K = 31
CONV_HALF = CONV_K // 2
CONV_GAP = 16
CONV_ROW_GROUP = 1
CONV_PARTIALS = 2
N_EXPERTS = 16
N_GROUPS = 4
EXPERTS_PER_GROUP = N_EXPERTS // N_GROUPS
SCAN_CHUNK = 64
SCAN_BLOCK = 256
ROW_TILE = 512
SCAN_FAST_LIMIT = 60.0
LANES = 128
SUBLANES = 8
VMEM_LIMIT = 56 * 1024 * 1024


def _cparams(sem, vmem=None):
    return pltpu.CompilerParams(dimension_semantics=sem, vmem_limit_bytes=vmem)


def _sigmoid(x):
    return 0.5 * jnp.tanh(0.5 * x) + 0.5


def _silu(x):
    return x * _sigmoid(x)


def _ada_kernel(c_ref, w_ref, b_ref, o_ref):
    a = _silu(c_ref[...]).astype(BF16)
    w = w_ref[0].astype(BF16)
    o_ref[0] = jnp.dot(a, w, preferred_element_type=F32) + b_ref[0]


def _ada(cc, w_ada, b_ada, tn=1536):
    depth, d, d6 = w_ada.shape
    rows = cc.shape[0]
    return pl.pallas_call(
        _ada_kernel,
        out_shape=jax.ShapeDtypeStruct((depth, rows, d6), F32),
        grid=(depth, d6 // tn),
        in_specs=[
            pl.BlockSpec((rows, d), lambda l, j: (0, 0)),
            pl.BlockSpec((1, d, tn), lambda l, j: (l, 0, j)),
            pl.BlockSpec((1, 1, tn), lambda l, j: (l, 0, j)),
        ],
        out_specs=pl.BlockSpec((1, rows, tn), lambda l, j: (l, 0, j)),
        compiler_params=_cparams(("arbitrary", "arbitrary"), VMEM_LIMIT),
        name="ada",
    )(cc, w_ada, b_ada.reshape(depth, 1, d6))


def _modulate(x, g, shift, scale):
    r = lax.rsqrt(jnp.mean(x * x, axis=-1, keepdims=True) + EPS)
    return (x * r * g) * (1.0 + scale) + shift


def _rows_from(refs, n_lat_tiles):
    if len(refs) == 1:
        return refs[0][...]
    return jnp.where(pl.program_id(0) >= n_lat_tiles, refs[1][...], refs[0][...])


def _row_specs(arrays, tm):
    width = arrays[0].shape[1]
    if len(arrays) == 1:
        return [pl.BlockSpec((tm, width), lambda i, *_: (i, 0))], 0
    n_lat_tiles = arrays[0].shape[0] // tm
    return [pl.BlockSpec((tm, width), lambda i, *_: (jnp.minimum(i, n_lat_tiles - 1), 0)),
            pl.BlockSpec((tm, width), lambda i, *_: (jnp.maximum(i - n_lat_tiles, 0), 0))], n_lat_tiles


def _inproj_kernel(*refs, n_src, n_lat_tiles, n_full_tiles):
    x = _rows_from(refs[:n_src], n_lat_tiles)
    (g_ref, sh_ref, sc_ref, w_ref, lbf_ref, lbb_ref,
     u_ref, q_ref, i_ref, lff_ref, kf_ref, lfb_ref, kb_ref, og_ref) = refs[n_src:]
    hb = _modulate(x, g_ref[...], sh_ref[0], sc_ref[0]).astype(BF16)

    def proj(j):
        return jnp.dot(hb, w_ref[0, :, j * D_GROUP:(j + 1) * D_GROUP], preferred_element_type=F32)

    i_ref[...] = proj(3).astype(BF16)
    for j, lb_ref, lf_ref, k_ref in ((4, lbf_ref, lff_ref, kf_ref), (5, lbb_ref, lfb_ref, kb_ref)):
        lb = lb_ref[...]
        f = lb + (1.0 - lb) * jax.nn.sigmoid(proj(j))
        lf_ref[...] = jnp.log(f)
        k_ref[...] = (1.0 - f).astype(BF16)

    def mixer_inputs():
        u_ref[...] = (proj(0) * _sigmoid(proj(1))).astype(BF16)
        q_ref[...] = _silu(proj(2)).astype(BF16)
        og_ref[...] = _silu(proj(6)).astype(BF16)

    if n_full_tiles is None:
        mixer_inputs()
    else:
        pl.when(pl.program_id(0) < n_full_tiles)(mixer_inputs)

        @pl.when(pl.program_id(0) >= n_full_tiles)
        def _():
            for ref in (u_ref, q_ref, og_ref):
                ref[...] = jnp.zeros(ref.shape, ref.dtype)


def _inproj(xs, g, sh, sc, w_bf, layer, lbf, lbb, tm, tiles_per_seq, n_full_rows=None):
    n = sum(a.shape[0] for a in xs)
    d = xs[0].shape[1]
    x_specs, n_lat_tiles = _row_specs(xs, tm)
    nb = sh.shape[0] - 1
    mod_spec = pl.BlockSpec((1, 1, d), lambda i: (jnp.minimum(i // tiles_per_seq, nb), 0, 0))
    row_spec = pl.BlockSpec((tm, D_GROUP), lambda i: (i, 0))
    vec_spec = pl.BlockSpec((1, D_GROUP), lambda i: (0, 0))
    outs = [jax.ShapeDtypeStruct((n, D_GROUP), dt)
            for dt in (BF16, BF16, BF16, F32, BF16, F32, BF16, BF16)]
    return pl.pallas_call(
        functools.partial(_inproj_kernel, n_src=len(xs), n_lat_tiles=n_lat_tiles,
                          n_full_tiles=None if n_full_rows is None else n_full_rows // tm),
        out_shape=outs,
        grid=(n // tm,),
        in_specs=x_specs + [
            pl.BlockSpec((1, d), lambda i: (0, 0)),
            mod_spec, mod_spec,
            pl.BlockSpec((1, d, N_IN_GROUPS * D_GROUP), lambda i: (layer, 0, 0)),
            vec_spec, vec_spec,
        ],
        out_specs=[row_spec] * 8,
        compiler_params=_cparams(("arbitrary",), VMEM_LIMIT),
        name="inproj",
    )(*xs, g, sh, sc, w_bf, lbf, lbb)


def _scan_tables(c, backward):
    levels = int(np.log2(c))
    r_all = np.zeros((levels * c + 2 * c + 8, c), np.float32)
    for lev in range(levels):
        h = 1 << lev
        for r in range(c):
            bd = (r // (2 * h)) * 2 * h + h
            if not backward:
                if r >= bd:
                    r_all[lev * c + r, bd:r + 1] = 1.0
                else:
                    r_all[lev * c + r, r + 1:bd] = 1.0
            else:
                if r < bd:
                    r_all[lev * c + r, r:bd] = 1.0
                else:
                    r_all[lev * c + r, bd:r] = 1.0
    base = levels * c
    for r in range(c):
        if not backward:
            r_all[base + r, :r + 1] = 1.0
            r_all[base + c + r, r + 1:] = 1.0
        else:
            r_all[base + r, r:] = 1.0
            r_all[base + c + r, :r] = 1.0
    r_all[base + 2 * c:, :] = 1.0
    return r_all


def _scan_kernel(*refs, n_chunks, compute_o):
    c = SCAN_CHUNK
    levels = int(np.log2(c))
    (rf_ref, rb_ref, qf_ref, vf_ref, lff_ref, kf_ref,
     qb_ref, vb_ref, lfb_ref, kb_ref, s0_ref, trif_ref, trib_ref) = refs[:13]
    pos = 13
    if compute_o:
        lff_next_ref, lfb_next_ref = refs[pos:pos + 2]
        of_ref, ob_ref, sfin_ref, st_ref, bsum_ref, flag_ref = refs[pos + 2:pos + 8]
    else:
        sfin_ref, st_ref = refs[pos:pos + 2]
        of_ref = ob_ref = None
    j = pl.program_id(1)

    @pl.when(j == 0)
    def _():
        st_ref[...] = s0_ref[0]

    row = lax.broadcasted_iota(jnp.int32, (c, 1), 0)
    ri = lax.broadcasted_iota(jnp.int32, (c, c), 0)
    ci = lax.broadcasted_iota(jnp.int32, (c, c), 1)
    upper = [(row & (2 * (1 << lev) - 1)) >= (1 << lev) for lev in range(levels)]
    same_parent = [(ri >> (lev + 1)) == (ci >> (lev + 1)) for lev in range(levels)]
    diag = ri == ci
    dn_t = (((1,), (1,)), ((), ()))
    dn_tl = (((0,), (0,)), ((), ()))

    half = c // 2
    tri = [jnp.where(ci <= ri, 1.0, 0.0).astype(BF16), jnp.where(ci >= ri, 1.0, 0.0).astype(BF16)]
    first_half = [row < half, row >= half]
    re = lax.broadcasted_iota(jnp.int32, (c, 4 * c), 0)
    ce = lax.broadcasted_iota(jnp.int32, (c, 4 * c), 1)
    cs = ce & (c - 1)
    same_half = (re >= half) == (cs >= half)
    own = (ce & (2 * c - 1)) < c
    mask_ext = [
        (own & same_half & (cs <= re)) | (jnp.logical_not(own) & (re >= half) & (cs < half)),
        (own & same_half & (cs >= re)) | (jnp.logical_not(own) & (re < half) & (cs >= half)),
    ]

    mrow, trow = [half - 1, half], [c - 1, 0]

    def prefix(tri_mat, g):
        g_hi = g.astype(BF16)
        g_lo = (g - g_hi.astype(F32)).astype(BF16)
        return (jnp.dot(tri_mat, g_hi, preferred_element_type=F32)
                + jnp.dot(tri_mat, g_lo, preferred_element_type=F32))

    def next_state(st, hs, kh, vh, b, tot):
        kd = (kh * jnp.exp(tot[:, hs] - b[:, hs])).astype(BF16)
        return st * jnp.exp(tot[:, hs]) + lax.dot_general(vh, kd, dn_tl, preferred_element_type=F32)

    def state_only_chunk(dirn, lf_ref, k_ref, v_ref, r0):
        rows = pl.ds(r0, c)
        b = prefix(tri[dirn], lf_ref[rows, :])
        tot = b[trow[dirn]:trow[dirn] + 1, :]
        for h in range(HEADS):
            hs = slice(h * HEAD_DIM, (h + 1) * HEAD_DIM)
            st_ref[dirn * HEADS + h] = next_state(st_ref[dirn * HEADS + h], hs,
                                                  k_ref[rows, hs].astype(F32), v_ref[rows, hs], b, tot)

    def fast_chunk(dirn, q_ref, v_ref, k_ref, o_ref, rows, b, m, tot, states):
        fh = first_half[dirn]
        hd = HEAD_DIM

        def block_diag(a, bb):
            za = jnp.zeros((a.shape[0], bb.shape[1]), a.dtype)
            zb = jnp.zeros((bb.shape[0], a.shape[1]), a.dtype)
            return jnp.concatenate([jnp.concatenate([a, za], axis=1),
                                    jnp.concatenate([zb, bb], axis=1)], axis=0)

        for hp in range(HEADS // 2):
            ps = slice(2 * hp * hd, 2 * (hp + 1) * hd)
            bp, mp, tp = b[:, ps], m[:, ps], tot[:, ps]
            cdec = bp - jnp.where(fh, 0.0, mp)
            e_own = jnp.exp(cdec).astype(BF16)
            e_own_inv = jnp.exp(-cdec).astype(BF16)
            e_cross = jnp.where(fh, jnp.exp(jnp.minimum(mp - bp, 0.0)), 0.0).astype(BF16)
            e_inc = jnp.exp(bp).astype(BF16)
            e_dec = jnp.exp(tp - bp).astype(BF16)
            qp, kp, vp = q_ref[rows, ps], k_ref[rows, ps], v_ref[rows, ps]
            q1, qi = qp * e_own, qp * e_inc
            k1, k2, kd = kp * e_own_inv, kp * e_cross, kp * e_dec
            kcat = [jnp.concatenate([k1[:, s_], k2[:, s_]], axis=0) for s_ in (slice(0, hd), slice(hd, 2 * hd))]
            sc = lax.dot_general(q1, block_diag(*kcat), dn_t, preferred_element_type=F32)
            p = jnp.where(mask_ext[dirn], sc, 0.0).astype(BF16)
            v2 = [jnp.concatenate([vp[:, s_], vp[:, s_]], axis=0) for s_ in (slice(0, hd), slice(hd, 2 * hd))]
            st = [states[dirn * HEADS + 2 * hp + a] for a in range(2)]
            o = (jnp.dot(p, block_diag(*v2), preferred_element_type=F32)
                 + lax.dot_general(qi, block_diag(st[0].astype(BF16), st[1].astype(BF16)), dn_t,
                                   preferred_element_type=F32))
            o_ref[rows, ps] = o.astype(o_ref.dtype)
            upd = lax.dot_general(vp, kd, dn_tl, preferred_element_type=F32)
            e_tot = jnp.exp(tp)
            for a in range(2):
                sl = slice(a * hd, (a + 1) * hd)
                states[dirn * HEADS + 2 * hp + a] = st[a] * e_tot[:, sl] + upd[sl, sl]

    def robust_chunk(dirn, r_ref, q_ref, v_ref, lf_ref, k_ref, o_ref, r0):
        rows = pl.ds(r0, c)
        g = lf_ref[rows, :]
        g_hi = g.astype(BF16)
        g_lo = (g - g_hi.astype(F32)).astype(BF16)
        rmat = r_ref[...]
        e_all = jnp.exp(jnp.dot(rmat, g_hi, preferred_element_type=F32)
                        + jnp.dot(rmat, g_lo, preferred_element_type=F32))
        base = levels * c
        for h in range(HEADS):
            hs = slice(h * HEAD_DIM, (h + 1) * HEAD_DIM)
            qh = q_ref[rows, hs].astype(F32)
            kh = k_ref[rows, hs].astype(F32)
            vh = v_ref[rows, hs]
            st = st_ref[dirn * HEADS + h]
            if compute_o:
                att = jnp.where(diag, lax.dot_general(qh.astype(BF16), kh.astype(BF16), dn_t,
                                                      preferred_element_type=F32), 0.0)
                for lev in range(levels):
                    e_l = e_all[lev * c:(lev + 1) * c, hs]
                    q_side = upper[lev] if dirn == 0 else jnp.logical_not(upper[lev])
                    qt = jnp.where(q_side, qh * e_l, 0.0).astype(BF16)
                    kt = jnp.where(q_side, 0.0, kh * e_l).astype(BF16)
                    a_l = lax.dot_general(qt, kt, dn_t, preferred_element_type=F32)
                    att = att + jnp.where(same_parent[lev], a_l, 0.0)
                qi = (qh * e_all[base:base + c, hs]).astype(BF16)
                o = lax.dot_general(qi, st.astype(BF16), dn_t, preferred_element_type=F32)
                o = o + jnp.dot(att.astype(BF16), vh, preferred_element_type=F32)
                o_ref[rows, hs] = o.astype(o_ref.dtype)
            kd = (kh * e_all[base + c:base + 2 * c, hs]).astype(BF16)
            e_tot = e_all[base + 2 * c:base + 2 * c + 1, hs]
            st_ref[dirn * HEADS + h] = st * e_tot + lax.dot_general(
                vh, kd, dn_tl, preferred_element_type=F32)

    dir_refs = ((rf_ref, qf_ref, vf_ref, lff_ref, kf_ref, of_ref),
                (rb_ref, qb_ref, vb_ref, lfb_ref, kb_ref, ob_ref))

    def chunk_rows(ci_):
        return pl.multiple_of(ci_ * c, c), pl.multiple_of((n_chunks - 1 - ci_) * c, c)

    if not compute_o:
        def state_body(ci_, carry):
            for dirn, r0 in enumerate(chunk_rows(ci_)):
                _, _, v_ref, lf_ref, k_ref, _ = dir_refs[dirn]
                state_only_chunk(dirn, lf_ref, k_ref, v_ref, r0)
            return carry

        lax.fori_loop(0, n_chunks, state_body, 0)
    else:
        step_id = pl.program_id(0) * pl.num_programs(1) + j
        cur = step_id % 2

        def prepare(lf_refs, slot):
            worst = None
            for dirn in range(2):
                bsum = prefix((trif_ref, trib_ref)[dirn][...], lf_refs[dirn][...])
                bsum_ref[slot, dirn] = bsum
                for ch in range(n_chunks):
                    m = bsum[ch * c + mrow[dirn]:ch * c + mrow[dirn] + 1, :]
                    tot = bsum[ch * c + trow[dirn]:ch * c + trow[dirn] + 1, :]
                    w = jnp.minimum(m, tot - m)
                    worst = w if worst is None else jnp.minimum(worst, w)
            flag_ref[slot] = (jnp.min(worst) >= -SCAN_FAST_LIMIT).astype(jnp.int32)

        @pl.when(step_id == 0)
        def _():
            prepare((lff_ref, lfb_ref), 0)

        fast_ok = flag_ref[cur] == 1

        @pl.when(fast_ok)
        def _():
            states = [st_ref[i] for i in range(2 * HEADS)]
            for step in range(n_chunks):
                for dirn in range(2):
                    ch = step if dirn == 0 else n_chunks - 1 - step
                    _, q_ref, v_ref, _, k_ref, o_ref = dir_refs[dirn]
                    rows = slice(ch * c, (ch + 1) * c)
                    m = bsum_ref[cur, dirn, ch * c + mrow[dirn]:ch * c + mrow[dirn] + 1, :]
                    tot = bsum_ref[cur, dirn, ch * c + trow[dirn]:ch * c + trow[dirn] + 1, :]
                    fast_chunk(dirn, q_ref, v_ref, k_ref, o_ref, rows, bsum_ref[cur, dirn, rows, :],
                               m, tot, states)
            for i in range(2 * HEADS):
                st_ref[i] = states[i]
            prepare((lff_next_ref, lfb_next_ref), 1 - cur)

        @pl.when(jnp.logical_not(fast_ok))
        def _():
            def robust_body(ci_, carry):
                for dirn, r0 in enumerate(chunk_rows(ci_)):
                    r_ref, q_ref, v_ref, lf_ref, k_ref, o_ref = dir_refs[dirn]
                    robust_chunk(dirn, r_ref, q_ref, v_ref, lf_ref, k_ref, o_ref, r0)
                return carry

            prepare((lff_next_ref, lfb_next_ref), 1 - cur)
            lax.fori_loop(0, n_chunks, robust_body, 0)

    @pl.when(j == pl.num_programs(1) - 1)
    def _():
        sfin_ref[0] = st_ref[...]


def _scan(q, v, lff, kf, lfb, kb, s0, *, row_off, seq_len, compute_o):
    batch = s0.shape[0]
    t = min(SCAN_BLOCK, seq_len)
    nt = seq_len // t
    off = row_off // t
    fwd = lambda b, j: (off + b * nt + j, 0)
    bwd = lambda b, j: (off + b * nt + (nt - 1 - j), 0)
    ofwd = lambda b, j: (b * nt + j, 0)
    obwd = lambda b, j: (b * nt + (nt - 1 - j), 0)
    blk = lambda im: pl.BlockSpec((t, D_GROUP), im)
    rf = jnp.asarray(_scan_tables(SCAN_CHUNK, False), BF16)
    rb = jnp.asarray(_scan_tables(SCAN_CHUNK, True), BF16)
    rspec = pl.BlockSpec(rf.shape, lambda b, j: (0, 0))
    sspec = pl.BlockSpec((1, 2 * HEADS, HEAD_DIM, HEAD_DIM), lambda b, j: (b, 0, 0, 0))
    pos_t = np.arange(t)
    same_chunk = (pos_t[:, None] // SCAN_CHUNK) == (pos_t[None, :] // SCAN_CHUNK)
    tri_f = jnp.asarray(same_chunk & (pos_t[None, :] <= pos_t[:, None]), BF16)
    tri_b = jnp.asarray(same_chunk & (pos_t[None, :] >= pos_t[:, None]), BF16)
    tspec = pl.BlockSpec((t, t), lambda b, j: (0, 0))
    in_specs = [rspec, rspec, blk(fwd), blk(fwd), blk(fwd), blk(fwd),
                blk(bwd), blk(bwd), blk(bwd), blk(bwd), sspec, tspec, tspec]
    args = [rf, rb, q, v, lff, kf, q, v, lfb, kb, s0, tri_f, tri_b]
    out_shape, out_specs = [], []
    scratch = [pltpu.VMEM((2 * HEADS, HEAD_DIM, HEAD_DIM), F32)]
    if compute_o:
        def next_step(b, j):
            nxt = jnp.minimum(b * nt + j + 1, batch * nt - 1)
            return nxt // nt, nxt % nt

        in_specs += [blk(lambda b, j: fwd(*next_step(b, j))), blk(lambda b, j: bwd(*next_step(b, j)))]
        args += [lff, lfb]
        out_shape += [jax.ShapeDtypeStruct((batch * seq_len, D_GROUP), BF16)] * 2
        out_specs += [blk(ofwd), blk(obwd)]
        scratch += [pltpu.VMEM((2, 2, t, D_GROUP), F32), pltpu.SMEM((2,), I32)]
    out_shape.append(jax.ShapeDtypeStruct(s0.shape, F32))
    out_specs.append(sspec)
    kern = functools.partial(_scan_kernel, n_chunks=t // SCAN_CHUNK, compute_o=compute_o)
    return pl.pallas_call(
        kern,
        out_shape=out_shape,
        grid=(batch, nt),
        in_specs=in_specs,
        out_specs=out_specs,
        scratch_shapes=scratch,
        compiler_params=_cparams(("arbitrary", "arbitrary"), VMEM_LIMIT),
        name="hgrn_scan",
    )(*args)


def _ln_silu(y, g, b):
    mu = jnp.mean(y, axis=-1, keepdims=True)
    yc = y - mu
    var = jnp.mean(yc * yc, axis=-1, keepdims=True)
    return _silu(yc * lax.rsqrt(var + EPS) * g + b)


def _tree_sum(terms):
    while len(terms) > 1:
        terms = [terms[i] + terms[i + 1] if i + 1 < len(terms) else terms[i]
                 for i in range(0, len(terms), 2)]
    return terms[0]


def _conv_rows_kernel(u_ref, w_ref, b_ref, lg_ref, lb_ref, y_ref, pad_ref, acc_ref, *, seq, tb):
    nseq = tb // seq
    stride = seq + CONV_GAP
    n_pad = pad_ref.shape[2]
    n_lt = D_GROUP // LANES
    for cl in range(n_lt):
        ls = slice(cl * LANES, (cl + 1) * LANES)
        pad_ref[0, cl] = jnp.zeros(pad_ref.shape[2:], F32)
        for s in range(nseq):
            pad_ref[0, cl, CONV_GAP + s * stride:CONV_GAP + s * stride + seq, :] = (
                u_ref[s * seq:(s + 1) * seq, ls].astype(F32))
        for p in range(1, SUBLANES):
            pad_ref[p, cl, 0:n_pad - SUBLANES, :] = pad_ref[0, cl, p:p + n_pad - SUBLANES, :]
    rc = 64
    group = CONV_ROW_GROUP
    assert seq % rc == 0 and tb % (group * rc) == 0
    if seq == rc:
        member_off = [a * stride for a in range(group)]
        group_base = lambda g: g * (group * stride)
    else:
        gps = seq // (group * rc)
        assert seq % (group * rc) == 0
        member_off = [a * rc for a in range(group)]
        group_base = lambda g: (g // gps) * stride + (g % gps) * (group * rc)

    def one_group(g, carry):
        pad0 = pl.multiple_of(CONV_GAP + group_base(g), SUBLANES)
        out0 = pl.multiple_of(g * (group * rc), group * rc)
        for cl in range(n_lt):
            ls = slice(cl * LANES, (cl + 1) * LANES)
            parts = [[None] * CONV_PARTIALS for _ in range(group)]
            for k in range(CONV_K):
                off = k - CONV_HALF
                wk = w_ref[k:k + 1, ls]
                for a in range(group):
                    src = pl.ds(pad0 + member_off[a] + (off // SUBLANES) * SUBLANES, rc)
                    term = wk * pad_ref[off % SUBLANES, cl, src, :]
                    prev = parts[a][k % CONV_PARTIALS]
                    parts[a][k % CONV_PARTIALS] = term if prev is None else prev + term
            for a in range(group):
                acc_ref[pl.ds(out0 + a * rc, rc), ls] = _tree_sum(parts[a])
        return carry

    lax.fori_loop(0, tb // (group * rc), one_group, 0)
    y = acc_ref[...] + b_ref[...]
    y_ref[...] = _ln_silu(y, lg_ref[...], lb_ref[...]).astype(y_ref.dtype)


def _conv_rows(u, w, b, lg, lb, *, seq, row_off, n_rows):
    tb = max(seq, ROW_TILE) if n_rows % max(seq, ROW_TILE) == 0 else seq
    nseq = tb // seq
    off = row_off // tb
    vec = pl.BlockSpec((1, D_GROUP), lambda i: (0, 0))
    in_specs = [pl.BlockSpec((tb, D_GROUP), lambda i: (off + i, 0)),
                pl.BlockSpec((CONV_K, D_GROUP), lambda i: (0, 0)), vec, vec, vec]
    args = [u, w, b, lg, lb]
    kern = functools.partial(_conv_rows_kernel, seq=seq, tb=tb)
    return pl.pallas_call(
        kern,
        out_shape=jax.ShapeDtypeStruct((n_rows, D_GROUP), BF16),
        grid=(n_rows // tb,),
        in_specs=in_specs,
        out_specs=pl.BlockSpec((tb, D_GROUP), lambda i: (i, 0)),
        scratch_shapes=[pltpu.VMEM((SUBLANES, D_GROUP // LANES, CONV_GAP + nseq * (seq + CONV_GAP), LANES), F32),
                        pltpu.VMEM((tb, D_GROUP), F32)],
        compiler_params=_cparams(("arbitrary",), VMEM_LIMIT),
        name="conv_rows",
    )(*args)


CONV_COL_TILE = 16


def _conv_cols_kernel(u_ref, w_ref, b_ref, lg_ref, lb_ref, y_ref, uf_ref, *, n_rows):
    n_lt = D_GROUP // LANES
    row_stride = GRID_W + SUBLANES
    for cl in range(n_lt):
        ls = slice(cl * LANES, (cl + 1) * LANES)
        for r in range(n_rows):
            uf_ref[cl, r * row_stride:r * row_stride + GRID_W, :] = (
                u_ref[r * GRID_W:(r + 1) * GRID_W, ls].astype(F32))
    bias = b_ref[...]
    lg = lg_ref[...]
    lb = lb_ref[...]

    def col_tile(wi, carry):
        w0 = pl.multiple_of(wi * CONV_COL_TILE, CONV_COL_TILE)
        for r in range(n_rows):
            tiles = []
            for cl in range(n_lt):
                ls = slice(cl * LANES, (cl + 1) * LANES)
                parts = [None] * CONV_PARTIALS
                for k in range(CONV_K):
                    rr = r + k - CONV_HALF
                    if 0 <= rr < n_rows:
                        src = pl.ds(pl.multiple_of(rr * row_stride + w0, SUBLANES), CONV_COL_TILE)
                        term = w_ref[k:k + 1, ls] * uf_ref[cl, src, :]
                        prev = parts[k % CONV_PARTIALS]
                        parts[k % CONV_PARTIALS] = term if prev is None else prev + term
                tiles.append(_tree_sum([p for p in parts if p is not None]))
            dst = pl.ds(pl.multiple_of(r * GRID_W + w0, CONV_COL_TILE), CONV_COL_TILE)
            y_ref[dst, :] = _ln_silu(jnp.concatenate(tiles, axis=1) + bias, lg, lb).astype(y_ref.dtype)
        return carry

    lax.fori_loop(0, GRID_W // CONV_COL_TILE, col_tile, 0)


def _conv_cols(u, w, b, lg, lb, *, batch, seq_len):
    vec = pl.BlockSpec((1, D_GROUP), lambda bi: (0, 0))
    blk = pl.BlockSpec((seq_len, D_GROUP), lambda bi: (bi, 0))
    return pl.pallas_call(
        functools.partial(_conv_cols_kernel, n_rows=seq_len // GRID_W),
        out_shape=jax.ShapeDtypeStruct((batch * seq_len, D_GROUP), BF16),
        grid=(batch,),
        in_specs=[blk, pl.BlockSpec((CONV_K, D_GROUP), lambda bi: (0, 0)), vec, vec, vec],
        out_specs=blk,
        scratch_shapes=[pltpu.VMEM((D_GROUP // LANES, (seq_len // GRID_W) * (GRID_W + SUBLANES), LANES), F32)],
        compiler_params=_cparams(("arbitrary",), VMEM_LIMIT),
        name="conv_cols",
    )(u, w, b, lg, lb)


def _outproj_kernel(*refs, n_x, n_mix, n_lat_tiles):
    x = _rows_from(refs[:n_x], n_lat_tiles)
    pos = n_x
    ycx, of, ob = (_rows_from(refs[pos + k * n_mix:pos + (k + 1) * n_mix], n_lat_tiles) for k in range(3))
    og_ref, hg_ref, w_ref, g1_ref = refs[pos + 3 * n_mix:pos + 3 * n_mix + 4]
    route_in = refs[pos + 3 * n_mix + 4:pos + 3 * n_mix + 10]
    o_ref = refs[pos + 3 * n_mix + 10]
    route_out = refs[pos + 3 * n_mix + 11:]
    o = of.astype(F32) + ob.astype(F32)
    og = og_ref[...].astype(F32)
    hg = hg_ref[...]
    acc = jnp.dot(ycx, w_ref[0, 0:D_GROUP, :], preferred_element_type=F32)
    for h in range(HEADS):
        hs = slice(h * HEAD_DIM, (h + 1) * HEAD_DIM)
        oh = o[:, hs]
        r = lax.rsqrt(jnp.mean(oh * oh, axis=-1, keepdims=True) + EPS)
        yh = ((oh * r * hg) * og[:, hs]).astype(BF16)
        acc = acc + jnp.dot(yh, w_ref[0, D_GROUP + h * HEAD_DIM:D_GROUP + (h + 1) * HEAD_DIM, :],
                            preferred_element_type=F32)
    x = x + g1_ref[0] * acc
    o_ref[...] = x
    _route(x, *route_in, *route_out)


def _outproj_route(xs, mix, og, hg, w_bf, layer, g1, g2n, sh2, sc2, rw_t, rb, *, n_rows, tm, tiles_per_seq):
    d = xs[0].shape[1]
    nb = g1.shape[0] - 1
    n_tt = n_rows // tm
    x_specs, n_lat_tiles = _row_specs(xs, tm)
    mix_specs, mix_args = [], []
    for arrays in mix:
        specs, nl = _row_specs(arrays, tm)
        assert len(arrays) == len(mix[0]) and (nl == 0 or n_lat_tiles in (0, nl))
        n_lat_tiles = max(n_lat_tiles, nl)
        mix_specs += specs
        mix_args += list(arrays)
    row = lambda width: pl.BlockSpec((tm, width), lambda i: (i, 0))
    mod_spec = pl.BlockSpec((1, 1, d), lambda i: (jnp.minimum(i // tiles_per_seq, nb), 0, 0))
    pair = pl.BlockSpec((2, tm), lambda i: (0, i))
    kern = functools.partial(_outproj_kernel, n_x=len(xs), n_mix=len(mix[0]), n_lat_tiles=n_lat_tiles)
    before = jnp.asarray(np.arange(tm)[:, None] < np.arange(tm)[None, :], BF16)
    return pl.pallas_call(
        kern,
        out_shape=[jax.ShapeDtypeStruct((n_rows, d), F32),
                   jax.ShapeDtypeStruct((n_rows, d), BF16),
                   jax.ShapeDtypeStruct((2, n_rows), I32),
                   jax.ShapeDtypeStruct((2, n_rows), I32),
                   jax.ShapeDtypeStruct((2, n_rows), F32),
                   jax.ShapeDtypeStruct((n_tt, N_EXPERTS, LANES), I32)],
        grid=(n_tt,),
        in_specs=x_specs + mix_specs + [row(D_GROUP),
                  pl.BlockSpec((1, HEAD_DIM), lambda i: (0, 0)),
                  pl.BlockSpec((1,) + w_bf.shape[1:], lambda i: (layer, 0, 0)),
                  mod_spec,
                  pl.BlockSpec((1, d), lambda i: (0, 0)), mod_spec, mod_spec,
                  pl.BlockSpec((N_EXPERTS, d), lambda i: (0, 0)),
                  pl.BlockSpec((N_EXPERTS, 1), lambda i: (0, 0)),
                  pl.BlockSpec((tm, tm), lambda i: (0, 0))],
        out_specs=[row(d), row(d), pair, pair, pair,
                   pl.BlockSpec((1, N_EXPERTS, LANES), lambda i: (i, 0, 0))],
        compiler_params=_cparams(("arbitrary",), VMEM_LIMIT),
        name="outproj_route",
    )(*xs, *mix_args, og, hg, w_bf, g1, g2n, sh2, sc2, rw_t, rb, before)


def _route(x, g_ref, sh_ref, sc_ref, rw_ref, rb_ref, before_ref, h_ref, e_ref, rank_ref, w_ref, cnt_ref):
    tm = x.shape[0]
    h = _modulate(x, g_ref[...], sh_ref[0], sc_ref[0])
    h_ref[...] = h.astype(h_ref.dtype)
    logits = lax.dot_general(rw_ref[...], h, (((1,), (1,)), ((), ())),
                             precision=lax.Precision.HIGHEST, preferred_element_type=F32)
    s = jax.nn.sigmoid(logits)
    sb = s + rb_ref[...]
    s_rows = [s[e:e + 1, :] for e in range(N_EXPERTS)]
    sb_rows = [sb[e:e + 1, :] for e in range(N_EXPERTS)]

    def group_score(g):
        v = sb_rows[g * EXPERTS_PER_GROUP:(g + 1) * EXPERTS_PER_GROUP]
        best = None
        for a in range(EXPERTS_PER_GROUP):
            for b in range(a + 1, EXPERTS_PER_GROUP):
                p = v[a] + v[b]
                best = p if best is None else jnp.maximum(best, p)
        return best

    cur = group_score(0)
    best_g = jnp.zeros(cur.shape, I32)
    for g in range(1, N_GROUPS):
        gs = group_score(g)
        upd = gs > cur
        best_g = jnp.where(upd, g, best_g)
        cur = jnp.where(upd, gs, cur)

    def pick(rows, jdx):
        out = rows[jdx]
        for g in range(1, N_GROUPS):
            out = jnp.where(best_g == g, rows[g * EXPERTS_PER_GROUP + jdx], out)
        return out

    vb = [pick(sb_rows, jdx) for jdx in range(EXPERTS_PER_GROUP)]
    vs = [pick(s_rows, jdx) for jdx in range(EXPERTS_PER_GROUP)]

    def first_argmax(vals):
        m = vals[0]
        for v in vals[1:]:
            m = jnp.maximum(m, v)
        idx = jnp.full(m.shape, EXPERTS_PER_GROUP - 1, I32)
        for jdx in range(EXPERTS_PER_GROUP - 2, -1, -1):
            idx = jnp.where(vals[jdx] == m, jdx, idx)
        return idx

    def take(vals, idx):
        out = vals[EXPERTS_PER_GROUP - 1]
        for jdx in range(EXPERTS_PER_GROUP - 2, -1, -1):
            out = jnp.where(idx == jdx, vals[jdx], out)
        return out

    i0 = first_argmax(vb)
    vb2 = [jnp.where(i0 == jdx, -jnp.inf, vb[jdx]) for jdx in range(EXPERTS_PER_GROUP)]
    i1 = first_argmax(vb2)
    s0 = take(vs, i0)
    s1 = take(vs, i1)
    tot = s0 + s1
    w_ref[0:1, :] = s0 / tot
    w_ref[1:2, :] = s1 / tot
    e0 = best_g * EXPERTS_PER_GROUP + i0
    e1 = best_g * EXPERTS_PER_GROUP + i1
    e_ref[0:1, :] = e0
    e_ref[1:2, :] = e1

    eid = lax.broadcasted_iota(I32, (N_EXPERTS, tm), 0)
    hit0 = eid == e0
    hit1 = eid == e1
    onehot = jnp.where(jnp.logical_or(hit0, hit1), 1.0, 0.0)
    rank = jnp.dot(onehot.astype(BF16), before_ref[...], preferred_element_type=F32)
    rank_ref[0:1, :] = jnp.sum(jnp.where(hit0, rank, 0.0), axis=0, keepdims=True).astype(I32)
    rank_ref[1:2, :] = jnp.sum(jnp.where(hit1, rank, 0.0), axis=0, keepdims=True).astype(I32)
    cnt = jnp.sum(onehot, axis=1, keepdims=True).astype(I32)
    cnt_ref[0] = jnp.broadcast_to(cnt, (N_EXPERTS, LANES))


def _tables_kernel(cnt_ref, seg_start_ref, seg_len_ref, te_ref, misc_ref, *, n_tt, n_et, tm):
    def expert(e, row0):
        def tile(i, pos):
            n = cnt_ref[i * N_EXPERTS + e]
            n8 = ((n + SUBLANES - 1) // SUBLANES) * SUBLANES
            seg_start_ref[i * N_EXPERTS + e] = pos
            seg_len_ref[i * N_EXPERTS + e] = n8
            return pos + n8

        end = lax.fori_loop(0, n_tt, tile, row0)
        padded_end = row0 + ((end - row0 + tm - 1) // tm) * tm
        misc_ref[1 + e] = end
        misc_ref[1 + N_EXPERTS + e] = padded_end - end

        def mark(j, carry):
            te_ref[j] = e
            return carry

        lax.fori_loop(row0 // tm, padded_end // tm, mark, 0)
        return padded_end

    total = row0 = 0
    for e in range(N_EXPERTS):
        row0 = expert(e, row0)
    total = row0
    n_used = total // tm
    misc_ref[0] = n_used
    last = te_ref[jnp.maximum(n_used - 1, 0)]

    def fill(j, carry):
        te_ref[j] = last
        return carry

    lax.fori_loop(n_used, n_et, fill, 0)


def _tables(cnt, *, n_tt, n_et, tm):
    smem = pl.BlockSpec(memory_space=pltpu.SMEM)
    return pl.pallas_call(
        functools.partial(_tables_kernel, n_tt=n_tt, n_et=n_et, tm=tm),
        out_shape=[jax.ShapeDtypeStruct((n_tt * N_EXPERTS,), I32),
                   jax.ShapeDtypeStruct((n_tt * N_EXPERTS,), I32),
                   jax.ShapeDtypeStruct((n_et,), I32),
                   jax.ShapeDtypeStruct((1 + 2 * N_EXPERTS,), I32)],
        in_specs=[smem],
        out_specs=[smem, smem, smem, smem],
        name="route_tables",
    )(cnt)


def _piece_sizes(max_rows):
    sizes, s = [], SUBLANES
    while s <= max_rows:
        sizes.append(s)
        s *= 2
    return sizes[::-1]


def _segment_copies(src_ref, src_row, dst_ref, dst_row, n_rows, sem, sizes, start):
    aligned = lambda r: r if isinstance(r, int) else pl.multiple_of(r, SUBLANES)
    for sz in sizes:
        src = pl.ds(aligned(src_row), sz)
        dst = pl.ds(aligned(dst_row), sz)

        @pl.when((n_rows & sz) != 0)
        def _(src=src, dst=dst):
            cp = pltpu.make_async_copy(src_ref.at[src, :], dst_ref.at[dst, :], sem)
            if start:
                cp.start()
            else:
                cp.wait()

        src_row = src_row + (n_rows & sz)
        dst_row = dst_row + (n_rows & sz)


def _local_slots(seg_len_ref, tile, e_ref, rank_ref):
    offs, lo = [], 0
    for e in range(N_EXPERTS):
        offs.append(lo)
        lo = lo + seg_len_ref[tile * N_EXPERTS + e]
    e01 = e_ref[...]
    slot = rank_ref[...]
    for e in range(N_EXPERTS):
        slot = slot + jnp.where(e01 == e, offs[e], 0)
    return slot, offs


def _compact_rows(tm):
    return ((2 * tm + N_EXPERTS * (SUBLANES - 1) + LANES - 1) // LANES) * LANES


def _dispatch_kernel(seg_start_ref, seg_len_ref, misc_ref, h_ref, e_ref, rank_ref, xs_ref,
                     xc_ref, zero_ref, sems, *, sizes):
    tm = h_ref.shape[0]
    rc = xc_ref.shape[1]
    i = pl.program_id(0)
    buf = i % 2
    slot, _ = _local_slots(seg_len_ref, i, e_ref, rank_ref)
    rid = lax.broadcasted_iota(I32, (rc, tm), 0)
    sel = jnp.logical_or(rid == slot[0:1, :], rid == slot[1:2, :])
    perm = jnp.where(sel, 1.0, 0.0).astype(BF16)
    xc_ref[buf] = jnp.dot(perm, h_ref[...], preferred_element_type=F32)

    def segments(tile, b, start):
        lo = 0
        for e in range(N_EXPERTS):
            n8 = seg_len_ref[tile * N_EXPERTS + e]
            _segment_copies(xc_ref.at[b], lo, xs_ref, seg_start_ref[tile * N_EXPERTS + e], n8,
                            sems.at[b], sizes, start)
            lo = lo + n8

    @pl.when(i > 0)
    def _():
        segments(i - 1, 1 - buf, False)

    segments(i, buf, True)

    @pl.when(i == pl.num_programs(0) - 1)
    def _():
        segments(i, buf, False)
        zero_ref[...] = jnp.zeros(zero_ref.shape, F32)
        for start in (True, False):
            for e in range(N_EXPERTS):
                _segment_copies(zero_ref, 0, xs_ref, misc_ref[1 + e], misc_ref[1 + N_EXPERTS + e],
                                sems.at[0], sizes, start)

        def clear_tile(j, carry):
            cp = pltpu.make_async_copy(zero_ref, xs_ref.at[pl.ds(pl.multiple_of(j * tm, tm), tm), :],
                                       sems.at[0])
            cp.start()
            cp.wait()
            return carry

        lax.fori_loop(misc_ref[0], xs_ref.shape[0] // tm, clear_tile, 0)


def _dispatch(seg_start, seg_len, misc, h, e01, rank01, *, n_sorted, tm):
    n, d = h.shape
    pair = pl.BlockSpec((2, tm), lambda i, *_: (0, i))
    grid_spec = pltpu.PrefetchScalarGridSpec(
        num_scalar_prefetch=3,
        grid=(n // tm,),
        in_specs=[pl.BlockSpec((tm, d), lambda i, *_: (i, 0)), pair, pair],
        out_specs=pl.BlockSpec(memory_space=pl.ANY),
        scratch_shapes=[pltpu.VMEM((2, _compact_rows(tm), d), F32), pltpu.VMEM((tm, d), F32),
                        pltpu.SemaphoreType.DMA((2,))],
    )
    return pl.pallas_call(
        functools.partial(_dispatch_kernel, sizes=_piece_sizes(tm)),
        out_shape=jax.ShapeDtypeStruct((n_sorted, d), F32),
        grid_spec=grid_spec,
        compiler_params=_cparams(("arbitrary",), VMEM_LIMIT),
        name="dispatch",
    )(seg_start, seg_len, misc, h, e01, rank01)


def _moe_kernel(te_ref, misc_ref, xs_ref, wg_hbm, wu_hbm, wd_hbm, ys_ref,
                wg_buf, wu_buf, wd_buf, wgb_ref, wub_ref, wdb_ref, run_ref, sems, *, layer, tm):
    i = pl.program_id(0)
    e = te_ref[i]
    prev = te_ref[jnp.maximum(i - 1, 0)]
    n_used = misc_ref[0]

    def fetch(expert, slot):
        return [pltpu.make_async_copy(w.at[layer, expert], buf.at[slot], sems.at[slot, k])
                for k, (w, buf) in enumerate(((wg_hbm, wg_buf), (wu_hbm, wu_buf), (wd_hbm, wd_buf)))]

    @pl.when(i == 0)
    def _():
        run_ref[0] = 0
        for cp in fetch(e, 0):
            cp.start()

    @pl.when(jnp.logical_or(i == 0, e != prev))
    def _():
        slot = run_ref[0] % 2
        for cp in fetch(e, slot):
            cp.wait()
        nxt = (misc_ref[1 + e] + misc_ref[1 + N_EXPERTS + e]) // tm

        @pl.when(nxt < n_used)
        def _():
            for cp in fetch(te_ref[nxt], 1 - slot):
                cp.start()

        wgb_ref[...] = wg_buf[slot].astype(BF16)
        wub_ref[...] = wu_buf[slot].astype(BF16)
        wdb_ref[...] = wd_buf[slot].astype(BF16)
        run_ref[0] = run_ref[0] + 1

    @pl.when(i < misc_ref[0])
    def _():
        x = xs_ref[...].astype(BF16)
        gate = jnp.dot(x, wgb_ref[...], preferred_element_type=F32)
        up = jnp.dot(x, wub_ref[...], preferred_element_type=F32)
        act = (_silu(gate) * up).astype(BF16)
        ys_ref[...] = jnp.dot(act, wdb_ref[...], preferred_element_type=F32)

    @pl.when(i >= misc_ref[0])
    def _():
        ys_ref[...] = jnp.zeros(ys_ref.shape, F32)


def _moe(tile_expert, misc, xs, wg, wu, wd, layer, tm):
    n_sorted, d = xs.shape
    de = wg.shape[-1]
    xmap = lambda i, te, misc_ref: (jnp.minimum(i, jnp.maximum(misc_ref[0] - 1, 0)), 0)
    hbm = pl.BlockSpec(memory_space=pl.ANY)
    grid_spec = pltpu.PrefetchScalarGridSpec(
        num_scalar_prefetch=2,
        grid=(n_sorted // tm,),
        in_specs=[pl.BlockSpec((tm, d), xmap), hbm, hbm, hbm],
        out_specs=pl.BlockSpec((tm, d), lambda i, te, misc_ref: (i, 0)),
        scratch_shapes=[pltpu.VMEM((2, d, de), F32), pltpu.VMEM((2, d, de), F32),
                        pltpu.VMEM((2, de, d), F32),
                        pltpu.VMEM((d, de), BF16), pltpu.VMEM((d, de), BF16),
                        pltpu.VMEM((de, d), BF16),
                        pltpu.SMEM((1,), I32), pltpu.SemaphoreType.DMA((2, 3))],
    )
    return pl.pallas_call(
        functools.partial(_moe_kernel, layer=layer, tm=tm),
        out_shape=jax.ShapeDtypeStruct((n_sorted, d), F32),
        grid_spec=grid_spec,
        compiler_params=_cparams(("arbitrary",), VMEM_LIMIT),
        name="moe_experts",
    )(tile_expert, misc, xs, wg, wu, wd)


def _combine_kernel(seg_start_ref, seg_len_ref, x_ref, g2_ref, e_ref, rank_ref, w_ref, fg_ref, ys_ref,
                    o_ref, yc_ref, sems, *, sizes, final_norm):
    tm = x_ref.shape[0]
    rc = yc_ref.shape[1]
    i = pl.program_id(0)
    buf = i % 2

    def segments(tile, b, start):
        lo = 0
        for e in range(N_EXPERTS):
            n8 = seg_len_ref[tile * N_EXPERTS + e]
            _segment_copies(ys_ref, seg_start_ref[tile * N_EXPERTS + e], yc_ref.at[b], lo, n8,
                            sems.at[b], sizes, start)
            lo = lo + n8

    @pl.when(i == 0)
    def _():
        yc_ref[...] = jnp.zeros(yc_ref.shape, F32)
        segments(0, 0, True)

    @pl.when(i + 1 < pl.num_programs(0))
    def _():
        segments(i + 1, 1 - buf, True)

    segments(i, buf, False)
    slot, _ = _local_slots(seg_len_ref, i, e_ref, rank_ref)
    rid = lax.broadcasted_iota(I32, (rc, tm), 0)
    w = w_ref[...]
    pw = (jnp.where(rid == slot[0:1, :], w[0:1, :], 0.0)
          + jnp.where(rid == slot[1:2, :], w[1:2, :], 0.0))
    dn = (((0,), (0,)), ((), ()))
    out = lax.dot_general(pw.astype(BF16), yc_ref[buf].astype(BF16), dn, preferred_element_type=F32)
    x = x_ref[...] + g2_ref[0] * out
    if final_norm:
        x = x * lax.rsqrt(jnp.mean(x * x, axis=-1, keepdims=True) + EPS) * fg_ref[...]
    o_ref[...] = x


def _combine(seg_start, seg_len, xc, g2, e01, rank01, w01, fg, ys, *, n_rows, tm, tiles_per_seq,
             final_norm):
    d = xc.shape[1]
    nb = g2.shape[0] - 1
    pair = pl.BlockSpec((2, tm), lambda i, *_: (0, i))
    grid_spec = pltpu.PrefetchScalarGridSpec(
        num_scalar_prefetch=2,
        grid=(n_rows // tm,),
        in_specs=[pl.BlockSpec((tm, d), lambda i, *_: (i, 0)),
                  pl.BlockSpec((1, 1, d), lambda i, *_: (jnp.minimum(i // tiles_per_seq, nb), 0, 0)),
                  pair, pair, pair,
                  pl.BlockSpec((1, d), lambda i, *_: (0, 0)),
                  pl.BlockSpec(memory_space=pl.ANY)],
        out_specs=pl.BlockSpec((tm, d), lambda i, *_: (i, 0)),
        scratch_shapes=[pltpu.VMEM((2, _compact_rows(tm), d), F32), pltpu.SemaphoreType.DMA((2,))],
    )
    return pl.pallas_call(
        functools.partial(_combine_kernel, sizes=_piece_sizes(tm), final_norm=final_norm),
        out_shape=jax.ShapeDtypeStruct((n_rows, d), F32),
        grid_spec=grid_spec,
        compiler_params=_cparams(("arbitrary",), VMEM_LIMIT),
        name="combine",
    )(seg_start, seg_len, xc, g2, e01, rank01, w01, fg, ys)


def _lower_bounds(lb_param):
    p = jax.nn.softmax(lb_param.astype(F32), axis=0)
    return jnp.cumsum(p, axis=0) - p[0]


def kernel(x, c, ctx, c_ctx, w_ada, b_ada, norm1_g, norm2_g, w_in, conv_w, conv_b, conv_ln_g,
           conv_ln_b, lb_fwd, lb_bwd, hgrn_norm_g, w_out, router_w, router_bias, w_gate, w_up,
           w_down, final_norm_g):
    bn, seq, d = x.shape
    ctx_len = ctx.shape[1]
    depth = w_ada.shape[0]
    n = bn * seq
    nc = bn * ctx_len
    tm = min(ROW_TILE, seq, nc)
    assert seq % tm == 0 and nc % tm == 0 and seq % GRID_W == 0
    tps = seq // tm

    pad_rows = (-(bn + 1)) % 8
    cc = jnp.concatenate([c, c_ctx[None, :], jnp.zeros((pad_rows, d), F32)], axis=0)
    mod = _ada(cc, w_ada, b_ada)[:, :bn + 1]
    mod = mod.reshape(depth, bn + 1, 6, 1, d)

    lbs_f = _lower_bounds(lb_fwd)
    lbs_b = _lower_bounds(lb_bwd)
    w_in_bf = w_in.astype(BF16)
    w_out_bf = w_out.astype(BF16)
    rw_t = router_w.T.astype(F32)
    rb = router_bias.reshape(N_EXPERTS, 1).astype(F32)
    hg = hgrn_norm_g.astype(F32)
    fg = final_norm_g.reshape(1, d).astype(F32)

    rows_in = [x.reshape(n, d), ctx.reshape(nc, d)]
    s_zero = jnp.zeros((bn, 2 * HEADS, HEAD_DIM, HEAD_DIM), F32)

    for l in range(depth):
        last = l == depth - 1
        sh1, sc1, g1, sh2, sc2, g2 = (mod[l, :, k] for k in range(6))
        row = lambda a: a.reshape(1, -1).astype(F32)

        u, q, iv, lff, kf, lfb, kb, og = _inproj(
            rows_in, row(norm1_g[l]), sh1, sc1, w_in_bf, l, row(lbs_f[l]), row(lbs_b[l]), tm, tps,
            n_full_rows=n if last else None)

        cw, cb = conv_w[l].astype(F32), row(conv_b[l])
        clg, clb = row(conv_ln_g[l]), row(conv_ln_b[l])
        scan = functools.partial(_scan, q, iv, lff, kf, lfb, kb)
        if last:
            (s_ctx,) = scan(s_zero, row_off=n, seq_len=ctx_len, compute_o=False)
            mix_ctx = [[], [], []]
            n_rows = n
        else:
            of_c, ob_c, s_ctx = scan(s_zero, row_off=n, seq_len=ctx_len, compute_o=True)
            ycx_c = _conv_rows(u, cw, cb, clg, clb, seq=ctx_len, row_off=n, n_rows=nc)
            mix_ctx = [[ycx_c], [of_c], [ob_c]]
            n_rows = n + nc
        of, ob, _ = scan(s_ctx, row_off=0, seq_len=seq, compute_o=True)
        if l % 2 == 0:
            ycx = _conv_rows(u, cw, cb, clg, clb, seq=GRID_W, row_off=0, n_rows=n)
        else:
            ycx = _conv_cols(u, cw, cb, clg, clb, batch=bn, seq_len=seq)
        mix = [[a] + c_ for a, c_ in zip((ycx, of, ob), mix_ctx)]

        res_rows = rows_in if (len(rows_in) == 1 or not last) else [rows_in[0]]
        xc, h2, e01, rank01, w01, cnt = _outproj_route(
            res_rows, mix, og, hg[l].reshape(1, HEAD_DIM), w_out_bf, l, g1,
            row(norm2_g[l]), sh2, sc2, rw_t, rb, n_rows=n_rows, tm=tm, tiles_per_seq=tps)
        n_tt = n_rows // tm
        n_et = (2 * n_rows + n_tt * N_EXPERTS * (SUBLANES - 1) + tm - 1) // tm + N_EXPERTS
        seg_start, seg_len, te, misc = _tables(cnt[:, :, 0].reshape(-1), n_tt=n_tt, n_et=n_et, tm=tm)
        xs = _dispatch(seg_start, seg_len, misc, h2, e01, rank01, n_sorted=n_et * tm, tm=tm)
        ys = _moe(te, misc, xs, w_gate, w_up, w_down, l, tm)
        xc = _combine(seg_start, seg_len, xc, g2, e01, rank01, w01, fg, ys, n_rows=n_rows, tm=tm,
                      tiles_per_seq=tps, final_norm=last)
        rows_in = [xc]

    return xc[:n].reshape(bn, seq, d)
```

```python
import functools

import numpy as np
import jax
import jax.numpy as jnp
from jax import lax
from jax.experimental import pallas as pl
from jax.experimental.pallas import tpu as pltpu

F32 = jnp.float32
BF16 = jnp.bfloat16
I32 = jnp.int32

EPS = 1e-6
GRID_W = 64
HEADS = 4
HEAD_DIM = 128
D_GROUP = HEADS * HEAD_DIM
N_IN_GROUPS = 7
CONV_K = 31
CONV_HALF = CONV_K // 2
CONV_GAP = 16
CONV_ROW_GROUP = 1
CONV_PARTIALS = 2
N_EXPERTS = 16
N_GROUPS = 4
EXPERTS_PER_GROUP = N_EXPERTS // N_GROUPS
SCAN_CHUNK = 64
SCAN_BLOCK = 256
ROW_TILE = 512
SCAN_FAST_LIMIT = 60.0
LANES = 128
SUBLANES = 8
VMEM_LIMIT = 56 * 1024 * 1024


def _cparams(sem, vmem=None):
    return pltpu.CompilerParams(dimension_semantics=sem, vmem_limit_bytes=vmem)


def _sigmoid(x):
    return 0.5 * jnp.tanh(0.5 * x) + 0.5


def _silu(x):
    return x * _sigmoid(x)


def _ada_kernel(c_ref, w_ref, b_ref, o_ref):
    a = _silu(c_ref[...]).astype(BF16)
    w = w_ref[0].astype(BF16)
    o_ref[0] = jnp.dot(a, w, preferred_element_type=F32) + b_ref[0]


def _ada(cc, w_ada, b_ada, tn=1536):
    depth, d, d6 = w_ada.shape
    rows = cc.shape[0]
    return pl.pallas_call(
        _ada_kernel,
        out_shape=jax.ShapeDtypeStruct((depth, rows, d6), F32),
        grid=(depth, d6 // tn),
        in_specs=[
            pl.BlockSpec((rows, d), lambda l, j: (0, 0)),
            pl.BlockSpec((1, d, tn), lambda l, j: (l, 0, j)),
            pl.BlockSpec((1, 1, tn), lambda l, j: (l, 0, j)),
        ],
        out_specs=pl.BlockSpec((1, rows, tn), lambda l, j: (l, 0, j)),
        compiler_params=_cparams(("arbitrary", "arbitrary"), VMEM_LIMIT),
        name="ada",
    )(cc, w_ada, b_ada.reshape(depth, 1, d6))


def _modulate(x, g, shift, scale):
    r = lax.rsqrt(jnp.mean(x * x, axis=-1, keepdims=True) + EPS)
    return (x * r * g) * (1.0 + scale) + shift


def _rows_from(refs, n_lat_tiles):
    if len(refs) == 1:
        return refs[0][...]
    return jnp.where(pl.program_id(0) >= n_lat_tiles, refs[1][...], refs[0][...])


def _row_specs(arrays, tm):
    width = arrays[0].shape[1]
    if len(arrays) == 1:
        return [pl.BlockSpec((tm, width), lambda i, *_: (i, 0))], 0
    n_lat_tiles = arrays[0].shape[0] // tm
    return [pl.BlockSpec((tm, width), lambda i, *_: (jnp.minimum(i, n_lat_tiles - 1), 0)),
            pl.BlockSpec((tm, width), lambda i, *_: (jnp.maximum(i - n_lat_tiles, 0), 0))], n_lat_tiles


def _inproj_kernel(*refs, n_src, n_lat_tiles):
    x = _rows_from(refs[:n_src], n_lat_tiles)
    (g_ref, sh_ref, sc_ref, w_ref, lbf_ref, lbb_ref,
     u_ref, q_ref, i_ref, lff_ref, kf_ref, lfb_ref, kb_ref, og_ref) = refs[n_src:]
    hb = _modulate(x, g_ref[...], sh_ref[0], sc_ref[0]).astype(BF16)

    def proj(j):
        return jnp.dot(hb, w_ref[0, :, j * D_GROUP:(j + 1) * D_GROUP], preferred_element_type=F32)

    u_ref[...] = (proj(0) * _sigmoid(proj(1))).astype(BF16)
    q_ref[...] = _silu(proj(2)).astype(BF16)
    i_ref[...] = proj(3).astype(BF16)
    for j, lb_ref, lf_ref, k_ref in ((4, lbf_ref, lff_ref, kf_ref), (5, lbb_ref, lfb_ref, kb_ref)):
        lb = lb_ref[...]
        f = lb + (1.0 - lb) * jax.nn.sigmoid(proj(j))
        lf_ref[...] = jnp.log(f)
        k_ref[...] = (1.0 - f).astype(BF16)
    og_ref[...] = _silu(proj(6)).astype(BF16)


def _inproj(xs, g, sh, sc, w_bf, layer, lbf, lbb, tm, tiles_per_seq):
    n = sum(a.shape[0] for a in xs)
    d = xs[0].shape[1]
    x_specs, n_lat_tiles = _row_specs(xs, tm)
    nb = sh.shape[0] - 1
    mod_spec = pl.BlockSpec((1, 1, d), lambda i: (jnp.minimum(i // tiles_per_seq, nb), 0, 0))
    row_spec = pl.BlockSpec((tm, D_GROUP), lambda i: (i, 0))
    vec_spec = pl.BlockSpec((1, D_GROUP), lambda i: (0, 0))
    outs = [jax.ShapeDtypeStruct((n, D_GROUP), dt)
            for dt in (BF16, BF16, BF16, F32, BF16, F32, BF16, BF16)]
    return pl.pallas_call(
        functools.partial(_inproj_kernel, n_src=len(xs), n_lat_tiles=n_lat_tiles),
        out_shape=outs,
        grid=(n // tm,),
        in_specs=x_specs + [
            pl.BlockSpec((1, d), lambda i: (0, 0)),
            mod_spec, mod_spec,
            pl.BlockSpec((1, d, N_IN_GROUPS * D_GROUP), lambda i: (layer, 0, 0)),
            vec_spec, vec_spec,
        ],
        out_specs=[row_spec] * 8,
        compiler_params=_cparams(("arbitrary",), VMEM_LIMIT),
        name="inproj",
    )(*xs, g, sh, sc, w_bf, lbf, lbb)


def _scan_tables(c, backward):
    levels = int(np.log2(c))
    r_all = np.zeros((levels * c + 2 * c + 8, c), np.float32)
    for lev in range(levels):
        h = 1 << lev
        for r in range(c):
            bd = (r // (2 * h)) * 2 * h + h
            if not backward:
                if r >= bd:
                    r_all[lev * c + r, bd:r + 1] = 1.0
                else:
                    r_all[lev * c + r, r + 1:bd] = 1.0
            else:
                if r < bd:
                    r_all[lev * c + r, r:bd] = 1.0
                else:
                    r_all[lev * c + r, bd:r] = 1.0
    base = levels * c
    for r in range(c):
        if not backward:
            r_all[base + r, :r + 1] = 1.0
            r_all[base + c + r, r + 1:] = 1.0
        else:
            r_all[base + r, r:] = 1.0
            r_all[base + c + r, :r] = 1.0
    r_all[base + 2 * c:, :] = 1.0
    return r_all


def _scan_kernel(*refs, n_chunks, compute_o):
    c = SCAN_CHUNK
    levels = int(np.log2(c))
    (rf_ref, rb_ref, qf_ref, vf_ref, lff_ref, kf_ref,
     qb_ref, vb_ref, lfb_ref, kb_ref, s0_ref, trif_ref, trib_ref) = refs[:13]
    pos = 13
    if compute_o:
        lff_next_ref, lfb_next_ref = refs[pos:pos + 2]
        of_ref, ob_ref, sfin_ref, st_ref, bsum_ref, flag_ref = refs[pos + 2:pos + 8]
    else:
        sfin_ref, st_ref = refs[pos:pos + 2]
        of_ref = ob_ref = None
    j = pl.program_id(1)

    @pl.when(j == 0)
    def _():
        st_ref[...] = s0_ref[0]

    row = lax.broadcasted_iota(jnp.int32, (c, 1), 0)
    ri = lax.broadcasted_iota(jnp.int32, (c, c), 0)
    ci = lax.broadcasted_iota(jnp.int32, (c, c), 1)
    upper = [(row & (2 * (1 << lev) - 1)) >= (1 << lev) for lev in range(levels)]
    same_parent = [(ri >> (lev + 1)) == (ci >> (lev + 1)) for lev in range(levels)]
    diag = ri == ci
    dn_t = (((1,), (1,)), ((), ()))
    dn_tl = (((0,), (0,)), ((), ()))

    half = c // 2
    tri = [jnp.where(ci <= ri, 1.0, 0.0).astype(BF16), jnp.where(ci >= ri, 1.0, 0.0).astype(BF16)]
    first_half = [row < half, row >= half]
    re = lax.broadcasted_iota(jnp.int32, (c, 4 * c), 0)
    ce = lax.broadcasted_iota(jnp.int32, (c, 4 * c), 1)
    cs = ce & (c - 1)
    same_half = (re >= half) == (cs >= half)
    own = (ce & (2 * c - 1)) < c
    mask_ext = [
        (own & same_half & (cs <= re)) | (jnp.logical_not(own) & (re >= half) & (cs < half)),
        (own & same_half & (cs >= re)) | (jnp.logical_not(own) & (re < half) & (cs >= half)),
    ]

    mrow, trow = [half - 1, half], [c - 1, 0]

    def prefix(tri_mat, g):
        g_hi = g.astype(BF16)
        g_lo = (g - g_hi.astype(F32)).astype(BF16)
        return (jnp.dot(tri_mat, g_hi, preferred_element_type=F32)
                + jnp.dot(tri_mat, g_lo, preferred_element_type=F32))

    def next_state(st, hs, kh, vh, b, tot):
        kd = (kh * jnp.exp(tot[:, hs] - b[:, hs])).astype(BF16)
        return st * jnp.exp(tot[:, hs]) + lax.dot_general(vh, kd, dn_tl, preferred_element_type=F32)

    def state_only_chunk(dirn, lf_ref, k_ref, v_ref, r0):
        rows = pl.ds(r0, c)
        b = prefix(tri[dirn], lf_ref[rows, :])
        tot = b[trow[dirn]:trow[dirn] + 1, :]
        for h in range(HEADS):
            hs = slice(h * HEAD_DIM, (h + 1) * HEAD_DIM)
            st_ref[dirn * HEADS + h] = next_state(st_ref[dirn * HEADS + h], hs,
                                                  k_ref[rows, hs].astype(F32), v_ref[rows, hs], b, tot)

    def fast_chunk(dirn, q_ref, v_ref, k_ref, o_ref, rows, b, m, tot, states):
        fh = first_half[dirn]
        hd = HEAD_DIM

        def block_diag(a, bb):
            za = jnp.zeros((a.shape[0], bb.shape[1]), a.dtype)
            zb = jnp.zeros((bb.shape[0], a.shape[1]), a.dtype)
            return jnp.concatenate([jnp.concatenate([a, za], axis=1),
                                    jnp.concatenate([zb, bb], axis=1)], axis=0)

        for hp in range(HEADS // 2):
            ps = slice(2 * hp * hd, 2 * (hp + 1) * hd)
            bp, mp, tp = b[:, ps], m[:, ps], tot[:, ps]
            cdec = bp - jnp.where(fh, 0.0, mp)
            e_own = jnp.exp(cdec).astype(BF16)
            e_own_inv = jnp.exp(-cdec).astype(BF16)
            e_cross = jnp.where(fh, jnp.exp(jnp.minimum(mp - bp, 0.0)), 0.0).astype(BF16)
            e_inc = jnp.exp(bp).astype(BF16)
            e_dec = jnp.exp(tp - bp).astype(BF16)
            qp, kp, vp = q_ref[rows, ps], k_ref[rows, ps], v_ref[rows, ps]
            q1, qi = qp * e_own, qp * e_inc
            k1, k2, kd = kp * e_own_inv, kp * e_cross, kp * e_dec
            kcat = [jnp.concatenate([k1[:, s_], k2[:, s_]], axis=0) for s_ in (slice(0, hd), slice(hd, 2 * hd))]
            sc = lax.dot_general(q1, block_diag(*kcat), dn_t, preferred_element_type=F32)
            p = jnp.where(mask_ext[dirn], sc, 0.0).astype(BF16)
            v2 = [jnp.concatenate([vp[:, s_], vp[:, s_]], axis=0) for s_ in (slice(0, hd), slice(hd, 2 * hd))]
            st = [states[dirn * HEADS + 2 * hp + a] for a in range(2)]
            o = (jnp.dot(p, block_diag(*v2), preferred_element_type=F32)
                 + lax.dot_general(qi, block_diag(st[0].astype(BF16), st[1].astype(BF16)), dn_t,
                                   preferred_element_type=F32))
            o_ref[rows, ps] = o.astype(o_ref.dtype)
            upd = lax.dot_general(vp, kd, dn_tl, preferred_element_type=F32)
            e_tot = jnp.exp(tp)
            for a in range(2):
                sl = slice(a * hd, (a + 1) * hd)
                states[dirn * HEADS + 2 * hp + a] = st[a] * e_tot[:, sl] + upd[sl, sl]

    def robust_chunk(dirn, r_ref, q_ref, v_ref, lf_ref, k_ref, o_ref, r0):
        rows = pl.ds(r0, c)
        g = lf_ref[rows, :]
        g_hi = g.astype(BF16)
        g_lo = (g - g_hi.astype(F32)).astype(BF16)
        rmat = r_ref[...]
        e_all = jnp.exp(jnp.dot(rmat, g_hi, preferred_element_type=F32)
                        + jnp.dot(rmat, g_lo, preferred_element_type=F32))
        base = levels * c
        for h in range(HEADS):
            hs = slice(h * HEAD_DIM, (h + 1) * HEAD_DIM)
            qh = q_ref[rows, hs].astype(F32)
            kh = k_ref[rows, hs].astype(F32)
            vh = v_ref[rows, hs]
            st = st_ref[dirn * HEADS + h]
            if compute_o:
                att = jnp.where(diag, lax.dot_general(qh.astype(BF16), kh.astype(BF16), dn_t,
                                                      preferred_element_type=F32), 0.0)
                for lev in range(levels):
                    e_l = e_all[lev * c:(lev + 1) * c, hs]
                    q_side = upper[lev] if dirn == 0 else jnp.logical_not(upper[lev])
                    qt = jnp.where(q_side, qh * e_l, 0.0).astype(BF16)
                    kt = jnp.where(q_side, 0.0, kh * e_l).astype(BF16)
                    a_l = lax.dot_general(qt, kt, dn_t, preferred_element_type=F32)
                    att = att + jnp.where(same_parent[lev], a_l, 0.0)
                qi = (qh * e_all[base:base + c, hs]).astype(BF16)
                o = lax.dot_general(qi, st.astype(BF16), dn_t, preferred_element_type=F32)
                o = o + jnp.dot(att.astype(BF16), vh, preferred_element_type=F32)
                o_ref[rows, hs] = o.astype(o_ref.dtype)
            kd = (kh * e_all[base + c:base + 2 * c, hs]).astype(BF16)
            e_tot = e_all[base + 2 * c:base + 2 * c + 1, hs]
            st_ref[dirn * HEADS + h] = st * e_tot + lax.dot_general(
                vh, kd, dn_tl, preferred_element_type=F32)

    dir_refs = ((rf_ref, qf_ref, vf_ref, lff_ref, kf_ref, of_ref),
                (rb_ref, qb_ref, vb_ref, lfb_ref, kb_ref, ob_ref))

    def chunk_rows(ci_):
        return pl.multiple_of(ci_ * c, c), pl.multiple_of((n_chunks - 1 - ci_) * c, c)

    if not compute_o:
        def state_body(ci_, carry):
            for dirn, r0 in enumerate(chunk_rows(ci_)):
                _, _, v_ref, lf_ref, k_ref, _ = dir_refs[dirn]
                state_only_chunk(dirn, lf_ref, k_ref, v_ref, r0)
            return carry

        lax.fori_loop(0, n_chunks, state_body, 0)
    else:
        step_id = pl.program_id(0) * pl.num_programs(1) + j
        cur = step_id % 2

        def prepare(lf_refs, slot):
            worst = None
            for dirn in range(2):
                bsum = prefix((trif_ref, trib_ref)[dirn][...], lf_refs[dirn][...])
                bsum_ref[slot, dirn] = bsum
                for ch in range(n_chunks):
                    m = bsum[ch * c + mrow[dirn]:ch * c + mrow[dirn] + 1, :]
                    tot = bsum[ch * c + trow[dirn]:ch * c + trow[dirn] + 1, :]
                    w = jnp.minimum(m, tot - m)
                    worst = w if worst is None else jnp.minimum(worst, w)
            flag_ref[slot] = (jnp.min(worst) >= -SCAN_FAST_LIMIT).astype(jnp.int32)

        @pl.when(step_id == 0)
        def _():
            prepare((lff_ref, lfb_ref), 0)

        fast_ok = flag_ref[cur] == 1

        @pl.when(fast_ok)
        def _():
            states = [st_ref[i] for i in range(2 * HEADS)]
            for step in range(n_chunks):
                for dirn in range(2):
                    ch = step if dirn == 0 else n_chunks - 1 - step
                    _, q_ref, v_ref, _, k_ref, o_ref = dir_refs[dirn]
                    rows = slice(ch * c, (ch + 1) * c)
                    m = bsum_ref[cur, dirn, ch * c + mrow[dirn]:ch * c + mrow[dirn] + 1, :]
                    tot = bsum_ref[cur, dirn, ch * c + trow[dirn]:ch * c + trow[dirn] + 1, :]
                    fast_chunk(dirn, q_ref, v_ref, k_ref, o_ref, rows, bsum_ref[cur, dirn, rows, :],
                               m, tot, states)
            for i in range(2 * HEADS):
                st_ref[i] = states[i]
            prepare((lff_next_ref, lfb_next_ref), 1 - cur)

        @pl.when(jnp.logical_not(fast_ok))
        def _():
            def robust_body(ci_, carry):
                for dirn, r0 in enumerate(chunk_rows(ci_)):
                    r_ref, q_ref, v_ref, lf_ref, k_ref, o_ref = dir_refs[dirn]
                    robust_chunk(dirn, r_ref, q_ref, v_ref, lf_ref, k_ref, o_ref, r0)
                return carry

            prepare((lff_next_ref, lfb_next_ref), 1 - cur)
            lax.fori_loop(0, n_chunks, robust_body, 0)

    @pl.when(j == pl.num_programs(1) - 1)
    def _():
        sfin_ref[0] = st_ref[...]


def _scan(q, v, lff, kf, lfb, kb, s0, *, row_off, seq_len, compute_o):
    batch = s0.shape[0]
    t = min(SCAN_BLOCK, seq_len)
    nt = seq_len // t
    off = row_off // t
    fwd = lambda b, j: (off + b * nt + j, 0)
    bwd = lambda b, j: (off + b * nt + (nt - 1 - j), 0)
    ofwd = lambda b, j: (b * nt + j, 0)
    obwd = lambda b, j: (b * nt + (nt - 1 - j), 0)
    blk = lambda im: pl.BlockSpec((t, D_GROUP), im)
    rf = jnp.asarray(_scan_tables(SCAN_CHUNK, False), BF16)
    rb = jnp.asarray(_scan_tables(SCAN_CHUNK, True), BF16)
    rspec = pl.BlockSpec(rf.shape, lambda b, j: (0, 0))
    sspec = pl.BlockSpec((1, 2 * HEADS, HEAD_DIM, HEAD_DIM), lambda b, j: (b, 0, 0, 0))
    pos_t = np.arange(t)
    same_chunk = (pos_t[:, None] // SCAN_CHUNK) == (pos_t[None, :] // SCAN_CHUNK)
    tri_f = jnp.asarray(same_chunk & (pos_t[None, :] <= pos_t[:, None]), BF16)
    tri_b = jnp.asarray(same_chunk & (pos_t[None, :] >= pos_t[:, None]), BF16)
    tspec = pl.BlockSpec((t, t), lambda b, j: (0, 0))
    in_specs = [rspec, rspec, blk(fwd), blk(fwd), blk(fwd), blk(fwd),
                blk(bwd), blk(bwd), blk(bwd), blk(bwd), sspec, tspec, tspec]
    args = [rf, rb, q, v, lff, kf, q, v, lfb, kb, s0, tri_f, tri_b]
    out_shape, out_specs = [], []
    scratch = [pltpu.VMEM((2 * HEADS, HEAD_DIM, HEAD_DIM), F32)]
    if compute_o:
        def next_step(b, j):
            nxt = jnp.minimum(b * nt + j + 1, batch * nt - 1)
            return nxt // nt, nxt % nt

        in_specs += [blk(lambda b, j: fwd(*next_step(b, j))), blk(lambda b, j: bwd(*next_step(b, j)))]
        args += [lff, lfb]
        out_shape += [jax.ShapeDtypeStruct((batch * seq_len, D_GROUP), BF16)] * 2
        out_specs += [blk(ofwd), blk(obwd)]
        scratch += [pltpu.VMEM((2, 2, t, D_GROUP), F32), pltpu.SMEM((2,), I32)]
    out_shape.append(jax.ShapeDtypeStruct(s0.shape, F32))
    out_specs.append(sspec)
    kern = functools.partial(_scan_kernel, n_chunks=t // SCAN_CHUNK, compute_o=compute_o)
    return pl.pallas_call(
        kern,
        out_shape=out_shape,
        grid=(batch, nt),
        in_specs=in_specs,
        out_specs=out_specs,
        scratch_shapes=scratch,
        compiler_params=_cparams(("arbitrary", "arbitrary"), VMEM_LIMIT),
        name="hgrn_scan",
    )(*args)


def _ln_silu(y, g, b):
    mu = jnp.mean(y, axis=-1, keepdims=True)
    yc = y - mu
    var = jnp.mean(yc * yc, axis=-1, keepdims=True)
    return _silu(yc * lax.rsqrt(var + EPS) * g + b)


def _tree_sum(terms):
    while len(terms) > 1:
        terms = [terms[i] + terms[i + 1] if i + 1 < len(terms) else terms[i]
                 for i in range(0, len(terms), 2)]
    return terms[0]


def _conv_rows_kernel(u_ref, w_ref, b_ref, lg_ref, lb_ref, y_ref, pad_ref, acc_ref, *, seq, tb):
    nseq = tb // seq
    stride = seq + CONV_GAP
    n_pad = pad_ref.shape[2]
    n_lt = D_GROUP // LANES
    for cl in range(n_lt):
        ls = slice(cl * LANES, (cl + 1) * LANES)
        pad_ref[0, cl] = jnp.zeros(pad_ref.shape[2:], F32)
        for s in range(nseq):
            pad_ref[0, cl, CONV_GAP + s * stride:CONV_GAP + s * stride + seq, :] = (
                u_ref[s * seq:(s + 1) * seq, ls].astype(F32))
        for p in range(1, SUBLANES):
            pad_ref[p, cl, 0:n_pad - SUBLANES, :] = pad_ref[0, cl, p:p + n_pad - SUBLANES, :]
    rc = 64
    group = CONV_ROW_GROUP
    assert seq % rc == 0 and tb % (group * rc) == 0
    if seq == rc:
        member_off = [a * stride for a in range(group)]
        group_base = lambda g: g * (group * stride)
    else:
        gps = seq // (group * rc)
        assert seq % (group * rc) == 0
        member_off = [a * rc for a in range(group)]
        group_base = lambda g: (g // gps) * stride + (g % gps) * (group * rc)

    def one_group(g, carry):
        pad0 = pl.multiple_of(CONV_GAP + group_base(g), SUBLANES)
        out0 = pl.multiple_of(g * (group * rc), group * rc)
        for cl in range(n_lt):
            ls = slice(cl * LANES, (cl + 1) * LANES)
            parts = [[None] * CONV_PARTIALS for _ in range(group)]
            for k in range(CONV_K):
                off = k - CONV_HALF
                wk = w_ref[k:k + 1, ls]
                for a in range(group):
                    src = pl.ds(pad0 + member_off[a] + (off // SUBLANES) * SUBLANES, rc)
                    term = wk * pad_ref[off % SUBLANES, cl, src, :]
                    prev = parts[a][k % CONV_PARTIALS]
                    parts[a][k % CONV_PARTIALS] = term if prev is None else prev + term
            for a in range(group):
                acc_ref[pl.ds(out0 + a * rc, rc), ls] = _tree_sum(parts[a])
        return carry

    lax.fori_loop(0, tb // (group * rc), one_group, 0)
    y = acc_ref[...] + b_ref[...]
    y_ref[...] = _ln_silu(y, lg_ref[...], lb_ref[...]).astype(y_ref.dtype)


def _conv_rows(u, w, b, lg, lb, *, seq, row_off, n_rows):
    tb = max(seq, ROW_TILE) if n_rows % max(seq, ROW_TILE) == 0 else seq
    nseq = tb // seq
    off = row_off // tb
    vec = pl.BlockSpec((1, D_GROUP), lambda i: (0, 0))
    in_specs = [pl.BlockSpec((tb, D_GROUP), lambda i: (off + i, 0)),
                pl.BlockSpec((CONV_K, D_GROUP), lambda i: (0, 0)), vec, vec, vec]
    args = [u, w, b, lg, lb]
    kern = functools.partial(_conv_rows_kernel, seq=seq, tb=tb)
    return pl.pallas_call(
        kern,
        out_shape=jax.ShapeDtypeStruct((n_rows, D_GROUP), BF16),
        grid=(n_rows // tb,),
        in_specs=in_specs,
        out_specs=pl.BlockSpec((tb, D_GROUP), lambda i: (i, 0)),
        scratch_shapes=[pltpu.VMEM((SUBLANES, D_GROUP // LANES, CONV_GAP + nseq * (seq + CONV_GAP), LANES), F32),
                        pltpu.VMEM((tb, D_GROUP), F32)],
        compiler_params=_cparams(("arbitrary",), VMEM_LIMIT),
        name="conv_rows",
    )(*args)


CONV_COL_TILE = 16


def _conv_cols_kernel(u_ref, w_ref, b_ref, lg_ref, lb_ref, y_ref, uf_ref, *, n_rows):
    n_lt = D_GROUP // LANES
    row_stride = GRID_W + SUBLANES
    for cl in range(n_lt):
        ls = slice(cl * LANES, (cl + 1) * LANES)
        for r in range(n_rows):
            uf_ref[cl, r * row_stride:r * row_stride + GRID_W, :] = (
                u_ref[r * GRID_W:(r + 1) * GRID_W, ls].astype(F32))
    bias = b_ref[...]
    lg = lg_ref[...]
    lb = lb_ref[...]

    def col_tile(wi, carry):
        w0 = pl.multiple_of(wi * CONV_COL_TILE, CONV_COL_TILE)
        for r in range(n_rows):
            tiles = []
            for cl in range(n_lt):
                ls = slice(cl * LANES, (cl + 1) * LANES)
                parts = [None] * CONV_PARTIALS
                for k in range(CONV_K):
                    rr = r + k - CONV_HALF
                    if 0 <= rr < n_rows:
                        src = pl.ds(pl.multiple_of(rr * row_stride + w0, SUBLANES), CONV_COL_TILE)
                        term = w_ref[k:k + 1, ls] * uf_ref[cl, src, :]
                        prev = parts[k % CONV_PARTIALS]
                        parts[k % CONV_PARTIALS] = term if prev is None else prev + term
                tiles.append(_tree_sum([p for p in parts if p is not None]))
            dst = pl.ds(pl.multiple_of(r * GRID_W + w0, CONV_COL_TILE), CONV_COL_TILE)
            y_ref[dst, :] = _ln_silu(jnp.concatenate(tiles, axis=1) + bias, lg, lb).astype(y_ref.dtype)
        return carry

    lax.fori_loop(0, GRID_W // CONV_COL_TILE, col_tile, 0)


def _conv_cols(u, w, b, lg, lb, *, batch, seq_len):
    vec = pl.BlockSpec((1, D_GROUP), lambda bi: (0, 0))
    blk = pl.BlockSpec((seq_len, D_GROUP), lambda bi: (bi, 0))
    return pl.pallas_call(
        functools.partial(_conv_cols_kernel, n_rows=seq_len // GRID_W),
        out_shape=jax.ShapeDtypeStruct((batch * seq_len, D_GROUP), BF16),
        grid=(batch,),
        in_specs=[blk, pl.BlockSpec((CONV_K, D_GROUP), lambda bi: (0, 0)), vec, vec, vec],
        out_specs=blk,
        scratch_shapes=[pltpu.VMEM((D_GROUP // LANES, (seq_len // GRID_W) * (GRID_W + SUBLANES), LANES), F32)],
        compiler_params=_cparams(("arbitrary",), VMEM_LIMIT),
        name="conv_cols",
    )(u, w, b, lg, lb)


def _outproj_kernel(*refs, n_x, n_mix, n_lat_tiles):
    x = _rows_from(refs[:n_x], n_lat_tiles)
    pos = n_x
    ycx, of, ob = (_rows_from(refs[pos + k * n_mix:pos + (k + 1) * n_mix], n_lat_tiles) for k in range(3))
    og_ref, hg_ref, w_ref, g1_ref = refs[pos + 3 * n_mix:pos + 3 * n_mix + 4]
    route_in = refs[pos + 3 * n_mix + 4:pos + 3 * n_mix + 10]
    o_ref = refs[pos + 3 * n_mix + 10]
    route_out = refs[pos + 3 * n_mix + 11:]
    o = of.astype(F32) + ob.astype(F32)
    og = og_ref[...].astype(F32)
    hg = hg_ref[...]
    acc = jnp.dot(ycx, w_ref[0, 0:D_GROUP, :], preferred_element_type=F32)
    for h in range(HEADS):
        hs = slice(h * HEAD_DIM, (h + 1) * HEAD_DIM)
        oh = o[:, hs]
        r = lax.rsqrt(jnp.mean(oh * oh, axis=-1, keepdims=True) + EPS)
        yh = ((oh * r * hg) * og[:, hs]).astype(BF16)
        acc = acc + jnp.dot(yh, w_ref[0, D_GROUP + h * HEAD_DIM:D_GROUP + (h + 1) * HEAD_DIM, :],
                            preferred_element_type=F32)
    x = x + g1_ref[0] * acc
    o_ref[...] = x
    _route(x, *route_in, *route_out)


def _outproj_route(xs, mix, og, hg, w_bf, layer, g1, g2n, sh2, sc2, rw_t, rb, *, n_rows, tm, tiles_per_seq):
    d = xs[0].shape[1]
    nb = g1.shape[0] - 1
    n_tt = n_rows // tm
    x_specs, n_lat_tiles = _row_specs(xs, tm)
    mix_specs, mix_args = [], []
    for arrays in mix:
        specs, nl = _row_specs(arrays, tm)
        assert len(arrays) == len(mix[0]) and (nl == 0 or n_lat_tiles in (0, nl))
        n_lat_tiles = max(n_lat_tiles, nl)
        mix_specs += specs
        mix_args += list(arrays)
    row = lambda width: pl.BlockSpec((tm, width), lambda i: (i, 0))
    mod_spec = pl.BlockSpec((1, 1, d), lambda i: (jnp.minimum(i // tiles_per_seq, nb), 0, 0))
    pair = pl.BlockSpec((2, tm), lambda i: (0, i))
    kern = functools.partial(_outproj_kernel, n_x=len(xs), n_mix=len(mix[0]), n_lat_tiles=n_lat_tiles)
    before = jnp.asarray(np.arange(tm)[:, None] < np.arange(tm)[None, :], BF16)
    return pl.pallas_call(
        kern,
        out_shape=[jax.ShapeDtypeStruct((n_rows, d), F32),
                   jax.ShapeDtypeStruct((n_rows, d), BF16),
                   jax.ShapeDtypeStruct((2, n_rows), I32),
                   jax.ShapeDtypeStruct((2, n_rows), I32),
                   jax.ShapeDtypeStruct((2, n_rows), F32),
                   jax.ShapeDtypeStruct((n_tt, N_EXPERTS, LANES), I32)],
        grid=(n_tt,),
        in_specs=x_specs + mix_specs + [row(D_GROUP),
                  pl.BlockSpec((1, HEAD_DIM), lambda i: (0, 0)),
                  pl.BlockSpec((1,) + w_bf.shape[1:], lambda i: (layer, 0, 0)),
                  mod_spec,
                  pl.BlockSpec((1, d), lambda i: (0, 0)), mod_spec, mod_spec,
                  pl.BlockSpec((N_EXPERTS, d), lambda i: (0, 0)),
                  pl.BlockSpec((N_EXPERTS, 1), lambda i: (0, 0)),
                  pl.BlockSpec((tm, tm), lambda i: (0, 0))],
        out_specs=[row(d), row(d), pair, pair, pair,
                   pl.BlockSpec((1, N_EXPERTS, LANES), lambda i: (i, 0, 0))],
        compiler_params=_cparams(("arbitrary",), VMEM_LIMIT),
        name="outproj_route",
    )(*xs, *mix_args, og, hg, w_bf, g1, g2n, sh2, sc2, rw_t, rb, before)


def _route(x, g_ref, sh_ref, sc_ref, rw_ref, rb_ref, before_ref, h_ref, e_ref, rank_ref, w_ref, cnt_ref):
    tm = x.shape[0]
    h = _modulate(x, g_ref[...], sh_ref[0], sc_ref[0])
    h_ref[...] = h.astype(h_ref.dtype)
    logits = lax.dot_general(rw_ref[...], h, (((1,), (1,)), ((), ())),
                             precision=lax.Precision.HIGHEST, preferred_element_type=F32)
    s = jax.nn.sigmoid(logits)
    sb = s + rb_ref[...]
    s_rows = [s[e:e + 1, :] for e in range(N_EXPERTS)]
    sb_rows = [sb[e:e + 1, :] for e in range(N_EXPERTS)]

    def group_score(g):
        v = sb_rows[g * EXPERTS_PER_GROUP:(g + 1) * EXPERTS_PER_GROUP]
        best = None
        for a in range(EXPERTS_PER_GROUP):
            for b in range(a + 1, EXPERTS_PER_GROUP):
                p = v[a] + v[b]
                best = p if best is None else jnp.maximum(best, p)
        return best

    cur = group_score(0)
    best_g = jnp.zeros(cur.shape, I32)
    for g in range(1, N_GROUPS):
        gs = group_score(g)
        upd = gs > cur
        best_g = jnp.where(upd, g, best_g)
        cur = jnp.where(upd, gs, cur)

    def pick(rows, jdx):
        out = rows[jdx]
        for g in range(1, N_GROUPS):
            out = jnp.where(best_g == g, rows[g * EXPERTS_PER_GROUP + jdx], out)
        return out

    vb = [pick(sb_rows, jdx) for jdx in range(EXPERTS_PER_GROUP)]
    vs = [pick(s_rows, jdx) for jdx in range(EXPERTS_PER_GROUP)]

    def first_argmax(vals):
        m = vals[0]
        for v in vals[1:]:
            m = jnp.maximum(m, v)
        idx = jnp.full(m.shape, EXPERTS_PER_GROUP - 1, I32)
        for jdx in range(EXPERTS_PER_GROUP - 2, -1, -1):
            idx = jnp.where(vals[jdx] == m, jdx, idx)
        return idx

    def take(vals, idx):
        out = vals[EXPERTS_PER_GROUP - 1]
        for jdx in range(EXPERTS_PER_GROUP - 2, -1, -1):
            out = jnp.where(idx == jdx, vals[jdx], out)
        return out

    i0 = first_argmax(vb)
    vb2 = [jnp.where(i0 == jdx, -jnp.inf, vb[jdx]) for jdx in range(EXPERTS_PER_GROUP)]
    i1 = first_argmax(vb2)
    s0 = take(vs, i0)
    s1 = take(vs, i1)
    tot = s0 + s1
    w_ref[0:1, :] = s0 / tot
    w_ref[1:2, :] = s1 / tot
    e0 = best_g * EXPERTS_PER_GROUP + i0
    e1 = best_g * EXPERTS_PER_GROUP + i1
    e_ref[0:1, :] = e0
    e_ref[1:2, :] = e1

    eid = lax.broadcasted_iota(I32, (N_EXPERTS, tm), 0)
    hit0 = eid == e0
    hit1 = eid == e1
    onehot = jnp.where(jnp.logical_or(hit0, hit1), 1.0, 0.0)
    rank = jnp.dot(onehot.astype(BF16), before_ref[...], preferred_element_type=F32)
    rank_ref[0:1, :] = jnp.sum(jnp.where(hit0, rank, 0.0), axis=0, keepdims=True).astype(I32)
    rank_ref[1:2, :] = jnp.sum(jnp.where(hit1, rank, 0.0), axis=0, keepdims=True).astype(I32)
    cnt = jnp.sum(onehot, axis=1, keepdims=True).astype(I32)
    cnt_ref[0] = jnp.broadcast_to(cnt, (N_EXPERTS, LANES))


def _tables_kernel(cnt_ref, seg_start_ref, seg_len_ref, te_ref, misc_ref, *, n_tt, n_et, tm):
    def expert(e, row0):
        def tile(i, pos):
            n = cnt_ref[i * N_EXPERTS + e]
            n8 = ((n + SUBLANES - 1) // SUBLANES) * SUBLANES
            seg_start_ref[i * N_EXPERTS + e] = pos
            seg_len_ref[i * N_EXPERTS + e] = n8
            return pos + n8

        end = lax.fori_loop(0, n_tt, tile, row0)
        padded_end = row0 + ((end - row0 + tm - 1) // tm) * tm
        misc_ref[1 + e] = end
        misc_ref[1 + N_EXPERTS + e] = padded_end - end

        def mark(j, carry):
            te_ref[j] = e
            return carry

        lax.fori_loop(row0 // tm, padded_end // tm, mark, 0)
        return padded_end

    total = row0 = 0
    for e in range(N_EXPERTS):
        row0 = expert(e, row0)
    total = row0
    n_used = total // tm
    misc_ref[0] = n_used
    last = te_ref[jnp.maximum(n_used - 1, 0)]

    def fill(j, carry):
        te_ref[j] = last
        return carry

    lax.fori_loop(n_used, n_et, fill, 0)


def _tables(cnt, *, n_tt, n_et, tm):
    smem = pl.BlockSpec(memory_space=pltpu.SMEM)
    return pl.pallas_call(
        functools.partial(_tables_kernel, n_tt=n_tt, n_et=n_et, tm=tm),
        out_shape=[jax.ShapeDtypeStruct((n_tt * N_EXPERTS,), I32),
                   jax.ShapeDtypeStruct((n_tt * N_EXPERTS,), I32),
                   jax.ShapeDtypeStruct((n_et,), I32),
                   jax.ShapeDtypeStruct((1 + 2 * N_EXPERTS,), I32)],
        in_specs=[smem],
        out_specs=[smem, smem, smem, smem],
        name="route_tables",
    )(cnt)


def _piece_sizes(max_rows):
    sizes, s = [], SUBLANES
    while s <= max_rows:
        sizes.append(s)
        s *= 2
    return sizes[::-1]


def _segment_copies(src_ref, src_row, dst_ref, dst_row, n_rows, sem, sizes, start):
    aligned = lambda r: r if isinstance(r, int) else pl.multiple_of(r, SUBLANES)
    for sz in sizes:
        src = pl.ds(aligned(src_row), sz)
        dst = pl.ds(aligned(dst_row), sz)

        @pl.when((n_rows & sz) != 0)
        def _(src=src, dst=dst):
            cp = pltpu.make_async_copy(src_ref.at[src, :], dst_ref.at[dst, :], sem)
            if start:
                cp.start()
            else:
                cp.wait()

        src_row = src_row + (n_rows & sz)
        dst_row = dst_row + (n_rows & sz)


def _local_slots(seg_len_ref, tile, e_ref, rank_ref):
    offs, lo = [], 0
    for e in range(N_EXPERTS):
        offs.append(lo)
        lo = lo + seg_len_ref[tile * N_EXPERTS + e]
    e01 = e_ref[...]
    slot = rank_ref[...]
    for e in range(N_EXPERTS):
        slot = slot + jnp.where(e01 == e, offs[e], 0)
    return slot, offs


def _compact_rows(tm):
    return ((2 * tm + N_EXPERTS * (SUBLANES - 1) + LANES - 1) // LANES) * LANES


def _dispatch_kernel(seg_start_ref, seg_len_ref, misc_ref, h_ref, e_ref, rank_ref, xs_ref,
                     xc_ref, zero_ref, sems, *, sizes):
    tm = h_ref.shape[0]
    rc = xc_ref.shape[1]
    i = pl.program_id(0)
    buf = i % 2
    slot, _ = _local_slots(seg_len_ref, i, e_ref, rank_ref)
    rid = lax.broadcasted_iota(I32, (rc, tm), 0)
    sel = jnp.logical_or(rid == slot[0:1, :], rid == slot[1:2, :])
    perm = jnp.where(sel, 1.0, 0.0).astype(BF16)
    xc_ref[buf] = jnp.dot(perm, h_ref[...], preferred_element_type=F32)

    def segments(tile, b, start):
        lo = 0
        for e in range(N_EXPERTS):
            n8 = seg_len_ref[tile * N_EXPERTS + e]
            _segment_copies(xc_ref.at[b], lo, xs_ref, seg_start_ref[tile * N_EXPERTS + e], n8,
                            sems.at[b], sizes, start)
            lo = lo + n8

    @pl.when(i > 0)
    def _():
        segments(i - 1, 1 - buf, False)

    segments(i, buf, True)

    @pl.when(i == pl.num_programs(0) - 1)
    def _():
        segments(i, buf, False)
        zero_ref[...] = jnp.zeros(zero_ref.shape, F32)
        for start in (True, False):
            for e in range(N_EXPERTS):
                _segment_copies(zero_ref, 0, xs_ref, misc_ref[1 + e], misc_ref[1 + N_EXPERTS + e],
                                sems.at[0], sizes, start)

        def clear_tile(j, carry):
            cp = pltpu.make_async_copy(zero_ref, xs_ref.at[pl.ds(pl.multiple_of(j * tm, tm), tm), :],
                                       sems.at[0])
            cp.start()
            cp.wait()
            return carry

        lax.fori_loop(misc_ref[0], xs_ref.shape[0] // tm, clear_tile, 0)


def _dispatch(seg_start, seg_len, misc, h, e01, rank01, *, n_sorted, tm):
    n, d = h.shape
    pair = pl.BlockSpec((2, tm), lambda i, *_: (0, i))
    grid_spec = pltpu.PrefetchScalarGridSpec(
        num_scalar_prefetch=3,
        grid=(n // tm,),
        in_specs=[pl.BlockSpec((tm, d), lambda i, *_: (i, 0)), pair, pair],
        out_specs=pl.BlockSpec(memory_space=pl.ANY),
        scratch_shapes=[pltpu.VMEM((2, _compact_rows(tm), d), F32), pltpu.VMEM((tm, d), F32),
                        pltpu.SemaphoreType.DMA((2,))],
    )
    return pl.pallas_call(
        functools.partial(_dispatch_kernel, sizes=_piece_sizes(tm)),
        out_shape=jax.ShapeDtypeStruct((n_sorted, d), F32),
        grid_spec=grid_spec,
        compiler_params=_cparams(("arbitrary",), VMEM_LIMIT),
        name="dispatch",
    )(seg_start, seg_len, misc, h, e01, rank01)


def _moe_kernel(te_ref, misc_ref, xs_ref, wg_hbm, wu_hbm, wd_hbm, ys_ref,
                wg_buf, wu_buf, wd_buf, wgb_ref, wub_ref, wdb_ref, run_ref, sems, *, layer, tm):
    i = pl.program_id(0)
    e = te_ref[i]
    prev = te_ref[jnp.maximum(i - 1, 0)]
    n_used = misc_ref[0]

    def fetch(expert, slot):
        return [pltpu.make_async_copy(w.at[layer, expert], buf.at[slot], sems.at[slot, k])
                for k, (w, buf) in enumerate(((wg_hbm, wg_buf), (wu_hbm, wu_buf), (wd_hbm, wd_buf)))]

    @pl.when(i == 0)
    def _():
        run_ref[0] = 0
        for cp in fetch(e, 0):
            cp.start()

    @pl.when(jnp.logical_or(i == 0, e != prev))
    def _():
        slot = run_ref[0] % 2
        for cp in fetch(e, slot):
            cp.wait()
        nxt = (misc_ref[1 + e] + misc_ref[1 + N_EXPERTS + e]) // tm

        @pl.when(nxt < n_used)
        def _():
            for cp in fetch(te_ref[nxt], 1 - slot):
                cp.start()

        wgb_ref[...] = wg_buf[slot].astype(BF16)
        wub_ref[...] = wu_buf[slot].astype(BF16)
        wdb_ref[...] = wd_buf[slot].astype(BF16)
        run_ref[0] = run_ref[0] + 1

    @pl.when(i < misc_ref[0])
    def _():
        x = xs_ref[...].astype(BF16)
        gate = jnp.dot(x, wgb_ref[...], preferred_element_type=F32)
        up = jnp.dot(x, wub_ref[...], preferred_element_type=F32)
        act = (_silu(gate) * up).astype(BF16)
        ys_ref[...] = jnp.dot(act, wdb_ref[...], preferred_element_type=F32)

    @pl.when(i >= misc_ref[0])
    def _():
        ys_ref[...] = jnp.zeros(ys_ref.shape, F32)


def _moe(tile_expert, misc, xs, wg, wu, wd, layer, tm):
    n_sorted, d = xs.shape
    de = wg.shape[-1]
    xmap = lambda i, te, misc_ref: (jnp.minimum(i, jnp.maximum(misc_ref[0] - 1, 0)), 0)
    hbm = pl.BlockSpec(memory_space=pl.ANY)
    grid_spec = pltpu.PrefetchScalarGridSpec(
        num_scalar_prefetch=2,
        grid=(n_sorted // tm,),
        in_specs=[pl.BlockSpec((tm, d), xmap), hbm, hbm, hbm],
        out_specs=pl.BlockSpec((tm, d), lambda i, te, misc_ref: (i, 0)),
        scratch_shapes=[pltpu.VMEM((2, d, de), F32), pltpu.VMEM((2, d, de), F32),
                        pltpu.VMEM((2, de, d), F32),
                        pltpu.VMEM((d, de), BF16), pltpu.VMEM((d, de), BF16),
                        pltpu.VMEM((de, d), BF16),
                        pltpu.SMEM((1,), I32), pltpu.SemaphoreType.DMA((2, 3))],
    )
    return pl.pallas_call(
        functools.partial(_moe_kernel, layer=layer, tm=tm),
        out_shape=jax.ShapeDtypeStruct((n_sorted, d), F32),
        grid_spec=grid_spec,
        compiler_params=_cparams(("arbitrary",), VMEM_LIMIT),
        name="moe_experts",
    )(tile_expert, misc, xs, wg, wu, wd)


def _combine_kernel(seg_start_ref, seg_len_ref, x_ref, g2_ref, e_ref, rank_ref, w_ref, fg_ref, ys_ref,
                    o_ref, yc_ref, sems, *, sizes, final_norm):
    tm = x_ref.shape[0]
    rc = yc_ref.shape[1]
    i = pl.program_id(0)
    buf = i % 2

    def segments(tile, b, start):
        lo = 0
        for e in range(N_EXPERTS):
            n8 = seg_len_ref[tile * N_EXPERTS + e]
            _segment_copies(ys_ref, seg_start_ref[tile * N_EXPERTS + e], yc_ref.at[b], lo, n8,
                            sems.at[b], sizes, start)
            lo = lo + n8

    @pl.when(i == 0)
    def _():
        yc_ref[...] = jnp.zeros(yc_ref.shape, F32)
        segments(0, 0, True)

    @pl.when(i + 1 < pl.num_programs(0))
    def _():
        segments(i + 1, 1 - buf, True)

    segments(i, buf, False)
    slot, _ = _local_slots(seg_len_ref, i, e_ref, rank_ref)
    rid = lax.broadcasted_iota(I32, (rc, tm), 0)
    w = w_ref[...]
    pw = (jnp.where(rid == slot[0:1, :], w[0:1, :], 0.0)
          + jnp.where(rid == slot[1:2, :], w[1:2, :], 0.0))
    dn = (((0,), (0,)), ((), ()))
    out = lax.dot_general(pw.astype(BF16), yc_ref[buf].astype(BF16), dn, preferred_element_type=F32)
    x = x_ref[...] + g2_ref[0] * out
    if final_norm:
        x = x * lax.rsqrt(jnp.mean(x * x, axis=-1, keepdims=True) + EPS) * fg_ref[...]
    o_ref[...] = x


def _combine(seg_start, seg_len, xc, g2, e01, rank01, w01, fg, ys, *, n_rows, tm, tiles_per_seq,
             final_norm):
    d = xc.shape[1]
    nb = g2.shape[0] - 1
    pair = pl.BlockSpec((2, tm), lambda i, *_: (0, i))
    grid_spec = pltpu.PrefetchScalarGridSpec(
        num_scalar_prefetch=2,
        grid=(n_rows // tm,),
        in_specs=[pl.BlockSpec((tm, d), lambda i, *_: (i, 0)),
                  pl.BlockSpec((1, 1, d), lambda i, *_: (jnp.minimum(i // tiles_per_seq, nb), 0, 0)),
                  pair, pair, pair,
                  pl.BlockSpec((1, d), lambda i, *_: (0, 0)),
                  pl.BlockSpec(memory_space=pl.ANY)],
        out_specs=pl.BlockSpec((tm, d), lambda i, *_: (i, 0)),
        scratch_shapes=[pltpu.VMEM((2, _compact_rows(tm), d), F32), pltpu.SemaphoreType.DMA((2,))],
    )
    return pl.pallas_call(
        functools.partial(_combine_kernel, sizes=_piece_sizes(tm), final_norm=final_norm),
        out_shape=jax.ShapeDtypeStruct((n_rows, d), F32),
        grid_spec=grid_spec,
        compiler_params=_cparams(("arbitrary",), VMEM_LIMIT),
        name="combine",
    )(seg_start, seg_len, xc, g2, e01, rank01, w01, fg, ys)


def _lower_bounds(lb_param):
    p = jax.nn.softmax(lb_param.astype(F32), axis=0)
    return jnp.cumsum(p, axis=0) - p[0]


def kernel(x, c, ctx, c_ctx, w_ada, b_ada, norm1_g, norm2_g, w_in, conv_w, conv_b, conv_ln_g,
           conv_ln_b, lb_fwd, lb_bwd, hgrn_norm_g, w_out, router_w, router_bias, w_gate, w_up,
           w_down, final_norm_g):
    bn, seq, d = x.shape
    ctx_len = ctx.shape[1]
    depth = w_ada.shape[0]
    n = bn * seq
    nc = bn * ctx_len
    tm = min(ROW_TILE, seq, nc)
    assert seq % tm == 0 and nc % tm == 0 and seq % GRID_W == 0
    tps = seq // tm

    pad_rows = (-(bn + 1)) % 8
    cc = jnp.concatenate([c, c_ctx[None, :], jnp.zeros((pad_rows, d), F32)], axis=0)
    mod = _ada(cc, w_ada, b_ada)[:, :bn + 1]
    mod = mod.reshape(depth, bn + 1, 6, 1, d)

    lbs_f = _lower_bounds(lb_fwd)
    lbs_b = _lower_bounds(lb_bwd)
    w_in_bf = w_in.astype(BF16)
    w_out_bf = w_out.astype(BF16)
    rw_t = router_w.T.astype(F32)
    rb = router_bias.reshape(N_EXPERTS, 1).astype(F32)
    hg = hgrn_norm_g.astype(F32)
    fg = final_norm_g.reshape(1, d).astype(F32)

    rows_in = [x.reshape(n, d), ctx.reshape(nc, d)]
    s_zero = jnp.zeros((bn, 2 * HEADS, HEAD_DIM, HEAD_DIM), F32)

    for l in range(depth):
        last = l == depth - 1
        sh1, sc1, g1, sh2, sc2, g2 = (mod[l, :, k] for k in range(6))
        row = lambda a: a.reshape(1, -1).astype(F32)

        u, q, iv, lff, kf, lfb, kb, og = _inproj(
            rows_in, row(norm1_g[l]), sh1, sc1, w_in_bf, l, row(lbs_f[l]), row(lbs_b[l]), tm, tps)

        cw, cb = conv_w[l].astype(F32), row(conv_b[l])
        clg, clb = row(conv_ln_g[l]), row(conv_ln_b[l])
        scan = functools.partial(_scan, q, iv, lff, kf, lfb, kb)
        if last:
            (s_ctx,) = scan(s_zero, row_off=n, seq_len=ctx_len, compute_o=False)
            mix_ctx = [[], [], []]
            n_rows = n
        else:
            of_c, ob_c, s_ctx = scan(s_zero, row_off=n, seq_len=ctx_len, compute_o=True)
            ycx_c = _conv_rows(u, cw, cb, clg, clb, seq=ctx_len, row_off=n, n_rows=nc)
            mix_ctx = [[ycx_c], [of_c], [ob_c]]
            n_rows = n + nc
        of, ob, _ = scan(s_ctx, row_off=0, seq_len=seq, compute_o=True)
        if l % 2 == 0:
            ycx = _conv_rows(u, cw, cb, clg, clb, seq=GRID_W, row_off=0, n_rows=n)
        else:
            ycx = _conv_cols(u, cw, cb, clg, clb, batch=bn, seq_len=seq)
        mix = [[a] + c_ for a, c_ in zip((ycx, of, ob), mix_ctx)]

        res_rows = rows_in if (len(rows_in) == 1 or not last) else [rows_in[0]]
        xc, h2, e01, rank01, w01, cnt = _outproj_route(
            res_rows, mix, og, hg[l].reshape(1, HEAD_DIM), w_out_bf, l, g1,
            row(norm2_g[l]), sh2, sc2, rw_t, rb, n_rows=n_rows, tm=tm, tiles_per_seq=tps)
        n_tt = n_rows // tm
        n_et = (2 * n_rows + n_tt * N_EXPERTS * (SUBLANES - 1) + tm - 1) // tm + N_EXPERTS
        seg_start, seg_len, te, misc = _tables(cnt[:, :, 0].reshape(-1), n_tt=n_tt, n_et=n_et, tm=tm)
        xs = _dispatch(seg_start, seg_len, misc, h2, e01, rank01, n_sorted=n_et * tm, tm=tm)
        ys = _moe(te, misc, xs, w_gate, w_up, w_down, l, tm)
        xc = _combine(seg_start, seg_len, xc, g2, e01, rank01, w01, fg, ys, n_rows=n_rows, tm=tm,
                      tiles_per_seq=tps, final_norm=last)
        rows_in = [xc]

    return xc[:n].reshape(bn, seq, d)
```

```python
import functools

import numpy as np
import jax
import jax.numpy as jnp
from jax import lax
from jax.experimental import pallas as pl
from jax.experimental.pallas import tpu as pltpu

F32 = jnp.float32
BF16 = jnp.bfloat16
I32 = jnp.int32

EPS = 1e-6
GRID_W = 64
HEADS = 4
HEAD_DIM = 128
D_GROUP = HEADS * HEAD_DIM
N_IN_GROUPS = 7
CONV_K = 31
CONV_HALF = CONV_K // 2
CONV_GAP = 16
CONV_ROW_GROUP = 1
CONV_PARTIALS = 2
N_EXPERTS = 16
N_GROUPS = 4
EXPERTS_PER_GROUP = N_EXPERTS // N_GROUPS
SCAN_CHUNK = 64
SCAN_BLOCK = 256
ROW_TILE = 512
SCAN_FAST_LIMIT = 60.0
LANES = 128
SUBLANES = 8
VMEM_LIMIT = 56 * 1024 * 1024


def _cparams(sem, vmem=None, fuse_input=None, n_inputs=0):
    fusion = None if fuse_input is None else [k == fuse_input for k in range(n_inputs)]
    return pltpu.CompilerParams(dimension_semantics=sem, vmem_limit_bytes=vmem, allow_input_fusion=fusion)


def _sigmoid(x):
    return 0.5 * jnp.tanh(0.5 * x) + 0.5


def _silu(x):
    return x * _sigmoid(x)


def _ada_kernel(c_ref, w_ref, b_ref, o_ref):
    a = _silu(c_ref[...]).astype(BF16)
    w = w_ref[0].astype(BF16)
    o_ref[0] = jnp.dot(a, w, preferred_element_type=F32) + b_ref[0]


def _ada(cc, w_ada, b_ada, tn=1536):
    depth, d, d6 = w_ada.shape
    rows = cc.shape[0]
    return pl.pallas_call(
        _ada_kernel,
        out_shape=jax.ShapeDtypeStruct((depth, rows, d6), F32),
        grid=(depth, d6 // tn),
        in_specs=[
            pl.BlockSpec((rows, d), lambda l, j: (0, 0)),
            pl.BlockSpec((1, d, tn), lambda l, j: (l, 0, j)),
            pl.BlockSpec((1, 1, tn), lambda l, j: (l, 0, j)),
        ],
        out_specs=pl.BlockSpec((1, rows, tn), lambda l, j: (l, 0, j)),
        compiler_params=_cparams(("arbitrary", "arbitrary"), VMEM_LIMIT),
        name="ada",
    )(cc, w_ada, b_ada.reshape(depth, 1, d6))


def _modulate(x, g, shift, scale):
    r = lax.rsqrt(jnp.mean(x * x, axis=-1, keepdims=True) + EPS)
    return (x * r * g) * (1.0 + scale) + shift


def _rows_from(refs, n_lat_tiles):
    if len(refs) == 1:
        return refs[0][...]
    return jnp.where(pl.program_id(0) >= n_lat_tiles, refs[1][...], refs[0][...])


def _row_specs(arrays, tm):
    width = arrays[0].shape[1]
    if len(arrays) == 1:
        return [pl.BlockSpec((tm, width), lambda i, *_: (i, 0))], 0
    n_lat_tiles = arrays[0].shape[0] // tm
    return [pl.BlockSpec((tm, width), lambda i, *_: (jnp.minimum(i, n_lat_tiles - 1), 0)),
            pl.BlockSpec((tm, width), lambda i, *_: (jnp.maximum(i - n_lat_tiles, 0), 0))], n_lat_tiles


def _inproj_kernel(*refs, n_src, n_lat_tiles):
    x = _rows_from(refs[:n_src], n_lat_tiles)
    (g_ref, sh_ref, sc_ref, w_ref, lbf_ref, lbb_ref,
     u_ref, q_ref, i_ref, lff_ref, kf_ref, lfb_ref, kb_ref, og_ref) = refs[n_src:]
    hb = _modulate(x, g_ref[...], sh_ref[0], sc_ref[0]).astype(BF16)

    def proj(j):
        return jnp.dot(hb, w_ref[0, :, j * D_GROUP:(j + 1) * D_GROUP], preferred_element_type=F32)

    u_ref[...] = (proj(0) * _sigmoid(proj(1))).astype(BF16)
    q_ref[...] = _silu(proj(2)).astype(BF16)
    i_ref[...] = proj(3).astype(BF16)
    for j, lb_ref, lf_ref, k_ref in ((4, lbf_ref, lff_ref, kf_ref), (5, lbb_ref, lfb_ref, kb_ref)):
        lb = lb_ref[...]
        f = lb + (1.0 - lb) * jax.nn.sigmoid(proj(j))
        lf_ref[...] = jnp.log(f)
        k_ref[...] = (1.0 - f).astype(BF16)
    og_ref[...] = _silu(proj(6)).astype(BF16)


def _inproj(xs, g, sh, sc, w_bf, layer, lbf, lbb, tm, tiles_per_seq):
    n = sum(a.shape[0] for a in xs)
    d = xs[0].shape[1]
    x_specs, n_lat_tiles = _row_specs(xs, tm)
    nb = sh.shape[0] - 1
    mod_spec = pl.BlockSpec((1, 1, d), lambda i: (jnp.minimum(i // tiles_per_seq, nb), 0, 0))
    row_spec = pl.BlockSpec((tm, D_GROUP), lambda i: (i, 0))
    vec_spec = pl.BlockSpec((1, D_GROUP), lambda i: (0, 0))
    outs = [jax.ShapeDtypeStruct((n, D_GROUP), dt)
            for dt in (BF16, BF16, BF16, F32, BF16, F32, BF16, BF16)]
    return pl.pallas_call(
        functools.partial(_inproj_kernel, n_src=len(xs), n_lat_tiles=n_lat_tiles),
        out_shape=outs,
        grid=(n // tm,),
        in_specs=x_specs + [
            pl.BlockSpec((1, d), lambda i: (0, 0)),
            mod_spec, mod_spec,
            pl.BlockSpec((1, d, N_IN_GROUPS * D_GROUP), lambda i: (layer, 0, 0)),
            vec_spec, vec_spec,
        ],
        out_specs=[row_spec] * 8,
        compiler_params=_cparams(("arbitrary",), VMEM_LIMIT, fuse_input=len(xs) + 3, n_inputs=len(xs) + 6),
        name="inproj",
    )(*xs, g, sh, sc, w_bf, lbf, lbb)


def _scan_tables(c, backward):
    levels = int(np.log2(c))
    r_all = np.zeros((levels * c + 2 * c + 8, c), np.float32)
    for lev in range(levels):
        h = 1 << lev
        for r in range(c):
            bd = (r // (2 * h)) * 2 * h + h
            if not backward:
                if r >= bd:
                    r_all[lev * c + r, bd:r + 1] = 1.0
                else:
                    r_all[lev * c + r, r + 1:bd] = 1.0
            else:
                if r < bd:
                    r_all[lev * c + r, r:bd] = 1.0
                else:
                    r_all[lev * c + r, bd:r] = 1.0
    base = levels * c
    for r in range(c):
        if not backward:
            r_all[base + r, :r + 1] = 1.0
            r_all[base + c + r, r + 1:] = 1.0
        else:
            r_all[base + r, r:] = 1.0
            r_all[base + c + r, :r] = 1.0
    r_all[base + 2 * c:, :] = 1.0
    return r_all


def _scan_kernel(*refs, n_chunks, compute_o):
    c = SCAN_CHUNK
    levels = int(np.log2(c))
    (rf_ref, rb_ref, qf_ref, vf_ref, lff_ref, kf_ref,
     qb_ref, vb_ref, lfb_ref, kb_ref, s0_ref, trif_ref, trib_ref) = refs[:13]
    pos = 13
    if compute_o:
        lff_next_ref, lfb_next_ref = refs[pos:pos + 2]
        of_ref, ob_ref, sfin_ref, st_ref, bsum_ref, flag_ref = refs[pos + 2:pos + 8]
    else:
        sfin_ref, st_ref = refs[pos:pos + 2]
        of_ref = ob_ref = None
    j = pl.program_id(1)

    @pl.when(j == 0)
    def _():
        st_ref[...] = s0_ref[0]

    row = lax.broadcasted_iota(jnp.int32, (c, 1), 0)
    ri = lax.broadcasted_iota(jnp.int32, (c, c), 0)
    ci = lax.broadcasted_iota(jnp.int32, (c, c), 1)
    upper = [(row & (2 * (1 << lev) - 1)) >= (1 << lev) for lev in range(levels)]
    same_parent = [(ri >> (lev + 1)) == (ci >> (lev + 1)) for lev in range(levels)]
    diag = ri == ci
    dn_t = (((1,), (1,)), ((), ()))
    dn_tl = (((0,), (0,)), ((), ()))

    half = c // 2
    tri = [jnp.where(ci <= ri, 1.0, 0.0).astype(BF16), jnp.where(ci >= ri, 1.0, 0.0).astype(BF16)]
    first_half = [row < half, row >= half]
    re = lax.broadcasted_iota(jnp.int32, (c, 4 * c), 0)
    ce = lax.broadcasted_iota(jnp.int32, (c, 4 * c), 1)
    cs = ce & (c - 1)
    same_half = (re >= half) == (cs >= half)
    own = (ce & (2 * c - 1)) < c
    mask_ext = [
        (own & same_half & (cs <= re)) | (jnp.logical_not(own) & (re >= half) & (cs < half)),
        (own & same_half & (cs >= re)) | (jnp.logical_not(own) & (re < half) & (cs >= half)),
    ]

    mrow, trow = [half - 1, half], [c - 1, 0]

    def prefix(tri_mat, g):
        g_hi = g.astype(BF16)
        g_lo = (g - g_hi.astype(F32)).astype(BF16)
        return (jnp.dot(tri_mat, g_hi, preferred_element_type=F32)
                + jnp.dot(tri_mat, g_lo, preferred_element_type=F32))

    def next_state(st, hs, kh, vh, b, tot):
        kd = (kh * jnp.exp(tot[:, hs] - b[:, hs])).astype(BF16)
        return st * jnp.exp(tot[:, hs]) + lax.dot_general(vh, kd, dn_tl, preferred_element_type=F32)

    def state_only_chunk(dirn, lf_ref, k_ref, v_ref, r0):
        rows = pl.ds(r0, c)
        b = prefix(tri[dirn], lf_ref[rows, :])
        tot = b[trow[dirn]:trow[dirn] + 1, :]
        for h in range(HEADS):
            hs = slice(h * HEAD_DIM, (h + 1) * HEAD_DIM)
            st_ref[dirn * HEADS + h] = next_state(st_ref[dirn * HEADS + h], hs,
                                                  k_ref[rows, hs].astype(F32), v_ref[rows, hs], b, tot)

    def fast_chunk(dirn, q_ref, v_ref, k_ref, o_ref, rows, b, m, tot, states):
        fh = first_half[dirn]
        hd = HEAD_DIM

        def block_diag(a, bb):
            za = jnp.zeros((a.shape[0], bb.shape[1]), a.dtype)
            zb = jnp.zeros((bb.shape[0], a.shape[1]), a.dtype)
            return jnp.concatenate([jnp.concatenate([a, za], axis=1),
                                    jnp.concatenate([zb, bb], axis=1)], axis=0)

        for hp in range(HEADS // 2):
            ps = slice(2 * hp * hd, 2 * (hp + 1) * hd)
            bp, mp, tp = b[:, ps], m[:, ps], tot[:, ps]
            cdec = bp - jnp.where(fh, 0.0, mp)
            e_own = jnp.exp(cdec).astype(BF16)
            e_own_inv = jnp.exp(-cdec).astype(BF16)
            e_cross = jnp.where(fh, jnp.exp(jnp.minimum(mp - bp, 0.0)), 0.0).astype(BF16)
            e_inc = jnp.exp(bp).astype(BF16)
            e_dec = jnp.exp(tp - bp).astype(BF16)
            qp, kp, vp = q_ref[rows, ps], k_ref[rows, ps], v_ref[rows, ps]
            q1, qi = qp * e_own, qp * e_inc
            k1, k2, kd = kp * e_own_inv, kp * e_cross, kp * e_dec
            kcat = [jnp.concatenate([k1[:, s_], k2[:, s_]], axis=0) for s_ in (slice(0, hd), slice(hd, 2 * hd))]
            sc = lax.dot_general(q1, block_diag(*kcat), dn_t, preferred_element_type=F32)
            p = jnp.where(mask_ext[dirn], sc, 0.0).astype(BF16)
            v2 = [jnp.concatenate([vp[:, s_], vp[:, s_]], axis=0) for s_ in (slice(0, hd), slice(hd, 2 * hd))]
            st = [states[dirn * HEADS + 2 * hp + a] for a in range(2)]
            o = (jnp.dot(p, block_diag(*v2), preferred_element_type=F32)
                 + lax.dot_general(qi, block_diag(st[0].astype(BF16), st[1].astype(BF16)), dn_t,
                                   preferred_element_type=F32))
            o_ref[rows, ps] = o.astype(o_ref.dtype)
            upd = lax.dot_general(vp, kd, dn_tl, preferred_element_type=F32)
            e_tot = jnp.exp(tp)
            for a in range(2):
                sl = slice(a * hd, (a + 1) * hd)
                states[dirn * HEADS + 2 * hp + a] = st[a] * e_tot[:, sl] + upd[sl, sl]

    def robust_chunk(dirn, r_ref, q_ref, v_ref, lf_ref, k_ref, o_ref, r0):
        rows = pl.ds(r0, c)
        g = lf_ref[rows, :]
        g_hi = g.astype(BF16)
        g_lo = (g - g_hi.astype(F32)).astype(BF16)
        rmat = r_ref[...]
        e_all = jnp.exp(jnp.dot(rmat, g_hi, preferred_element_type=F32)
                        + jnp.dot(rmat, g_lo, preferred_element_type=F32))
        base = levels * c
        for h in range(HEADS):
            hs = slice(h * HEAD_DIM, (h + 1) * HEAD_DIM)
            qh = q_ref[rows, hs].astype(F32)
            kh = k_ref[rows, hs].astype(F32)
            vh = v_ref[rows, hs]
            st = st_ref[dirn * HEADS + h]
            if compute_o:
                att = jnp.where(diag, lax.dot_general(qh.astype(BF16), kh.astype(BF16), dn_t,
                                                      preferred_element_type=F32), 0.0)
                for lev in range(levels):
                    e_l = e_all[lev * c:(lev + 1) * c, hs]
                    q_side = upper[lev] if dirn == 0 else jnp.logical_not(upper[lev])
                    qt = jnp.where(q_side, qh * e_l, 0.0).astype(BF16)
                    kt = jnp.where(q_side, 0.0, kh * e_l).astype(BF16)
                    a_l = lax.dot_general(qt, kt, dn_t, preferred_element_type=F32)
                    att = att + jnp.where(same_parent[lev], a_l, 0.0)
                qi = (qh * e_all[base:base + c, hs]).astype(BF16)
                o = lax.dot_general(qi, st.astype(BF16), dn_t, preferred_element_type=F32)
                o = o + jnp.dot(att.astype(BF16), vh, preferred_element_type=F32)
                o_ref[rows, hs] = o.astype(o_ref.dtype)
            kd = (kh * e_all[base + c:base + 2 * c, hs]).astype(BF16)
            e_tot = e_all[base + 2 * c:base + 2 * c + 1, hs]
            st_ref[dirn * HEADS + h] = st * e_tot + lax.dot_general(
                vh, kd, dn_tl, preferred_element_type=F32)

    dir_refs = ((rf_ref, qf_ref, vf_ref, lff_ref, kf_ref, of_ref),
                (rb_ref, qb_ref, vb_ref, lfb_ref, kb_ref, ob_ref))

    def chunk_rows(ci_):
        return pl.multiple_of(ci_ * c, c), pl.multiple_of((n_chunks - 1 - ci_) * c, c)

    if not compute_o:
        def state_body(ci_, carry):
            for dirn, r0 in enumerate(chunk_rows(ci_)):
                _, _, v_ref, lf_ref, k_ref, _ = dir_refs[dirn]
                state_only_chunk(dirn, lf_ref, k_ref, v_ref, r0)
            return carry

        lax.fori_loop(0, n_chunks, state_body, 0)
    else:
        step_id = pl.program_id(0) * pl.num_programs(1) + j
        cur = step_id % 2

        def prepare(lf_refs, slot):
            worst = None
            for dirn in range(2):
                bsum = prefix((trif_ref, trib_ref)[dirn][...], lf_refs[dirn][...])
                bsum_ref[slot, dirn] = bsum
                for ch in range(n_chunks):
                    m = bsum[ch * c + mrow[dirn]:ch * c + mrow[dirn] + 1, :]
                    tot = bsum[ch * c + trow[dirn]:ch * c + trow[dirn] + 1, :]
                    w = jnp.minimum(m, tot - m)
                    worst = w if worst is None else jnp.minimum(worst, w)
            flag_ref[slot] = (jnp.min(worst) >= -SCAN_FAST_LIMIT).astype(jnp.int32)

        @pl.when(step_id == 0)
        def _():
            prepare((lff_ref, lfb_ref), 0)

        fast_ok = flag_ref[cur] == 1

        @pl.when(fast_ok)
        def _():
            states = [st_ref[i] for i in range(2 * HEADS)]
            for step in range(n_chunks):
                for dirn in range(2):
                    ch = step if dirn == 0 else n_chunks - 1 - step
                    _, q_ref, v_ref, _, k_ref, o_ref = dir_refs[dirn]
                    rows = slice(ch * c, (ch + 1) * c)
                    m = bsum_ref[cur, dirn, ch * c + mrow[dirn]:ch * c + mrow[dirn] + 1, :]
                    tot = bsum_ref[cur, dirn, ch * c + trow[dirn]:ch * c + trow[dirn] + 1, :]
                    fast_chunk(dirn, q_ref, v_ref, k_ref, o_ref, rows, bsum_ref[cur, dirn, rows, :],
                               m, tot, states)
            for i in range(2 * HEADS):
                st_ref[i] = states[i]
            prepare((lff_next_ref, lfb_next_ref), 1 - cur)

        @pl.when(jnp.logical_not(fast_ok))
        def _():
            def robust_body(ci_, carry):
                for dirn, r0 in enumerate(chunk_rows(ci_)):
                    r_ref, q_ref, v_ref, lf_ref, k_ref, o_ref = dir_refs[dirn]
                    robust_chunk(dirn, r_ref, q_ref, v_ref, lf_ref, k_ref, o_ref, r0)
                return carry

            prepare((lff_next_ref, lfb_next_ref), 1 - cur)
            lax.fori_loop(0, n_chunks, robust_body, 0)

    @pl.when(j == pl.num_programs(1) - 1)
    def _():
        sfin_ref[0] = st_ref[...]


def _scan(q, v, lff, kf, lfb, kb, s0, *, row_off, seq_len, compute_o):
    batch = s0.shape[0]
    t = min(SCAN_BLOCK, seq_len)
    nt = seq_len // t
    off = row_off // t
    fwd = lambda b, j: (off + b * nt + j, 0)
    bwd = lambda b, j: (off + b * nt + (nt - 1 - j), 0)
    ofwd = lambda b, j: (b * nt + j, 0)
    obwd = lambda b, j: (b * nt + (nt - 1 - j), 0)
    blk = lambda im: pl.BlockSpec((t, D_GROUP), im)
    rf = jnp.asarray(_scan_tables(SCAN_CHUNK, False), BF16)
    rb = jnp.asarray(_scan_tables(SCAN_CHUNK, True), BF16)
    rspec = pl.BlockSpec(rf.shape, lambda b, j: (0, 0))
    sspec = pl.BlockSpec((1, 2 * HEADS, HEAD_DIM, HEAD_DIM), lambda b, j: (b, 0, 0, 0))
    pos_t = np.arange(t)
    same_chunk = (pos_t[:, None] // SCAN_CHUNK) == (pos_t[None, :] // SCAN_CHUNK)
    tri_f = jnp.asarray(same_chunk & (pos_t[None, :] <= pos_t[:, None]), BF16)
    tri_b = jnp.asarray(same_chunk & (pos_t[None, :] >= pos_t[:, None]), BF16)
    tspec = pl.BlockSpec((t, t), lambda b, j: (0, 0))
    in_specs = [rspec, rspec, blk(fwd), blk(fwd), blk(fwd), blk(fwd),
                blk(bwd), blk(bwd), blk(bwd), blk(bwd), sspec, tspec, tspec]
    args = [rf, rb, q, v, lff, kf, q, v, lfb, kb, s0, tri_f, tri_b]
    out_shape, out_specs = [], []
    scratch = [pltpu.VMEM((2 * HEADS, HEAD_DIM, HEAD_DIM), F32)]
    if compute_o:
        def next_step(b, j):
            nxt = jnp.minimum(b * nt + j + 1, batch * nt - 1)
            return nxt // nt, nxt % nt

        in_specs += [blk(lambda b, j: fwd(*next_step(b, j))), blk(lambda b, j: bwd(*next_step(b, j)))]
        args += [lff, lfb]
        out_shape += [jax.ShapeDtypeStruct((batch * seq_len, D_GROUP), BF16)] * 2
        out_specs += [blk(ofwd), blk(obwd)]
        scratch += [pltpu.VMEM((2, 2, t, D_GROUP), F32), pltpu.SMEM((2,), I32)]
    out_shape.append(jax.ShapeDtypeStruct(s0.shape, F32))
    out_specs.append(sspec)
    kern = functools.partial(_scan_kernel, n_chunks=t // SCAN_CHUNK, compute_o=compute_o)
    return pl.pallas_call(
        kern,
        out_shape=out_shape,
        grid=(batch, nt),
        in_specs=in_specs,
        out_specs=out_specs,
        scratch_shapes=scratch,
        compiler_params=_cparams(("arbitrary", "arbitrary"), VMEM_LIMIT),
        name="hgrn_scan",
    )(*args)


def _ln_silu(y, g, b):
    mu = jnp.mean(y, axis=-1, keepdims=True)
    yc = y - mu
    var = jnp.mean(yc * yc, axis=-1, keepdims=True)
    return _silu(yc * lax.rsqrt(var + EPS) * g + b)


def _tree_sum(terms):
    while len(terms) > 1:
        terms = [terms[i] + terms[i + 1] if i + 1 < len(terms) else terms[i]
                 for i in range(0, len(terms), 2)]
    return terms[0]


def _conv_rows_kernel(u_ref, w_ref, b_ref, lg_ref, lb_ref, y_ref, pad_ref, acc_ref, *, seq, tb):
    nseq = tb // seq
    stride = seq + CONV_GAP
    n_pad = pad_ref.shape[2]
    n_lt = D_GROUP // LANES
    for cl in range(n_lt):
        ls = slice(cl * LANES, (cl + 1) * LANES)
        pad_ref[0, cl] = jnp.zeros(pad_ref.shape[2:], F32)
        for s in range(nseq):
            pad_ref[0, cl, CONV_GAP + s * stride:CONV_GAP + s * stride + seq, :] = (
                u_ref[s * seq:(s + 1) * seq, ls].astype(F32))
        for p in range(1, SUBLANES):
            pad_ref[p, cl, 0:n_pad - SUBLANES, :] = pad_ref[0, cl, p:p + n_pad - SUBLANES, :]
    rc = 64
    group = CONV_ROW_GROUP
    assert seq % rc == 0 and tb % (group * rc) == 0
    if seq == rc:
        member_off = [a * stride for a in range(group)]
        group_base = lambda g: g * (group * stride)
    else:
        gps = seq // (group * rc)
        assert seq % (group * rc) == 0
        member_off = [a * rc for a in range(group)]
        group_base = lambda g: (g // gps) * stride + (g % gps) * (group * rc)

    def one_group(g, carry):
        pad0 = pl.multiple_of(CONV_GAP + group_base(g), SUBLANES)
        out0 = pl.multiple_of(g * (group * rc), group * rc)
        for cl in range(n_lt):
            ls = slice(cl * LANES, (cl + 1) * LANES)
            parts = [[None] * CONV_PARTIALS for _ in range(group)]
            for k in range(CONV_K):
                off = k - CONV_HALF
                wk = w_ref[k:k + 1, ls]
                for a in range(group):
                    src = pl.ds(pad0 + member_off[a] + (off // SUBLANES) * SUBLANES, rc)
                    term = wk * pad_ref[off % SUBLANES, cl, src, :]
                    prev = parts[a][k % CONV_PARTIALS]
                    parts[a][k % CONV_PARTIALS] = term if prev is None else prev + term
            for a in range(group):
                acc_ref[pl.ds(out0 + a * rc, rc), ls] = _tree_sum(parts[a])
        return carry

    lax.fori_loop(0, tb // (group * rc), one_group, 0)
    y = acc_ref[...] + b_ref[...]
    y_ref[...] = _ln_silu(y, lg_ref[...], lb_ref[...]).astype(y_ref.dtype)


def _conv_rows(u, w, b, lg, lb, *, seq, row_off, n_rows):
    tb = max(seq, ROW_TILE) if n_rows % max(seq, ROW_TILE) == 0 else seq
    nseq = tb // seq
    off = row_off // tb
    vec = pl.BlockSpec((1, D_GROUP), lambda i: (0, 0))
    in_specs = [pl.BlockSpec((tb, D_GROUP), lambda i: (off + i, 0)),
                pl.BlockSpec((CONV_K, D_GROUP), lambda i: (0, 0)), vec, vec, vec]
    args = [u, w, b, lg, lb]
    kern = functools.partial(_conv_rows_kernel, seq=seq, tb=tb)
    return pl.pallas_call(
        kern,
        out_shape=jax.ShapeDtypeStruct((n_rows, D_GROUP), BF16),
        grid=(n_rows // tb,),
        in_specs=in_specs,
        out_specs=pl.BlockSpec((tb, D_GROUP), lambda i: (i, 0)),
        scratch_shapes=[pltpu.VMEM((SUBLANES, D_GROUP // LANES, CONV_GAP + nseq * (seq + CONV_GAP), LANES), F32),
                        pltpu.VMEM((tb, D_GROUP), F32)],
        compiler_params=_cparams(("arbitrary",), VMEM_LIMIT),
        name="conv_rows",
    )(*args)


CONV_COL_TILE = 16


def _conv_cols_kernel(u_ref, w_ref, b_ref, lg_ref, lb_ref, y_ref, uf_ref, *, n_rows):
    n_lt = D_GROUP // LANES
    row_stride = GRID_W + SUBLANES
    for cl in range(n_lt):
        ls = slice(cl * LANES, (cl + 1) * LANES)
        for r in range(n_rows):
            uf_ref[cl, r * row_stride:r * row_stride + GRID_W, :] = (
                u_ref[r * GRID_W:(r + 1) * GRID_W, ls].astype(F32))
    bias = b_ref[...]
    lg = lg_ref[...]
    lb = lb_ref[...]

    def col_tile(wi, carry):
        w0 = pl.multiple_of(wi * CONV_COL_TILE, CONV_COL_TILE)
        for r in range(n_rows):
            tiles = []
            for cl in range(n_lt):
                ls = slice(cl * LANES, (cl + 1) * LANES)
                parts = [None] * CONV_PARTIALS
                for k in range(CONV_K):
                    rr = r + k - CONV_HALF
                    if 0 <= rr < n_rows:
                        src = pl.ds(pl.multiple_of(rr * row_stride + w0, SUBLANES), CONV_COL_TILE)
                        term = w_ref[k:k + 1, ls] * uf_ref[cl, src, :]
                        prev = parts[k % CONV_PARTIALS]
                        parts[k % CONV_PARTIALS] = term if prev is None else prev + term
                tiles.append(_tree_sum([p for p in parts if p is not None]))
            dst = pl.ds(pl.multiple_of(r * GRID_W + w0, CONV_COL_TILE), CONV_COL_TILE)
            y_ref[dst, :] = _ln_silu(jnp.concatenate(tiles, axis=1) + bias, lg, lb).astype(y_ref.dtype)
        return carry

    lax.fori_loop(0, GRID_W // CONV_COL_TILE, col_tile, 0)


def _conv_cols(u, w, b, lg, lb, *, batch, seq_len):
    vec = pl.BlockSpec((1, D_GROUP), lambda bi: (0, 0))
    blk = pl.BlockSpec((seq_len, D_GROUP), lambda bi: (bi, 0))
    return pl.pallas_call(
        functools.partial(_conv_cols_kernel, n_rows=seq_len // GRID_W),
        out_shape=jax.ShapeDtypeStruct((batch * seq_len, D_GROUP), BF16),
        grid=(batch,),
        in_specs=[blk, pl.BlockSpec((CONV_K, D_GROUP), lambda bi: (0, 0)), vec, vec, vec],
        out_specs=blk,
        scratch_shapes=[pltpu.VMEM((D_GROUP // LANES, (seq_len // GRID_W) * (GRID_W + SUBLANES), LANES), F32)],
        compiler_params=_cparams(("arbitrary",), VMEM_LIMIT),
        name="conv_cols",
    )(u, w, b, lg, lb)


def _outproj_kernel(*refs, n_x, n_mix, n_lat_tiles):
    x = _rows_from(refs[:n_x], n_lat_tiles)
    pos = n_x
    ycx, of, ob = (_rows_from(refs[pos + k * n_mix:pos + (k + 1) * n_mix], n_lat_tiles) for k in range(3))
    og_ref, hg_ref, w_ref, g1_ref = refs[pos + 3 * n_mix:pos + 3 * n_mix + 4]
    route_in = refs[pos + 3 * n_mix + 4:pos + 3 * n_mix + 10]
    o_ref = refs[pos + 3 * n_mix + 10]
    route_out = refs[pos + 3 * n_mix + 11:]
    o = of.astype(F32) + ob.astype(F32)
    og = og_ref[...].astype(F32)
    hg = hg_ref[...]
    acc = jnp.dot(ycx, w_ref[0, 0:D_GROUP, :], preferred_element_type=F32)
    for h in range(HEADS):
        hs = slice(h * HEAD_DIM, (h + 1) * HEAD_DIM)
        oh = o[:, hs]
        r = lax.rsqrt(jnp.mean(oh * oh, axis=-1, keepdims=True) + EPS)
        yh = ((oh * r * hg) * og[:, hs]).astype(BF16)
        acc = acc + jnp.dot(yh, w_ref[0, D_GROUP + h * HEAD_DIM:D_GROUP + (h + 1) * HEAD_DIM, :],
                            preferred_element_type=F32)
    x = x + g1_ref[0] * acc
    o_ref[...] = x
    _route(x, *route_in, *route_out)


def _outproj_route(xs, mix, og, hg, w_bf, layer, g1, g2n, sh2, sc2, rw_t, rb, *, n_rows, tm, tiles_per_seq):
    d = xs[0].shape[1]
    nb = g1.shape[0] - 1
    n_tt = n_rows // tm
    x_specs, n_lat_tiles = _row_specs(xs, tm)
    mix_specs, mix_args = [], []
    for arrays in mix:
        specs, nl = _row_specs(arrays, tm)
        assert len(arrays) == len(mix[0]) and (nl == 0 or n_lat_tiles in (0, nl))
        n_lat_tiles = max(n_lat_tiles, nl)
        mix_specs += specs
        mix_args += list(arrays)
    row = lambda width: pl.BlockSpec((tm, width), lambda i: (i, 0))
    mod_spec = pl.BlockSpec((1, 1, d), lambda i: (jnp.minimum(i // tiles_per_seq, nb), 0, 0))
    pair = pl.BlockSpec((2, tm), lambda i: (0, i))
    kern = functools.partial(_outproj_kernel, n_x=len(xs), n_mix=len(mix[0]), n_lat_tiles=n_lat_tiles)
    before = jnp.asarray(np.arange(tm)[:, None] < np.arange(tm)[None, :], BF16)
    return pl.pallas_call(
        kern,
        out_shape=[jax.ShapeDtypeStruct((n_rows, d), F32),
                   jax.ShapeDtypeStruct((n_rows, d), BF16),
                   jax.ShapeDtypeStruct((2, n_rows), I32),
                   jax.ShapeDtypeStruct((2, n_rows), I32),
                   jax.ShapeDtypeStruct((2, n_rows), F32),
                   jax.ShapeDtypeStruct((n_tt, N_EXPERTS, LANES), I32)],
        grid=(n_tt,),
        in_specs=x_specs + mix_specs + [row(D_GROUP),
                  pl.BlockSpec((1, HEAD_DIM), lambda i: (0, 0)),
                  pl.BlockSpec((1,) + w_bf.shape[1:], lambda i: (layer, 0, 0)),
                  mod_spec,
                  pl.BlockSpec((1, d), lambda i: (0, 0)), mod_spec, mod_spec,
                  pl.BlockSpec((N_EXPERTS, d), lambda i: (0, 0)),
                  pl.BlockSpec((N_EXPERTS, 1), lambda i: (0, 0)),
                  pl.BlockSpec((tm, tm), lambda i: (0, 0))],
        out_specs=[row(d), row(d), pair, pair, pair,
                   pl.BlockSpec((1, N_EXPERTS, LANES), lambda i: (i, 0, 0))],
        compiler_params=_cparams(("arbitrary",), VMEM_LIMIT, fuse_input=len(xs) + len(mix_args) + 2,
                                 n_inputs=len(xs) + len(mix_args) + 10),
        name="outproj_route",
    )(*xs, *mix_args, og, hg, w_bf, g1, g2n, sh2, sc2, rw_t, rb, before)


def _route(x, g_ref, sh_ref, sc_ref, rw_ref, rb_ref, before_ref, h_ref, e_ref, rank_ref, w_ref, cnt_ref):
    tm = x.shape[0]
    h = _modulate(x, g_ref[...], sh_ref[0], sc_ref[0])
    h_ref[...] = h.astype(h_ref.dtype)
    logits = lax.dot_general(rw_ref[...], h, (((1,), (1,)), ((), ())),
                             precision=lax.Precision.HIGHEST, preferred_element_type=F32)
    s = jax.nn.sigmoid(logits)
    sb = s + rb_ref[...]
    s_rows = [s[e:e + 1, :] for e in range(N_EXPERTS)]
    sb_rows = [sb[e:e + 1, :] for e in range(N_EXPERTS)]

    def group_score(g):
        v = sb_rows[g * EXPERTS_PER_GROUP:(g + 1) * EXPERTS_PER_GROUP]
        best = None
        for a in range(EXPERTS_PER_GROUP):
            for b in range(a + 1, EXPERTS_PER_GROUP):
                p = v[a] + v[b]
                best = p if best is None else jnp.maximum(best, p)
        return best

    cur = group_score(0)
    best_g = jnp.zeros(cur.shape, I32)
    for g in range(1, N_GROUPS):
        gs = group_score(g)
        upd = gs > cur
        best_g = jnp.where(upd, g, best_g)
        cur = jnp.where(upd, gs, cur)

    def pick(rows, jdx):
        out = rows[jdx]
        for g in range(1, N_GROUPS):
            out = jnp.where(best_g == g, rows[g * EXPERTS_PER_GROUP + jdx], out)
        return out

    vb = [pick(sb_rows, jdx) for jdx in range(EXPERTS_PER_GROUP)]
    vs = [pick(s_rows, jdx) for jdx in range(EXPERTS_PER_GROUP)]

    def first_argmax(vals):
        m = vals[0]
        for v in vals[1:]:
            m = jnp.maximum(m, v)
        idx = jnp.full(m.shape, EXPERTS_PER_GROUP - 1, I32)
        for jdx in range(EXPERTS_PER_GROUP - 2, -1, -1):
            idx = jnp.where(vals[jdx] == m, jdx, idx)
        return idx

    def take(vals, idx):
        out = vals[EXPERTS_PER_GROUP - 1]
        for jdx in range(EXPERTS_PER_GROUP - 2, -1, -1):
            out = jnp.where(idx == jdx, vals[jdx], out)
        return out

    i0 = first_argmax(vb)
    vb2 = [jnp.where(i0 == jdx, -jnp.inf, vb[jdx]) for jdx in range(EXPERTS_PER_GROUP)]
    i1 = first_argmax(vb2)
    s0 = take(vs, i0)
    s1 = take(vs, i1)
    tot = s0 + s1
    w_ref[0:1, :] = s0 / tot
    w_ref[1:2, :] = s1 / tot
    e0 = best_g * EXPERTS_PER_GROUP + i0
    e1 = best_g * EXPERTS_PER_GROUP + i1
    e_ref[0:1, :] = e0
    e_ref[1:2, :] = e1

    eid = lax.broadcasted_iota(I32, (N_EXPERTS, tm), 0)
    hit0 = eid == e0
    hit1 = eid == e1
    onehot = jnp.where(jnp.logical_or(hit0, hit1), 1.0, 0.0)
    rank = jnp.dot(onehot.astype(BF16), before_ref[...], preferred_element_type=F32)
    rank_ref[0:1, :] = jnp.sum(jnp.where(hit0, rank, 0.0), axis=0, keepdims=True).astype(I32)
    rank_ref[1:2, :] = jnp.sum(jnp.where(hit1, rank, 0.0), axis=0, keepdims=True).astype(I32)
    cnt = jnp.sum(onehot, axis=1, keepdims=True).astype(I32)
    cnt_ref[0] = jnp.broadcast_to(cnt, (N_EXPERTS, LANES))


def _tables_kernel(cnt_ref, seg_start_ref, seg_len_ref, te_ref, misc_ref, *, n_tt, n_et, tm):
    def expert(e, row0):
        def tile(i, pos):
            n = cnt_ref[i * N_EXPERTS + e]
            n8 = ((n + SUBLANES - 1) // SUBLANES) * SUBLANES
            seg_start_ref[i * N_EXPERTS + e] = pos
            seg_len_ref[i * N_EXPERTS + e] = n8
            return pos + n8

        end = lax.fori_loop(0, n_tt, tile, row0)
        padded_end = row0 + ((end - row0 + tm - 1) // tm) * tm
        misc_ref[1 + e] = end
        misc_ref[1 + N_EXPERTS + e] = padded_end - end

        def mark(j, carry):
            te_ref[j] = e
            return carry

        lax.fori_loop(row0 // tm, padded_end // tm, mark, 0)
        return padded_end

    total = row0 = 0
    for e in range(N_EXPERTS):
        row0 = expert(e, row0)
    total = row0
    n_used = total // tm
    misc_ref[0] = n_used
    last = te_ref[jnp.maximum(n_used - 1, 0)]

    def fill(j, carry):
        te_ref[j] = last
        return carry

    lax.fori_loop(n_used, n_et, fill, 0)


def _tables(cnt, *, n_tt, n_et, tm):
    smem = pl.BlockSpec(memory_space=pltpu.SMEM)
    return pl.pallas_call(
        functools.partial(_tables_kernel, n_tt=n_tt, n_et=n_et, tm=tm),
        out_shape=[jax.ShapeDtypeStruct((n_tt * N_EXPERTS,), I32),
                   jax.ShapeDtypeStruct((n_tt * N_EXPERTS,), I32),
                   jax.ShapeDtypeStruct((n_et,), I32),
                   jax.ShapeDtypeStruct((1 + 2 * N_EXPERTS,), I32)],
        in_specs=[smem],
        out_specs=[smem, smem, smem, smem],
        name="route_tables",
    )(cnt)


def _piece_sizes(max_rows):
    sizes, s = [], SUBLANES
    while s <= max_rows:
        sizes.append(s)
        s *= 2
    return sizes[::-1]


def _segment_copies(src_ref, src_row, dst_ref, dst_row, n_rows, sem, sizes, start):
    aligned = lambda r: r if isinstance(r, int) else pl.multiple_of(r, SUBLANES)
    for sz in sizes:
        src = pl.ds(aligned(src_row), sz)
        dst = pl.ds(aligned(dst_row), sz)

        @pl.when((n_rows & sz) != 0)
        def _(src=src, dst=dst):
            cp = pltpu.make_async_copy(src_ref.at[src, :], dst_ref.at[dst, :], sem)
            if start:
                cp.start()
            else:
                cp.wait()

        src_row = src_row + (n_rows & sz)
        dst_row = dst_row + (n_rows & sz)


def _local_slots(seg_len_ref, tile, e_ref, rank_ref):
    offs, lo = [], 0
    for e in range(N_EXPERTS):
        offs.append(lo)
        lo = lo + seg_len_ref[tile * N_EXPERTS + e]
    e01 = e_ref[...]
    slot = rank_ref[...]
    for e in range(N_EXPERTS):
        slot = slot + jnp.where(e01 == e, offs[e], 0)
    return slot, offs


def _compact_rows(tm):
    return ((2 * tm + N_EXPERTS * (SUBLANES - 1) + LANES - 1) // LANES) * LANES


def _dispatch_kernel(seg_start_ref, seg_len_ref, misc_ref, h_ref, e_ref, rank_ref, xs_ref,
                     xc_ref, zero_ref, sems, *, sizes):
    tm = h_ref.shape[0]
    rc = xc_ref.shape[1]
    i = pl.program_id(0)
    buf = i % 2
    slot, _ = _local_slots(seg_len_ref, i, e_ref, rank_ref)
    rid = lax.broadcasted_iota(I32, (rc, tm), 0)
    sel = jnp.logical_or(rid == slot[0:1, :], rid == slot[1:2, :])
    perm = jnp.where(sel, 1.0, 0.0).astype(BF16)
    xc_ref[buf] = jnp.dot(perm, h_ref[...], preferred_element_type=F32)

    def segments(tile, b, start):
        lo = 0
        for e in range(N_EXPERTS):
            n8 = seg_len_ref[tile * N_EXPERTS + e]
            _segment_copies(xc_ref.at[b], lo, xs_ref, seg_start_ref[tile * N_EXPERTS + e], n8,
                            sems.at[b], sizes, start)
            lo = lo + n8

    @pl.when(i > 0)
    def _():
        segments(i - 1, 1 - buf, False)

    segments(i, buf, True)

    @pl.when(i == pl.num_programs(0) - 1)
    def _():
        segments(i, buf, False)
        zero_ref[...] = jnp.zeros(zero_ref.shape, F32)
        for start in (True, False):
            for e in range(N_EXPERTS):
                _segment_copies(zero_ref, 0, xs_ref, misc_ref[1 + e], misc_ref[1 + N_EXPERTS + e],
                                sems.at[0], sizes, start)

        def clear_tile(j, carry):
            cp = pltpu.make_async_copy(zero_ref, xs_ref.at[pl.ds(pl.multiple_of(j * tm, tm), tm), :],
                                       sems.at[0])
            cp.start()
            cp.wait()
            return carry

        lax.fori_loop(misc_ref[0], xs_ref.shape[0] // tm, clear_tile, 0)


def _dispatch(seg_start, seg_len, misc, h, e01, rank01, *, n_sorted, tm):
    n, d = h.shape
    pair = pl.BlockSpec((2, tm), lambda i, *_: (0, i))
    grid_spec = pltpu.PrefetchScalarGridSpec(
        num_scalar_prefetch=3,
        grid=(n // tm,),
        in_specs=[pl.BlockSpec((tm, d), lambda i, *_: (i, 0)), pair, pair],
        out_specs=pl.BlockSpec(memory_space=pl.ANY),
        scratch_shapes=[pltpu.VMEM((2, _compact_rows(tm), d), F32), pltpu.VMEM((tm, d), F32),
                        pltpu.SemaphoreType.DMA((2,))],
    )
    return pl.pallas_call(
        functools.partial(_dispatch_kernel, sizes=_piece_sizes(tm)),
        out_shape=jax.ShapeDtypeStruct((n_sorted, d), F32),
        grid_spec=grid_spec,
        compiler_params=_cparams(("arbitrary",), VMEM_LIMIT),
        name="dispatch",
    )(seg_start, seg_len, misc, h, e01, rank01)


def _moe_kernel(te_ref, misc_ref, xs_ref, wg_hbm, wu_hbm, wd_hbm, ys_ref,
                wg_buf, wu_buf, wd_buf, wgb_ref, wub_ref, wdb_ref, run_ref, sems, *, layer, tm):
    i = pl.program_id(0)
    e = te_ref[i]
    prev = te_ref[jnp.maximum(i - 1, 0)]
    n_used = misc_ref[0]

    def fetch(expert, slot):
        return [pltpu.make_async_copy(w.at[layer, expert], buf.at[slot], sems.at[slot, k])
                for k, (w, buf) in enumerate(((wg_hbm, wg_buf), (wu_hbm, wu_buf), (wd_hbm, wd_buf)))]

    @pl.when(i == 0)
    def _():
        run_ref[0] = 0
        for cp in fetch(e, 0):
            cp.start()

    @pl.when(jnp.logical_or(i == 0, e != prev))
    def _():
        slot = run_ref[0] % 2
        for cp in fetch(e, slot):
            cp.wait()
        nxt = (misc_ref[1 + e] + misc_ref[1 + N_EXPERTS + e]) // tm

        @pl.when(nxt < n_used)
        def _():
            for cp in fetch(te_ref[nxt], 1 - slot):
                cp.start()

        wgb_ref[...] = wg_buf[slot].astype(BF16)
        wub_ref[...] = wu_buf[slot].astype(BF16)
        wdb_ref[...] = wd_buf[slot].astype(BF16)
        run_ref[0] = run_ref[0] + 1

    @pl.when(i < misc_ref[0])
    def _():
        x = xs_ref[...].astype(BF16)
        gate = jnp.dot(x, wgb_ref[...], preferred_element_type=F32)
        up = jnp.dot(x, wub_ref[...], preferred_element_type=F32)
        act = (_silu(gate) * up).astype(BF16)
        ys_ref[...] = jnp.dot(act, wdb_ref[...], preferred_element_type=F32)

    @pl.when(i >= misc_ref[0])
    def _():
        ys_ref[...] = jnp.zeros(ys_ref.shape, F32)


def _moe(tile_expert, misc, xs, wg, wu, wd, layer, tm):
    n_sorted, d = xs.shape
    de = wg.shape[-1]
    xmap = lambda i, te, misc_ref: (jnp.minimum(i, jnp.maximum(misc_ref[0] - 1, 0)), 0)
    hbm = pl.BlockSpec(memory_space=pl.ANY)
    grid_spec = pltpu.PrefetchScalarGridSpec(
        num_scalar_prefetch=2,
        grid=(n_sorted // tm,),
        in_specs=[pl.BlockSpec((tm, d), xmap), hbm, hbm, hbm],
        out_specs=pl.BlockSpec((tm, d), lambda i, te, misc_ref: (i, 0)),
        scratch_shapes=[pltpu.VMEM((2, d, de), F32), pltpu.VMEM((2, d, de), F32),
                        pltpu.VMEM((2, de, d), F32),
                        pltpu.VMEM((d, de), BF16), pltpu.VMEM((d, de), BF16),
                        pltpu.VMEM((de, d), BF16),
                        pltpu.SMEM((1,), I32), pltpu.SemaphoreType.DMA((2, 3))],
    )
    return pl.pallas_call(
        functools.partial(_moe_kernel, layer=layer, tm=tm),
        out_shape=jax.ShapeDtypeStruct((n_sorted, d), F32),
        grid_spec=grid_spec,
        compiler_params=_cparams(("arbitrary",), VMEM_LIMIT),
        name="moe_experts",
    )(tile_expert, misc, xs, wg, wu, wd)


def _combine_kernel(seg_start_ref, seg_len_ref, x_ref, g2_ref, e_ref, rank_ref, w_ref, fg_ref, ys_ref,
                    o_ref, yc_ref, sems, *, sizes, final_norm):
    tm = x_ref.shape[0]
    rc = yc_ref.shape[1]
    i = pl.program_id(0)
    buf = i % 2

    def segments(tile, b, start):
        lo = 0
        for e in range(N_EXPERTS):
            n8 = seg_len_ref[tile * N_EXPERTS + e]
            _segment_copies(ys_ref, seg_start_ref[tile * N_EXPERTS + e], yc_ref.at[b], lo, n8,
                            sems.at[b], sizes, start)
            lo = lo + n8

    @pl.when(i == 0)
    def _():
        yc_ref[...] = jnp.zeros(yc_ref.shape, F32)
        segments(0, 0, True)

    @pl.when(i + 1 < pl.num_programs(0))
    def _():
        segments(i + 1, 1 - buf, True)

    segments(i, buf, False)
    slot, _ = _local_slots(seg_len_ref, i, e_ref, rank_ref)
    rid = lax.broadcasted_iota(I32, (rc, tm), 0)
    w = w_ref[...]
    pw = (jnp.where(rid == slot[0:1, :], w[0:1, :], 0.0)
          + jnp.where(rid == slot[1:2, :], w[1:2, :], 0.0))
    dn = (((0,), (0,)), ((), ()))
    out = lax.dot_general(pw.astype(BF16), yc_ref[buf].astype(BF16), dn, preferred_element_type=F32)
    x = x_ref[...] + g2_ref[0] * out
    if final_norm:
        x = x * lax.rsqrt(jnp.mean(x * x, axis=-1, keepdims=True) + EPS) * fg_ref[...]
    o_ref[...] = x


def _combine(seg_start, seg_len, xc, g2, e01, rank01, w01, fg, ys, *, n_rows, tm, tiles_per_seq,
             final_norm):
    d = xc.shape[1]
    nb = g2.shape[0] - 1
    pair = pl.BlockSpec((2, tm), lambda i, *_: (0, i))
    grid_spec = pltpu.PrefetchScalarGridSpec(
        num_scalar_prefetch=2,
        grid=(n_rows // tm,),
        in_specs=[pl.BlockSpec((tm, d), lambda i, *_: (i, 0)),
                  pl.BlockSpec((1, 1, d), lambda i, *_: (jnp.minimum(i // tiles_per_seq, nb), 0, 0)),
                  pair, pair, pair,
                  pl.BlockSpec((1, d), lambda i, *_: (0, 0)),
                  pl.BlockSpec(memory_space=pl.ANY)],
        out_specs=pl.BlockSpec((tm, d), lambda i, *_: (i, 0)),
        scratch_shapes=[pltpu.VMEM((2, _compact_rows(tm), d), F32), pltpu.SemaphoreType.DMA((2,))],
    )
    return pl.pallas_call(
        functools.partial(_combine_kernel, sizes=_piece_sizes(tm), final_norm=final_norm),
        out_shape=jax.ShapeDtypeStruct((n_rows, d), F32),
        grid_spec=grid_spec,
        compiler_params=_cparams(("arbitrary",), VMEM_LIMIT),
        name="combine",
    )(seg_start, seg_len, xc, g2, e01, rank01, w01, fg, ys)


def _lower_bounds(lb_param):
    p = jax.nn.softmax(lb_param.astype(F32), axis=0)
    return jnp.cumsum(p, axis=0) - p[0]


def kernel(x, c, ctx, c_ctx, w_ada, b_ada, norm1_g, norm2_g, w_in, conv_w, conv_b, conv_ln_g,
           conv_ln_b, lb_fwd, lb_bwd, hgrn_norm_g, w_out, router_w, router_bias, w_gate, w_up,
           w_down, final_norm_g):
    bn, seq, d = x.shape
    ctx_len = ctx.shape[1]
    depth = w_ada.shape[0]
    n = bn * seq
    nc = bn * ctx_len
    tm = min(ROW_TILE, seq, nc)
    assert seq % tm == 0 and nc % tm == 0 and seq % GRID_W == 0
    tps = seq // tm

    pad_rows = (-(bn + 1)) % 8
    cc = jnp.concatenate([c, c_ctx[None, :], jnp.zeros((pad_rows, d), F32)], axis=0)
    mod = _ada(cc, w_ada, b_ada)[:, :bn + 1]
    mod = mod.reshape(depth, bn + 1, 6, 1, d)

    lbs_f = _lower_bounds(lb_fwd)
    lbs_b = _lower_bounds(lb_bwd)
    w_in_bf = w_in.astype(BF16)
    w_out_bf = w_out.astype(BF16)
    rw_t = router_w.T.astype(F32)
    rb = router_bias.reshape(N_EXPERTS, 1).astype(F32)
    hg = hgrn_norm_g.astype(F32)
    fg = final_norm_g.reshape(1, d).astype(F32)

    rows_in = [x.reshape(n, d), ctx.reshape(nc, d)]
    s_zero = jnp.zeros((bn, 2 * HEADS, HEAD_DIM, HEAD_DIM), F32)

    for l in range(depth):
        last = l == depth - 1
        sh1, sc1, g1, sh2, sc2, g2 = (mod[l, :, k] for k in range(6))
        row = lambda a: a.reshape(1, -1).astype(F32)

        u, q, iv, lff, kf, lfb, kb, og = _inproj(
            rows_in, row(norm1_g[l]), sh1, sc1, w_in_bf, l, row(lbs_f[l]), row(lbs_b[l]), tm, tps)

        cw, cb = conv_w[l].astype(F32), row(conv_b[l])
        clg, clb = row(conv_ln_g[l]), row(conv_ln_b[l])
        scan = functools.partial(_scan, q, iv, lff, kf, lfb, kb)
        if last:
            (s_ctx,) = scan(s_zero, row_off=n, seq_len=ctx_len, compute_o=False)
            mix_ctx = [[], [], []]
            n_rows = n
        else:
            of_c, ob_c, s_ctx = scan(s_zero, row_off=n, seq_len=ctx_len, compute_o=True)
            ycx_c = _conv_rows(u, cw, cb, clg, clb, seq=ctx_len, row_off=n, n_rows=nc)
            mix_ctx = [[ycx_c], [of_c], [ob_c]]
            n_rows = n + nc
        of, ob, _ = scan(s_ctx, row_off=0, seq_len=seq, compute_o=True)
        if l % 2 == 0:
            ycx = _conv_rows(u, cw, cb, clg, clb, seq=GRID_W, row_off=0, n_rows=n)
        else:
            ycx = _conv_cols(u, cw, cb, clg, clb, batch=bn, seq_len=seq)
        mix = [[a] + c_ for a, c_ in zip((ycx, of, ob), mix_ctx)]

        res_rows = rows_in if (len(rows_in) == 1 or not last) else [rows_in[0]]
        xc, h2, e01, rank01, w01, cnt = _outproj_route(
            res_rows, mix, og, hg[l].reshape(1, HEAD_DIM), w_out_bf, l, g1,
            row(norm2_g[l]), sh2, sc2, rw_t, rb, n_rows=n_rows, tm=tm, tiles_per_seq=tps)
        n_tt = n_rows // tm
        n_et = (2 * n_rows + n_tt * N_EXPERTS * (SUBLANES - 1) + tm - 1) // tm + N_EXPERTS
        seg_start, seg_len, te, misc = _tables(cnt[:, :, 0].reshape(-1), n_tt=n_tt, n_et=n_et, tm=tm)
        xs = _dispatch(seg_start, seg_len, misc, h2, e01, rank01, n_sorted=n_et * tm, tm=tm)
        ys = _moe(te, misc, xs, w_gate, w_up, w_down, l, tm)
        xc = _combine(seg_start, seg_len, xc, g2, e01, rank01, w01, fg, ys, n_rows=n_rows, tm=tm,
                      tiles_per_seq=tps, final_norm=last)
        rows_in = [xc]

    return xc[:n].reshape(bn, seq, d)
```

```python
import functools

import numpy as np
import jax
import jax.numpy as jnp
from jax import lax
from jax.experimental import pallas as pl
from jax.experimental.pallas import tpu as pltpu

F32 = jnp.float32
BF16 = jnp.bfloat16
I32 = jnp.int32

EPS = 1e-6
GRID_W = 64
HEADS = 4
HEAD_DIM = 128
D_GROUP = HEADS * HEAD_DIM
N_IN_GROUPS = 7
CONV_K = 31
CONV_HALF = CONV_K // 2
CONV_GAP = 16
CONV_ROW_GROUP = 1
CONV_PARTIALS = 2
N_EXPERTS = 16
N_GROUPS = 4
EXPERTS_PER_GROUP = N_EXPERTS // N_GROUPS
SCAN_CHUNK = 64
SCAN_BLOCK = 256
ROW_TILE = 512
SCAN_FAST_LIMIT = 60.0
LANES = 128
SUBLANES = 8
VMEM_LIMIT = 56 * 1024 * 1024


def _cparams(sem, vmem=None):
    return pltpu.CompilerParams(dimension_semantics=sem, vmem_limit_bytes=vmem)


def _sigmoid(x):
    return 0.5 * jnp.tanh(0.5 * x) + 0.5


def _silu(x):
    return x * _sigmoid(x)


def _ada_kernel(c_ref, w_ref, b_ref, o_ref):
    a = _silu(c_ref[...]).astype(BF16)
    w = w_ref[0].astype(BF16)
    o_ref[0] = jnp.dot(a, w, preferred_element_type=F32) + b_ref[0]


def _ada(cc, w_ada, b_ada, tn=1536):
    depth, d, d6 = w_ada.shape
    rows = cc.shape[0]
    return pl.pallas_call(
        _ada_kernel,
        out_shape=jax.ShapeDtypeStruct((depth, rows, d6), F32),
        grid=(depth, d6 // tn),
        in_specs=[
            pl.BlockSpec((rows, d), lambda l, j: (0, 0)),
            pl.BlockSpec((1, d, tn), lambda l, j: (l, 0, j)),
            pl.BlockSpec((1, 1, tn), lambda l, j: (l, 0, j)),
        ],
        out_specs=pl.BlockSpec((1, rows, tn), lambda l, j: (l, 0, j)),
        compiler_params=_cparams(("arbitrary", "arbitrary"), VMEM_LIMIT),
        name="ada",
    )(cc, w_ada, b_ada.reshape(depth, 1, d6))


def _modulate(x, g, shift, scale):
    r = lax.rsqrt(jnp.mean(x * x, axis=-1, keepdims=True) + EPS)
    return (x * r * g) * (1.0 + scale) + shift


def _rows_from(refs, n_lat_tiles):
    if len(refs) == 1:
        return refs[0][...]
    return jnp.where(pl.program_id(0) >= n_lat_tiles, refs[1][...], refs[0][...])


def _row_specs(arrays, tm):
    width = arrays[0].shape[1]
    if len(arrays) == 1:
        return [pl.BlockSpec((tm, width), lambda i, *_: (i, 0))], 0
    n_lat_tiles = arrays[0].shape[0] // tm
    return [pl.BlockSpec((tm, width), lambda i, *_: (jnp.minimum(i, n_lat_tiles - 1), 0)),
            pl.BlockSpec((tm, width), lambda i, *_: (jnp.maximum(i - n_lat_tiles, 0), 0))], n_lat_tiles


def _inproj_kernel(*refs, n_src, n_lat_tiles):
    x = _rows_from(refs[:n_src], n_lat_tiles)
    (g_ref, sh_ref, sc_ref, w_ref, lbf_ref, lbb_ref,
     u_ref, q_ref, i_ref, lff_ref, kf_ref, lfb_ref, kb_ref, og_ref) = refs[n_src:]
    hb = _modulate(x, g_ref[...], sh_ref[0], sc_ref[0]).astype(BF16)

    def proj(j):
        return jnp.dot(hb, w_ref[0, :, j * D_GROUP:(j + 1) * D_GROUP], preferred_element_type=F32)

    u_ref[...] = (proj(0) * _sigmoid(proj(1))).astype(BF16)
    q_ref[...] = _silu(proj(2)).astype(BF16)
    i_ref[...] = proj(3).astype(BF16)
    for j, lb_ref, lf_ref, k_ref in ((4, lbf_ref, lff_ref, kf_ref), (5, lbb_ref, lfb_ref, kb_ref)):
        lb = lb_ref[...]
        f = lb + (1.0 - lb) * jax.nn.sigmoid(proj(j))
        lf_ref[...] = jnp.log(f)
        k_ref[...] = (1.0 - f).astype(BF16)
    og_ref[...] = _silu(proj(6)).astype(BF16)


def _inproj(xs, g, sh, sc, w_bf, layer, lbf, lbb, tm, tiles_per_seq):
    n = sum(a.shape[0] for a in xs)
    d = xs[0].shape[1]
    x_specs, n_lat_tiles = _row_specs(xs, tm)
    nb = sh.shape[0] - 1
    mod_spec = pl.BlockSpec((1, 1, d), lambda i: (jnp.minimum(i // tiles_per_seq, nb), 0, 0))
    row_spec = pl.BlockSpec((tm, D_GROUP), lambda i: (i, 0))
    vec_spec = pl.BlockSpec((1, D_GROUP), lambda i: (0, 0))
    outs = [jax.ShapeDtypeStruct((n, D_GROUP), dt)
            for dt in (BF16, BF16, BF16, F32, BF16, F32, BF16, BF16)]
    return pl.pallas_call(
        functools.partial(_inproj_kernel, n_src=len(xs), n_lat_tiles=n_lat_tiles),
        out_shape=outs,
        grid=(n // tm,),
        in_specs=x_specs + [
            pl.BlockSpec((1, d), lambda i: (0, 0)),
            mod_spec, mod_spec,
            pl.BlockSpec((1, d, N_IN_GROUPS * D_GROUP), lambda i: (layer, 0, 0)),
            vec_spec, vec_spec,
        ],
        out_specs=[row_spec] * 8,
        compiler_params=_cparams(("arbitrary",), VMEM_LIMIT),
        name="inproj",
    )(*xs, g, sh, sc, w_bf, lbf, lbb)


def _scan_tables(c, backward):
    levels = int(np.log2(c))
    r_all = np.zeros((levels * c + 2 * c + 8, c), np.float32)
    for lev in range(levels):
        h = 1 << lev
        for r in range(c):
            bd = (r // (2 * h)) * 2 * h + h
            if not backward:
                if r >= bd:
                    r_all[lev * c + r, bd:r + 1] = 1.0
                else:
                    r_all[lev * c + r, r + 1:bd] = 1.0
            else:
                if r < bd:
                    r_all[lev * c + r, r:bd] = 1.0
                else:
                    r_all[lev * c + r, bd:r] = 1.0
    base = levels * c
    for r in range(c):
        if not backward:
            r_all[base + r, :r + 1] = 1.0
            r_all[base + c + r, r + 1:] = 1.0
        else:
            r_all[base + r, r:] = 1.0
            r_all[base + c + r, :r] = 1.0
    r_all[base + 2 * c:, :] = 1.0
    return r_all


def _scan_kernel(*refs, n_chunks, compute_o):
    c = SCAN_CHUNK
    levels = int(np.log2(c))
    (rf_ref, rb_ref, qf_ref, vf_ref, lff_ref, kf_ref,
     qb_ref, vb_ref, lfb_ref, kb_ref, s0_ref, trif_ref, trib_ref) = refs[:13]
    pos = 13
    if compute_o:
        lff_next_ref, lfb_next_ref = refs[pos:pos + 2]
        of_ref, ob_ref, sfin_ref, st_ref, bsum_ref, flag_ref = refs[pos + 2:pos + 8]
    else:
        sfin_ref, st_ref = refs[pos:pos + 2]
        of_ref = ob_ref = None
    j = pl.program_id(1)

    @pl.when(j == 0)
    def _():
        st_ref[...] = s0_ref[0]

    row = lax.broadcasted_iota(jnp.int32, (c, 1), 0)
    ri = lax.broadcasted_iota(jnp.int32, (c, c), 0)
    ci = lax.broadcasted_iota(jnp.int32, (c, c), 1)
    upper = [(row & (2 * (1 << lev) - 1)) >= (1 << lev) for lev in range(levels)]
    same_parent = [(ri >> (lev + 1)) == (ci >> (lev + 1)) for lev in range(levels)]
    diag = ri == ci
    dn_t = (((1,), (1,)), ((), ()))
    dn_tl = (((0,), (0,)), ((), ()))

    half = c // 2
    tri = [jnp.where(ci <= ri, 1.0, 0.0).astype(BF16), jnp.where(ci >= ri, 1.0, 0.0).astype(BF16)]
    first_half = [row < half, row >= half]
    re = lax.broadcasted_iota(jnp.int32, (c, 4 * c), 0)
    ce = lax.broadcasted_iota(jnp.int32, (c, 4 * c), 1)
    cs = ce & (c - 1)
    same_half = (re >= half) == (cs >= half)
    own = (ce & (2 * c - 1)) < c
    mask_ext = [
        (own & same_half & (cs <= re)) | (jnp.logical_not(own) & (re >= half) & (cs < half)),
        (own & same_half & (cs >= re)) | (jnp.logical_not(own) & (re < half) & (cs >= half)),
    ]

    mrow, trow = [half - 1, half], [c - 1, 0]

    def prefix(tri_mat, g):
        g_hi = g.astype(BF16)
        g_lo = (g - g_hi.astype(F32)).astype(BF16)
        return (jnp.dot(tri_mat, g_hi, preferred_element_type=F32)
                + jnp.dot(tri_mat, g_lo, preferred_element_type=F32))

    def next_state(st, hs, kh, vh, b, tot):
        kd = (kh * jnp.exp(tot[:, hs] - b[:, hs])).astype(BF16)
        return st * jnp.exp(tot[:, hs]) + lax.dot_general(vh, kd, dn_tl, preferred_element_type=F32)

    def state_only_chunk(dirn, lf_ref, k_ref, v_ref, r0):
        rows = pl.ds(r0, c)
        b = prefix(tri[dirn], lf_ref[rows, :])
        tot = b[trow[dirn]:trow[dirn] + 1, :]
        for h in range(HEADS):
            hs = slice(h * HEAD_DIM, (h + 1) * HEAD_DIM)
            st_ref[dirn * HEADS + h] = next_state(st_ref[dirn * HEADS + h], hs,
                                                  k_ref[rows, hs].astype(F32), v_ref[rows, hs], b, tot)

    def fast_chunk(dirn, q_ref, v_ref, k_ref, o_ref, rows, b, m, tot, states):
        fh = first_half[dirn]
        hd = HEAD_DIM

        def block_diag(a, bb):
            za = jnp.zeros((a.shape[0], bb.shape[1]), a.dtype)
            zb = jnp.zeros((bb.shape[0], a.shape[1]), a.dtype)
            return jnp.concatenate([jnp.concatenate([a, za], axis=1),
                                    jnp.concatenate([zb, bb], axis=1)], axis=0)

        for hp in range(HEADS // 2):
            ps = slice(2 * hp * hd, 2 * (hp + 1) * hd)
            bp, mp, tp = b[:, ps], m[:, ps], tot[:, ps]
            cdec = bp - jnp.where(fh, 0.0, mp)
            e_own = jnp.exp(cdec).astype(BF16)
            e_own_inv = jnp.exp(-cdec).astype(BF16)
            e_cross = jnp.where(fh, jnp.exp(jnp.minimum(mp - bp, 0.0)), 0.0).astype(BF16)
            e_inc = jnp.exp(bp).astype(BF16)
            e_dec = jnp.exp(tp - bp).astype(BF16)
            qp, kp, vp = q_ref[rows, ps], k_ref[rows, ps], v_ref[rows, ps]
            q1, qi = qp * e_own, qp * e_inc
            k1, k2, kd = kp * e_own_inv, kp * e_cross, kp * e_dec
            kcat = [jnp.concatenate([k1[:, s_], k2[:, s_]], axis=0) for s_ in (slice(0, hd), slice(hd, 2 * hd))]
            sc = lax.dot_general(q1, block_diag(*kcat), dn_t, preferred_element_type=F32)
            p = jnp.where(mask_ext[dirn], sc, 0.0).astype(BF16)
            v2 = [jnp.concatenate([vp[:, s_], vp[:, s_]], axis=0) for s_ in (slice(0, hd), slice(hd, 2 * hd))]
            st = [states[dirn * HEADS + 2 * hp + a] for a in range(2)]
            o = (jnp.dot(p, block_diag(*v2), preferred_element_type=F32)
                 + lax.dot_general(qi, block_diag(st[0].astype(BF16), st[1].astype(BF16)), dn_t,
                                   preferred_element_type=F32))
            o_ref[rows, ps] = o.astype(o_ref.dtype)
            upd = lax.dot_general(vp, kd, dn_tl, preferred_element_type=F32)
            e_tot = jnp.exp(tp)
            for a in range(2):
                sl = slice(a * hd, (a + 1) * hd)
                states[dirn * HEADS + 2 * hp + a] = st[a] * e_tot[:, sl] + upd[sl, sl]

    def robust_chunk(dirn, r_ref, q_ref, v_ref, lf_ref, k_ref, o_ref, r0):
        rows = pl.ds(r0, c)
        g = lf_ref[rows, :]
        g_hi = g.astype(BF16)
        g_lo = (g - g_hi.astype(F32)).astype(BF16)
        rmat = r_ref[...]
        e_all = jnp.exp(jnp.dot(rmat, g_hi, preferred_element_type=F32)
                        + jnp.dot(rmat, g_lo, preferred_element_type=F32))
        base = levels * c
        for h in range(HEADS):
            hs = slice(h * HEAD_DIM, (h + 1) * HEAD_DIM)
            qh = q_ref[rows, hs].astype(F32)
            kh = k_ref[rows, hs].astype(F32)
            vh = v_ref[rows, hs]
            st = st_ref[dirn * HEADS + h]
            if compute_o:
                att = jnp.where(diag, lax.dot_general(qh.astype(BF16), kh.astype(BF16), dn_t,
                                                      preferred_element_type=F32), 0.0)
                for lev in range(levels):
                    e_l = e_all[lev * c:(lev + 1) * c, hs]
                    q_side = upper[lev] if dirn == 0 else jnp.logical_not(upper[lev])
                    qt = jnp.where(q_side, qh * e_l, 0.0).astype(BF16)
                    kt = jnp.where(q_side, 0.0, kh * e_l).astype(BF16)
                    a_l = lax.dot_general(qt, kt, dn_t, preferred_element_type=F32)
                    att = att + jnp.where(same_parent[lev], a_l, 0.0)
                qi = (qh * e_all[base:base + c, hs]).astype(BF16)
                o = lax.dot_general(qi, st.astype(BF16), dn_t, preferred_element_type=F32)
                o = o + jnp.dot(att.astype(BF16), vh, preferred_element_type=F32)
                o_ref[rows, hs] = o.astype(o_ref.dtype)
            kd = (kh * e_all[base + c:base + 2 * c, hs]).astype(BF16)
            e_tot = e_all[base + 2 * c:base + 2 * c + 1, hs]
            st_ref[dirn * HEADS + h] = st * e_tot + lax.dot_general(
                vh, kd, dn_tl, preferred_element_type=F32)

    dir_refs = ((rf_ref, qf_ref, vf_ref, lff_ref, kf_ref, of_ref),
                (rb_ref, qb_ref, vb_ref, lfb_ref, kb_ref, ob_ref))

    def chunk_rows(ci_):
        return pl.multiple_of(ci_ * c, c), pl.multiple_of((n_chunks - 1 - ci_) * c, c)

    if not compute_o:
        def state_body(ci_, carry):
            for dirn, r0 in enumerate(chunk_rows(ci_)):
                _, _, v_ref, lf_ref, k_ref, _ = dir_refs[dirn]
                state_only_chunk(dirn, lf_ref, k_ref, v_ref, r0)
            return carry

        lax.fori_loop(0, n_chunks, state_body, 0)
    else:
        step_id = pl.program_id(0) * pl.num_programs(1) + j
        cur = step_id % 2

        def prepare(lf_refs, slot):
            worst = None
            for dirn in range(2):
                bsum = prefix((trif_ref, trib_ref)[dirn][...], lf_refs[dirn][...])
                bsum_ref[slot, dirn] = bsum
                for ch in range(n_chunks):
                    m = bsum[ch * c + mrow[dirn]:ch * c + mrow[dirn] + 1, :]
                    tot = bsum[ch * c + trow[dirn]:ch * c + trow[dirn] + 1, :]
                    w = jnp.minimum(m, tot - m)
                    worst = w if worst is None else jnp.minimum(worst, w)
            flag_ref[slot] = (jnp.min(worst) >= -SCAN_FAST_LIMIT).astype(jnp.int32)

        @pl.when(step_id == 0)
        def _():
            prepare((lff_ref, lfb_ref), 0)

        fast_ok = flag_ref[cur] == 1

        @pl.when(fast_ok)
        def _():
            states = [st_ref[i] for i in range(2 * HEADS)]
            for step in range(n_chunks):
                for dirn in range(2):
                    ch = step if dirn == 0 else n_chunks - 1 - step
                    _, q_ref, v_ref, _, k_ref, o_ref = dir_refs[dirn]
                    rows = slice(ch * c, (ch + 1) * c)
                    m = bsum_ref[cur, dirn, ch * c + mrow[dirn]:ch * c + mrow[dirn] + 1, :]
                    tot = bsum_ref[cur, dirn, ch * c + trow[dirn]:ch * c + trow[dirn] + 1, :]
                    fast_chunk(dirn, q_ref, v_ref, k_ref, o_ref, rows, bsum_ref[cur, dirn, rows, :],
                               m, tot, states)
            for i in range(2 * HEADS):
                st_ref[i] = states[i]
            prepare((lff_next_ref, lfb_next_ref), 1 - cur)

        @pl.when(jnp.logical_not(fast_ok))
        def _():
            def robust_body(ci_, carry):
                for dirn, r0 in enumerate(chunk_rows(ci_)):
                    r_ref, q_ref, v_ref, lf_ref, k_ref, o_ref = dir_refs[dirn]
                    robust_chunk(dirn, r_ref, q_ref, v_ref, lf_ref, k_ref, o_ref, r0)
                return carry

            prepare((lff_next_ref, lfb_next_ref), 1 - cur)
            lax.fori_loop(0, n_chunks, robust_body, 0)

    @pl.when(j == pl.num_programs(1) - 1)
    def _():
        sfin_ref[0] = st_ref[...]


def _scan(q, v, lff, kf, lfb, kb, s0, *, row_off, seq_len, compute_o):
    batch = s0.shape[0]
    t = min(SCAN_BLOCK, seq_len)
    nt = seq_len // t
    off = row_off // t
    fwd = lambda b, j: (off + b * nt + j, 0)
    bwd = lambda b, j: (off + b * nt + (nt - 1 - j), 0)
    ofwd = lambda b, j: (b * nt + j, 0)
    obwd = lambda b, j: (b * nt + (nt - 1 - j), 0)
    blk = lambda im: pl.BlockSpec((t, D_GROUP), im)
    rf = jnp.asarray(_scan_tables(SCAN_CHUNK, False), BF16)
    rb = jnp.asarray(_scan_tables(SCAN_CHUNK, True), BF16)
    rspec = pl.BlockSpec(rf.shape, lambda b, j: (0, 0))
    sspec = pl.BlockSpec((1, 2 * HEADS, HEAD_DIM, HEAD_DIM), lambda b, j: (b, 0, 0, 0))
    pos_t = np.arange(t)
    same_chunk = (pos_t[:, None] // SCAN_CHUNK) == (pos_t[None, :] // SCAN_CHUNK)
    tri_f = jnp.asarray(same_chunk & (pos_t[None, :] <= pos_t[:, None]), BF16)
    tri_b = jnp.asarray(same_chunk & (pos_t[None, :] >= pos_t[:, None]), BF16)
    tspec = pl.BlockSpec((t, t), lambda b, j: (0, 0))
    in_specs = [rspec, rspec, blk(fwd), blk(fwd), blk(fwd), blk(fwd),
                blk(bwd), blk(bwd), blk(bwd), blk(bwd), sspec, tspec, tspec]
    args = [rf, rb, q, v, lff, kf, q, v, lfb, kb, s0, tri_f, tri_b]
    out_shape, out_specs = [], []
    scratch = [pltpu.VMEM((2 * HEADS, HEAD_DIM, HEAD_DIM), F32)]
    if compute_o:
        def next_step(b, j):
            nxt = jnp.minimum(b * nt + j + 1, batch * nt - 1)
            return nxt // nt, nxt % nt

        in_specs += [blk(lambda b, j: fwd(*next_step(b, j))), blk(lambda b, j: bwd(*next_step(b, j)))]
        args += [lff, lfb]
        out_shape += [jax.ShapeDtypeStruct((batch * seq_len, D_GROUP), BF16)] * 2
        out_specs += [blk(ofwd), blk(obwd)]
        scratch += [pltpu.VMEM((2, 2, t, D_GROUP), F32), pltpu.SMEM((2,), I32)]
    out_shape.append(jax.ShapeDtypeStruct(s0.shape, F32))
    out_specs.append(sspec)
    kern = functools.partial(_scan_kernel, n_chunks=t // SCAN_CHUNK, compute_o=compute_o)
    return pl.pallas_call(
        kern,
        out_shape=out_shape,
        grid=(batch, nt),
        in_specs=in_specs,
        out_specs=out_specs,
        scratch_shapes=scratch,
        compiler_params=_cparams(("arbitrary", "arbitrary"), VMEM_LIMIT),
        name="hgrn_scan",
    )(*args)


def _ln_silu(y, g, b):
    mu = jnp.mean(y, axis=-1, keepdims=True)
    yc = y - mu
    var = jnp.mean(yc * yc, axis=-1, keepdims=True)
    return _silu(yc * lax.rsqrt(var + EPS) * g + b)


def _tree_sum(terms):
    while len(terms) > 1:
        terms = [terms[i] + terms[i + 1] if i + 1 < len(terms) else terms[i]
                 for i in range(0, len(terms), 2)]
    return terms[0]


def _conv_rows_kernel(u_ref, w_ref, b_ref, lg_ref, lb_ref, y_ref, pad_ref, acc_ref, *, seq, tb):
    nseq = tb // seq
    stride = seq + CONV_GAP
    n_pad = pad_ref.shape[2]
    n_lt = D_GROUP // LANES
    for cl in range(n_lt):
        ls = slice(cl * LANES, (cl + 1) * LANES)
        pad_ref[0, cl] = jnp.zeros(pad_ref.shape[2:], F32)
        for s in range(nseq):
            pad_ref[0, cl, CONV_GAP + s * stride:CONV_GAP + s * stride + seq, :] = (
                u_ref[s * seq:(s + 1) * seq, ls].astype(F32))
        for p in range(1, SUBLANES):
            pad_ref[p, cl, 0:n_pad - SUBLANES, :] = pad_ref[0, cl, p:p + n_pad - SUBLANES, :]
    rc = 64
    group = CONV_ROW_GROUP
    assert seq % rc == 0 and tb % (group * rc) == 0
    if seq == rc:
        member_off = [a * stride for a in range(group)]
        group_base = lambda g: g * (group * stride)
    else:
        gps = seq // (group * rc)
        assert seq % (group * rc) == 0
        member_off = [a * rc for a in range(group)]
        group_base = lambda g: (g // gps) * stride + (g % gps) * (group * rc)

    def one_group(g, carry):
        pad0 = pl.multiple_of(CONV_GAP + group_base(g), SUBLANES)
        out0 = pl.multiple_of(g * (group * rc), group * rc)
        for cl in range(n_lt):
            ls = slice(cl * LANES, (cl + 1) * LANES)
            parts = [[None] * CONV_PARTIALS for _ in range(group)]
            for k in range(CONV_K):
                off = k - CONV_HALF
                wk = w_ref[k:k + 1, ls]
                for a in range(group):
                    src = pl.ds(pad0 + member_off[a] + (off // SUBLANES) * SUBLANES, rc)
                    term = wk * pad_ref[off % SUBLANES, cl, src, :]
                    prev = parts[a][k % CONV_PARTIALS]
                    parts[a][k % CONV_PARTIALS] = term if prev is None else prev + term
            for a in range(group):
                acc_ref[pl.ds(out0 + a * rc, rc), ls] = _tree_sum(parts[a])
        return carry

    lax.fori_loop(0, tb // (group * rc), one_group, 0)
    y = acc_ref[...] + b_ref[...]
    y_ref[...] = _ln_silu(y, lg_ref[...], lb_ref[...]).astype(y_ref.dtype)


def _conv_rows(u, w, b, lg, lb, *, seq, row_off, n_rows):
    tb = max(seq, ROW_TILE) if n_rows % max(seq, ROW_TILE) == 0 else seq
    nseq = tb // seq
    off = row_off // tb
    vec = pl.BlockSpec((1, D_GROUP), lambda i: (0, 0))
    in_specs = [pl.BlockSpec((tb, D_GROUP), lambda i: (off + i, 0)),
                pl.BlockSpec((CONV_K, D_GROUP), lambda i: (0, 0)), vec, vec, vec]
    args = [u, w, b, lg, lb]
    kern = functools.partial(_conv_rows_kernel, seq=seq, tb=tb)
    return pl.pallas_call(
        kern,
        out_shape=jax.ShapeDtypeStruct((n_rows, D_GROUP), BF16),
        grid=(n_rows // tb,),
        in_specs=in_specs,
        out_specs=pl.BlockSpec((tb, D_GROUP), lambda i: (i, 0)),
        scratch_shapes=[pltpu.VMEM((SUBLANES, D_GROUP // LANES, CONV_GAP + nseq * (seq + CONV_GAP), LANES), F32),
                        pltpu.VMEM((tb, D_GROUP), F32)],
        compiler_params=_cparams(("arbitrary",), VMEM_LIMIT),
        name="conv_rows",
    )(*args)


CONV_COL_TILE = 16


def _conv_cols_kernel(u_ref, w_ref, b_ref, lg_ref, lb_ref, y_ref, uf_ref, *, n_rows):
    n_lt = D_GROUP // LANES
    row_stride = GRID_W + SUBLANES
    for cl in range(n_lt):
        ls = slice(cl * LANES, (cl + 1) * LANES)
        for r in range(n_rows):
            uf_ref[cl, r * row_stride:r * row_stride + GRID_W, :] = (
                u_ref[r * GRID_W:(r + 1) * GRID_W, ls].astype(F32))
    bias = b_ref[...]
    lg = lg_ref[...]
    lb = lb_ref[...]

    def col_tile(wi, carry):
        w0 = pl.multiple_of(wi * CONV_COL_TILE, CONV_COL_TILE)
        for r in range(n_rows):
            tiles = []
            for cl in range(n_lt):
                ls = slice(cl * LANES, (cl + 1) * LANES)
                parts = [None] * CONV_PARTIALS
                for k in range(CONV_K):
                    rr = r + k - CONV_HALF
                    if 0 <= rr < n_rows:
                        src = pl.ds(pl.multiple_of(rr * row_stride + w0, SUBLANES), CONV_COL_TILE)
                        term = w_ref[k:k + 1, ls] * uf_ref[cl, src, :]
                        prev = parts[k % CONV_PARTIALS]
                        parts[k % CONV_PARTIALS] = term if prev is None else prev + term
                tiles.append(_tree_sum([p for p in parts if p is not None]))
            dst = pl.ds(pl.multiple_of(r * GRID_W + w0, CONV_COL_TILE), CONV_COL_TILE)
            y_ref[dst, :] = _ln_silu(jnp.concatenate(tiles, axis=1) + bias, lg, lb).astype(y_ref.dtype)
        return carry

    lax.fori_loop(0, GRID_W // CONV_COL_TILE, col_tile, 0)


def _conv_cols(u, w, b, lg, lb, *, batch, seq_len):
    vec = pl.BlockSpec((1, D_GROUP), lambda bi: (0, 0))
    blk = pl.BlockSpec((seq_len, D_GROUP), lambda bi: (bi, 0))
    return pl.pallas_call(
        functools.partial(_conv_cols_kernel, n_rows=seq_len // GRID_W),
        out_shape=jax.ShapeDtypeStruct((batch * seq_len, D_GROUP), BF16),
        grid=(batch,),
        in_specs=[blk, pl.BlockSpec((CONV_K, D_GROUP), lambda bi: (0, 0)), vec, vec, vec],
        out_specs=blk,
        scratch_shapes=[pltpu.VMEM((D_GROUP // LANES, (seq_len // GRID_W) * (GRID_W + SUBLANES), LANES), F32)],
        compiler_params=_cparams(("arbitrary",), VMEM_LIMIT),
        name="conv_cols",
    )(u, w, b, lg, lb)


def _outproj_kernel(*refs, n_x, n_mix, n_lat_tiles):
    x = _rows_from(refs[:n_x], n_lat_tiles)
    pos = n_x
    ycx, of, ob = (_rows_from(refs[pos + k * n_mix:pos + (k + 1) * n_mix], n_lat_tiles) for k in range(3))
    og_ref, hg_ref, w_ref, g1_ref = refs[pos + 3 * n_mix:pos + 3 * n_mix + 4]
    route_in = refs[pos + 3 * n_mix + 4:pos + 3 * n_mix + 10]
    o_ref = refs[pos + 3 * n_mix + 10]
    route_out = refs[pos + 3 * n_mix + 11:]
    o = of.astype(F32) + ob.astype(F32)
    og = og_ref[...].astype(F32)
    hg = hg_ref[...]
    acc = jnp.dot(ycx, w_ref[0, 0:D_GROUP, :], preferred_element_type=F32)
    for h in range(HEADS):
        hs = slice(h * HEAD_DIM, (h + 1) * HEAD_DIM)
        oh = o[:, hs]
        r = lax.rsqrt(jnp.mean(oh * oh, axis=-1, keepdims=True) + EPS)
        yh = ((oh * r * hg) * og[:, hs]).astype(BF16)
        acc = acc + jnp.dot(yh, w_ref[0, D_GROUP + h * HEAD_DIM:D_GROUP + (h + 1) * HEAD_DIM, :],
                            preferred_element_type=F32)
    x = x + g1_ref[0] * acc
    o_ref[...] = x
    _route(x, *route_in, *route_out)


def _outproj_route(xs, mix, og, hg, w_bf, layer, g1, g2n, sh2, sc2, rw_t, rb, *, n_rows, tm, tiles_per_seq):
    d = xs[0].shape[1]
    nb = g1.shape[0] - 1
    n_tt = n_rows // tm
    x_specs, n_lat_tiles = _row_specs(xs, tm)
    mix_specs, mix_args = [], []
    for arrays in mix:
        specs, nl = _row_specs(arrays, tm)
        assert len(arrays) == len(mix[0]) and (nl == 0 or n_lat_tiles in (0, nl))
        n_lat_tiles = max(n_lat_tiles, nl)
        mix_specs += specs
        mix_args += list(arrays)
    row = lambda width: pl.BlockSpec((tm, width), lambda i: (i, 0))
    mod_spec = pl.BlockSpec((1, 1, d), lambda i: (jnp.minimum(i // tiles_per_seq, nb), 0, 0))
    pair = pl.BlockSpec((2, tm), lambda i: (0, i))
    kern = functools.partial(_outproj_kernel, n_x=len(xs), n_mix=len(mix[0]), n_lat_tiles=n_lat_tiles)
    before = jnp.asarray(np.arange(tm)[:, None] < np.arange(tm)[None, :], BF16)
    return pl.pallas_call(
        kern,
        out_shape=[jax.ShapeDtypeStruct((n_rows, d), F32),
                   jax.ShapeDtypeStruct((n_rows, d), BF16),
                   jax.ShapeDtypeStruct((2, n_rows), I32),
                   jax.ShapeDtypeStruct((2, n_rows), I32),
                   jax.ShapeDtypeStruct((2, n_rows), F32),
                   jax.ShapeDtypeStruct((n_tt, N_EXPERTS, LANES), I32)],
        grid=(n_tt,),
        in_specs=x_specs + mix_specs + [row(D_GROUP),
                  pl.BlockSpec((1, HEAD_DIM), lambda i: (0, 0)),
                  pl.BlockSpec((1,) + w_bf.shape[1:], lambda i: (layer, 0, 0)),
                  mod_spec,
                  pl.BlockSpec((1, d), lambda i: (0, 0)), mod_spec, mod_spec,
                  pl.BlockSpec((N_EXPERTS, d), lambda i: (0, 0)),
                  pl.BlockSpec((N_EXPERTS, 1), lambda i: (0, 0)),
                  pl.BlockSpec((tm, tm), lambda i: (0, 0))],
        out_specs=[row(d), row(d), pair, pair, pair,
                   pl.BlockSpec((1, N_EXPERTS, LANES), lambda i: (i, 0, 0))],
        compiler_params=_cparams(("arbitrary",), VMEM_LIMIT),
        name="outproj_route",
    )(*xs, *mix_args, og, hg, w_bf, g1, g2n, sh2, sc2, rw_t, rb, before)


def _route(x, g_ref, sh_ref, sc_ref, rw_ref, rb_ref, before_ref, h_ref, e_ref, rank_ref, w_ref, cnt_ref):
    tm = x.shape[0]
    h = _modulate(x, g_ref[...], sh_ref[0], sc_ref[0])
    h_ref[...] = h.astype(h_ref.dtype)
    logits = lax.dot_general(rw_ref[...], h, (((1,), (1,)), ((), ())),
                             precision=lax.Precision.HIGHEST, preferred_element_type=F32)
    s = jax.nn.sigmoid(logits)
    sb = s + rb_ref[...]
    s_rows = [s[e:e + 1, :] for e in range(N_EXPERTS)]
    sb_rows = [sb[e:e + 1, :] for e in range(N_EXPERTS)]

    def group_score(g):
        v = sb_rows[g * EXPERTS_PER_GROUP:(g + 1) * EXPERTS_PER_GROUP]
        best = None
        for a in range(EXPERTS_PER_GROUP):
            for b in range(a + 1, EXPERTS_PER_GROUP):
                p = v[a] + v[b]
                best = p if best is None else jnp.maximum(best, p)
        return best

    cur = group_score(0)
    best_g = jnp.zeros(cur.shape, I32)
    for g in range(1, N_GROUPS):
        gs = group_score(g)
        upd = gs > cur
        best_g = jnp.where(upd, g, best_g)
        cur = jnp.where(upd, gs, cur)

    def pick(rows, jdx):
        out = rows[jdx]
        for g in range(1, N_GROUPS):
            out = jnp.where(best_g == g, rows[g * EXPERTS_PER_GROUP + jdx], out)
        return out

    vb = [pick(sb_rows, jdx) for jdx in range(EXPERTS_PER_GROUP)]
    vs = [pick(s_rows, jdx) for jdx in range(EXPERTS_PER_GROUP)]

    def first_argmax(vals):
        m = vals[0]
        for v in vals[1:]:
            m = jnp.maximum(m, v)
        idx = jnp.full(m.shape, EXPERTS_PER_GROUP - 1, I32)
        for jdx in range(EXPERTS_PER_GROUP - 2, -1, -1):
            idx = jnp.where(vals[jdx] == m, jdx, idx)
        return idx

    def take(vals, idx):
        out = vals[EXPERTS_PER_GROUP - 1]
        for jdx in range(EXPERTS_PER_GROUP - 2, -1, -1):
            out = jnp.where(idx == jdx, vals[jdx], out)
        return out

    i0 = first_argmax(vb)
    vb2 = [jnp.where(i0 == jdx, -jnp.inf, vb[jdx]) for jdx in range(EXPERTS_PER_GROUP)]
    i1 = first_argmax(vb2)
    s0 = take(vs, i0)
    s1 = take(vs, i1)
    tot = s0 + s1
    w_ref[0:1, :] = s0 / tot
    w_ref[1:2, :] = s1 / tot
    e0 = best_g * EXPERTS_PER_GROUP + i0
    e1 = best_g * EXPERTS_PER_GROUP + i1
    e_ref[0:1, :] = e0
    e_ref[1:2, :] = e1

    eid = lax.broadcasted_iota(I32, (N_EXPERTS, tm), 0)
    hit0 = eid == e0
    hit1 = eid == e1
    onehot = jnp.where(jnp.logical_or(hit0, hit1), 1.0, 0.0)
    rank = jnp.dot(onehot.astype(BF16), before_ref[...], preferred_element_type=F32)
    rank_ref[0:1, :] = jnp.sum(jnp.where(hit0, rank, 0.0), axis=0, keepdims=True).astype(I32)
    rank_ref[1:2, :] = jnp.sum(jnp.where(hit1, rank, 0.0), axis=0, keepdims=True).astype(I32)
    cnt = jnp.sum(onehot, axis=1, keepdims=True).astype(I32)
    cnt_ref[0] = jnp.broadcast_to(cnt, (N_EXPERTS, LANES))


def _tables_kernel(cnt_ref, seg_start_ref, seg_len_ref, te_ref, misc_ref, *, n_tt, n_et, tm):
    def expert(e, row0):
        def tile(i, pos):
            n = cnt_ref[i * N_EXPERTS + e]
            n8 = ((n + SUBLANES - 1) // SUBLANES) * SUBLANES
            seg_start_ref[i * N_EXPERTS + e] = pos
            seg_len_ref[i * N_EXPERTS + e] = n8
            return pos + n8

        end = lax.fori_loop(0, n_tt, tile, row0)
        padded_end = row0 + ((end - row0 + tm - 1) // tm) * tm
        misc_ref[1 + e] = end
        misc_ref[1 + N_EXPERTS + e] = padded_end - end

        def mark(j, carry):
            te_ref[j] = e
            return carry

        lax.fori_loop(row0 // tm, padded_end // tm, mark, 0)
        return padded_end

    total = row0 = 0
    for e in range(N_EXPERTS):
        row0 = expert(e, row0)
    total = row0
    n_used = total // tm
    misc_ref[0] = n_used
    last = te_ref[jnp.maximum(n_used - 1, 0)]

    def fill(j, carry):
        te_ref[j] = last
        return carry

    lax.fori_loop(n_used, n_et, fill, 0)


def _tables(cnt, *, n_tt, n_et, tm):
    smem = pl.BlockSpec(memory_space=pltpu.SMEM)
    return pl.pallas_call(
        functools.partial(_tables_kernel, n_tt=n_tt, n_et=n_et, tm=tm),
        out_shape=[jax.ShapeDtypeStruct((n_tt * N_EXPERTS,), I32),
                   jax.ShapeDtypeStruct((n_tt * N_EXPERTS,), I32),
                   jax.ShapeDtypeStruct((n_et,), I32),
                   jax.ShapeDtypeStruct((1 + 2 * N_EXPERTS,), I32)],
        in_specs=[smem],
        out_specs=[smem, smem, smem, smem],
        name="route_tables",
    )(cnt)


def _piece_sizes(max_rows):
    sizes, s = [], SUBLANES
    while s <= max_rows:
        sizes.append(s)
        s *= 2
    return sizes[::-1]


def _segment_copies(src_ref, src_row, dst_ref, dst_row, n_rows, sem, sizes, start, lane=0):
    aligned = lambda r: r if isinstance(r, int) else pl.multiple_of(r, SUBLANES)
    for k, sz in enumerate(sizes):
        src = pl.ds(aligned(src_row), sz)
        dst = pl.ds(aligned(dst_row), sz)

        @pl.when((n_rows & sz) != 0)
        def _(src=src, dst=dst, k=k):
            cp = pltpu.make_async_copy(src_ref.at[src, :], dst_ref.at[dst, :], sem)
            if start:
                cp.start(priority=(lane + k) % 2)
            else:
                cp.wait()

        src_row = src_row + (n_rows & sz)
        dst_row = dst_row + (n_rows & sz)


def _local_slots(seg_len_ref, tile, e_ref, rank_ref):
    offs, lo = [], 0
    for e in range(N_EXPERTS):
        offs.append(lo)
        lo = lo + seg_len_ref[tile * N_EXPERTS + e]
    e01 = e_ref[...]
    slot = rank_ref[...]
    for e in range(N_EXPERTS):
        slot = slot + jnp.where(e01 == e, offs[e], 0)
    return slot, offs


def _compact_rows(tm):
    return ((2 * tm + N_EXPERTS * (SUBLANES - 1) + LANES - 1) // LANES) * LANES


def _dispatch_kernel(seg_start_ref, seg_len_ref, misc_ref, h_ref, e_ref, rank_ref, xs_ref,
                     xc_ref, zero_ref, sems, *, sizes):
    tm = h_ref.shape[0]
    rc = xc_ref.shape[1]
    i = pl.program_id(0)
    buf = i % 2
    slot, _ = _local_slots(seg_len_ref, i, e_ref, rank_ref)
    rid = lax.broadcasted_iota(I32, (rc, tm), 0)
    sel = jnp.logical_or(rid == slot[0:1, :], rid == slot[1:2, :])
    perm = jnp.where(sel, 1.0, 0.0).astype(BF16)
    xc_ref[buf] = jnp.dot(perm, h_ref[...], preferred_element_type=F32)

    def segments(tile, b, start):
        lo = 0
        for e in range(N_EXPERTS):
            n8 = seg_len_ref[tile * N_EXPERTS + e]
            _segment_copies(xc_ref.at[b], lo, xs_ref, seg_start_ref[tile * N_EXPERTS + e], n8,
                            sems.at[b], sizes, start, lane=e % 2)
            lo = lo + n8

    @pl.when(i > 0)
    def _():
        segments(i - 1, 1 - buf, False)

    segments(i, buf, True)

    @pl.when(i == pl.num_programs(0) - 1)
    def _():
        segments(i, buf, False)
        zero_ref[...] = jnp.zeros(zero_ref.shape, F32)
        for start in (True, False):
            for e in range(N_EXPERTS):
                _segment_copies(zero_ref, 0, xs_ref, misc_ref[1 + e], misc_ref[1 + N_EXPERTS + e],
                                sems.at[0], sizes, start)

        def clear_tile(j, carry):
            cp = pltpu.make_async_copy(zero_ref, xs_ref.at[pl.ds(pl.multiple_of(j * tm, tm), tm), :],
                                       sems.at[0])
            cp.start()
            cp.wait()
            return carry

        lax.fori_loop(misc_ref[0], xs_ref.shape[0] // tm, clear_tile, 0)


def _dispatch(seg_start, seg_len, misc, h, e01, rank01, *, n_sorted, tm):
    n, d = h.shape
    pair = pl.BlockSpec((2, tm), lambda i, *_: (0, i))
    grid_spec = pltpu.PrefetchScalarGridSpec(
        num_scalar_prefetch=3,
        grid=(n // tm,),
        in_specs=[pl.BlockSpec((tm, d), lambda i, *_: (i, 0)), pair, pair],
        out_specs=pl.BlockSpec(memory_space=pl.ANY),
        scratch_shapes=[pltpu.VMEM((2, _compact_rows(tm), d), F32), pltpu.VMEM((tm, d), F32),
                        pltpu.SemaphoreType.DMA((2,))],
    )
    return pl.pallas_call(
        functools.partial(_dispatch_kernel, sizes=_piece_sizes(tm)),
        out_shape=jax.ShapeDtypeStruct((n_sorted, d), F32),
        grid_spec=grid_spec,
        compiler_params=_cparams(("arbitrary",), VMEM_LIMIT),
        name="dispatch",
    )(seg_start, seg_len, misc, h, e01, rank01)


def _moe_kernel(te_ref, misc_ref, xs_ref, wg_hbm, wu_hbm, wd_hbm, ys_ref,
                wg_buf, wu_buf, wd_buf, wgb_ref, wub_ref, wdb_ref, run_ref, sems, *, layer, tm):
    i = pl.program_id(0)
    e = te_ref[i]
    prev = te_ref[jnp.maximum(i - 1, 0)]
    n_used = misc_ref[0]

    def fetch(expert, slot):
        return [pltpu.make_async_copy(w.at[layer, expert], buf.at[slot], sems.at[slot, k])
                for k, (w, buf) in enumerate(((wg_hbm, wg_buf), (wu_hbm, wu_buf), (wd_hbm, wd_buf)))]

    @pl.when(i == 0)
    def _():
        run_ref[0] = 0
        for cp in fetch(e, 0):
            cp.start()

    @pl.when(jnp.logical_or(i == 0, e != prev))
    def _():
        slot = run_ref[0] % 2
        for cp in fetch(e, slot):
            cp.wait()
        nxt = (misc_ref[1 + e] + misc_ref[1 + N_EXPERTS + e]) // tm

        @pl.when(nxt < n_used)
        def _():
            for cp in fetch(te_ref[nxt], 1 - slot):
                cp.start()

        wgb_ref[...] = wg_buf[slot].astype(BF16)
        wub_ref[...] = wu_buf[slot].astype(BF16)
        wdb_ref[...] = wd_buf[slot].astype(BF16)
        run_ref[0] = run_ref[0] + 1

    @pl.when(i < misc_ref[0])
    def _():
        x = xs_ref[...].astype(BF16)
        gate = jnp.dot(x, wgb_ref[...], preferred_element_type=F32)
        up = jnp.dot(x, wub_ref[...], preferred_element_type=F32)
        act = (_silu(gate) * up).astype(BF16)
        ys_ref[...] = jnp.dot(act, wdb_ref[...], preferred_element_type=F32)

    @pl.when(i >= misc_ref[0])
    def _():
        ys_ref[...] = jnp.zeros(ys_ref.shape, F32)


def _moe(tile_expert, misc, xs, wg, wu, wd, layer, tm):
    n_sorted, d = xs.shape
    de = wg.shape[-1]
    xmap = lambda i, te, misc_ref: (jnp.minimum(i, jnp.maximum(misc_ref[0] - 1, 0)), 0)
    hbm = pl.BlockSpec(memory_space=pl.ANY)
    grid_spec = pltpu.PrefetchScalarGridSpec(
        num_scalar_prefetch=2,
        grid=(n_sorted // tm,),
        in_specs=[pl.BlockSpec((tm, d), xmap), hbm, hbm, hbm],
        out_specs=pl.BlockSpec((tm, d), lambda i, te, misc_ref: (i, 0)),
        scratch_shapes=[pltpu.VMEM((2, d, de), F32), pltpu.VMEM((2, d, de), F32),
                        pltpu.VMEM((2, de, d), F32),
                        pltpu.VMEM((d, de), BF16), pltpu.VMEM((d, de), BF16),
                        pltpu.VMEM((de, d), BF16),
                        pltpu.SMEM((1,), I32), pltpu.SemaphoreType.DMA((2, 3))],
    )
    return pl.pallas_call(
        functools.partial(_moe_kernel, layer=layer, tm=tm),
        out_shape=jax.ShapeDtypeStruct((n_sorted, d), F32),
        grid_spec=grid_spec,
        compiler_params=_cparams(("arbitrary",), VMEM_LIMIT),
        name="moe_experts",
    )(tile_expert, misc, xs, wg, wu, wd)


def _combine_kernel(seg_start_ref, seg_len_ref, x_ref, g2_ref, e_ref, rank_ref, w_ref, fg_ref, ys_ref,
                    o_ref, yc_ref, sems, *, sizes, final_norm):
    tm = x_ref.shape[0]
    rc = yc_ref.shape[1]
    i = pl.program_id(0)
    buf = i % 2

    def segments(tile, b, start):
        lo = 0
        for e in range(N_EXPERTS):
            n8 = seg_len_ref[tile * N_EXPERTS + e]
            _segment_copies(ys_ref, seg_start_ref[tile * N_EXPERTS + e], yc_ref.at[b], lo, n8,
                            sems.at[b], sizes, start, lane=e % 2)
            lo = lo + n8

    @pl.when(i == 0)
    def _():
        yc_ref[...] = jnp.zeros(yc_ref.shape, F32)
        segments(0, 0, True)

    @pl.when(i + 1 < pl.num_programs(0))
    def _():
        segments(i + 1, 1 - buf, True)

    segments(i, buf, False)
    slot, _ = _local_slots(seg_len_ref, i, e_ref, rank_ref)
    rid = lax.broadcasted_iota(I32, (rc, tm), 0)
    w = w_ref[...]
    pw = (jnp.where(rid == slot[0:1, :], w[0:1, :], 0.0)
          + jnp.where(rid == slot[1:2, :], w[1:2, :], 0.0))
    dn = (((0,), (0,)), ((), ()))
    out = lax.dot_general(pw.astype(BF16), yc_ref[buf].astype(BF16), dn, preferred_element_type=F32)
    x = x_ref[...] + g2_ref[0] * out
    if final_norm:
        x = x * lax.rsqrt(jnp.mean(x * x, axis=-1, keepdims=True) + EPS) * fg_ref[...]
    o_ref[...] = x


def _combine(seg_start, seg_len, xc, g2, e01, rank01, w01, fg, ys, *, n_rows, tm, tiles_per_seq,
             final_norm):
    d = xc.shape[1]
    nb = g2.shape[0] - 1
    pair = pl.BlockSpec((2, tm), lambda i, *_: (0, i))
    grid_spec = pltpu.PrefetchScalarGridSpec(
        num_scalar_prefetch=2,
        grid=(n_rows // tm,),
        in_specs=[pl.BlockSpec((tm, d), lambda i, *_: (i, 0)),
                  pl.BlockSpec((1, 1, d), lambda i, *_: (jnp.minimum(i // tiles_per_seq, nb), 0, 0)),
                  pair, pair, pair,
                  pl.BlockSpec((1, d), lambda i, *_: (0, 0)),
                  pl.BlockSpec(memory_space=pl.ANY)],
        out_specs=pl.BlockSpec((tm, d), lambda i, *_: (i, 0)),
        scratch_shapes=[pltpu.VMEM((2, _compact_rows(tm), d), F32), pltpu.SemaphoreType.DMA((2,))],
    )
    return pl.pallas_call(
        functools.partial(_combine_kernel, sizes=_piece_sizes(tm), final_norm=final_norm),
        out_shape=jax.ShapeDtypeStruct((n_rows, d), F32),
        grid_spec=grid_spec,
        compiler_params=_cparams(("arbitrary",), VMEM_LIMIT),
        name="combine",
    )(seg_start, seg_len, xc, g2, e01, rank01, w01, fg, ys)


def _lower_bounds(lb_param):
    p = jax.nn.softmax(lb_param.astype(F32), axis=0)
    return jnp.cumsum(p, axis=0) - p[0]


def kernel(x, c, ctx, c_ctx, w_ada, b_ada, norm1_g, norm2_g, w_in, conv_w, conv_b, conv_ln_g,
           conv_ln_b, lb_fwd, lb_bwd, hgrn_norm_g, w_out, router_w, router_bias, w_gate, w_up,
           w_down, final_norm_g):
    bn, seq, d = x.shape
    ctx_len = ctx.shape[1]
    depth = w_ada.shape[0]
    n = bn * seq
    nc = bn * ctx_len
    tm = min(ROW_TILE, seq, nc)
    assert seq % tm == 0 and nc % tm == 0 and seq % GRID_W == 0
    tps = seq // tm

    pad_rows = (-(bn + 1)) % 8
    cc = jnp.concatenate([c, c_ctx[None, :], jnp.zeros((pad_rows, d), F32)], axis=0)
    mod = _ada(cc, w_ada, b_ada)[:, :bn + 1]
    mod = mod.reshape(depth, bn + 1, 6, 1, d)

    lbs_f = _lower_bounds(lb_fwd)
    lbs_b = _lower_bounds(lb_bwd)
    w_in_bf = w_in.astype(BF16)
    w_out_bf = w_out.astype(BF16)
    rw_t = router_w.T.astype(F32)
    rb = router_bias.reshape(N_EXPERTS, 1).astype(F32)
    hg = hgrn_norm_g.astype(F32)
    fg = final_norm_g.reshape(1, d).astype(F32)

    rows_in = [x.reshape(n, d), ctx.reshape(nc, d)]
    s_zero = jnp.zeros((bn, 2 * HEADS, HEAD_DIM, HEAD_DIM), F32)

    for l in range(depth):
        last = l == depth - 1
        sh1, sc1, g1, sh2, sc2, g2 = (mod[l, :, k] for k in range(6))
        row = lambda a: a.reshape(1, -1).astype(F32)

        u, q, iv, lff, kf, lfb, kb, og = _inproj(
            rows_in, row(norm1_g[l]), sh1, sc1, w_in_bf, l, row(lbs_f[l]), row(lbs_b[l]), tm, tps)

        cw, cb = conv_w[l].astype(F32), row(conv_b[l])
        clg, clb = row(conv_ln_g[l]), row(conv_ln_b[l])
        scan = functools.partial(_scan, q, iv, lff, kf, lfb, kb)
        if last:
            (s_ctx,) = scan(s_zero, row_off=n, seq_len=ctx_len, compute_o=False)
            mix_ctx = [[], [], []]
            n_rows = n
        else:
            of_c, ob_c, s_ctx = scan(s_zero, row_off=n, seq_len=ctx_len, compute_o=True)
            ycx_c = _conv_rows(u, cw, cb, clg, clb, seq=ctx_len, row_off=n, n_rows=nc)
            mix_ctx = [[ycx_c], [of_c], [ob_c]]
            n_rows = n + nc
        of, ob, _ = scan(s_ctx, row_off=0, seq_len=seq, compute_o=True)
        if l % 2 == 0:
            ycx = _conv_rows(u, cw, cb, clg, clb, seq=GRID_W, row_off=0, n_rows=n)
        else:
            ycx = _conv_cols(u, cw, cb, clg, clb, batch=bn, seq_len=seq)
        mix = [[a] + c_ for a, c_ in zip((ycx, of, ob), mix_ctx)]

        res_rows = rows_in if (len(rows_in) == 1 or not last) else [rows_in[0]]
        xc, h2, e01, rank01, w01, cnt = _outproj_route(
            res_rows, mix, og, hg[l].reshape(1, HEAD_DIM), w_out_bf, l, g1,
            row(norm2_g[l]), sh2, sc2, rw_t, rb, n_rows=n_rows, tm=tm, tiles_per_seq=tps)
        n_tt = n_rows // tm
        n_et = (2 * n_rows + n_tt * N_EXPERTS * (SUBLANES - 1) + tm - 1) // tm + N_EXPERTS
        seg_start, seg_len, te, misc = _tables(cnt[:, :, 0].reshape(-1), n_tt=n_tt, n_et=n_et, tm=tm)
        xs = _dispatch(seg_start, seg_len, misc, h2, e01, rank01, n_sorted=n_et * tm, tm=tm)
        ys = _moe(te, misc, xs, w_gate, w_up, w_down, l, tm)
        xc = _combine(seg_start, seg_len, xc, g2, e01, rank01, w01, fg, ys, n_rows=n_rows, tm=tm,
                      tiles_per_seq=tps, final_norm=last)
        rows_in = [xc]

    return xc[:n].reshape(bn, seq, d)
```
